```python
import math
import jax, jax.numpy as jnp
from jax import lax
import numpy as np


D_MODEL = 2048
BATCH = 2
SEQ = 4096
DEPTH = 1

D_SSD = D_MODEL // 2
SSD_HEAD_DIM = 64
SSD_HEADS = D_SSD // SSD_HEAD_DIM
SSD_GROUPS = 4
D_STATE = 128
CONV_WIDTH = 4
CHUNK = 128
D_CONV = D_SSD + 2 * SSD_GROUPS * D_STATE
D_POOL = D_MODEL - D_SSD
POOL_WINDOWS = (2, 4, 8, 16)
POOL_GROUPS = len(POOL_WINDOWS)
POOL_GROUP_DIM = D_POOL // POOL_GROUPS
D_MIX = D_SSD + D_POOL
D_IN_PROJ = D_SSD + D_CONV + SSD_HEADS + D_POOL
N_MEM = 256
XATTN_HEADS = 4
XATTN_HEAD_DIM = D_MODEL // XATTN_HEADS
N_EXPERTS = 32
TOP_K = 4
D_FF = D_MODEL
SWIGLU_LIMIT = 7.0
SWIGLU_ALPHA = 1.702
MOE_BLOCK = 128
EPS = 1e-5

kernel_name = 'hybrid_ssd_pool_xattn_moe_layer'


def _rmsnorm(x, w):
    x32 = x.astype(jnp.float32)
    y = x32 * lax.rsqrt(jnp.mean(x32 * x32, axis=-1, keepdims=True) + EPS)
    return (y * w.astype(jnp.float32)).astype(x.dtype)


def _ssd_chunked(xh, dt, a, bm, cm):
    b, s = xh.shape[:2]
    c = s // CHUNK
    r = SSD_HEADS // SSD_GROUPS
    dtype = xh.dtype
    xr = (xh * dt[..., None].astype(dtype)).reshape(b, c, CHUNK, SSD_GROUPS, r, SSD_HEAD_DIM)
    a_cs = jnp.cumsum((dt * a).reshape(b, c, CHUNK, SSD_GROUPS, r), axis=2)
    br = bm.reshape(b, c, CHUNK, SSD_GROUPS, D_STATE)
    cr = cm.reshape(b, c, CHUNK, SSD_GROUPS, D_STATE)
    causal = jnp.tril(jnp.ones((CHUNK, CHUNK), dtype=bool))[None, None, :, :, None, None]
    seg = a_cs[:, :, :, None] - a_cs[:, :, None, :]
    decay_ls = jnp.exp(jnp.where(causal, seg, -jnp.inf)).astype(dtype)
    cb = jnp.einsum('bclgn,bcsgn->bclsg', cr, br)
    y_diag = jnp.einsum('bclsg,bclsgr,bcsgrp->bclgrp', cb, decay_ls, xr)
    decay_to_end = jnp.exp(a_cs[:, :, -1:] - a_cs).astype(dtype)
    states = jnp.einsum('bclgn,bclgr,bclgrp->bcgrpn', br, decay_to_end, xr)
    chunk_decay = jnp.exp(a_cs[:, :, -1]).astype(dtype)

    def step(h, inp):
        st, dec = inp
        return h * dec[..., None, None] + st, h

    h0 = jnp.zeros((b, SSD_GROUPS, r, SSD_HEAD_DIM, D_STATE), dtype)
    _, prev = lax.scan(step, h0, (jnp.moveaxis(states, 1, 0), jnp.moveaxis(chunk_decay, 1, 0)))
    prev = jnp.moveaxis(prev, 0, 1)
    y_off = jnp.einsum('bclgn,bcgrpn,bclgr->bclgrp', cr, prev, jnp.exp(a_cs).astype(dtype))
    return (y_diag + y_off).reshape(b, s, SSD_HEADS, SSD_HEAD_DIM)


def _pool_mixer(u, pool_w, pool_scale):
    b, s, _ = u.shape
    pos_count = jnp.arange(1, s + 1, dtype=jnp.float32)[None, :, None]
    outs = []
    for g, w in enumerate(POOL_WINDOWS):
        ug = u[..., g * POOL_GROUP_DIM:(g + 1) * POOL_GROUP_DIM].astype(jnp.float32)
        csp = jnp.concatenate([jnp.zeros_like(ug[:, :1]), jnp.cumsum(ug, axis=1)], axis=1)
        lo = jnp.concatenate([jnp.zeros_like(ug[:, :w - 1]), csp[:, :s - w + 1]], axis=1)
        mean = (csp[:, 1:] - lo) / jnp.minimum(pos_count, float(w))
        outs.append(mean - ug)
    pooled = jnp.stack(outs, axis=2).astype(u.dtype)
    mixed = jnp.einsum('bsgc,gcd->bsgd', pooled, pool_w)
    return mixed.reshape(b, s, D_POOL) * pool_scale


def _mixer(hn, w_in, conv_w, conv_b, dt_bias, a_log, d_skip, ssd_norm, pool_w, pool_scale, w_out):
    b, s, _ = hn.shape
    proj = hn @ w_in
    z, xbc, dt_raw, u = jnp.split(proj, [D_SSD, D_SSD + D_CONV, D_SSD + D_CONV + SSD_HEADS], axis=-1)
    xbc = lax.conv_general_dilated(xbc, conv_w[:, None, :], (1,), [(CONV_WIDTH - 1, 0)],
                                   dimension_numbers=('NWC', 'WIO', 'NWC'),
                                   feature_group_count=D_CONV)
    xbc = jax.nn.silu(xbc + conv_b)
    xs, bs, cs = jnp.split(xbc, [D_SSD, D_SSD + SSD_GROUPS * D_STATE], axis=-1)
    dt = jax.nn.softplus(dt_raw.astype(jnp.float32) + dt_bias.astype(jnp.float32))
    a = -jnp.exp(a_log.astype(jnp.float32))
    xh = xs.reshape(b, s, SSD_HEADS, SSD_HEAD_DIM)
    y = _ssd_chunked(xh, dt, a, bs.reshape(b, s, SSD_GROUPS, D_STATE), cs.reshape(b, s, SSD_GROUPS, D_STATE))
    y = y + d_skip[:, None] * xh
    y = (y.reshape(b, s, D_SSD) * jax.nn.silu(z)).reshape(b, s, SSD_GROUPS, D_SSD // SSD_GROUPS)
    y = _rmsnorm(y, ssd_norm.reshape(SSD_GROUPS, D_SSD // SSD_GROUPS)).reshape(b, s, D_SSD)
    yp = _pool_mixer(u, pool_w, pool_scale)
    return jnp.concatenate([y, yp], axis=-1) @ w_out


def _cross_attn(hn, mem_n, w_q, w_kv, w_o):
    b, s, _ = hn.shape
    q = (hn @ w_q).reshape(b, s, XATTN_HEADS, XATTN_HEAD_DIM)
    k, v = jnp.split(mem_n @ w_kv, 2, axis=-1)
    k = k.reshape(b, -1, XATTN_HEADS, XATTN_HEAD_DIM)
    v = v.reshape(b, -1, XATTN_HEADS, XATTN_HEAD_DIM)
    scores = jnp.einsum('bshd,bmhd->bhsm', q.astype(jnp.float32), k.astype(jnp.float32)) * (XATTN_HEAD_DIM ** -0.5)
    p = jax.nn.softmax(scores, axis=-1).astype(v.dtype)
    o = jnp.einsum('bhsm,bmhd->bshd', p, v).reshape(b, s, D_MODEL)
    return o @ w_o


def _moe(hn, w_router, b_router, w_gate, b_gate, w_up, b_up, w_down, b_down):
    b, s, d = hn.shape
    t = b * s
    xf = hn.reshape(t, d)
    logits = (xf @ w_router + b_router).astype(jnp.float32)
    top_val, top_idx = lax.top_k(logits, TOP_K)
    gates = jax.nn.softmax(top_val, axis=-1).astype(hn.dtype)
    tk = t * TOP_K
    n_blocks = (tk + N_EXPERTS * (MOE_BLOCK - 1) + MOE_BLOCK - 1) // MOE_BLOCK
    n_rows = n_blocks * MOE_BLOCK
    flat_e = top_idx.reshape(tk).astype(jnp.int32)
    flat_tok = jnp.arange(tk, dtype=jnp.int32) // TOP_K
    flat_g = gates.reshape(tk)
    order = jnp.argsort(flat_e, stable=True)
    sorted_e = flat_e[order]
    counts = jnp.bincount(flat_e, length=N_EXPERTS).astype(jnp.int32)
    padded = (counts + MOE_BLOCK - 1) // MOE_BLOCK * MOE_BLOCK
    start = jnp.cumsum(counts) - counts
    padded_end = jnp.cumsum(padded)
    padded_start = padded_end - padded
    dest = padded_start[sorted_e] + jnp.arange(tk, dtype=jnp.int32) - start[sorted_e]
    row_tok = jnp.full((n_rows,), t, jnp.int32).at[dest].set(flat_tok[order])
    row_gate = jnp.zeros((n_rows,), hn.dtype).at[dest].set(flat_g[order])
    block_e = jnp.clip(jnp.searchsorted(padded_end, jnp.arange(n_blocks, dtype=jnp.int32) * MOE_BLOCK,
                                        side='right'), 0, N_EXPERTS - 1)
    x_pad = jnp.concatenate([xf, jnp.zeros((1, d), xf.dtype)], axis=0)
    rows = x_pad[row_tok].reshape(n_blocks, MOE_BLOCK, d)

    def expert_block(args):
        xb, e = args
        g = xb @ w_gate[e] + b_gate[e]
        u = xb @ w_up[e] + b_up[e]
        g = jnp.minimum(g, SWIGLU_LIMIT)
        u = jnp.clip(u, -SWIGLU_LIMIT, SWIGLU_LIMIT)
        act = g * jax.nn.sigmoid(SWIGLU_ALPHA * g) * (u + 1.0)
        return act @ w_down[e] + b_down[e]

    y = lax.map(expert_block, (rows, block_e)).reshape(n_rows, d)
    out = jnp.zeros((t + 1, d), y.dtype).at[row_tok].add(y * row_gate[:, None])
    return out[:t].reshape(b, s, d)


def setup_inputs(seed: int = 0) -> dict:
    key = jax.random.key(seed)
    ks = jax.random.split(key, 32)
    f32 = jnp.float32
    L = DEPTH

    def w(k, shape, fan_in):
        return jax.random.normal(k, shape, f32) * (fan_in ** -0.5)

    def gain(k, shape):
        return 1.0 + 0.05 * jax.random.normal(k, shape, f32)

    def small(k, shape):
        return 0.01 * jax.random.normal(k, shape, f32)

    dt0 = jnp.exp(jax.random.uniform(ks[5], (L, SSD_HEADS), f32, math.log(1e-3), math.log(1e-1)))
    return {
        'x': jax.random.normal(ks[0], (BATCH, SEQ, D_MODEL), f32),
        'mem': jax.random.normal(ks[1], (BATCH, N_MEM, D_MODEL), f32),
        'norm_mix': gain(ks[2], (L, D_MODEL)),
        'w_in': w(ks[3], (L, D_MODEL, D_IN_PROJ), D_MODEL),
        'conv_w': w(ks[4], (L, CONV_WIDTH, D_CONV), CONV_WIDTH),
        'conv_b': small(ks[6], (L, D_CONV)),
        'dt_bias': dt0 + jnp.log(-jnp.expm1(-dt0)),
        'a_log': jnp.log(jax.random.uniform(ks[7], (L, SSD_HEADS), f32, 1.0, 16.0)),
        'd_skip': gain(ks[8], (L, SSD_HEADS)),
        'ssd_norm': gain(ks[9], (L, D_SSD)),
        'pool_w': w(ks[10], (L, POOL_GROUPS, POOL_GROUP_DIM, POOL_GROUP_DIM), POOL_GROUP_DIM),
        'pool_scale': 1.0 + 0.1 * jax.random.normal(ks[11], (L, D_POOL), f32),
        'w_out': w(ks[12], (L, D_MIX, D_MODEL), D_MIX),
        'norm_xattn': gain(ks[13], (L, D_MODEL)),
        'norm_mem': gain(ks[14], (L, D_MODEL)),
        'w_q': w(ks[15], (L, D_MODEL, D_MODEL), D_MODEL),
        'w_kv': w(ks[16], (L, D_MODEL, 2 * D_MODEL), D_MODEL),
        'w_o': w(ks[17], (L, D_MODEL, D_MODEL), D_MODEL),
        'norm_ffn': gain(ks[18], (L, D_MODEL)),
        'w_router': w(ks[19], (L, D_MODEL, N_EXPERTS), D_MODEL),
        'b_router': small(ks[20], (L, N_EXPERTS)),
        'w_gate': w(ks[21], (L, N_EXPERTS, D_MODEL, D_FF), D_MODEL),
        'b_gate': small(ks[22], (L, N_EXPERTS, D_FF)),
        'w_up': w(ks[23], (L, N_EXPERTS, D_MODEL, D_FF), D_MODEL),
        'b_up': small(ks[24], (L, N_EXPERTS, D_FF)),
        'w_down': w(ks[25], (L, N_EXPERTS, D_FF, D_MODEL), D_FF),
        'b_down': small(ks[26], (L, N_EXPERTS, D_MODEL)),
        'norm_final': gain(ks[27], (D_MODEL,)),
    }


def reference(x, mem, norm_mix, w_in, conv_w, conv_b, dt_bias, a_log, d_skip, ssd_norm, pool_w,
              pool_scale, w_out, norm_xattn, norm_mem, w_q, w_kv, w_o, norm_ffn, w_router, b_router,
              w_gate, b_gate, w_up, b_up, w_down, b_down, norm_final):
    h = x
    for l in range(DEPTH):
        h = h + _mixer(_rmsnorm(h, norm_mix[l]), w_in[l], conv_w[l], conv_b[l], dt_bias[l], a_log[l],
                       d_skip[l], ssd_norm[l], pool_w[l], pool_scale[l], w_out[l])
        h = h + _cross_attn(_rmsnorm(h, norm_xattn[l]), _rmsnorm(mem, norm_mem[l]), w_q[l], w_kv[l], w_o[l])
        h = h + _moe(_rmsnorm(h, norm_ffn[l]), w_router[l], b_router[l], w_gate[l], b_gate[l],
                     w_up[l], b_up[l], w_down[l], b_down[l])
    return _rmsnorm(h, norm_final)
```

```python
import functools

import jax
import jax.numpy as jnp
from jax import lax
from jax.experimental import pallas as pl
from jax.experimental.pallas import tpu as pltpu

F32 = jnp.float32
BF16 = jnp.bfloat16

D_MODEL = 2048
D_SSD = 1024
SSD_HEAD_DIM = 64
SSD_HEADS = 16
SSD_GROUPS = 4
HEADS_PER_GROUP = SSD_HEADS // SSD_GROUPS
D_STATE = 128
CONV_WIDTH = 4
CHUNK = 128
D_CONV = D_SSD + 2 * SSD_GROUPS * D_STATE
D_POOL = 1024
POOL_WINDOWS = (2, 4, 8, 16)
POOL_GROUP_DIM = 256
N_MEM = 256
XATTN_HEADS = 4
XATTN_HEAD_DIM = 512
N_EXPERTS = 32
TOP_K = 4
D_FF = 2048
SWIGLU_LIMIT = 7.0
SWIGLU_ALPHA = 1.702
EPS = 1e-5

LANES = 128
VMEM_LIMIT = 56 * 1024 * 1024
NEG_BIG = -1e30

MOE_BLOCK = 128
MOE_TN = 1024
GATHER_ROWS = 256
COMBINE_TOKENS = 128


def _cparams(sem):
    return pltpu.CompilerParams(dimension_semantics=sem, vmem_limit_bytes=VMEM_LIMIT)


def _rms(x, w):
    ms = jnp.mean(x * x, axis=-1, keepdims=True)
    return x * lax.rsqrt(ms + EPS) * w


def _mm_kernel(*refs, has_norm, has_res):
    it = iter(refs)
    a_ref = next(it)
    w_ref = next(it)
    nw_ref = next(it) if has_norm else None
    r_ref = next(it) if has_res else None
    o_ref = next(it)
    wb_ref = next(it)

    @pl.when(pl.program_id(1) == 0)
    def _():
        wb_ref[...] = w_ref[...].astype(BF16)

    a = a_ref[...]
    if has_norm:
        a = _rms(a.astype(F32), nw_ref[...])
    acc = jnp.dot(a.astype(BF16), wb_ref[...], preferred_element_type=F32)
    if has_res:
        acc = acc + r_ref[...]
    o_ref[...] = acc.astype(o_ref.dtype)


def _mm(a, w, norm_w=None, res=None, out_dtype=F32, tm=512, tn=1024):
    m, k = a.shape
    n = w.shape[1]
    tn = min(tn, n)
    tm = min(tm, m)
    assert m % tm == 0 and n % tn == 0
    in_specs = [pl.BlockSpec((tm, k), lambda j, i: (i, 0)),
                pl.BlockSpec((k, tn), lambda j, i: (0, j))]
    args = [a, w]
    if norm_w is not None:
        in_specs.append(pl.BlockSpec((1, k), lambda j, i: (0, 0)))
        args.append(norm_w.reshape(1, k))
    if res is not None:
        in_specs.append(pl.BlockSpec((tm, tn), lambda j, i: (i, j)))
        args.append(res)
    return pl.pallas_call(
        functools.partial(_mm_kernel, has_norm=norm_w is not None, has_res=res is not None),
        grid=(n // tn, m // tm),
        in_specs=in_specs,
        out_specs=pl.BlockSpec((tm, tn), lambda j, i: (i, j)),
        out_shape=jax.ShapeDtypeStruct((m, n), out_dtype),
        scratch_shapes=[pltpu.VMEM((k, tn), BF16)],
        compiler_params=_cparams(("arbitrary", "arbitrary")),
    )(*args)


def _sigmoid(x):
    return 1.0 / (1.0 + jnp.exp(-x))


def _shift_rows(cur, prev, j, rows):
    return jnp.where(rows < j, pltpu.roll(prev, j, 0), pltpu.roll(cur, j, 0))


def _conv_silu(cur, prev, w_ref, b_ref, col0, rows):
    width = cur.shape[1]
    acc = cur * w_ref[CONV_WIDTH - 1:CONV_WIDTH, col0:col0 + width] + b_ref[:, col0:col0 + width]
    for j in range(1, CONV_WIDTH):
        k = CONV_WIDTH - 1 - j
        acc = acc + _shift_rows(cur, prev, j, rows) * w_ref[k:k + 1, col0:col0 + width]
    return acc * _sigmoid(acc)


def _mixer_kernel(z_ref, x_ref, bc_ref, u_ref, dt_ref, convw_ref, convb_ref, dtb_ref, alog_ref,
                  dskip_ref, ssdn_ref, poolw_ref, pscale_ref, out_ref,
                  px_ref, pbc_ref, pu_ref, state_ref):
    c = pl.program_id(1)

    @pl.when(c == 0)
    def _():
        px_ref[...] = jnp.zeros_like(px_ref)
        pbc_ref[...] = jnp.zeros_like(pbc_ref)
        pu_ref[...] = jnp.zeros_like(pu_ref)
        state_ref[...] = jnp.zeros_like(state_ref)

    rows = lax.broadcasted_iota(jnp.int32, (CHUNK, D_SSD), 0)
    x_raw = x_ref[...]
    bc_raw = bc_ref[...]
    xs = _conv_silu(x_raw, px_ref[...], convw_ref, convb_ref, 0, rows)
    bcs = _conv_silu(bc_raw, pbc_ref[...], convw_ref, convb_ref, D_SSD, rows)
    px_ref[...] = x_raw
    pbc_ref[...] = bc_raw

    dt_in = dt_ref[...] + dtb_ref[...]
    dt = jnp.maximum(dt_in, 0.0) + jnp.log(1.0 + jnp.exp(-jnp.abs(dt_in)))
    a = -jnp.exp(alog_ref[...])
    dta = dt * a
    ri = lax.broadcasted_iota(jnp.int32, (CHUNK, CHUNK), 0)
    ci = lax.broadcasted_iota(jnp.int32, (CHUNK, CHUNK), 1)
    causal = ri >= ci
    tril = jnp.where(causal, 1.0, 0.0).astype(F32)
    acs = jnp.dot(tril, dta, preferred_element_type=F32, precision=lax.Precision.HIGHEST)
    acs_t = acs.T
    dt_t = dt.T
    lo = ci < SSD_HEAD_DIM

    y_pairs = []
    for g in range(SSD_GROUPS):
        bg = bcs[:, g * D_STATE:(g + 1) * D_STATE].astype(BF16)
        cg = bcs[:, SSD_GROUPS * D_STATE + g * D_STATE:SSD_GROUPS * D_STATE + (g + 1) * D_STATE].astype(BF16)
        cb = lax.dot_general(cg, bg, (((1,), (1,)), ((), ())), preferred_element_type=F32)
        gw = HEADS_PER_GROUP * SSD_HEAD_DIM
        prev_g = state_ref[g * gw:(g + 1) * gw, :]
        yoff = lax.dot_general(cg, prev_g.astype(BF16), (((1,), (1,)), ((), ())),
                               preferred_element_type=F32)
        xw_parts = []
        for pr in range(HEADS_PER_GROUP // 2):
            pair = g * (HEADS_PER_GROUP // 2) + pr
            xs_pair = xs[:, pair * LANES:(pair + 1) * LANES]
            y_pair = None
            ecols = []
            wcols = []
            for q in range(2):
                h = pair * 2 + q
                col = acs[:, h:h + 1]
                seg = col - acs_t[h:h + 1, :]
                decay = jnp.exp(jnp.where(causal, seg, NEG_BIG))
                mat = (cb * decay * dt_t[h:h + 1, :]).astype(BF16)
                keep = lo if q == 0 else jnp.logical_not(lo)
                xh = jnp.where(keep, xs_pair, 0.0).astype(BF16)
                yd = jnp.dot(mat, xh, preferred_element_type=F32)
                y_pair = yd if y_pair is None else y_pair + yd
                ecols.append(jnp.exp(col))
                wcols.append(dt[:, h:h + 1] * jnp.exp(acs[CHUNK - 1:CHUNK, h:h + 1] - col))
            e_pair = jnp.where(lo, ecols[0], ecols[1])
            w_pair = jnp.where(lo, wcols[0], wcols[1])
            y_pairs.append(y_pair + yoff[:, pr * LANES:(pr + 1) * LANES] * e_pair)
            xw_parts.append(xs_pair * w_pair)
        xw = jnp.concatenate(xw_parts, axis=1).astype(BF16)
        s_new = lax.dot_general(xw, bg, (((0,), (0,)), ((), ())), preferred_element_type=F32)
        for r in range(HEADS_PER_GROUP):
            h = g * HEADS_PER_GROUP + r
            dec = jnp.exp(acs[CHUNK - 1:CHUNK, h:h + 1])
            lo_r = h * SSD_HEAD_DIM
            state_ref[lo_r:lo_r + SSD_HEAD_DIM, :] = (
                state_ref[lo_r:lo_r + SSD_HEAD_DIM, :] * dec
                + s_new[r * SSD_HEAD_DIM:(r + 1) * SSD_HEAD_DIM, :])

    y = jnp.concatenate(y_pairs, axis=1)
    y = y + dskip_ref[...] * xs
    z = z_ref[...]
    y = y * (z * _sigmoid(z))
    gdim = D_SSD // SSD_GROUPS
    for g in range(SSD_GROUPS):
        yg = _rms(y[:, g * gdim:(g + 1) * gdim], ssdn_ref[:, g * gdim:(g + 1) * gdim])
        out_ref[:, g * gdim:(g + 1) * gdim] = yg.astype(out_ref.dtype)

    u_cur = u_ref[...]
    u_prev = pu_ref[...]
    pos = (c * CHUNK + lax.broadcasted_iota(jnp.int32, (CHUNK, POOL_GROUP_DIM), 0) + 1).astype(F32)
    for g, win in enumerate(POOL_WINDOWS):
        sl = slice(g * POOL_GROUP_DIM, (g + 1) * POOL_GROUP_DIM)
        ug = u_cur[:, sl]
        ext = jnp.concatenate([u_prev[:, sl], ug], axis=0)
        step = 1
        while step < win:
            ext = ext + pltpu.roll(ext, step, 0)
            step *= 2
        pooled = ext[CHUNK:, :] / jnp.minimum(pos, float(win)) - ug
        mixed = jnp.dot(pooled.astype(BF16), poolw_ref[g].astype(BF16), preferred_element_type=F32)
        out_ref[:, D_SSD + g * POOL_GROUP_DIM:D_SSD + (g + 1) * POOL_GROUP_DIM] = (
            mixed * pscale_ref[:, sl]).astype(out_ref.dtype)
    pu_ref[...] = u_cur


def _mixer(proj, dtp, conv_w, conv_b, dt_bias, a_log, d_skip, ssd_norm, pool_w, pool_scale, batch, seq):
    nchunk = seq // CHUNK
    row_map = lambda col: (lambda b, c: (b * nchunk + c, col))
    const2 = lambda b, c: (0, 0)
    pad = LANES - SSD_HEADS
    dtb = jnp.pad(dt_bias, (0, pad)).reshape(1, LANES)
    alog = jnp.pad(a_log, (0, pad)).reshape(1, LANES)
    dskip = jnp.repeat(d_skip, SSD_HEAD_DIM).reshape(1, D_SSD)
    return pl.pallas_call(
        _mixer_kernel,
        grid=(batch, nchunk),
        in_specs=[
            pl.BlockSpec((CHUNK, D_SSD), row_map(0)),
            pl.BlockSpec((CHUNK, D_SSD), row_map(1)),
            pl.BlockSpec((CHUNK, D_SSD), row_map(2)),
            pl.BlockSpec((CHUNK, D_POOL), row_map(3)),
            pl.BlockSpec((CHUNK, LANES), row_map(0)),
            pl.BlockSpec((CONV_WIDTH, D_CONV), const2),
            pl.BlockSpec((1, D_CONV), const2),
            pl.BlockSpec((1, LANES), const2),
            pl.BlockSpec((1, LANES), const2),
            pl.BlockSpec((1, D_SSD), const2),
            pl.BlockSpec((1, D_SSD), const2),
            pl.BlockSpec((len(POOL_WINDOWS), POOL_GROUP_DIM, POOL_GROUP_DIM), lambda b, c: (0, 0, 0)),
            pl.BlockSpec((1, D_POOL), const2),
        ],
        out_specs=pl.BlockSpec((CHUNK, D_MODEL), lambda b, c: (b * nchunk + c, 0)),
        out_shape=jax.ShapeDtypeStruct((batch * seq, D_MODEL), BF16),
        scratch_shapes=[pltpu.VMEM((CHUNK, D_SSD), F32), pltpu.VMEM((CHUNK, D_SSD), F32),
                        pltpu.VMEM((CHUNK, D_POOL), F32), pltpu.VMEM((SSD_HEADS * SSD_HEAD_DIM, D_STATE), F32)],
        compiler_params=_cparams(("arbitrary", "arbitrary")),
    )(proj, proj, proj, proj, dtp, conv_w, conv_b.reshape(1, D_CONV), dtb, alog, dskip,
      ssd_norm.reshape(1, D_SSD), pool_w, pool_scale.reshape(1, D_POOL))


def _attn_kernel(q_ref, k_ref, v_ref, o_ref):
    scale = XATTN_HEAD_DIM ** -0.5
    for h in range(XATTN_HEADS):
        sl = slice(h * XATTN_HEAD_DIM, (h + 1) * XATTN_HEAD_DIM)
        s = lax.dot_general(q_ref[:, sl], k_ref[:, sl], (((1,), (1,)), ((), ())),
                            preferred_element_type=F32) * scale
        s = s - jnp.max(s, axis=-1, keepdims=True)
        p = jnp.exp(s)
        p = p / jnp.sum(p, axis=-1, keepdims=True)
        o_ref[:, sl] = jnp.dot(p.astype(BF16), v_ref[:, sl], preferred_element_type=F32).astype(o_ref.dtype)


def _attn(q, kv, batch, seq, tq=512):
    nq = seq // tq
    return pl.pallas_call(
        _attn_kernel,
        grid=(batch, nq),
        in_specs=[pl.BlockSpec((tq, D_MODEL), lambda b, i: (b * nq + i, 0)),
                  pl.BlockSpec((N_MEM, D_MODEL), lambda b, i: (b, 0)),
                  pl.BlockSpec((N_MEM, D_MODEL), lambda b, i: (b, 1))],
        out_specs=pl.BlockSpec((tq, D_MODEL), lambda b, i: (b * nq + i, 0)),
        out_shape=jax.ShapeDtypeStruct((batch * seq, D_MODEL), BF16),
        compiler_params=_cparams(("arbitrary", "arbitrary")),
    )(q, kv, kv)


def _router_kernel(h_ref, nw_ref, wr_ref, br_ref, idx_ref, gate_ref):
    hn = _rms(h_ref[...], nw_ref[...])
    logits = jnp.dot(hn, wr_ref[...], preferred_element_type=F32,
                     precision=lax.Precision.HIGHEST) + br_ref[...]
    lane = lax.broadcasted_iota(jnp.int32, logits.shape, 1)
    vals = []
    idx_out = jnp.zeros(logits.shape, jnp.int32)
    for k in range(TOP_K):
        m = jnp.max(logits, axis=-1, keepdims=True)
        idx = jnp.min(jnp.where(logits == m, lane, LANES), axis=-1, keepdims=True)
        idx_out = jnp.where(lane == k, idx, idx_out)
        logits = jnp.where(lane == idx, -jnp.inf, logits)
        vals.append(m)
    exps = [jnp.exp(v - vals[0]) for v in vals]
    den = exps[0] + exps[1] + exps[2] + exps[3]
    gate_out = jnp.zeros(logits.shape, F32)
    for k in range(TOP_K):
        gate_out = jnp.where(lane == k, exps[k] / den, gate_out)
    idx_ref[...] = idx_out
    gate_ref[...] = gate_out


def _router(h, norm_w, w_router, b_router, tm=512):
    t = h.shape[0]
    pad = LANES - N_EXPERTS
    wr = jnp.pad(w_router, ((0, 0), (0, pad)))
    br = jnp.pad(b_router, (0, pad), constant_values=NEG_BIG).reshape(1, LANES)
    return pl.pallas_call(
        _router_kernel,
        grid=(t // tm,),
        in_specs=[pl.BlockSpec((tm, D_MODEL), lambda i: (i, 0)),
                  pl.BlockSpec((1, D_MODEL), lambda i: (0, 0)),
                  pl.BlockSpec((D_MODEL, LANES), lambda i: (0, 0)),
                  pl.BlockSpec((1, LANES), lambda i: (0, 0))],
        out_specs=[pl.BlockSpec((tm, LANES), lambda i: (i, 0)),
                   pl.BlockSpec((tm, LANES), lambda i: (i, 0))],
        out_shape=[jax.ShapeDtypeStruct((t, LANES), jnp.int32),
                   jax.ShapeDtypeStruct((t, LANES), F32)],
        compiler_params=_cparams(("arbitrary",)),
    )(h, norm_w.reshape(1, D_MODEL), wr, br)


def _gather_kernel(tok_ref, src_ref, dst_ref, sem):
    base = pl.program_id(0) * GATHER_ROWS

    def issue(r, carry):
        t = tok_ref[0, 0, r]
        pltpu.make_async_copy(src_ref.at[pl.ds(t, 1)], dst_ref.at[pl.ds(base + r, 1)], sem).start()
        return carry

    lax.fori_loop(0, GATHER_ROWS, issue, 0)
    pltpu.make_async_copy(dst_ref.at[pl.ds(0, GATHER_ROWS)], dst_ref.at[pl.ds(0, GATHER_ROWS)], sem).wait()


def _gather_rows(src, row_tok):
    n_rows = row_tok.shape[0]
    steps = n_rows // GATHER_ROWS
    return pl.pallas_call(
        _gather_kernel,
        grid=(steps,),
        in_specs=[pl.BlockSpec((1, 1, GATHER_ROWS), lambda i: (i, 0, 0), memory_space=pltpu.SMEM),
                  pl.BlockSpec(memory_space=pl.ANY)],
        out_specs=pl.BlockSpec(memory_space=pl.ANY),
        out_shape=jax.ShapeDtypeStruct((n_rows, src.shape[1]), src.dtype),
        scratch_shapes=[pltpu.SemaphoreType.DMA(())],
        compiler_params=_cparams(("arbitrary",)),
    )(row_tok.reshape(steps, 1, GATHER_ROWS), src)


def _expert_up_kernel(blk_ref, e_ref, j_ref, jw_ref, flag_ref, x_ref, nw_ref, wg_ref, wu_ref, bg_ref, bu_ref,
                      act_ref, wgb_ref, wub_ref):
    flag = flag_ref[pl.program_id(0)]

    @pl.when((flag & 1) != 0)
    def _():
        wgb_ref[...] = wg_ref[...].astype(BF16)
        wub_ref[...] = wu_ref[...].astype(BF16)

    @pl.when((flag & 2) != 0)
    def _():
        xb = _rms(x_ref[...], nw_ref[...]).astype(BF16)
        g = jnp.dot(xb, wgb_ref[...], preferred_element_type=F32) + bg_ref[...]
        u = jnp.dot(xb, wub_ref[...], preferred_element_type=F32) + bu_ref[...]
        g = jnp.minimum(g, SWIGLU_LIMIT)
        u = jnp.clip(u, -SWIGLU_LIMIT, SWIGLU_LIMIT)
        act = g * _sigmoid(SWIGLU_ALPHA * g) * (u + 1.0)
        act_ref[...] = act.astype(act_ref.dtype)

    @pl.when((flag & 2) == 0)
    def _():
        act_ref[...] = jnp.zeros_like(act_ref)


def _expert_down_kernel(blk_ref, e_ref, j_ref, jw_ref, flag_ref, a_ref, gate_ref, wd_ref, bd_ref, y_ref, wdb_ref):
    flag = flag_ref[pl.program_id(0)]

    @pl.when((flag & 1) != 0)
    def _():
        wdb_ref[...] = wd_ref[...].astype(BF16)

    @pl.when((flag & 2) != 0)
    def _():
        y = jnp.dot(a_ref[...], wdb_ref[...], preferred_element_type=F32) + bd_ref[...]
        y_ref[...] = y * gate_ref[...]

    @pl.when((flag & 2) == 0)
    def _():
        y_ref[...] = jnp.zeros_like(y_ref)


def _step_tables(block_e, n_used, n_blocks, n_j):
    e_ids = jnp.arange(N_EXPERTS, dtype=jnp.int32)
    blk_ids = jnp.arange(n_blocks, dtype=jnp.int32)
    used = blk_ids < n_used
    nb_e = jnp.sum((block_e[None, :] == e_ids[:, None]) & used[None, :], axis=1).astype(jnp.int32)
    bend = jnp.cumsum(nb_e)
    bstart = bend - nb_e
    n_steps = n_blocks * n_j
    s = jnp.arange(n_steps, dtype=jnp.int32)
    valid = s < n_used * n_j
    s_c = jnp.minimum(s, n_used * n_j - 1)
    e = jnp.clip(jnp.searchsorted(bend * n_j, s_c, side='right'), 0, N_EXPERTS - 1).astype(jnp.int32)
    q = s_c - bstart[e] * n_j
    nb = jnp.maximum(nb_e[e], 1)
    jw = q // nb
    i = q % nb
    tail = s - n_used * n_j
    blk = jnp.where(valid, bstart[e] + i, n_used + tail // n_j)
    j = jnp.where(valid, jw, tail % n_j)
    first = (i == 0) & valid
    flag = first.astype(jnp.int32) + 2 * valid.astype(jnp.int32)
    return blk.astype(jnp.int32), e, j.astype(jnp.int32), jw.astype(jnp.int32), flag


def _experts(x_rows, row_gate, tables, norm_w, w_gate, b_gate, w_up, b_up, w_down, b_down, n_blocks):
    n_rows = x_rows.shape[0]
    n_steps = n_blocks * (D_FF // MOE_TN)
    wmap = lambda s, blk, e, j, jw, f: (e[s], 0, jw[s])
    rowmap = lambda s, blk, e, j, jw, f: (blk[s], 0)
    outmap = lambda s, blk, e, j, jw, f: (blk[s], j[s])
    const = lambda s, blk, e, j, jw, f: (0, 0)
    act = pl.pallas_call(
        _expert_up_kernel,
        grid_spec=pltpu.PrefetchScalarGridSpec(
            num_scalar_prefetch=5, grid=(n_steps,),
            in_specs=[pl.BlockSpec((MOE_BLOCK, D_MODEL), rowmap),
                      pl.BlockSpec((1, D_MODEL), const),
                      pl.BlockSpec((None, D_MODEL, MOE_TN), wmap),
                      pl.BlockSpec((None, D_MODEL, MOE_TN), wmap),
                      pl.BlockSpec((None, 1, MOE_TN), wmap),
                      pl.BlockSpec((None, 1, MOE_TN), wmap)],
            out_specs=pl.BlockSpec((MOE_BLOCK, MOE_TN), outmap),
            scratch_shapes=[pltpu.VMEM((D_MODEL, MOE_TN), BF16), pltpu.VMEM((D_MODEL, MOE_TN), BF16)]),
        out_shape=jax.ShapeDtypeStruct((n_rows, D_FF), BF16),
        compiler_params=_cparams(("arbitrary",)),
    )(*tables, x_rows, norm_w.reshape(1, D_MODEL), w_gate, w_up,
      b_gate.reshape(N_EXPERTS, 1, D_FF), b_up.reshape(N_EXPERTS, 1, D_FF))
    return pl.pallas_call(
        _expert_down_kernel,
        grid_spec=pltpu.PrefetchScalarGridSpec(
            num_scalar_prefetch=5, grid=(n_steps,),
            in_specs=[pl.BlockSpec((MOE_BLOCK, D_FF), rowmap),
                      pl.BlockSpec((MOE_BLOCK, 1), rowmap),
                      pl.BlockSpec((None, D_FF, MOE_TN), wmap),
                      pl.BlockSpec((None, 1, MOE_TN), wmap)],
            out_specs=pl.BlockSpec((MOE_BLOCK, MOE_TN), outmap),
            scratch_shapes=[pltpu.VMEM((D_FF, MOE_TN), BF16)]),
        out_shape=jax.ShapeDtypeStruct((n_rows, D_MODEL), F32),
        compiler_params=_cparams(("arbitrary",)),
    )(*tables, act, row_gate.reshape(n_rows, 1), w_down, b_down.reshape(N_EXPERTS, 1, D_MODEL))


def _combine_kernel(dest_ref, h_ref, nw_ref, y_ref, o_ref, buf_ref, sem):
    def issue(r, carry):
        for k in range(TOP_K):
            d = dest_ref[0, 0, r * TOP_K + k]
            pltpu.make_async_copy(y_ref.at[pl.ds(d, 1)], buf_ref.at[k, pl.ds(r, 1)], sem).start()
        return carry

    lax.fori_loop(0, COMBINE_TOKENS, issue, 0)
    pltpu.make_async_copy(buf_ref, buf_ref, sem).wait()
    acc = h_ref[...]
    for k in range(TOP_K):
        acc = acc + buf_ref[k]
    o_ref[...] = _rms(acc, nw_ref[...])


def _combine(h, y_rows, dest, norm_w):
    t = h.shape[0]
    steps = t // COMBINE_TOKENS
    return pl.pallas_call(
        _combine_kernel,
        grid=(steps,),
        in_specs=[pl.BlockSpec((1, 1, COMBINE_TOKENS * TOP_K), lambda i: (i, 0, 0), memory_space=pltpu.SMEM),
                  pl.BlockSpec((COMBINE_TOKENS, D_MODEL), lambda i: (i, 0)),
                  pl.BlockSpec((1, D_MODEL), lambda i: (0, 0)),
                  pl.BlockSpec(memory_space=pl.ANY)],
        out_specs=pl.BlockSpec((COMBINE_TOKENS, D_MODEL), lambda i: (i, 0)),
        out_shape=jax.ShapeDtypeStruct((t, D_MODEL), F32),
        scratch_shapes=[pltpu.VMEM((TOP_K, COMBINE_TOKENS, D_MODEL), F32), pltpu.SemaphoreType.DMA(())],
        compiler_params=_cparams(("arbitrary",)),
    )(dest.reshape(steps, 1, COMBINE_TOKENS * TOP_K), h, norm_w.reshape(1, D_MODEL), y_rows)


def _moe(h, norm_ffn, w_router, b_router, w_gate, b_gate, w_up, b_up, w_down, b_down, norm_final):
    t = h.shape[0]
    tk = t * TOP_K
    n_blocks = (tk + N_EXPERTS * (MOE_BLOCK - 1) + MOE_BLOCK - 1) // MOE_BLOCK
    n_rows = n_blocks * MOE_BLOCK
    idx_pad, gate_pad = _router(h, norm_ffn, w_router, b_router)
    flat_e = idx_pad[:, :TOP_K].reshape(tk)
    flat_g = gate_pad[:, :TOP_K].reshape(tk)
    onehot = (flat_e[:, None] == jnp.arange(N_EXPERTS, dtype=jnp.int32)[None, :]).astype(jnp.int32)
    csum = jnp.cumsum(onehot, axis=0)
    counts = csum[-1]
    pos_in_e = jnp.sum((csum - onehot) * onehot, axis=1)
    padded = (counts + MOE_BLOCK - 1) // MOE_BLOCK * MOE_BLOCK
    padded_end = jnp.cumsum(padded)
    padded_start = padded_end - padded
    dest = (padded_start[flat_e] + pos_in_e).astype(jnp.int32)
    flat_tok = jnp.arange(tk, dtype=jnp.int32) // TOP_K
    row_tok = jnp.zeros((n_rows,), jnp.int32).at[dest].set(flat_tok)
    row_gate = jnp.zeros((n_rows,), F32).at[dest].set(flat_g)
    block_e = jnp.clip(jnp.searchsorted(padded_end, jnp.arange(n_blocks, dtype=jnp.int32) * MOE_BLOCK,
                                        side='right'), 0, N_EXPERTS - 1).astype(jnp.int32)
    n_used = (padded_end[-1] // MOE_BLOCK).astype(jnp.int32)
    tables = _step_tables(block_e, n_used, n_blocks, D_FF // MOE_TN)

    x_rows = _gather_rows(h, row_tok)
    y_rows = _experts(x_rows, row_gate, tables, norm_ffn, w_gate, b_gate, w_up, b_up, w_down, b_down, n_blocks)
    return _combine(h, y_rows, dest, norm_final)


def kernel(x, mem, norm_mix, w_in, conv_w, conv_b, dt_bias, a_log, d_skip, ssd_norm, pool_w, pool_scale,
           w_out, norm_xattn, norm_mem, w_q, w_kv, w_o, norm_ffn, w_router, b_router, w_gate, b_gate,
           w_up, b_up, w_down, b_down, norm_final):
    batch, seq, d = x.shape
    t = batch * seq
    h = x.reshape(t, d)
    depth = norm_mix.shape[0]
    assert depth == 1, "the final norm is fused into the MoE combine of the single layer"
    for l in range(depth):
        dt_lo = D_SSD + D_CONV
        w_main = jnp.concatenate([w_in[l][:, :dt_lo], w_in[l][:, dt_lo + SSD_HEADS:]], axis=1)
        w_dt = jnp.pad(w_in[l][:, dt_lo:dt_lo + SSD_HEADS], ((0, 0), (0, LANES - SSD_HEADS)))
        proj = _mm(h, w_main, norm_w=norm_mix[l])
        dtp = _mm(h, w_dt, norm_w=norm_mix[l])
        mix = _mixer(proj, dtp, conv_w[l], conv_b[l], dt_bias[l], a_log[l], d_skip[l], ssd_norm[l],
                     pool_w[l], pool_scale[l], batch, seq)
        h = _mm(mix, w_out[l], res=h)

        q = _mm(h, w_q[l], norm_w=norm_xattn[l], out_dtype=BF16)
        kv = _mm(mem.reshape(batch * N_MEM, d), w_kv[l], norm_w=norm_mem[l], out_dtype=BF16)
        o = _attn(q, kv, batch, seq)
        h = _mm(o, w_o[l], res=h)

        out = _moe(h, norm_ffn[l], w_router[l], b_router[l], w_gate[l], b_gate[l], w_up[l], b_up[l],
                   w_down[l], b_down[l], norm_final)
    return out.reshape(batch, seq, d)
```

```python
import functools

import jax
import jax.numpy as jnp
from jax import lax
from jax.experimental import pallas as pl
from jax.experimental.pallas import tpu as pltpu

F32 = jnp.float32
BF16 = jnp.bfloat16

D_MODEL = 2048
D_SSD = 1024
SSD_HEAD_DIM = 64
SSD_HEADS = 16
SSD_GROUPS = 4
HEADS_PER_GROUP = SSD_HEADS // SSD_GROUPS
D_STATE = 128
CONV_WIDTH = 4
CHUNK = 128
D_CONV = D_SSD + 2 * SSD_GROUPS * D_STATE
D_POOL = 1024
POOL_WINDOWS = (2, 4, 8, 16)
POOL_GROUP_DIM = 256
N_MEM = 256
XATTN_HEADS = 4
XATTN_HEAD_DIM = 512
N_EXPERTS = 32
TOP_K = 4
D_FF = 2048
SWIGLU_LIMIT = 7.0
SWIGLU_ALPHA = 1.702
EPS = 1e-5

LANES = 128
VMEM_LIMIT = 56 * 1024 * 1024
NEG_BIG = -1e30

MOE_BLOCK = 128
MOE_TN = 1024
GATHER_ROWS = 256
COMBINE_TOKENS = 128


def _cparams(sem):
    return pltpu.CompilerParams(dimension_semantics=sem, vmem_limit_bytes=VMEM_LIMIT)


def _rms(x, w):
    ms = jnp.mean(x * x, axis=-1, keepdims=True)
    return x * lax.rsqrt(ms + EPS) * w


def _mm_kernel(*refs, has_norm, has_res):
    it = iter(refs)
    a_ref = next(it)
    w_ref = next(it)
    nw_ref = next(it) if has_norm else None
    r_ref = next(it) if has_res else None
    o_ref = next(it)
    wb_ref = next(it)

    @pl.when(pl.program_id(1) == 0)
    def _():
        wb_ref[...] = w_ref[...].astype(BF16)

    a = a_ref[...]
    if has_norm:
        a = _rms(a.astype(F32), nw_ref[...])
    acc = jnp.dot(a.astype(BF16), wb_ref[...], preferred_element_type=F32)
    if has_res:
        acc = acc + r_ref[...]
    o_ref[...] = acc.astype(o_ref.dtype)


def _mm(a, w, norm_w=None, res=None, out_dtype=F32, tm=512, tn=1024):
    m, k = a.shape
    n = w.shape[1]
    tn = min(tn, n)
    tm = min(tm, m)
    assert m % tm == 0 and n % tn == 0
    in_specs = [pl.BlockSpec((tm, k), lambda j, i: (i, 0)),
                pl.BlockSpec((k, tn), lambda j, i: (0, j))]
    args = [a, w]
    if norm_w is not None:
        in_specs.append(pl.BlockSpec((1, k), lambda j, i: (0, 0)))
        args.append(norm_w.reshape(1, k))
    if res is not None:
        in_specs.append(pl.BlockSpec((tm, tn), lambda j, i: (i, j)))
        args.append(res)
    return pl.pallas_call(
        functools.partial(_mm_kernel, has_norm=norm_w is not None, has_res=res is not None),
        grid=(n // tn, m // tm),
        in_specs=in_specs,
        out_specs=pl.BlockSpec((tm, tn), lambda j, i: (i, j)),
        out_shape=jax.ShapeDtypeStruct((m, n), out_dtype),
        scratch_shapes=[pltpu.VMEM((k, tn), BF16)],
        compiler_params=_cparams(("arbitrary", "arbitrary")),
    )(*args)


def _sigmoid(x):
    return 1.0 / (1.0 + jnp.exp(-x))


def _shift_rows(cur, prev, j, rows):
    return jnp.where(rows < j, pltpu.roll(prev, j, 0), pltpu.roll(cur, j, 0))


def _conv_silu(cur, prev, w_ref, b_ref, col0, rows):
    width = cur.shape[1]
    acc = cur * w_ref[CONV_WIDTH - 1:CONV_WIDTH, col0:col0 + width] + b_ref[:, col0:col0 + width]
    for j in range(1, CONV_WIDTH):
        k = CONV_WIDTH - 1 - j
        acc = acc + _shift_rows(cur, prev, j, rows) * w_ref[k:k + 1, col0:col0 + width]
    return acc * _sigmoid(acc)


def _mixer_kernel(z_ref, x_ref, bc_ref, u_ref, dt_ref, convw_ref, convb_ref, dtb_ref, alog_ref,
                  dskip_ref, ssdn_ref, poolw_ref, pscale_ref, out_ref,
                  px_ref, pbc_ref, pu_ref, state_ref):
    c = pl.program_id(1)

    @pl.when(c == 0)
    def _():
        px_ref[...] = jnp.zeros_like(px_ref)
        pbc_ref[...] = jnp.zeros_like(pbc_ref)
        pu_ref[...] = jnp.zeros_like(pu_ref)
        state_ref[...] = jnp.zeros_like(state_ref)

    rows = lax.broadcasted_iota(jnp.int32, (CHUNK, D_SSD), 0)
    x_raw = x_ref[...]
    bc_raw = bc_ref[...]
    xs = _conv_silu(x_raw, px_ref[...], convw_ref, convb_ref, 0, rows)
    bcs = _conv_silu(bc_raw, pbc_ref[...], convw_ref, convb_ref, D_SSD, rows)
    px_ref[...] = x_raw
    pbc_ref[...] = bc_raw

    dt_in = dt_ref[...] + dtb_ref[...]
    dt = jnp.maximum(dt_in, 0.0) + jnp.log(1.0 + jnp.exp(-jnp.abs(dt_in)))
    a = -jnp.exp(alog_ref[...])
    dta = dt * a
    ri = lax.broadcasted_iota(jnp.int32, (CHUNK, CHUNK), 0)
    ci = lax.broadcasted_iota(jnp.int32, (CHUNK, CHUNK), 1)
    causal = ri >= ci
    tril = jnp.where(causal, 1.0, 0.0).astype(F32)
    acs = jnp.dot(tril, dta, preferred_element_type=F32, precision=lax.Precision.HIGHEST)
    acs_t = acs.T
    dt_t = dt.T
    lo = ci < SSD_HEAD_DIM

    y_pairs = []
    for g in range(SSD_GROUPS):
        bg = bcs[:, g * D_STATE:(g + 1) * D_STATE].astype(BF16)
        cg = bcs[:, SSD_GROUPS * D_STATE + g * D_STATE:SSD_GROUPS * D_STATE + (g + 1) * D_STATE].astype(BF16)
        cb = lax.dot_general(cg, bg, (((1,), (1,)), ((), ())), preferred_element_type=F32)
        gw = HEADS_PER_GROUP * SSD_HEAD_DIM
        prev_g = state_ref[g * gw:(g + 1) * gw, :]
        yoff = lax.dot_general(cg, prev_g.astype(BF16), (((1,), (1,)), ((), ())),
                               preferred_element_type=F32)
        xw_parts = []
        for pr in range(HEADS_PER_GROUP // 2):
            pair = g * (HEADS_PER_GROUP // 2) + pr
            xs_pair = xs[:, pair * LANES:(pair + 1) * LANES]
            y_pair = None
            ecols = []
            wcols = []
            for q in range(2):
                h = pair * 2 + q
                col = acs[:, h:h + 1]
                seg = col - acs_t[h:h + 1, :]
                decay = jnp.exp(jnp.where(causal, seg, NEG_BIG))
                mat = (cb * decay * dt_t[h:h + 1, :]).astype(BF16)
                keep = lo if q == 0 else jnp.logical_not(lo)
                xh = jnp.where(keep, xs_pair, 0.0).astype(BF16)
                yd = jnp.dot(mat, xh, preferred_element_type=F32)
                y_pair = yd if y_pair is None else y_pair + yd
                ecols.append(jnp.exp(col))
                wcols.append(dt[:, h:h + 1] * jnp.exp(acs[CHUNK - 1:CHUNK, h:h + 1] - col))
            e_pair = jnp.where(lo, ecols[0], ecols[1])
            w_pair = jnp.where(lo, wcols[0], wcols[1])
            y_pairs.append(y_pair + yoff[:, pr * LANES:(pr + 1) * LANES] * e_pair)
            xw_parts.append(xs_pair * w_pair)
        xw = jnp.concatenate(xw_parts, axis=1).astype(BF16)
        s_new = lax.dot_general(xw, bg, (((0,), (0,)), ((), ())), preferred_element_type=F32)
        for r in range(HEADS_PER_GROUP):
            h = g * HEADS_PER_GROUP + r
            dec = jnp.exp(acs[CHUNK - 1:CHUNK, h:h + 1])
            lo_r = h * SSD_HEAD_DIM
            state_ref[lo_r:lo_r + SSD_HEAD_DIM, :] = (
                state_ref[lo_r:lo_r + SSD_HEAD_DIM, :] * dec
                + s_new[r * SSD_HEAD_DIM:(r + 1) * SSD_HEAD_DIM, :])

    y = jnp.concatenate(y_pairs, axis=1)
    y = y + dskip_ref[...] * xs
    z = z_ref[...]
    y = y * (z * _sigmoid(z))
    gdim = D_SSD // SSD_GROUPS
    for g in range(SSD_GROUPS):
        yg = _rms(y[:, g * gdim:(g + 1) * gdim], ssdn_ref[:, g * gdim:(g + 1) * gdim])
        out_ref[:, g * gdim:(g + 1) * gdim] = yg.astype(out_ref.dtype)

    u_cur = u_ref[...]
    u_prev = pu_ref[...]
    pos = (c * CHUNK + lax.broadcasted_iota(jnp.int32, (CHUNK, POOL_GROUP_DIM), 0) + 1).astype(F32)
    for g, win in enumerate(POOL_WINDOWS):
        sl = slice(g * POOL_GROUP_DIM, (g + 1) * POOL_GROUP_DIM)
        ug = u_cur[:, sl]
        ext = jnp.concatenate([u_prev[:, sl], ug], axis=0)
        step = 1
        while step < win:
            ext = ext + pltpu.roll(ext, step, 0)
            step *= 2
        pooled = ext[CHUNK:, :] / jnp.minimum(pos, float(win)) - ug
        mixed = jnp.dot(pooled.astype(BF16), poolw_ref[g].astype(BF16), preferred_element_type=F32)
        out_ref[:, D_SSD + g * POOL_GROUP_DIM:D_SSD + (g + 1) * POOL_GROUP_DIM] = (
            mixed * pscale_ref[:, sl]).astype(out_ref.dtype)
    pu_ref[...] = u_cur


def _mixer(proj, dtp, conv_w, conv_b, dt_bias, a_log, d_skip, ssd_norm, pool_w, pool_scale, batch, seq):
    nchunk = seq // CHUNK
    row_map = lambda col: (lambda b, c: (b * nchunk + c, col))
    const2 = lambda b, c: (0, 0)
    pad = LANES - SSD_HEADS
    dtb = jnp.pad(dt_bias, (0, pad)).reshape(1, LANES)
    alog = jnp.pad(a_log, (0, pad)).reshape(1, LANES)
    dskip = jnp.repeat(d_skip, SSD_HEAD_DIM).reshape(1, D_SSD)
    return pl.pallas_call(
        _mixer_kernel,
        grid=(batch, nchunk),
        in_specs=[
            pl.BlockSpec((CHUNK, D_SSD), row_map(0)),
            pl.BlockSpec((CHUNK, D_SSD), row_map(1)),
            pl.BlockSpec((CHUNK, D_SSD), row_map(2)),
            pl.BlockSpec((CHUNK, D_POOL), row_map(3)),
            pl.BlockSpec((CHUNK, LANES), row_map(0)),
            pl.BlockSpec((CONV_WIDTH, D_CONV), const2),
            pl.BlockSpec((1, D_CONV), const2),
            pl.BlockSpec((1, LANES), const2),
            pl.BlockSpec((1, LANES), const2),
            pl.BlockSpec((1, D_SSD), const2),
            pl.BlockSpec((1, D_SSD), const2),
            pl.BlockSpec((len(POOL_WINDOWS), POOL_GROUP_DIM, POOL_GROUP_DIM), lambda b, c: (0, 0, 0)),
            pl.BlockSpec((1, D_POOL), const2),
        ],
        out_specs=pl.BlockSpec((CHUNK, D_MODEL), lambda b, c: (b * nchunk + c, 0)),
        out_shape=jax.ShapeDtypeStruct((batch * seq, D_MODEL), BF16),
        scratch_shapes=[pltpu.VMEM((CHUNK, D_SSD), F32), pltpu.VMEM((CHUNK, D_SSD), F32),
                        pltpu.VMEM((CHUNK, D_POOL), F32), pltpu.VMEM((SSD_HEADS * SSD_HEAD_DIM, D_STATE), F32)],
        compiler_params=_cparams(("arbitrary", "arbitrary")),
    )(proj, proj, proj, proj, dtp, conv_w, conv_b.reshape(1, D_CONV), dtb, alog, dskip,
      ssd_norm.reshape(1, D_SSD), pool_w, pool_scale.reshape(1, D_POOL))


def _attn_kernel(q_ref, k_ref, v_ref, o_ref):
    scale = XATTN_HEAD_DIM ** -0.5
    for h in range(XATTN_HEADS):
        sl = slice(h * XATTN_HEAD_DIM, (h + 1) * XATTN_HEAD_DIM)
        s = lax.dot_general(q_ref[:, sl], k_ref[:, sl], (((1,), (1,)), ((), ())),
                            preferred_element_type=F32) * scale
        s = s - jnp.max(s, axis=-1, keepdims=True)
        p = jnp.exp(s)
        p = p / jnp.sum(p, axis=-1, keepdims=True)
        o_ref[:, sl] = jnp.dot(p.astype(BF16), v_ref[:, sl], preferred_element_type=F32).astype(o_ref.dtype)


def _attn(q, kv, batch, seq, tq=512):
    nq = seq // tq
    return pl.pallas_call(
        _attn_kernel,
        grid=(batch, nq),
        in_specs=[pl.BlockSpec((tq, D_MODEL), lambda b, i: (b * nq + i, 0)),
                  pl.BlockSpec((N_MEM, D_MODEL), lambda b, i: (b, 0)),
                  pl.BlockSpec((N_MEM, D_MODEL), lambda b, i: (b, 1))],
        out_specs=pl.BlockSpec((tq, D_MODEL), lambda b, i: (b * nq + i, 0)),
        out_shape=jax.ShapeDtypeStruct((batch * seq, D_MODEL), BF16),
        compiler_params=_cparams(("arbitrary", "arbitrary")),
    )(q, kv, kv)


def _router_kernel(h_ref, nw_ref, wr_ref, br_ref, idx_ref, gate_ref):
    hn = _rms(h_ref[...], nw_ref[...])
    logits = jnp.dot(hn, wr_ref[...], preferred_element_type=F32,
                     precision=lax.Precision.HIGHEST) + br_ref[...]
    lane = lax.broadcasted_iota(jnp.int32, logits.shape, 1)
    vals = []
    idx_out = jnp.zeros(logits.shape, jnp.int32)
    for k in range(TOP_K):
        m = jnp.max(logits, axis=-1, keepdims=True)
        idx = jnp.min(jnp.where(logits == m, lane, LANES), axis=-1, keepdims=True)
        idx_out = jnp.where(lane == k, idx, idx_out)
        logits = jnp.where(lane == idx, -jnp.inf, logits)
        vals.append(m)
    exps = [jnp.exp(v - vals[0]) for v in vals]
    den = exps[0] + exps[1] + exps[2] + exps[3]
    gate_out = jnp.zeros(logits.shape, F32)
    for k in range(TOP_K):
        gate_out = jnp.where(lane == k, exps[k] / den, gate_out)
    idx_ref[...] = idx_out
    gate_ref[...] = gate_out


def _router(h, norm_w, w_router, b_router, tm=512):
    t = h.shape[0]
    pad = LANES - N_EXPERTS
    wr = jnp.pad(w_router, ((0, 0), (0, pad)))
    br = jnp.pad(b_router, (0, pad), constant_values=NEG_BIG).reshape(1, LANES)
    return pl.pallas_call(
        _router_kernel,
        grid=(t // tm,),
        in_specs=[pl.BlockSpec((tm, D_MODEL), lambda i: (i, 0)),
                  pl.BlockSpec((1, D_MODEL), lambda i: (0, 0)),
                  pl.BlockSpec((D_MODEL, LANES), lambda i: (0, 0)),
                  pl.BlockSpec((1, LANES), lambda i: (0, 0))],
        out_specs=[pl.BlockSpec((tm, LANES), lambda i: (i, 0)),
                   pl.BlockSpec((tm, LANES), lambda i: (i, 0))],
        out_shape=[jax.ShapeDtypeStruct((t, LANES), jnp.int32),
                   jax.ShapeDtypeStruct((t, LANES), F32)],
        compiler_params=_cparams(("arbitrary",)),
    )(h, norm_w.reshape(1, D_MODEL), wr, br)


def _gather_kernel(tok_ref, src_ref, dst_ref, sem):
    def issue(r, carry):
        t = tok_ref[0, 0, r]
        pltpu.make_async_copy(src_ref.at[pl.ds(t, 1)], dst_ref.at[pl.ds(r, 1)], sem).start()
        return carry

    lax.fori_loop(0, GATHER_ROWS, issue, 0)
    pltpu.make_async_copy(dst_ref, dst_ref, sem).wait()


def _gather_rows(src, row_tok):
    n_rows = row_tok.shape[0]
    steps = n_rows // GATHER_ROWS
    return pl.pallas_call(
        _gather_kernel,
        grid=(steps,),
        in_specs=[pl.BlockSpec((1, 1, GATHER_ROWS), lambda i: (i, 0, 0), memory_space=pltpu.SMEM),
                  pl.BlockSpec(memory_space=pl.ANY)],
        out_specs=pl.BlockSpec((GATHER_ROWS, src.shape[1]), lambda i: (i, 0)),
        out_shape=jax.ShapeDtypeStruct((n_rows, src.shape[1]), src.dtype),
        scratch_shapes=[pltpu.SemaphoreType.DMA(())],
        compiler_params=_cparams(("arbitrary",)),
    )(row_tok.reshape(steps, 1, GATHER_ROWS), src)


def _expert_up_kernel(blk_ref, e_ref, j_ref, jw_ref, flag_ref, x_ref, nw_ref, wg_ref, wu_ref, bg_ref, bu_ref,
                      act_ref, wgb_ref, wub_ref):
    flag = flag_ref[pl.program_id(0)]

    @pl.when((flag & 1) != 0)
    def _():
        wgb_ref[...] = wg_ref[...].astype(BF16)
        wub_ref[...] = wu_ref[...].astype(BF16)

    @pl.when((flag & 2) != 0)
    def _():
        xb = _rms(x_ref[...], nw_ref[...]).astype(BF16)
        g = jnp.dot(xb, wgb_ref[...], preferred_element_type=F32) + bg_ref[...]
        u = jnp.dot(xb, wub_ref[...], preferred_element_type=F32) + bu_ref[...]
        g = jnp.minimum(g, SWIGLU_LIMIT)
        u = jnp.clip(u, -SWIGLU_LIMIT, SWIGLU_LIMIT)
        act = g * _sigmoid(SWIGLU_ALPHA * g) * (u + 1.0)
        act_ref[...] = act.astype(act_ref.dtype)

    @pl.when((flag & 2) == 0)
    def _():
        act_ref[...] = jnp.zeros_like(act_ref)


def _expert_down_kernel(blk_ref, e_ref, j_ref, jw_ref, flag_ref, a_ref, gate_ref, wd_ref, bd_ref, y_ref, wdb_ref):
    flag = flag_ref[pl.program_id(0)]

    @pl.when((flag & 1) != 0)
    def _():
        wdb_ref[...] = wd_ref[...].astype(BF16)

    @pl.when((flag & 2) != 0)
    def _():
        y = jnp.dot(a_ref[...], wdb_ref[...], preferred_element_type=F32) + bd_ref[...]
        y_ref[...] = y * gate_ref[...]

    @pl.when((flag & 2) == 0)
    def _():
        y_ref[...] = jnp.zeros_like(y_ref)


def _step_tables(block_e, n_used, n_blocks, n_j):
    e_ids = jnp.arange(N_EXPERTS, dtype=jnp.int32)
    blk_ids = jnp.arange(n_blocks, dtype=jnp.int32)
    used = blk_ids < n_used
    nb_e = jnp.sum((block_e[None, :] == e_ids[:, None]) & used[None, :], axis=1).astype(jnp.int32)
    bend = jnp.cumsum(nb_e)
    bstart = bend - nb_e
    n_steps = n_blocks * n_j
    s = jnp.arange(n_steps, dtype=jnp.int32)
    valid = s < n_used * n_j
    s_c = jnp.minimum(s, n_used * n_j - 1)
    e = jnp.clip(jnp.searchsorted(bend * n_j, s_c, side='right'), 0, N_EXPERTS - 1).astype(jnp.int32)
    q = s_c - bstart[e] * n_j
    nb = jnp.maximum(nb_e[e], 1)
    jw = q // nb
    i = q % nb
    tail = s - n_used * n_j
    blk = jnp.where(valid, bstart[e] + i, n_used + tail // n_j)
    j = jnp.where(valid, jw, tail % n_j)
    first = (i == 0) & valid
    flag = first.astype(jnp.int32) + 2 * valid.astype(jnp.int32)
    return blk.astype(jnp.int32), e, j.astype(jnp.int32), jw.astype(jnp.int32), flag


def _experts(x_rows, row_gate, tables, norm_w, w_gate, b_gate, w_up, b_up, w_down, b_down, n_blocks):
    n_rows = x_rows.shape[0]
    n_steps = n_blocks * (D_FF // MOE_TN)
    wmap = lambda s, blk, e, j, jw, f: (e[s], 0, jw[s])
    rowmap = lambda s, blk, e, j, jw, f: (blk[s], 0)
    outmap = lambda s, blk, e, j, jw, f: (blk[s], j[s])
    const = lambda s, blk, e, j, jw, f: (0, 0)
    act = pl.pallas_call(
        _expert_up_kernel,
        grid_spec=pltpu.PrefetchScalarGridSpec(
            num_scalar_prefetch=5, grid=(n_steps,),
            in_specs=[pl.BlockSpec((MOE_BLOCK, D_MODEL), rowmap),
                      pl.BlockSpec((1, D_MODEL), const),
                      pl.BlockSpec((None, D_MODEL, MOE_TN), wmap),
                      pl.BlockSpec((None, D_MODEL, MOE_TN), wmap),
                      pl.BlockSpec((None, 1, MOE_TN), wmap),
                      pl.BlockSpec((None, 1, MOE_TN), wmap)],
            out_specs=pl.BlockSpec((MOE_BLOCK, MOE_TN), outmap),
            scratch_shapes=[pltpu.VMEM((D_MODEL, MOE_TN), BF16), pltpu.VMEM((D_MODEL, MOE_TN), BF16)]),
        out_shape=jax.ShapeDtypeStruct((n_rows, D_FF), BF16),
        compiler_params=_cparams(("arbitrary",)),
    )(*tables, x_rows, norm_w.reshape(1, D_MODEL), w_gate, w_up,
      b_gate.reshape(N_EXPERTS, 1, D_FF), b_up.reshape(N_EXPERTS, 1, D_FF))
    return pl.pallas_call(
        _expert_down_kernel,
        grid_spec=pltpu.PrefetchScalarGridSpec(
            num_scalar_prefetch=5, grid=(n_steps,),
            in_specs=[pl.BlockSpec((MOE_BLOCK, D_FF), rowmap),
                      pl.BlockSpec((MOE_BLOCK, 1), rowmap),
                      pl.BlockSpec((None, D_FF, MOE_TN), wmap),
                      pl.BlockSpec((None, 1, MOE_TN), wmap)],
            out_specs=pl.BlockSpec((MOE_BLOCK, MOE_TN), outmap),
            scratch_shapes=[pltpu.VMEM((D_FF, MOE_TN), BF16)]),
        out_shape=jax.ShapeDtypeStruct((n_rows, D_MODEL), F32),
        compiler_params=_cparams(("arbitrary",)),
    )(*tables, act, row_gate.reshape(n_rows, 1), w_down, b_down.reshape(N_EXPERTS, 1, D_MODEL))


def _combine_kernel(dest_ref, h_ref, nw_ref, y_ref, o_ref, buf_ref, sem):
    def issue(r, carry):
        for k in range(TOP_K):
            d = dest_ref[0, 0, r * TOP_K + k]
            pltpu.make_async_copy(y_ref.at[pl.ds(d, 1)], buf_ref.at[k, pl.ds(r, 1)], sem).start()
        return carry

    lax.fori_loop(0, COMBINE_TOKENS, issue, 0)
    pltpu.make_async_copy(buf_ref, buf_ref, sem).wait()
    acc = h_ref[...]
    for k in range(TOP_K):
        acc = acc + buf_ref[k]
    o_ref[...] = _rms(acc, nw_ref[...])


def _combine(h, y_rows, dest, norm_w):
    t = h.shape[0]
    steps = t // COMBINE_TOKENS
    return pl.pallas_call(
        _combine_kernel,
        grid=(steps,),
        in_specs=[pl.BlockSpec((1, 1, COMBINE_TOKENS * TOP_K), lambda i: (i, 0, 0), memory_space=pltpu.SMEM),
                  pl.BlockSpec((COMBINE_TOKENS, D_MODEL), lambda i: (i, 0)),
                  pl.BlockSpec((1, D_MODEL), lambda i: (0, 0)),
                  pl.BlockSpec(memory_space=pl.ANY)],
        out_specs=pl.BlockSpec((COMBINE_TOKENS, D_MODEL), lambda i: (i, 0)),
        out_shape=jax.ShapeDtypeStruct((t, D_MODEL), F32),
        scratch_shapes=[pltpu.VMEM((TOP_K, COMBINE_TOKENS, D_MODEL), F32), pltpu.SemaphoreType.DMA(())],
        compiler_params=_cparams(("arbitrary",)),
    )(dest.reshape(steps, 1, COMBINE_TOKENS * TOP_K), h, norm_w.reshape(1, D_MODEL), y_rows)


def _moe(h, norm_ffn, w_router, b_router, w_gate, b_gate, w_up, b_up, w_down, b_down, norm_final):
    t = h.shape[0]
    tk = t * TOP_K
    n_blocks = (tk + N_EXPERTS * (MOE_BLOCK - 1) + MOE_BLOCK - 1) // MOE_BLOCK
    n_rows = n_blocks * MOE_BLOCK
    idx_pad, gate_pad = _router(h, norm_ffn, w_router, b_router)
    flat_e = idx_pad[:, :TOP_K].reshape(tk)
    flat_g = gate_pad[:, :TOP_K].reshape(tk)
    onehot = (flat_e[:, None] == jnp.arange(N_EXPERTS, dtype=jnp.int32)[None, :]).astype(jnp.int32)
    csum = jnp.cumsum(onehot, axis=0)
    counts = csum[-1]
    pos_in_e = jnp.sum((csum - onehot) * onehot, axis=1)
    padded = (counts + MOE_BLOCK - 1) // MOE_BLOCK * MOE_BLOCK
    padded_end = jnp.cumsum(padded)
    padded_start = padded_end - padded
    dest = (padded_start[flat_e] + pos_in_e).astype(jnp.int32)
    flat_tok = jnp.arange(tk, dtype=jnp.int32) // TOP_K
    row_tok = jnp.zeros((n_rows,), jnp.int32).at[dest].set(flat_tok)
    row_gate = jnp.zeros((n_rows,), F32).at[dest].set(flat_g)
    block_e = jnp.clip(jnp.searchsorted(padded_end, jnp.arange(n_blocks, dtype=jnp.int32) * MOE_BLOCK,
                                        side='right'), 0, N_EXPERTS - 1).astype(jnp.int32)
    n_used = (padded_end[-1] // MOE_BLOCK).astype(jnp.int32)
    tables = _step_tables(block_e, n_used, n_blocks, D_FF // MOE_TN)

    x_rows = _gather_rows(h, row_tok)
    y_rows = _experts(x_rows, row_gate, tables, norm_ffn, w_gate, b_gate, w_up, b_up, w_down, b_down, n_blocks)
    return _combine(h, y_rows, dest, norm_final)


def kernel(x, mem, norm_mix, w_in, conv_w, conv_b, dt_bias, a_log, d_skip, ssd_norm, pool_w, pool_scale,
           w_out, norm_xattn, norm_mem, w_q, w_kv, w_o, norm_ffn, w_router, b_router, w_gate, b_gate,
           w_up, b_up, w_down, b_down, norm_final):
    batch, seq, d = x.shape
    t = batch * seq
    h = x.reshape(t, d)
    depth = norm_mix.shape[0]
    assert depth == 1, "the final norm is fused into the MoE combine of the single layer"
    for l in range(depth):
        dt_lo = D_SSD + D_CONV
        w_main = jnp.concatenate([w_in[l][:, :dt_lo], w_in[l][:, dt_lo + SSD_HEADS:]], axis=1)
        w_dt = jnp.pad(w_in[l][:, dt_lo:dt_lo + SSD_HEADS], ((0, 0), (0, LANES - SSD_HEADS)))
        proj = _mm(h, w_main, norm_w=norm_mix[l])
        dtp = _mm(h, w_dt, norm_w=norm_mix[l])
        mix = _mixer(proj, dtp, conv_w[l], conv_b[l], dt_bias[l], a_log[l], d_skip[l], ssd_norm[l],
                     pool_w[l], pool_scale[l], batch, seq)
        h = _mm(mix, w_out[l], res=h)

        q = _mm(h, w_q[l], norm_w=norm_xattn[l], out_dtype=BF16)
        kv = _mm(mem.reshape(batch * N_MEM, d), w_kv[l], norm_w=norm_mem[l], out_dtype=BF16)
        o = _attn(q, kv, batch, seq)
        h = _mm(o, w_o[l], res=h)

        out = _moe(h, norm_ffn[l], w_router[l], b_router[l], w_gate[l], b_gate[l], w_up[l], b_up[l],
                   w_down[l], b_down[l], norm_final)
    return out.reshape(batch, seq, d)
```

```python
import functools

import jax
import jax.numpy as jnp
from jax import lax
from jax.experimental import pallas as pl
from jax.experimental.pallas import tpu as pltpu

F32 = jnp.float32
BF16 = jnp.bfloat16

D_MODEL = 2048
D_SSD = 1024
SSD_HEAD_DIM = 64
SSD_HEADS = 16
SSD_GROUPS = 4
HEADS_PER_GROUP = SSD_HEADS // SSD_GROUPS
D_STATE = 128
CONV_WIDTH = 4
CHUNK = 128
D_CONV = D_SSD + 2 * SSD_GROUPS * D_STATE
D_POOL = 1024
POOL_WINDOWS = (2, 4, 8, 16)
POOL_GROUP_DIM = 256
N_MEM = 256
XATTN_HEADS = 4
XATTN_HEAD_DIM = 512
N_EXPERTS = 32
TOP_K = 4
D_FF = 2048
SWIGLU_LIMIT = 7.0
SWIGLU_ALPHA = 1.702
EPS = 1e-5

LANES = 128
VMEM_LIMIT = 56 * 1024 * 1024
NEG_BIG = -1e30

MOE_BLOCK = 256
MOE_UP_TN = 1024
MOE_DOWN_TN = 2048
MOE_IN_BUFS = 3
MOE_OUT_BUFS = 2
ROUTER_TOKENS = 512
SCATTER_TOKENS = 256
COMBINE_TOKENS = 128
ROW_DMA_PRIORITY = 1


def _cparams(sem):
    return pltpu.CompilerParams(dimension_semantics=sem, vmem_limit_bytes=VMEM_LIMIT)


def _rms(x, w):
    ms = jnp.mean(x * x, axis=-1, keepdims=True)
    return x * lax.rsqrt(ms + EPS) * w


def _mm_kernel(*refs, has_norm, has_res):
    it = iter(refs)
    a_ref = next(it)
    w_ref = next(it)
    nw_ref = next(it) if has_norm else None
    r_ref = next(it) if has_res else None
    o_ref = next(it)
    wb_ref = next(it)

    @pl.when(pl.program_id(1) == 0)
    def _():
        wb_ref[...] = w_ref[...].astype(BF16)

    a = a_ref[...]
    if has_norm:
        a = _rms(a.astype(F32), nw_ref[...])
    acc = jnp.dot(a.astype(BF16), wb_ref[...], preferred_element_type=F32)
    if has_res:
        acc = acc + r_ref[...]
    o_ref[...] = acc.astype(o_ref.dtype)


def _mm(a, w, norm_w=None, res=None, out_dtype=F32, tm=512, tn=1024):
    m, k = a.shape
    n = w.shape[1]
    tn = min(tn, n)
    tm = min(tm, m)
    assert m % tm == 0 and n % tn == 0
    in_specs = [pl.BlockSpec((tm, k), lambda j, i: (i, 0)),
                pl.BlockSpec((k, tn), lambda j, i: (0, j))]
    args = [a, w]
    if norm_w is not None:
        in_specs.append(pl.BlockSpec((1, k), lambda j, i: (0, 0)))
        args.append(norm_w.reshape(1, k))
    if res is not None:
        in_specs.append(pl.BlockSpec((tm, tn), lambda j, i: (i, j)))
        args.append(res)
    return pl.pallas_call(
        functools.partial(_mm_kernel, has_norm=norm_w is not None, has_res=res is not None),
        grid=(n // tn, m // tm),
        in_specs=in_specs,
        out_specs=pl.BlockSpec((tm, tn), lambda j, i: (i, j)),
        out_shape=jax.ShapeDtypeStruct((m, n), out_dtype),
        scratch_shapes=[pltpu.VMEM((k, tn), BF16)],
        compiler_params=_cparams(("arbitrary", "arbitrary")),
    )(*args)


def _sigmoid(x):
    return 1.0 / (1.0 + jnp.exp(-x))


def _shift_rows(cur, prev, j, rows):
    return jnp.where(rows < j, pltpu.roll(prev, j, 0), pltpu.roll(cur, j, 0))


def _conv_silu(cur, prev, w_ref, b_ref, col0, rows):
    width = cur.shape[1]
    acc = cur * w_ref[CONV_WIDTH - 1:CONV_WIDTH, col0:col0 + width] + b_ref[:, col0:col0 + width]
    for j in range(1, CONV_WIDTH):
        k = CONV_WIDTH - 1 - j
        acc = acc + _shift_rows(cur, prev, j, rows) * w_ref[k:k + 1, col0:col0 + width]
    return acc * _sigmoid(acc)


def _mixer_kernel(z_ref, x_ref, bc_ref, u_ref, dt_ref, convw_ref, convb_ref, dtb_ref, alog_ref,
                  dskip_ref, ssdn_ref, poolw_ref, pscale_ref, out_ref,
                  px_ref, pbc_ref, pu_ref, state_ref):
    c = pl.program_id(1)

    @pl.when(c == 0)
    def _():
        px_ref[...] = jnp.zeros_like(px_ref)
        pbc_ref[...] = jnp.zeros_like(pbc_ref)
        pu_ref[...] = jnp.zeros_like(pu_ref)
        state_ref[...] = jnp.zeros_like(state_ref)

    rows = lax.broadcasted_iota(jnp.int32, (CHUNK, D_SSD), 0)
    x_raw = x_ref[...]
    bc_raw = bc_ref[...]
    xs = _conv_silu(x_raw, px_ref[...], convw_ref, convb_ref, 0, rows)
    bcs = _conv_silu(bc_raw, pbc_ref[...], convw_ref, convb_ref, D_SSD, rows)
    px_ref[...] = x_raw
    pbc_ref[...] = bc_raw

    dt_in = dt_ref[...] + dtb_ref[...]
    dt = jnp.maximum(dt_in, 0.0) + jnp.log(1.0 + jnp.exp(-jnp.abs(dt_in)))
    a = -jnp.exp(alog_ref[...])
    dta = dt * a
    ri = lax.broadcasted_iota(jnp.int32, (CHUNK, CHUNK), 0)
    ci = lax.broadcasted_iota(jnp.int32, (CHUNK, CHUNK), 1)
    causal = ri >= ci
    tril = jnp.where(causal, 1.0, 0.0).astype(F32)
    acs = jnp.dot(tril, dta, preferred_element_type=F32, precision=lax.Precision.HIGHEST)
    acs_t = acs.T
    dt_t = dt.T
    lo = ci < SSD_HEAD_DIM

    y_pairs = []
    for g in range(SSD_GROUPS):
        bg = bcs[:, g * D_STATE:(g + 1) * D_STATE].astype(BF16)
        cg = bcs[:, SSD_GROUPS * D_STATE + g * D_STATE:SSD_GROUPS * D_STATE + (g + 1) * D_STATE].astype(BF16)
        cb = lax.dot_general(cg, bg, (((1,), (1,)), ((), ())), preferred_element_type=F32)
        gw = HEADS_PER_GROUP * SSD_HEAD_DIM
        prev_g = state_ref[g * gw:(g + 1) * gw, :]
        yoff = lax.dot_general(cg, prev_g.astype(BF16), (((1,), (1,)), ((), ())),
                               preferred_element_type=F32)
        xw_parts = []
        for pr in range(HEADS_PER_GROUP // 2):
            pair = g * (HEADS_PER_GROUP // 2) + pr
            xs_pair = xs[:, pair * LANES:(pair + 1) * LANES]
            y_pair = None
            ecols = []
            wcols = []
            for q in range(2):
                h = pair * 2 + q
                col = acs[:, h:h + 1]
                seg = col - acs_t[h:h + 1, :]
                decay = jnp.exp(jnp.where(causal, seg, NEG_BIG))
                mat = (cb * decay * dt_t[h:h + 1, :]).astype(BF16)
                keep = lo if q == 0 else jnp.logical_not(lo)
                xh = jnp.where(keep, xs_pair, 0.0).astype(BF16)
                yd = jnp.dot(mat, xh, preferred_element_type=F32)
                y_pair = yd if y_pair is None else y_pair + yd
                ecols.append(jnp.exp(col))
                wcols.append(dt[:, h:h + 1] * jnp.exp(acs[CHUNK - 1:CHUNK, h:h + 1] - col))
            e_pair = jnp.where(lo, ecols[0], ecols[1])
            w_pair = jnp.where(lo, wcols[0], wcols[1])
            y_pairs.append(y_pair + yoff[:, pr * LANES:(pr + 1) * LANES] * e_pair)
            xw_parts.append(xs_pair * w_pair)
        xw = jnp.concatenate(xw_parts, axis=1).astype(BF16)
        s_new = lax.dot_general(xw, bg, (((0,), (0,)), ((), ())), preferred_element_type=F32)
        for r in range(HEADS_PER_GROUP):
            h = g * HEADS_PER_GROUP + r
            dec = jnp.exp(acs[CHUNK - 1:CHUNK, h:h + 1])
            lo_r = h * SSD_HEAD_DIM
            state_ref[lo_r:lo_r + SSD_HEAD_DIM, :] = (
                state_ref[lo_r:lo_r + SSD_HEAD_DIM, :] * dec
                + s_new[r * SSD_HEAD_DIM:(r + 1) * SSD_HEAD_DIM, :])

    y = jnp.concatenate(y_pairs, axis=1)
    y = y + dskip_ref[...] * xs
    z = z_ref[...]
    y = y * (z * _sigmoid(z))
    gdim = D_SSD // SSD_GROUPS
    for g in range(SSD_GROUPS):
        yg = _rms(y[:, g * gdim:(g + 1) * gdim], ssdn_ref[:, g * gdim:(g + 1) * gdim])
        out_ref[:, g * gdim:(g + 1) * gdim] = yg.astype(out_ref.dtype)

    u_cur = u_ref[...]
    u_prev = pu_ref[...]
    pos = (c * CHUNK + lax.broadcasted_iota(jnp.int32, (CHUNK, POOL_GROUP_DIM), 0) + 1).astype(F32)
    for g, win in enumerate(POOL_WINDOWS):
        sl = slice(g * POOL_GROUP_DIM, (g + 1) * POOL_GROUP_DIM)
        ug = u_cur[:, sl]
        ext = jnp.concatenate([u_prev[:, sl], ug], axis=0)
        step = 1
        while step < win:
            ext = ext + pltpu.roll(ext, step, 0)
            step *= 2
        pooled = ext[CHUNK:, :] / jnp.minimum(pos, float(win)) - ug
        mixed = jnp.dot(pooled.astype(BF16), poolw_ref[g].astype(BF16), preferred_element_type=F32)
        out_ref[:, D_SSD + g * POOL_GROUP_DIM:D_SSD + (g + 1) * POOL_GROUP_DIM] = (
            mixed * pscale_ref[:, sl]).astype(out_ref.dtype)
    pu_ref[...] = u_cur


def _mixer(proj, dtp, conv_w, conv_b, dt_bias, a_log, d_skip, ssd_norm, pool_w, pool_scale, batch, seq):
    nchunk = seq // CHUNK
    row_map = lambda col: (lambda b, c: (b * nchunk + c, col))
    const2 = lambda b, c: (0, 0)
    pad = LANES - SSD_HEADS
    dtb = jnp.pad(dt_bias, (0, pad)).reshape(1, LANES)
    alog = jnp.pad(a_log, (0, pad)).reshape(1, LANES)
    dskip = jnp.repeat(d_skip, SSD_HEAD_DIM).reshape(1, D_SSD)
    return pl.pallas_call(
        _mixer_kernel,
        grid=(batch, nchunk),
        in_specs=[
            pl.BlockSpec((CHUNK, D_SSD), row_map(0)),
            pl.BlockSpec((CHUNK, D_SSD), row_map(1)),
            pl.BlockSpec((CHUNK, D_SSD), row_map(2)),
            pl.BlockSpec((CHUNK, D_POOL), row_map(3)),
            pl.BlockSpec((CHUNK, LANES), row_map(0)),
            pl.BlockSpec((CONV_WIDTH, D_CONV), const2),
            pl.BlockSpec((1, D_CONV), const2),
            pl.BlockSpec((1, LANES), const2),
            pl.BlockSpec((1, LANES), const2),
            pl.BlockSpec((1, D_SSD), const2),
            pl.BlockSpec((1, D_SSD), const2),
            pl.BlockSpec((len(POOL_WINDOWS), POOL_GROUP_DIM, POOL_GROUP_DIM), lambda b, c: (0, 0, 0)),
            pl.BlockSpec((1, D_POOL), const2),
        ],
        out_specs=pl.BlockSpec((CHUNK, D_MODEL), lambda b, c: (b * nchunk + c, 0)),
        out_shape=jax.ShapeDtypeStruct((batch * seq, D_MODEL), BF16),
        scratch_shapes=[pltpu.VMEM((CHUNK, D_SSD), F32), pltpu.VMEM((CHUNK, D_SSD), F32),
                        pltpu.VMEM((CHUNK, D_POOL), F32), pltpu.VMEM((SSD_HEADS * SSD_HEAD_DIM, D_STATE), F32)],
        compiler_params=_cparams(("arbitrary", "arbitrary")),
    )(proj, proj, proj, proj, dtp, conv_w, conv_b.reshape(1, D_CONV), dtb, alog, dskip,
      ssd_norm.reshape(1, D_SSD), pool_w, pool_scale.reshape(1, D_POOL))


def _attn_kernel(q_ref, k_ref, v_ref, o_ref):
    scale = XATTN_HEAD_DIM ** -0.5
    for h in range(XATTN_HEADS):
        sl = slice(h * XATTN_HEAD_DIM, (h + 1) * XATTN_HEAD_DIM)
        s = lax.dot_general(q_ref[:, sl], k_ref[:, sl], (((1,), (1,)), ((), ())),
                            preferred_element_type=F32) * scale
        s = s - jnp.max(s, axis=-1, keepdims=True)
        p = jnp.exp(s)
        p = p / jnp.sum(p, axis=-1, keepdims=True)
        o_ref[:, sl] = jnp.dot(p.astype(BF16), v_ref[:, sl], preferred_element_type=F32).astype(o_ref.dtype)


def _attn(q, kv, batch, seq, tq=512):
    nq = seq // tq
    return pl.pallas_call(
        _attn_kernel,
        grid=(batch, nq),
        in_specs=[pl.BlockSpec((tq, D_MODEL), lambda b, i: (b * nq + i, 0)),
                  pl.BlockSpec((N_MEM, D_MODEL), lambda b, i: (b, 0)),
                  pl.BlockSpec((N_MEM, D_MODEL), lambda b, i: (b, 1))],
        out_specs=pl.BlockSpec((tq, D_MODEL), lambda b, i: (b * nq + i, 0)),
        out_shape=jax.ShapeDtypeStruct((batch * seq, D_MODEL), BF16),
        compiler_params=_cparams(("arbitrary", "arbitrary")),
    )(q, kv, kv)


def _router_kernel(h_ref, nw_ref, wr_ref, br_ref, idx_ref, gate_ref, pos_ref, cnt_ref, run_ref):
    @pl.when(pl.program_id(0) == 0)
    def _():
        run_ref[...] = jnp.zeros_like(run_ref)

    hn = _rms(h_ref[...], nw_ref[...])
    logits = jnp.dot(hn, wr_ref[...], preferred_element_type=F32,
                     precision=lax.Precision.HIGHEST) + br_ref[...]
    tm = logits.shape[0]
    lane = lax.broadcasted_iota(jnp.int32, logits.shape, 1)
    vals = []
    hots = []
    idx_out = jnp.zeros(logits.shape, jnp.int32)
    for k in range(TOP_K):
        m = jnp.max(logits, axis=-1, keepdims=True)
        idx = jnp.min(jnp.where(logits == m, lane, LANES), axis=-1, keepdims=True)
        hit = lane == idx
        idx_out = jnp.where(lane == k, idx, idx_out)
        logits = jnp.where(hit, -jnp.inf, logits)
        vals.append(m)
        hots.append(jnp.where(hit, 1.0, 0.0))
    exps = [jnp.exp(v - vals[0]) for v in vals]
    den = exps[0] + exps[1] + exps[2] + exps[3]
    gate_out = jnp.zeros(logits.shape, F32)
    for k in range(TOP_K):
        gate_out = jnp.where(lane == k, exps[k] / den, gate_out)
    idx_ref[...] = idx_out
    gate_ref[...] = gate_out

    hot = hots[0] + hots[1] + hots[2] + hots[3]
    ri = lax.broadcasted_iota(jnp.int32, (tm, tm), 0)
    ci = lax.broadcasted_iota(jnp.int32, (tm, tm), 1)
    before = jnp.where(ri > ci, 1.0, 0.0).astype(BF16)
    ahead = jnp.dot(before, hot.astype(BF16), preferred_element_type=F32) + run_ref[...]
    pos_out = jnp.zeros(logits.shape, F32)
    for k in range(TOP_K):
        pos_out = jnp.where(lane == k, jnp.sum(ahead * hots[k], axis=-1, keepdims=True), pos_out)
    pos_ref[...] = pos_out
    run_ref[...] = run_ref[...] + jnp.sum(hot, axis=0, keepdims=True)
    cnt_ref[...] = run_ref[...]


def _router(h, norm_w, w_router, b_router):
    t = h.shape[0]
    tm = ROUTER_TOKENS
    pad = LANES - N_EXPERTS
    wr = jnp.pad(w_router, ((0, 0), (0, pad)))
    br = jnp.pad(b_router, (0, pad), constant_values=NEG_BIG).reshape(1, LANES)
    tok_spec = pl.BlockSpec((tm, LANES), lambda i: (i, 0))
    return pl.pallas_call(
        _router_kernel,
        grid=(t // tm,),
        in_specs=[pl.BlockSpec((tm, D_MODEL), lambda i: (i, 0)),
                  pl.BlockSpec((1, D_MODEL), lambda i: (0, 0)),
                  pl.BlockSpec((D_MODEL, LANES), lambda i: (0, 0)),
                  pl.BlockSpec((1, LANES), lambda i: (0, 0))],
        out_specs=[tok_spec, tok_spec, tok_spec, pl.BlockSpec((1, LANES), lambda i: (0, 0))],
        out_shape=[jax.ShapeDtypeStruct((t, LANES), jnp.int32),
                   jax.ShapeDtypeStruct((t, LANES), F32),
                   jax.ShapeDtypeStruct((t, LANES), F32),
                   jax.ShapeDtypeStruct((1, LANES), F32)],
        scratch_shapes=[pltpu.VMEM((1, LANES), F32)],
        compiler_params=_cparams(("arbitrary",)),
    )(h, norm_w.reshape(1, D_MODEL), wr, br)


def _dest_kernel(idx_ref, pos_ref, start_ref, dest_ref):
    idx = idx_ref[...]
    lane = lax.broadcasted_iota(jnp.int32, idx.shape, 1)
    out = pos_ref[...]
    for k in range(TOP_K):
        hit = lane == idx[:, k:k + 1]
        base = jnp.sum(jnp.where(hit, start_ref[...], 0.0), axis=-1, keepdims=True)
        out = jnp.where(lane == k, out + base, out)
    dest_ref[...] = out.astype(jnp.int32)


def _dest_rows(idx_pad, pos_pad, start_pad):
    t = idx_pad.shape[0]
    tm = ROUTER_TOKENS
    tok_spec = pl.BlockSpec((tm, LANES), lambda i: (i, 0))
    return pl.pallas_call(
        _dest_kernel,
        grid=(t // tm,),
        in_specs=[tok_spec, tok_spec, pl.BlockSpec((1, LANES), lambda i: (0, 0))],
        out_specs=tok_spec,
        out_shape=jax.ShapeDtypeStruct((t, LANES), jnp.int32),
        compiler_params=_cparams(("arbitrary",)),
    )(idx_pad, pos_pad, start_pad)


def _scatter_kernel(dest_ref, end_ref, h_ref, x_ref, zero_ref, zsem, sem):
    n_blocks = x_ref.shape[0] // MOE_BLOCK

    @pl.when(pl.program_id(0) == 0)
    def _():
        zero_ref[...] = jnp.zeros_like(zero_ref)
        n_used = end_ref[N_EXPERTS - 1] // MOE_BLOCK

        def zero_copy(b):
            return pltpu.make_async_copy(zero_ref, x_ref.at[pl.ds(pl.multiple_of(b * MOE_BLOCK, MOE_BLOCK),
                                                                   MOE_BLOCK)], zsem)

        def last_block(e):
            prev_end = jnp.where(e == 0, 0, end_ref[jnp.maximum(e - 1, 0)])
            return end_ref[e] // MOE_BLOCK - 1, end_ref[e] > prev_end

        def start_e(e, carry):
            b, nonempty = last_block(e)

            @pl.when(nonempty)
            def _():
                zero_copy(b).start()
            return carry

        def wait_e(e, carry):
            b, nonempty = last_block(e)

            @pl.when(nonempty)
            def _():
                zero_copy(b).wait()
            return carry

        lax.fori_loop(0, N_EXPERTS, start_e, 0)
        lax.fori_loop(n_used, n_blocks, lambda b, c: (zero_copy(b).start(), c)[1], 0)
        lax.fori_loop(0, N_EXPERTS, wait_e, 0)
        lax.fori_loop(n_used, n_blocks, lambda b, c: (zero_copy(b).wait(), c)[1], 0)

    def issue(r, carry):
        for k in range(TOP_K):
            d = dest_ref[0, 0, r * TOP_K + k]
            pltpu.make_async_copy(h_ref.at[pl.ds(r, 1)], x_ref.at[pl.ds(d, 1)], sem).start(priority=k % 2)
        return carry

    lax.fori_loop(0, SCATTER_TOKENS, issue, 0)
    n_sent = SCATTER_TOKENS * TOP_K
    pltpu.make_async_copy(x_ref.at[pl.ds(0, n_sent)], x_ref.at[pl.ds(0, n_sent)], sem).wait()


def _scatter_rows(h, dest, padded_end, n_rows):
    t = h.shape[0]
    steps = t // SCATTER_TOKENS
    return pl.pallas_call(
        _scatter_kernel,
        grid=(steps,),
        in_specs=[pl.BlockSpec((1, 1, SCATTER_TOKENS * TOP_K), lambda i: (i, 0, 0), memory_space=pltpu.SMEM),
                  pl.BlockSpec(memory_space=pltpu.SMEM),
                  pl.BlockSpec((SCATTER_TOKENS, D_MODEL), lambda i: (i, 0))],
        out_specs=pl.BlockSpec(memory_space=pl.ANY),
        out_shape=jax.ShapeDtypeStruct((n_rows, D_MODEL), F32),
        scratch_shapes=[pltpu.VMEM((MOE_BLOCK, D_MODEL), F32), pltpu.SemaphoreType.DMA(()),
                        pltpu.SemaphoreType.DMA(())],
        compiler_params=_cparams(("arbitrary",)),
    )(dest.reshape(steps, 1, SCATTER_TOKENS * TOP_K), padded_end, h)


def _row_block_loop(start_ref, end_ref, src_ref, dst_ref, in_buf, out_buf, in_sem, out_sem, prepare, compute):
    e = pl.program_id(0)
    j = pl.program_id(1)
    n_in = in_buf.shape[0]
    n_out = out_buf.shape[0]
    tn = out_buf.shape[2]
    col0 = pl.multiple_of(j * tn, tn)
    row0 = start_ref[e]
    nb = (end_ref[e] - row0) // MOE_BLOCK

    def rows(i):
        return pl.ds(pl.multiple_of(row0 + i * MOE_BLOCK, MOE_BLOCK), MOE_BLOCK)

    def in_copy(i, slot):
        return pltpu.make_async_copy(src_ref.at[rows(i)], in_buf.at[slot], in_sem.at[slot])

    def out_copy(i, slot):
        return pltpu.make_async_copy(out_buf.at[slot], dst_ref.at[rows(i), pl.ds(col0, tn)], out_sem.at[slot])

    for i0 in range(n_in - 1):
        @pl.when(i0 < nb)
        def _():
            in_copy(i0, i0).start(priority=ROW_DMA_PRIORITY)

    prepare()

    def body(i, carry):
        slot = i % n_in
        in_copy(i, slot).wait()
        ahead = i + n_in - 1

        @pl.when(ahead < nb)
        def _():
            in_copy(ahead, ahead % n_in).start(priority=ROW_DMA_PRIORITY)

        oslot = i % n_out

        @pl.when(i >= n_out)
        def _():
            out_copy(i - n_out, oslot).wait()

        out_buf[oslot] = compute(in_buf[slot]).astype(out_buf.dtype)
        out_copy(i, oslot).start(priority=ROW_DMA_PRIORITY)
        return carry

    lax.fori_loop(0, nb, body, 0)
    lax.fori_loop(jnp.maximum(nb - n_out, 0), nb, lambda i, c: (out_copy(i, i % n_out).wait(), c)[1], 0)

    @pl.when(e == N_EXPERTS - 1)
    def _():
        n_blocks = dst_ref.shape[0] // MOE_BLOCK
        n_used = end_ref[N_EXPERTS - 1] // MOE_BLOCK
        out_buf[0] = jnp.zeros(out_buf.shape[1:], out_buf.dtype)

        def tail_copy(b):
            r = pl.ds(pl.multiple_of(b * MOE_BLOCK, MOE_BLOCK), MOE_BLOCK)
            return pltpu.make_async_copy(out_buf.at[0], dst_ref.at[r, pl.ds(col0, tn)], out_sem.at[0])

        lax.fori_loop(n_used, n_blocks, lambda b, c: (tail_copy(b).start(), c)[1], 0)
        lax.fori_loop(n_used, n_blocks, lambda b, c: (tail_copy(b).wait(), c)[1], 0)


def _expert_up_kernel(start_ref, end_ref, x_ref, nw_ref, wg_ref, wu_ref, bg_ref, bu_ref, act_ref,
                      wgb_ref, wub_ref, x_buf, a_buf, in_sem, out_sem):
    def prepare():
        wgb_ref[...] = wg_ref[...].astype(BF16)
        wub_ref[...] = wu_ref[...].astype(BF16)

    def compute(x):
        xb = _rms(x, nw_ref[...]).astype(BF16)
        g = jnp.dot(xb, wgb_ref[...], preferred_element_type=F32) + bg_ref[...]
        u = jnp.dot(xb, wub_ref[...], preferred_element_type=F32) + bu_ref[...]
        g = jnp.minimum(g, SWIGLU_LIMIT)
        u = jnp.clip(u, -SWIGLU_LIMIT, SWIGLU_LIMIT)
        return g * _sigmoid(SWIGLU_ALPHA * g) * (u + 1.0)

    _row_block_loop(start_ref, end_ref, x_ref, act_ref, x_buf, a_buf, in_sem, out_sem, prepare, compute)


def _expert_down_kernel(start_ref, end_ref, a_ref, wd_ref, bd_ref, y_ref, wdb_ref, a_buf, y_buf, in_sem, out_sem):
    def prepare():
        wdb_ref[...] = wd_ref[...].astype(BF16)

    def compute(a):
        return jnp.dot(a, wdb_ref[...], preferred_element_type=F32) + bd_ref[...]

    _row_block_loop(start_ref, end_ref, a_ref, y_ref, a_buf, y_buf, in_sem, out_sem, prepare, compute)


def _experts(x_rows, padded_start, padded_end, norm_w, w_gate, b_gate, w_up, b_up, w_down, b_down):
    n_rows = x_rows.shape[0]
    wmap = lambda e, j, s0, s1: (e, 0, j)
    any_spec = pl.BlockSpec(memory_space=pl.ANY)
    sems = [pltpu.SemaphoreType.DMA((MOE_IN_BUFS,)), pltpu.SemaphoreType.DMA((MOE_OUT_BUFS,))]
    act = pl.pallas_call(
        _expert_up_kernel,
        grid_spec=pltpu.PrefetchScalarGridSpec(
            num_scalar_prefetch=2, grid=(N_EXPERTS, D_FF // MOE_UP_TN),
            in_specs=[any_spec,
                      pl.BlockSpec((1, D_MODEL), lambda e, j, s0, s1: (0, 0)),
                      pl.BlockSpec((None, D_MODEL, MOE_UP_TN), wmap),
                      pl.BlockSpec((None, D_MODEL, MOE_UP_TN), wmap),
                      pl.BlockSpec((None, 1, MOE_UP_TN), wmap),
                      pl.BlockSpec((None, 1, MOE_UP_TN), wmap)],
            out_specs=any_spec,
            scratch_shapes=[pltpu.VMEM((D_MODEL, MOE_UP_TN), BF16), pltpu.VMEM((D_MODEL, MOE_UP_TN), BF16),
                            pltpu.VMEM((MOE_IN_BUFS, MOE_BLOCK, D_MODEL), F32),
                            pltpu.VMEM((MOE_OUT_BUFS, MOE_BLOCK, MOE_UP_TN), BF16)] + sems),
        out_shape=jax.ShapeDtypeStruct((n_rows, D_FF), BF16),
        compiler_params=_cparams(("arbitrary", "arbitrary")),
    )(padded_start, padded_end, x_rows, norm_w.reshape(1, D_MODEL), w_gate, w_up,
      b_gate.reshape(N_EXPERTS, 1, D_FF), b_up.reshape(N_EXPERTS, 1, D_FF))
    return pl.pallas_call(
        _expert_down_kernel,
        grid_spec=pltpu.PrefetchScalarGridSpec(
            num_scalar_prefetch=2, grid=(N_EXPERTS, D_MODEL // MOE_DOWN_TN),
            in_specs=[any_spec,
                      pl.BlockSpec((None, D_FF, MOE_DOWN_TN), wmap),
                      pl.BlockSpec((None, 1, MOE_DOWN_TN), wmap)],
            out_specs=any_spec,
            scratch_shapes=[pltpu.VMEM((D_FF, MOE_DOWN_TN), BF16),
                            pltpu.VMEM((MOE_IN_BUFS, MOE_BLOCK, D_FF), BF16),
                            pltpu.VMEM((MOE_OUT_BUFS, MOE_BLOCK, MOE_DOWN_TN), F32)] + sems),
        out_shape=jax.ShapeDtypeStruct((n_rows, D_MODEL), F32),
        compiler_params=_cparams(("arbitrary", "arbitrary")),
    )(padded_start, padded_end, act, w_down, b_down.reshape(N_EXPERTS, 1, D_MODEL))


def _combine_kernel(dest_ref, h_ref, gate_ref, nw_ref, y_ref, o_ref, buf_ref, sem):
    def issue(r, carry):
        for k in range(TOP_K):
            d = dest_ref[0, 0, r * TOP_K + k]
            pltpu.make_async_copy(y_ref.at[pl.ds(d, 1)], buf_ref.at[k, pl.ds(r, 1)], sem).start(priority=k % 2)
        return carry

    lax.fori_loop(0, COMBINE_TOKENS, issue, 0)
    pltpu.make_async_copy(buf_ref, buf_ref, sem).wait()
    acc = h_ref[...]
    gate = gate_ref[...]
    for k in range(TOP_K):
        acc = acc + gate[:, k:k + 1] * buf_ref[k]
    o_ref[...] = _rms(acc, nw_ref[...])


def _combine(h, y_rows, dest, gate_pad, norm_w):
    t = h.shape[0]
    steps = t // COMBINE_TOKENS
    return pl.pallas_call(
        _combine_kernel,
        grid=(steps,),
        in_specs=[pl.BlockSpec((1, 1, COMBINE_TOKENS * TOP_K), lambda i: (i, 0, 0), memory_space=pltpu.SMEM),
                  pl.BlockSpec((COMBINE_TOKENS, D_MODEL), lambda i: (i, 0)),
                  pl.BlockSpec((COMBINE_TOKENS, LANES), lambda i: (i, 0)),
                  pl.BlockSpec((1, D_MODEL), lambda i: (0, 0)),
                  pl.BlockSpec(memory_space=pl.ANY)],
        out_specs=pl.BlockSpec((COMBINE_TOKENS, D_MODEL), lambda i: (i, 0)),
        out_shape=jax.ShapeDtypeStruct((t, D_MODEL), F32),
        scratch_shapes=[pltpu.VMEM((TOP_K, COMBINE_TOKENS, D_MODEL), F32), pltpu.SemaphoreType.DMA(())],
        compiler_params=_cparams(("arbitrary",)),
    )(dest.reshape(steps, 1, COMBINE_TOKENS * TOP_K), h, gate_pad, norm_w.reshape(1, D_MODEL), y_rows)


def _moe(h, norm_ffn, w_router, b_router, w_gate, b_gate, w_up, b_up, w_down, b_down, norm_final):
    t = h.shape[0]
    tk = t * TOP_K
    n_blocks = (tk + N_EXPERTS * (MOE_BLOCK - 1) + MOE_BLOCK - 1) // MOE_BLOCK
    n_rows = n_blocks * MOE_BLOCK
    idx_pad, gate_pad, pos_pad, cnt_pad = _router(h, norm_ffn, w_router, b_router)
    counts = cnt_pad[0, :N_EXPERTS].astype(jnp.int32)
    padded = (counts + MOE_BLOCK - 1) // MOE_BLOCK * MOE_BLOCK
    padded_end = jnp.cumsum(padded).astype(jnp.int32)
    padded_start = padded_end - padded
    start_pad = jnp.pad(padded_start.astype(F32), (0, LANES - N_EXPERTS)).reshape(1, LANES)
    dest = _dest_rows(idx_pad, pos_pad, start_pad)[:, :TOP_K].reshape(tk)
    x_rows = _scatter_rows(h, dest, padded_end, n_rows)
    y_rows = _experts(x_rows, padded_start, padded_end, norm_ffn, w_gate, b_gate, w_up, b_up, w_down, b_down)
    return _combine(h, y_rows, dest, gate_pad, norm_final)


def kernel(x, mem, norm_mix, w_in, conv_w, conv_b, dt_bias, a_log, d_skip, ssd_norm, pool_w, pool_scale,
           w_out, norm_xattn, norm_mem, w_q, w_kv, w_o, norm_ffn, w_router, b_router, w_gate, b_gate,
           w_up, b_up, w_down, b_down, norm_final):
    batch, seq, d = x.shape
    t = batch * seq
    h = x.reshape(t, d)
    depth = norm_mix.shape[0]
    assert depth == 1, "the final norm is fused into the MoE combine of the single layer"
    for l in range(depth):
        dt_lo = D_SSD + D_CONV
        w_main = jnp.concatenate([w_in[l][:, :dt_lo], w_in[l][:, dt_lo + SSD_HEADS:]], axis=1)
        w_dt = jnp.pad(w_in[l][:, dt_lo:dt_lo + SSD_HEADS], ((0, 0), (0, LANES - SSD_HEADS)))
        proj = _mm(h, w_main, norm_w=norm_mix[l])
        dtp = _mm(h, w_dt, norm_w=norm_mix[l])
        mix = _mixer(proj, dtp, conv_w[l], conv_b[l], dt_bias[l], a_log[l], d_skip[l], ssd_norm[l],
                     pool_w[l], pool_scale[l], batch, seq)
        h = _mm(mix, w_out[l], res=h)

        q = _mm(h, w_q[l], norm_w=norm_xattn[l], out_dtype=BF16)
        kv = _mm(mem.reshape(batch * N_MEM, d), w_kv[l], norm_w=norm_mem[l], out_dtype=BF16)
        o = _attn(q, kv, batch, seq)
        h = _mm(o, w_o[l], res=h)

        out = _moe(h, norm_ffn[l], w_router[l], b_router[l], w_gate[l], b_gate[l], w_up[l], b_up[l],
                   w_down[l], b_down[l], norm_final)
    return out.reshape(batch, seq, d)
```

```python
import functools

import jax
import jax.numpy as jnp
from jax import lax
from jax.experimental import pallas as pl
from jax.experimental.pallas import tpu as pltpu

F32 = jnp.float32
BF16 = jnp.bfloat16

D_MODEL = 2048
D_SSD = 1024
SSD_HEAD_DIM = 64
SSD_HEADS = 16
SSD_GROUPS = 4
HEADS_PER_GROUP = SSD_HEADS // SSD_GROUPS
D_STATE = 128
CONV_WIDTH = 4
CHUNK = 128
D_CONV = D_SSD + 2 * SSD_GROUPS * D_STATE
D_POOL = 1024
POOL_WINDOWS = (2, 4, 8, 16)
POOL_GROUP_DIM = 256
N_MEM = 256
XATTN_HEADS = 4
XATTN_HEAD_DIM = 512
N_EXPERTS = 32
TOP_K = 4
D_FF = 2048
SWIGLU_LIMIT = 7.0
SWIGLU_ALPHA = 1.702
EPS = 1e-5

LANES = 128
VMEM_LIMIT = 56 * 1024 * 1024
NEG_BIG = -1e30

MOE_BLOCK = 256
MOE_UP_TN = 1024
MOE_DOWN_TN = 2048
MOE_IN_BUFS = 3
MOE_OUT_BUFS = 2
W_CHUNKS = 4
W_RING = 3
ROUTER_TOKENS = 512
SCATTER_TOKENS = 256
COMBINE_TOKENS = 128
ROW_DMA_PRIORITY = 1


def _cparams(sem):
    return pltpu.CompilerParams(dimension_semantics=sem, vmem_limit_bytes=VMEM_LIMIT)


def _rms(x, w):
    ms = jnp.mean(x * x, axis=-1, keepdims=True)
    return x * lax.rsqrt(ms + EPS) * w


def _mm_kernel(*refs, has_norm, has_res):
    it = iter(refs)
    a_ref = next(it)
    w_ref = next(it)
    nw_ref = next(it) if has_norm else None
    r_ref = next(it) if has_res else None
    o_ref = next(it)
    wb_ref = next(it)

    @pl.when(pl.program_id(1) == 0)
    def _():
        wb_ref[...] = w_ref[...].astype(BF16)

    a = a_ref[...]
    if has_norm:
        a = _rms(a.astype(F32), nw_ref[...])
    acc = jnp.dot(a.astype(BF16), wb_ref[...], preferred_element_type=F32)
    if has_res:
        acc = acc + r_ref[...]
    o_ref[...] = acc.astype(o_ref.dtype)


def _mm(a, w, norm_w=None, res=None, out_dtype=F32, tm=512, tn=1024):
    m, k = a.shape
    n = w.shape[1]
    tn = min(tn, n)
    tm = min(tm, m)
    assert m % tm == 0 and n % tn == 0
    in_specs = [pl.BlockSpec((tm, k), lambda j, i: (i, 0)),
                pl.BlockSpec((k, tn), lambda j, i: (0, j))]
    args = [a, w]
    if norm_w is not None:
        in_specs.append(pl.BlockSpec((1, k), lambda j, i: (0, 0)))
        args.append(norm_w.reshape(1, k))
    if res is not None:
        in_specs.append(pl.BlockSpec((tm, tn), lambda j, i: (i, j)))
        args.append(res)
    return pl.pallas_call(
        functools.partial(_mm_kernel, has_norm=norm_w is not None, has_res=res is not None),
        grid=(n // tn, m // tm),
        in_specs=in_specs,
        out_specs=pl.BlockSpec((tm, tn), lambda j, i: (i, j)),
        out_shape=jax.ShapeDtypeStruct((m, n), out_dtype),
        scratch_shapes=[pltpu.VMEM((k, tn), BF16)],
        compiler_params=_cparams(("arbitrary", "arbitrary")),
    )(*args)


def _sigmoid(x):
    return 1.0 / (1.0 + jnp.exp(-x))


def _shift_rows(cur, prev, j, rows):
    return jnp.where(rows < j, pltpu.roll(prev, j, 0), pltpu.roll(cur, j, 0))


def _conv_silu(cur, prev, w_ref, b_ref, col0, rows):
    width = cur.shape[1]
    acc = cur * w_ref[CONV_WIDTH - 1:CONV_WIDTH, col0:col0 + width] + b_ref[:, col0:col0 + width]
    for j in range(1, CONV_WIDTH):
        k = CONV_WIDTH - 1 - j
        acc = acc + _shift_rows(cur, prev, j, rows) * w_ref[k:k + 1, col0:col0 + width]
    return acc * _sigmoid(acc)


def _mixer_kernel(z_ref, x_ref, bc_ref, u_ref, dt_ref, convw_ref, convb_ref, dtb_ref, alog_ref,
                  dskip_ref, ssdn_ref, poolw_ref, pscale_ref, out_ref,
                  px_ref, pbc_ref, pu_ref, state_ref):
    c = pl.program_id(1)

    @pl.when(c == 0)
    def _():
        px_ref[...] = jnp.zeros_like(px_ref)
        pbc_ref[...] = jnp.zeros_like(pbc_ref)
        pu_ref[...] = jnp.zeros_like(pu_ref)
        state_ref[...] = jnp.zeros_like(state_ref)

    rows = lax.broadcasted_iota(jnp.int32, (CHUNK, D_SSD), 0)
    x_raw = x_ref[...]
    bc_raw = bc_ref[...]
    xs = _conv_silu(x_raw, px_ref[...], convw_ref, convb_ref, 0, rows)
    bcs = _conv_silu(bc_raw, pbc_ref[...], convw_ref, convb_ref, D_SSD, rows)
    px_ref[...] = x_raw
    pbc_ref[...] = bc_raw

    dt_in = dt_ref[...] + dtb_ref[...]
    dt = jnp.maximum(dt_in, 0.0) + jnp.log(1.0 + jnp.exp(-jnp.abs(dt_in)))
    a = -jnp.exp(alog_ref[...])
    dta = dt * a
    ri = lax.broadcasted_iota(jnp.int32, (CHUNK, CHUNK), 0)
    ci = lax.broadcasted_iota(jnp.int32, (CHUNK, CHUNK), 1)
    causal = ri >= ci
    tril = jnp.where(causal, 1.0, 0.0).astype(F32)
    acs = jnp.dot(tril, dta, preferred_element_type=F32, precision=lax.Precision.HIGHEST)
    acs_t = acs.T
    dt_t = dt.T
    lo = ci < SSD_HEAD_DIM

    y_pairs = []
    for g in range(SSD_GROUPS):
        bg = bcs[:, g * D_STATE:(g + 1) * D_STATE].astype(BF16)
        cg = bcs[:, SSD_GROUPS * D_STATE + g * D_STATE:SSD_GROUPS * D_STATE + (g + 1) * D_STATE].astype(BF16)
        cb = lax.dot_general(cg, bg, (((1,), (1,)), ((), ())), preferred_element_type=F32)
        gw = HEADS_PER_GROUP * SSD_HEAD_DIM
        prev_g = state_ref[g * gw:(g + 1) * gw, :]
        yoff = lax.dot_general(cg, prev_g.astype(BF16), (((1,), (1,)), ((), ())),
                               preferred_element_type=F32)
        xw_parts = []
        for pr in range(HEADS_PER_GROUP // 2):
            pair = g * (HEADS_PER_GROUP // 2) + pr
            xs_pair = xs[:, pair * LANES:(pair + 1) * LANES]
            y_pair = None
            ecols = []
            wcols = []
            for q in range(2):
                h = pair * 2 + q
                col = acs[:, h:h + 1]
                seg = col - acs_t[h:h + 1, :]
                decay = jnp.exp(jnp.where(causal, seg, NEG_BIG))
                mat = (cb * decay * dt_t[h:h + 1, :]).astype(BF16)
                keep = lo if q == 0 else jnp.logical_not(lo)
                xh = jnp.where(keep, xs_pair, 0.0).astype(BF16)
                yd = jnp.dot(mat, xh, preferred_element_type=F32)
                y_pair = yd if y_pair is None else y_pair + yd
                ecols.append(jnp.exp(col))
                wcols.append(dt[:, h:h + 1] * jnp.exp(acs[CHUNK - 1:CHUNK, h:h + 1] - col))
            e_pair = jnp.where(lo, ecols[0], ecols[1])
            w_pair = jnp.where(lo, wcols[0], wcols[1])
            y_pairs.append(y_pair + yoff[:, pr * LANES:(pr + 1) * LANES] * e_pair)
            xw_parts.append(xs_pair * w_pair)
        xw = jnp.concatenate(xw_parts, axis=1).astype(BF16)
        s_new = lax.dot_general(xw, bg, (((0,), (0,)), ((), ())), preferred_element_type=F32)
        for r in range(HEADS_PER_GROUP):
            h = g * HEADS_PER_GROUP + r
            dec = jnp.exp(acs[CHUNK - 1:CHUNK, h:h + 1])
            lo_r = h * SSD_HEAD_DIM
            state_ref[lo_r:lo_r + SSD_HEAD_DIM, :] = (
                state_ref[lo_r:lo_r + SSD_HEAD_DIM, :] * dec
                + s_new[r * SSD_HEAD_DIM:(r + 1) * SSD_HEAD_DIM, :])

    y = jnp.concatenate(y_pairs, axis=1)
    y = y + dskip_ref[...] * xs
    z = z_ref[...]
    y = y * (z * _sigmoid(z))
    gdim = D_SSD // SSD_GROUPS
    for g in range(SSD_GROUPS):
        yg = _rms(y[:, g * gdim:(g + 1) * gdim], ssdn_ref[:, g * gdim:(g + 1) * gdim])
        out_ref[:, g * gdim:(g + 1) * gdim] = yg.astype(out_ref.dtype)

    u_cur = u_ref[...]
    u_prev = pu_ref[...]
    pos = (c * CHUNK + lax.broadcasted_iota(jnp.int32, (CHUNK, POOL_GROUP_DIM), 0) + 1).astype(F32)
    for g, win in enumerate(POOL_WINDOWS):
        sl = slice(g * POOL_GROUP_DIM, (g + 1) * POOL_GROUP_DIM)
        ug = u_cur[:, sl]
        ext = jnp.concatenate([u_prev[:, sl], ug], axis=0)
        step = 1
        while step < win:
            ext = ext + pltpu.roll(ext, step, 0)
            step *= 2
        pooled = ext[CHUNK:, :] / jnp.minimum(pos, float(win)) - ug
        mixed = jnp.dot(pooled.astype(BF16), poolw_ref[g].astype(BF16), preferred_element_type=F32)
        out_ref[:, D_SSD + g * POOL_GROUP_DIM:D_SSD + (g + 1) * POOL_GROUP_DIM] = (
            mixed * pscale_ref[:, sl]).astype(out_ref.dtype)
    pu_ref[...] = u_cur


def _mixer(proj, dtp, conv_w, conv_b, dt_bias, a_log, d_skip, ssd_norm, pool_w, pool_scale, batch, seq):
    nchunk = seq // CHUNK
    row_map = lambda col: (lambda b, c: (b * nchunk + c, col))
    const2 = lambda b, c: (0, 0)
    pad = LANES - SSD_HEADS
    dtb = jnp.pad(dt_bias, (0, pad)).reshape(1, LANES)
    alog = jnp.pad(a_log, (0, pad)).reshape(1, LANES)
    dskip = jnp.repeat(d_skip, SSD_HEAD_DIM).reshape(1, D_SSD)
    return pl.pallas_call(
        _mixer_kernel,
        grid=(batch, nchunk),
        in_specs=[
            pl.BlockSpec((CHUNK, D_SSD), row_map(0)),
            pl.BlockSpec((CHUNK, D_SSD), row_map(1)),
            pl.BlockSpec((CHUNK, D_SSD), row_map(2)),
            pl.BlockSpec((CHUNK, D_POOL), row_map(3)),
            pl.BlockSpec((CHUNK, LANES), row_map(0)),
            pl.BlockSpec((CONV_WIDTH, D_CONV), const2),
            pl.BlockSpec((1, D_CONV), const2),
            pl.BlockSpec((1, LANES), const2),
            pl.BlockSpec((1, LANES), const2),
            pl.BlockSpec((1, D_SSD), const2),
            pl.BlockSpec((1, D_SSD), const2),
            pl.BlockSpec((len(POOL_WINDOWS), POOL_GROUP_DIM, POOL_GROUP_DIM), lambda b, c: (0, 0, 0)),
            pl.BlockSpec((1, D_POOL), const2),
        ],
        out_specs=pl.BlockSpec((CHUNK, D_MODEL), lambda b, c: (b * nchunk + c, 0)),
        out_shape=jax.ShapeDtypeStruct((batch * seq, D_MODEL), BF16),
        scratch_shapes=[pltpu.VMEM((CHUNK, D_SSD), F32), pltpu.VMEM((CHUNK, D_SSD), F32),
                        pltpu.VMEM((CHUNK, D_POOL), F32), pltpu.VMEM((SSD_HEADS * SSD_HEAD_DIM, D_STATE), F32)],
        compiler_params=_cparams(("arbitrary", "arbitrary")),
    )(proj, proj, proj, proj, dtp, conv_w, conv_b.reshape(1, D_CONV), dtb, alog, dskip,
      ssd_norm.reshape(1, D_SSD), pool_w, pool_scale.reshape(1, D_POOL))


def _attn_kernel(q_ref, k_ref, v_ref, o_ref):
    scale = XATTN_HEAD_DIM ** -0.5
    for h in range(XATTN_HEADS):
        sl = slice(h * XATTN_HEAD_DIM, (h + 1) * XATTN_HEAD_DIM)
        s = lax.dot_general(q_ref[:, sl], k_ref[:, sl], (((1,), (1,)), ((), ())),
                            preferred_element_type=F32) * scale
        s = s - jnp.max(s, axis=-1, keepdims=True)
        p = jnp.exp(s)
        p = p / jnp.sum(p, axis=-1, keepdims=True)
        o_ref[:, sl] = jnp.dot(p.astype(BF16), v_ref[:, sl], preferred_element_type=F32).astype(o_ref.dtype)


def _attn(q, kv, batch, seq, tq=512):
    nq = seq // tq
    return pl.pallas_call(
        _attn_kernel,
        grid=(batch, nq),
        in_specs=[pl.BlockSpec((tq, D_MODEL), lambda b, i: (b * nq + i, 0)),
                  pl.BlockSpec((N_MEM, D_MODEL), lambda b, i: (b, 0)),
                  pl.BlockSpec((N_MEM, D_MODEL), lambda b, i: (b, 1))],
        out_specs=pl.BlockSpec((tq, D_MODEL), lambda b, i: (b * nq + i, 0)),
        out_shape=jax.ShapeDtypeStruct((batch * seq, D_MODEL), BF16),
        compiler_params=_cparams(("arbitrary", "arbitrary")),
    )(q, kv, kv)


def _router_kernel(h_ref, nw_ref, wr_ref, br_ref, idx_ref, gate_ref, pos_ref, cnt_ref, run_ref):
    @pl.when(pl.program_id(0) == 0)
    def _():
        run_ref[...] = jnp.zeros_like(run_ref)

    hn = _rms(h_ref[...], nw_ref[...])
    logits = jnp.dot(hn, wr_ref[...], preferred_element_type=F32,
                     precision=lax.Precision.HIGHEST) + br_ref[...]
    tm = logits.shape[0]
    lane = lax.broadcasted_iota(jnp.int32, logits.shape, 1)
    vals = []
    hots = []
    idx_out = jnp.zeros(logits.shape, jnp.int32)
    for k in range(TOP_K):
        m = jnp.max(logits, axis=-1, keepdims=True)
        idx = jnp.min(jnp.where(logits == m, lane, LANES), axis=-1, keepdims=True)
        hit = lane == idx
        idx_out = jnp.where(lane == k, idx, idx_out)
        logits = jnp.where(hit, -jnp.inf, logits)
        vals.append(m)
        hots.append(jnp.where(hit, 1.0, 0.0))
    exps = [jnp.exp(v - vals[0]) for v in vals]
    den = exps[0] + exps[1] + exps[2] + exps[3]
    gate_out = jnp.zeros(logits.shape, F32)
    for k in range(TOP_K):
        gate_out = jnp.where(lane == k, exps[k] / den, gate_out)
    idx_ref[...] = idx_out
    gate_ref[...] = gate_out

    hot = hots[0] + hots[1] + hots[2] + hots[3]
    ri = lax.broadcasted_iota(jnp.int32, (tm, tm), 0)
    ci = lax.broadcasted_iota(jnp.int32, (tm, tm), 1)
    before = jnp.where(ri > ci, 1.0, 0.0).astype(BF16)
    ahead = jnp.dot(before, hot.astype(BF16), preferred_element_type=F32) + run_ref[...]
    pos_out = jnp.zeros(logits.shape, F32)
    for k in range(TOP_K):
        pos_out = jnp.where(lane == k, jnp.sum(ahead * hots[k], axis=-1, keepdims=True), pos_out)
    pos_ref[...] = pos_out
    run_ref[...] = run_ref[...] + jnp.sum(hot, axis=0, keepdims=True)
    cnt_ref[...] = run_ref[...]


def _router(h, norm_w, w_router, b_router):
    t = h.shape[0]
    tm = ROUTER_TOKENS
    pad = LANES - N_EXPERTS
    wr = jnp.pad(w_router, ((0, 0), (0, pad)))
    br = jnp.pad(b_router, (0, pad), constant_values=NEG_BIG).reshape(1, LANES)
    tok_spec = pl.BlockSpec((tm, LANES), lambda i: (i, 0))
    return pl.pallas_call(
        _router_kernel,
        grid=(t // tm,),
        in_specs=[pl.BlockSpec((tm, D_MODEL), lambda i: (i, 0)),
                  pl.BlockSpec((1, D_MODEL), lambda i: (0, 0)),
                  pl.BlockSpec((D_MODEL, LANES), lambda i: (0, 0)),
                  pl.BlockSpec((1, LANES), lambda i: (0, 0))],
        out_specs=[tok_spec, tok_spec, tok_spec, pl.BlockSpec((1, LANES), lambda i: (0, 0))],
        out_shape=[jax.ShapeDtypeStruct((t, LANES), jnp.int32),
                   jax.ShapeDtypeStruct((t, LANES), F32),
                   jax.ShapeDtypeStruct((t, LANES), F32),
                   jax.ShapeDtypeStruct((1, LANES), F32)],
        scratch_shapes=[pltpu.VMEM((1, LANES), F32)],
        compiler_params=_cparams(("arbitrary",)),
    )(h, norm_w.reshape(1, D_MODEL), wr, br)


def _dest_kernel(idx_ref, pos_ref, start_ref, dest_ref):
    idx = idx_ref[...]
    lane = lax.broadcasted_iota(jnp.int32, idx.shape, 1)
    out = pos_ref[...]
    for k in range(TOP_K):
        hit = lane == idx[:, k:k + 1]
        base = jnp.sum(jnp.where(hit, start_ref[...], 0.0), axis=-1, keepdims=True)
        out = jnp.where(lane == k, out + base, out)
    dest_ref[...] = out.astype(jnp.int32)


def _dest_rows(idx_pad, pos_pad, start_pad):
    t = idx_pad.shape[0]
    tm = ROUTER_TOKENS
    tok_spec = pl.BlockSpec((tm, LANES), lambda i: (i, 0))
    return pl.pallas_call(
        _dest_kernel,
        grid=(t // tm,),
        in_specs=[tok_spec, tok_spec, pl.BlockSpec((1, LANES), lambda i: (0, 0))],
        out_specs=tok_spec,
        out_shape=jax.ShapeDtypeStruct((t, LANES), jnp.int32),
        compiler_params=_cparams(("arbitrary",)),
    )(idx_pad, pos_pad, start_pad)


def _scatter_kernel(dest_ref, end_ref, h_ref, x_ref, zero_ref, zsem, sem):
    n_blocks = x_ref.shape[0] // MOE_BLOCK

    @pl.when(pl.program_id(0) == 0)
    def _():
        zero_ref[...] = jnp.zeros_like(zero_ref)
        n_used = end_ref[N_EXPERTS - 1] // MOE_BLOCK

        def zero_copy(b):
            return pltpu.make_async_copy(zero_ref, x_ref.at[pl.ds(pl.multiple_of(b * MOE_BLOCK, MOE_BLOCK),
                                                                   MOE_BLOCK)], zsem)

        def last_block(e):
            prev_end = jnp.where(e == 0, 0, end_ref[jnp.maximum(e - 1, 0)])
            return end_ref[e] // MOE_BLOCK - 1, end_ref[e] > prev_end

        def start_e(e, carry):
            b, nonempty = last_block(e)

            @pl.when(nonempty)
            def _():
                zero_copy(b).start()
            return carry

        def wait_e(e, carry):
            b, nonempty = last_block(e)

            @pl.when(nonempty)
            def _():
                zero_copy(b).wait()
            return carry

        lax.fori_loop(0, N_EXPERTS, start_e, 0)
        lax.fori_loop(n_used, n_blocks, lambda b, c: (zero_copy(b).start(), c)[1], 0)
        lax.fori_loop(0, N_EXPERTS, wait_e, 0)
        lax.fori_loop(n_used, n_blocks, lambda b, c: (zero_copy(b).wait(), c)[1], 0)

    def issue(r, carry):
        for k in range(TOP_K):
            d = dest_ref[0, 0, r * TOP_K + k]
            pltpu.make_async_copy(h_ref.at[pl.ds(r, 1)], x_ref.at[pl.ds(d, 1)], sem).start(priority=k % 2)
        return carry

    lax.fori_loop(0, SCATTER_TOKENS, issue, 0)
    n_sent = SCATTER_TOKENS * TOP_K
    pltpu.make_async_copy(x_ref.at[pl.ds(0, n_sent)], x_ref.at[pl.ds(0, n_sent)], sem).wait()


def _scatter_rows(h, dest, padded_end, n_rows):
    t = h.shape[0]
    steps = t // SCATTER_TOKENS
    return pl.pallas_call(
        _scatter_kernel,
        grid=(steps,),
        in_specs=[pl.BlockSpec((1, 1, SCATTER_TOKENS * TOP_K), lambda i: (i, 0, 0), memory_space=pltpu.SMEM),
                  pl.BlockSpec(memory_space=pltpu.SMEM),
                  pl.BlockSpec((SCATTER_TOKENS, D_MODEL), lambda i: (i, 0))],
        out_specs=pl.BlockSpec(memory_space=pl.ANY),
        out_shape=jax.ShapeDtypeStruct((n_rows, D_MODEL), F32),
        scratch_shapes=[pltpu.VMEM((MOE_BLOCK, D_MODEL), F32), pltpu.SemaphoreType.DMA(()),
                        pltpu.SemaphoreType.DMA(())],
        compiler_params=_cparams(("arbitrary",)),
    )(dest.reshape(steps, 1, SCATTER_TOKENS * TOP_K), padded_end, h)


def _item_tables(padded_start, padded_end, n_j, n_items_max):
    nb_e = (padded_end - padded_start) // MOE_BLOCK
    bstart = padded_start // MOE_BLOCK
    gend_e = jnp.cumsum(nb_e * n_j).astype(jnp.int32)
    gstart_e = gend_e - nb_e * n_j
    n_items = gend_e[N_EXPERTS - 1]
    g = jnp.minimum(jnp.arange(n_items_max, dtype=jnp.int32), n_items - 1)
    e = jnp.clip(jnp.searchsorted(gend_e, g, side='right'), 0, N_EXPERTS - 1).astype(jnp.int32)
    q = g - gstart_e[e]
    nb = jnp.maximum(nb_e[e], 1)
    j = q // nb
    i = q % nb
    row = bstart[e] + i
    meta = e * 8 + j * 2 + (i == 0).astype(jnp.int32)
    group_end = gstart_e[e] + (j + 1) * nb
    return (meta.astype(jnp.int32), row.astype(jnp.int32), group_end.astype(jnp.int32),
            n_items.reshape(1).astype(jnp.int32))


def _item_loop(meta_ref, row_ref, gend_ref, n_ref, src_ref, dst_ref, w_refs, ring, ring_sem, wb_ref,
               in_buf, out_buf, in_sem, out_sem, compute):
    n = n_ref[0]
    n_in = in_buf.shape[0]
    n_out = out_buf.shape[0]
    tn = out_buf.shape[2]
    n_j = dst_ref.shape[1] // tn
    n_mats = len(w_refs)
    cpm = W_CHUNKS // n_mats
    kc = w_refs[0].shape[1] // cpm

    def rows(g):
        return pl.ds(pl.multiple_of(row_ref[g] * MOE_BLOCK, MOE_BLOCK), MOE_BLOCK)

    def cols(m):
        return pl.ds(pl.multiple_of(((m >> 1) & 3) * tn, tn), tn)

    def in_copy(g):
        slot = g % n_in
        return pltpu.make_async_copy(src_ref.at[rows(g)], in_buf.at[slot], in_sem.at[slot])

    def out_copy(g):
        slot = g % n_out
        return pltpu.make_async_copy(out_buf.at[slot], dst_ref.at[rows(g), cols(meta_ref[g])], out_sem.at[slot])

    def krows(c):
        r0 = (c % cpm) * kc
        return pl.ds(r0 if isinstance(c, int) else pl.multiple_of(r0, kc), kc)

    def chunk_copy(m, c, k):
        slot = c % W_RING
        return pltpu.make_async_copy(w_refs[k].at[m >> 3, krows(c), cols(m)], ring.at[slot], ring_sem.at[slot])

    def for_matrix(c, fn):
        if isinstance(c, int):
            return fn(c // cpm)
        for k in range(n_mats):
            @pl.when(c // cpm == k)
            def _(k=k):
                fn(k)

    def start_chunk(m, c):
        for_matrix(c, lambda k: chunk_copy(m, c, k).start())

    def finish_chunk(m, c, wslot):
        def fn(k):
            chunk_copy(m, c, k).wait()
            wb_ref[wslot, k, krows(c), :] = ring[c % W_RING].astype(BF16)
        for_matrix(c, fn)

        @pl.when(c + W_RING < W_CHUNKS)
        def _():
            start_chunk(m, c + W_RING)

    for c0 in range(W_RING):
        start_chunk(meta_ref[0], c0)
    for g0 in range(n_in - 1):
        @pl.when(g0 < n)
        def _():
            in_copy(g0).start(priority=ROW_DMA_PRIORITY)

    def body(g, carry):
        parity, done, nm = carry
        m = meta_ref[g]
        first = (m & 1) == 1

        @pl.when(first)
        def _():
            lax.fori_loop(done, W_CHUNKS, lambda c, z: (finish_chunk(m, c, 1 - parity), z)[1], 0)
            nxt = gend_ref[g]

            @pl.when(nxt < n)
            def _():
                for c0 in range(W_RING):
                    start_chunk(meta_ref[nxt], c0)

        nxt = gend_ref[g]
        parity = jnp.where(first, 1 - parity, parity)
        done = jnp.where(first, 0, done)
        nm = jnp.where(first, jnp.where(nxt < n, meta_ref[jnp.minimum(nxt, n - 1)], -1), nm)

        in_copy(g).wait()

        @pl.when(g + n_in - 1 < n)
        def _():
            in_copy(g + n_in - 1).start(priority=ROW_DMA_PRIORITY)

        @pl.when(g >= n_out)
        def _():
            out_copy(g - n_out).wait()

        out_buf[g % n_out] = compute(in_buf[g % n_in], wb_ref.at[parity], m).astype(out_buf.dtype)
        out_copy(g).start(priority=ROW_DMA_PRIORITY)

        active = (nm >= 0) & (done < W_CHUNKS)

        @pl.when(active)
        def _():
            finish_chunk(nm, done, 1 - parity)

        return parity, jnp.where(active, done + 1, done), nm

    lax.fori_loop(0, n, body, (jnp.int32(1), jnp.int32(0), meta_ref[0]))
    lax.fori_loop(jnp.maximum(n - n_out, 0), n, lambda g, c: (out_copy(g).wait(), c)[1], 0)

    n_blocks = dst_ref.shape[0] // MOE_BLOCK
    n_used = n // n_j
    out_buf[0] = jnp.zeros(out_buf.shape[1:], out_buf.dtype)
    for jt in range(n_j):
        def tail_copy(b, jt=jt):
            r = pl.ds(pl.multiple_of(b * MOE_BLOCK, MOE_BLOCK), MOE_BLOCK)
            return pltpu.make_async_copy(out_buf.at[0], dst_ref.at[r, jt * tn:(jt + 1) * tn], out_sem.at[0])

        lax.fori_loop(n_used, n_blocks, lambda b, c: (tail_copy(b).start(), c)[1], 0)
        lax.fori_loop(n_used, n_blocks, lambda b, c: (tail_copy(b).wait(), c)[1], 0)


def _expert_up_kernel(meta_ref, row_ref, gend_ref, n_ref, x_ref, nw_ref, bg_ref, bu_ref, wg_ref, wu_ref, act_ref,
                      ring, wb_ref, x_buf, a_buf, ring_sem, in_sem, out_sem):
    n_j = D_FF // MOE_UP_TN

    def compute(x, w, m):
        tile = (m >> 3) * n_j + ((m >> 1) & 3)
        xb = _rms(x, nw_ref[...]).astype(BF16)
        g = jnp.dot(xb, w[0], preferred_element_type=F32) + bg_ref[pl.ds(tile, 1), :]
        u = jnp.dot(xb, w[1], preferred_element_type=F32) + bu_ref[pl.ds(tile, 1), :]
        g = jnp.minimum(g, SWIGLU_LIMIT)
        u = jnp.clip(u, -SWIGLU_LIMIT, SWIGLU_LIMIT)
        return g * _sigmoid(SWIGLU_ALPHA * g) * (u + 1.0)

    _item_loop(meta_ref, row_ref, gend_ref, n_ref, x_ref, act_ref, (wg_ref, wu_ref), ring, ring_sem, wb_ref,
               x_buf, a_buf, in_sem, out_sem, compute)


def _expert_down_kernel(meta_ref, row_ref, gend_ref, n_ref, a_ref, bd_ref, wd_ref, y_ref,
                        ring, wb_ref, a_buf, y_buf, ring_sem, in_sem, out_sem):
    n_j = D_MODEL // MOE_DOWN_TN

    def compute(a, w, m):
        tile = (m >> 3) * n_j + ((m >> 1) & 3)
        return jnp.dot(a, w[0], preferred_element_type=F32) + bd_ref[pl.ds(tile, 1), :]

    _item_loop(meta_ref, row_ref, gend_ref, n_ref, a_ref, y_ref, (wd_ref,), ring, ring_sem, wb_ref,
               a_buf, y_buf, in_sem, out_sem, compute)


def _experts(x_rows, padded_start, padded_end, norm_w, w_gate, b_gate, w_up, b_up, w_down, b_down):
    n_rows = x_rows.shape[0]
    n_blocks = n_rows // MOE_BLOCK
    any_spec = pl.BlockSpec(memory_space=pl.ANY)
    sems = [pltpu.SemaphoreType.DMA((W_RING,)), pltpu.SemaphoreType.DMA((MOE_IN_BUFS,)),
            pltpu.SemaphoreType.DMA((MOE_OUT_BUFS,))]

    def full(shape):
        return pl.BlockSpec(shape, lambda i, *_: (0,) * len(shape))

    n_j = D_FF // MOE_UP_TN
    act = pl.pallas_call(
        _expert_up_kernel,
        grid_spec=pltpu.PrefetchScalarGridSpec(
            num_scalar_prefetch=4, grid=(1,),
            in_specs=[any_spec, full((1, D_MODEL)), full((N_EXPERTS * n_j, MOE_UP_TN)),
                      full((N_EXPERTS * n_j, MOE_UP_TN)), any_spec, any_spec],
            out_specs=any_spec,
            scratch_shapes=[pltpu.VMEM((W_RING, 2 * D_MODEL // W_CHUNKS, MOE_UP_TN), F32),
                            pltpu.VMEM((2, 2, D_MODEL, MOE_UP_TN), BF16),
                            pltpu.VMEM((MOE_IN_BUFS, MOE_BLOCK, D_MODEL), F32),
                            pltpu.VMEM((MOE_OUT_BUFS, MOE_BLOCK, MOE_UP_TN), BF16)] + sems),
        out_shape=jax.ShapeDtypeStruct((n_rows, D_FF), BF16),
        compiler_params=_cparams(("arbitrary",)),
    )(*_item_tables(padded_start, padded_end, n_j, n_blocks * n_j), x_rows, norm_w.reshape(1, D_MODEL),
      b_gate.reshape(N_EXPERTS * n_j, MOE_UP_TN), b_up.reshape(N_EXPERTS * n_j, MOE_UP_TN), w_gate, w_up)

    n_j = D_MODEL // MOE_DOWN_TN
    return pl.pallas_call(
        _expert_down_kernel,
        grid_spec=pltpu.PrefetchScalarGridSpec(
            num_scalar_prefetch=4, grid=(1,),
            in_specs=[any_spec, full((N_EXPERTS * n_j, MOE_DOWN_TN)), any_spec],
            out_specs=any_spec,
            scratch_shapes=[pltpu.VMEM((W_RING, D_FF // W_CHUNKS, MOE_DOWN_TN), F32),
                            pltpu.VMEM((2, 1, D_FF, MOE_DOWN_TN), BF16),
                            pltpu.VMEM((MOE_IN_BUFS, MOE_BLOCK, D_FF), BF16),
                            pltpu.VMEM((MOE_OUT_BUFS, MOE_BLOCK, MOE_DOWN_TN), F32)] + sems),
        out_shape=jax.ShapeDtypeStruct((n_rows, D_MODEL), F32),
        compiler_params=_cparams(("arbitrary",)),
    )(*_item_tables(padded_start, padded_end, n_j, n_blocks * n_j), act,
      b_down.reshape(N_EXPERTS * n_j, MOE_DOWN_TN), w_down)


def _combine_kernel(dest_ref, h_ref, gate_ref, nw_ref, y_ref, o_ref, buf_ref, sem):
    def issue(r, carry):
        for k in range(TOP_K):
            d = dest_ref[0, 0, r * TOP_K + k]
            pltpu.make_async_copy(y_ref.at[pl.ds(d, 1)], buf_ref.at[k, pl.ds(r, 1)], sem).start(priority=k % 2)
        return carry

    lax.fori_loop(0, COMBINE_TOKENS, issue, 0)
    pltpu.make_async_copy(buf_ref, buf_ref, sem).wait()
    acc = h_ref[...]
    gate = gate_ref[...]
    for k in range(TOP_K):
        acc = acc + gate[:, k:k + 1] * buf_ref[k]
    o_ref[...] = _rms(acc, nw_ref[...])


def _combine(h, y_rows, dest, gate_pad, norm_w):
    t = h.shape[0]
    steps = t // COMBINE_TOKENS
    return pl.pallas_call(
        _combine_kernel,
        grid=(steps,),
        in_specs=[pl.BlockSpec((1, 1, COMBINE_TOKENS * TOP_K), lambda i: (i, 0, 0), memory_space=pltpu.SMEM),
                  pl.BlockSpec((COMBINE_TOKENS, D_MODEL), lambda i: (i, 0)),
                  pl.BlockSpec((COMBINE_TOKENS, LANES), lambda i: (i, 0)),
                  pl.BlockSpec((1, D_MODEL), lambda i: (0, 0)),
                  pl.BlockSpec(memory_space=pl.ANY)],
        out_specs=pl.BlockSpec((COMBINE_TOKENS, D_MODEL), lambda i: (i, 0)),
        out_shape=jax.ShapeDtypeStruct((t, D_MODEL), F32),
        scratch_shapes=[pltpu.VMEM((TOP_K, COMBINE_TOKENS, D_MODEL), F32), pltpu.SemaphoreType.DMA(())],
        compiler_params=_cparams(("arbitrary",)),
    )(dest.reshape(steps, 1, COMBINE_TOKENS * TOP_K), h, gate_pad, norm_w.reshape(1, D_MODEL), y_rows)


def _moe(h, norm_ffn, w_router, b_router, w_gate, b_gate, w_up, b_up, w_down, b_down, norm_final):
    t = h.shape[0]
    tk = t * TOP_K
    n_blocks = (tk + N_EXPERTS * (MOE_BLOCK - 1) + MOE_BLOCK - 1) // MOE_BLOCK
    n_rows = n_blocks * MOE_BLOCK
    idx_pad, gate_pad, pos_pad, cnt_pad = _router(h, norm_ffn, w_router, b_router)
    counts = cnt_pad[0, :N_EXPERTS].astype(jnp.int32)
    padded = (counts + MOE_BLOCK - 1) // MOE_BLOCK * MOE_BLOCK
    padded_end = jnp.cumsum(padded).astype(jnp.int32)
    padded_start = padded_end - padded
    start_pad = jnp.pad(padded_start.astype(F32), (0, LANES - N_EXPERTS)).reshape(1, LANES)
    dest = _dest_rows(idx_pad, pos_pad, start_pad)[:, :TOP_K].reshape(tk)
    x_rows = _scatter_rows(h, dest, padded_end, n_rows)
    y_rows = _experts(x_rows, padded_start, padded_end, norm_ffn, w_gate, b_gate, w_up, b_up, w_down, b_down)
    return _combine(h, y_rows, dest, gate_pad, norm_final)


def kernel(x, mem, norm_mix, w_in, conv_w, conv_b, dt_bias, a_log, d_skip, ssd_norm, pool_w, pool_scale,
           w_out, norm_xattn, norm_mem, w_q, w_kv, w_o, norm_ffn, w_router, b_router, w_gate, b_gate,
           w_up, b_up, w_down, b_down, norm_final):
    batch, seq, d = x.shape
    t = batch * seq
    h = x.reshape(t, d)
    depth = norm_mix.shape[0]
    assert depth == 1, "the final norm is fused into the MoE combine of the single layer"
    for l in range(depth):
        dt_lo = D_SSD + D_CONV
        w_main = jnp.concatenate([w_in[l][:, :dt_lo], w_in[l][:, dt_lo + SSD_HEADS:]], axis=1)
        w_dt = jnp.pad(w_in[l][:, dt_lo:dt_lo + SSD_HEADS], ((0, 0), (0, LANES - SSD_HEADS)))
        proj = _mm(h, w_main, norm_w=norm_mix[l])
        dtp = _mm(h, w_dt, norm_w=norm_mix[l])
        mix = _mixer(proj, dtp, conv_w[l], conv_b[l], dt_bias[l], a_log[l], d_skip[l], ssd_norm[l],
                     pool_w[l], pool_scale[l], batch, seq)
        h = _mm(mix, w_out[l], res=h)

        q = _mm(h, w_q[l], norm_w=norm_xattn[l], out_dtype=BF16)
        kv = _mm(mem.reshape(batch * N_MEM, d), w_kv[l], norm_w=norm_mem[l], out_dtype=BF16)
        o = _attn(q, kv, batch, seq)
        h = _mm(o, w_o[l], res=h)

        out = _moe(h, norm_ffn[l], w_router[l], b_router[l], w_gate[l], b_gate[l], w_up[l], b_up[l],
                   w_down[l], b_down[l], norm_final)
    return out.reshape(batch, seq, d)
```

```python
import functools

import jax
import jax.numpy as jnp
from jax import lax
from jax.experimental import pallas as pl
from jax.experimental.pallas import tpu as pltpu

F32 = jnp.float32
BF16 = jnp.bfloat16

D_MODEL = 2048
D_SSD = 1024
SSD_HEAD_DIM = 64
SSD_HEADS = 16
SSD_GROUPS = 4
HEADS_PER_GROUP = SSD_HEADS // SSD_GROUPS
D_STATE = 128
CONV_WIDTH = 4
CHUNK = 128
D_CONV = D_SSD + 2 * SSD_GROUPS * D_STATE
D_POOL = 1024
POOL_WINDOWS = (2, 4, 8, 16)
POOL_GROUP_DIM = 256
N_MEM = 256
XATTN_HEADS = 4
XATTN_HEAD_DIM = 512
N_EXPERTS = 32
TOP_K = 4
D_FF = 2048
SWIGLU_LIMIT = 7.0
SWIGLU_ALPHA = 1.702
EPS = 1e-5

LANES = 128
SUBLANES = 8
VMEM_LIMIT = 56 * 1024 * 1024
NEG_BIG = -1e30

MOE_BLOCK = 256
MOE_UP_TN = 1024
MOE_DOWN_TN = 2048
MOE_IN_BUFS = 3
MOE_OUT_BUFS = 2
W_CHUNKS = 4
W_RING = 3
ROUTER_TOKENS = 512
SCATTER_TOKENS = 256
COMBINE_TOKENS = 128
ROW_DMA_PRIORITY = 1


def _cparams(sem):
    return pltpu.CompilerParams(dimension_semantics=sem, vmem_limit_bytes=VMEM_LIMIT)


def _rms(x, w):
    ms = jnp.mean(x * x, axis=-1, keepdims=True)
    return x * lax.rsqrt(ms + EPS) * w


def _mm_kernel(*refs, has_norm, has_res):
    it = iter(refs)
    a_ref = next(it)
    w_ref = next(it)
    nw_ref = next(it) if has_norm else None
    r_ref = next(it) if has_res else None
    o_ref = next(it)
    wb_ref = next(it)

    @pl.when(pl.program_id(1) == 0)
    def _():
        wb_ref[...] = w_ref[...].astype(BF16)

    a = a_ref[...]
    if has_norm:
        a = _rms(a.astype(F32), nw_ref[...])
    acc = jnp.dot(a.astype(BF16), wb_ref[...], preferred_element_type=F32)
    if has_res:
        acc = acc + r_ref[...]
    o_ref[...] = acc.astype(o_ref.dtype)


def _mm(a, w, norm_w=None, res=None, out_dtype=F32, tm=512, tn=1024):
    m, k = a.shape
    n = w.shape[1]
    tn = min(tn, n)
    tm = min(tm, m)
    assert m % tm == 0 and n % tn == 0
    in_specs = [pl.BlockSpec((tm, k), lambda j, i: (i, 0)),
                pl.BlockSpec((k, tn), lambda j, i: (0, j))]
    args = [a, w]
    if norm_w is not None:
        in_specs.append(pl.BlockSpec((1, k), lambda j, i: (0, 0)))
        args.append(norm_w.reshape(1, k))
    if res is not None:
        in_specs.append(pl.BlockSpec((tm, tn), lambda j, i: (i, j)))
        args.append(res)
    return pl.pallas_call(
        functools.partial(_mm_kernel, has_norm=norm_w is not None, has_res=res is not None),
        grid=(n // tn, m // tm),
        in_specs=in_specs,
        out_specs=pl.BlockSpec((tm, tn), lambda j, i: (i, j)),
        out_shape=jax.ShapeDtypeStruct((m, n), out_dtype),
        scratch_shapes=[pltpu.VMEM((k, tn), BF16)],
        compiler_params=_cparams(("arbitrary", "arbitrary")),
    )(*args)


def _sigmoid(x):
    return 1.0 / (1.0 + jnp.exp(-x))


def _shift_rows(cur, prev, j, rows):
    return jnp.where(rows < j, pltpu.roll(prev, j, 0), pltpu.roll(cur, j, 0))


def _conv_silu(cur, prev, w_ref, b_ref, col0, rows):
    width = cur.shape[1]
    acc = cur * w_ref[CONV_WIDTH - 1:CONV_WIDTH, col0:col0 + width] + b_ref[:, col0:col0 + width]
    for j in range(1, CONV_WIDTH):
        k = CONV_WIDTH - 1 - j
        acc = acc + _shift_rows(cur, prev, j, rows) * w_ref[k:k + 1, col0:col0 + width]
    return acc * _sigmoid(acc)


def _mixer_kernel(z_ref, x_ref, bc_ref, u_ref, dt_ref, convw_ref, convb_ref, dtb_ref, alog_ref,
                  dskip_ref, ssdn_ref, poolw_ref, pscale_ref, out_ref,
                  px_ref, pbc_ref, pu_ref, state_ref):
    c = pl.program_id(1)

    @pl.when(c == 0)
    def _():
        px_ref[...] = jnp.zeros_like(px_ref)
        pbc_ref[...] = jnp.zeros_like(pbc_ref)
        pu_ref[...] = jnp.zeros_like(pu_ref)
        state_ref[...] = jnp.zeros_like(state_ref)

    rows = lax.broadcasted_iota(jnp.int32, (CHUNK, D_SSD), 0)
    x_raw = x_ref[...]
    bc_raw = bc_ref[...]
    xs = _conv_silu(x_raw, px_ref[...], convw_ref, convb_ref, 0, rows)
    bcs = _conv_silu(bc_raw, pbc_ref[...], convw_ref, convb_ref, D_SSD, rows)
    px_ref[...] = x_raw
    pbc_ref[...] = bc_raw

    dt_in = dt_ref[...] + dtb_ref[...]
    dt = jnp.maximum(dt_in, 0.0) + jnp.log(1.0 + jnp.exp(-jnp.abs(dt_in)))
    a = -jnp.exp(alog_ref[...])
    dta = dt * a
    ri = lax.broadcasted_iota(jnp.int32, (CHUNK, CHUNK), 0)
    ci = lax.broadcasted_iota(jnp.int32, (CHUNK, CHUNK), 1)
    causal = ri >= ci
    tril = jnp.where(causal, 1.0, 0.0).astype(F32)
    acs = jnp.dot(tril, dta, preferred_element_type=F32, precision=lax.Precision.HIGHEST)
    acs_t = acs.T
    dt_t = dt.T
    lo = ci < SSD_HEAD_DIM

    y_pairs = []
    for g in range(SSD_GROUPS):
        bg = bcs[:, g * D_STATE:(g + 1) * D_STATE].astype(BF16)
        cg = bcs[:, SSD_GROUPS * D_STATE + g * D_STATE:SSD_GROUPS * D_STATE + (g + 1) * D_STATE].astype(BF16)
        cb = lax.dot_general(cg, bg, (((1,), (1,)), ((), ())), preferred_element_type=F32)
        gw = HEADS_PER_GROUP * SSD_HEAD_DIM
        prev_g = state_ref[g * gw:(g + 1) * gw, :]
        yoff = lax.dot_general(cg, prev_g.astype(BF16), (((1,), (1,)), ((), ())),
                               preferred_element_type=F32)
        xw_parts = []
        for pr in range(HEADS_PER_GROUP // 2):
            pair = g * (HEADS_PER_GROUP // 2) + pr
            xs_pair = xs[:, pair * LANES:(pair + 1) * LANES]
            y_pair = None
            ecols = []
            wcols = []
            for q in range(2):
                h = pair * 2 + q
                col = acs[:, h:h + 1]
                seg = col - acs_t[h:h + 1, :]
                decay = jnp.exp(jnp.where(causal, seg, NEG_BIG))
                mat = (cb * decay * dt_t[h:h + 1, :]).astype(BF16)
                keep = lo if q == 0 else jnp.logical_not(lo)
                xh = jnp.where(keep, xs_pair, 0.0).astype(BF16)
                yd = jnp.dot(mat, xh, preferred_element_type=F32)
                y_pair = yd if y_pair is None else y_pair + yd
                ecols.append(jnp.exp(col))
                wcols.append(dt[:, h:h + 1] * jnp.exp(acs[CHUNK - 1:CHUNK, h:h + 1] - col))
            e_pair = jnp.where(lo, ecols[0], ecols[1])
            w_pair = jnp.where(lo, wcols[0], wcols[1])
            y_pairs.append(y_pair + yoff[:, pr * LANES:(pr + 1) * LANES] * e_pair)
            xw_parts.append(xs_pair * w_pair)
        xw = jnp.concatenate(xw_parts, axis=1).astype(BF16)
        s_new = lax.dot_general(xw, bg, (((0,), (0,)), ((), ())), preferred_element_type=F32)
        for r in range(HEADS_PER_GROUP):
            h = g * HEADS_PER_GROUP + r
            dec = jnp.exp(acs[CHUNK - 1:CHUNK, h:h + 1])
            lo_r = h * SSD_HEAD_DIM
            state_ref[lo_r:lo_r + SSD_HEAD_DIM, :] = (
                state_ref[lo_r:lo_r + SSD_HEAD_DIM, :] * dec
                + s_new[r * SSD_HEAD_DIM:(r + 1) * SSD_HEAD_DIM, :])

    y = jnp.concatenate(y_pairs, axis=1)
    y = y + dskip_ref[...] * xs
    z = z_ref[...]
    y = y * (z * _sigmoid(z))
    gdim = D_SSD // SSD_GROUPS
    for g in range(SSD_GROUPS):
        yg = _rms(y[:, g * gdim:(g + 1) * gdim], ssdn_ref[:, g * gdim:(g + 1) * gdim])
        out_ref[:, g * gdim:(g + 1) * gdim] = yg.astype(out_ref.dtype)

    u_cur = u_ref[...]
    u_prev = pu_ref[...]
    pos = (c * CHUNK + lax.broadcasted_iota(jnp.int32, (CHUNK, POOL_GROUP_DIM), 0) + 1).astype(F32)
    for g, win in enumerate(POOL_WINDOWS):
        sl = slice(g * POOL_GROUP_DIM, (g + 1) * POOL_GROUP_DIM)
        ug = u_cur[:, sl]
        ext = jnp.concatenate([u_prev[:, sl], ug], axis=0)
        step = 1
        while step < win:
            ext = ext + pltpu.roll(ext, step, 0)
            step *= 2
        pooled = ext[CHUNK:, :] / jnp.minimum(pos, float(win)) - ug
        mixed = jnp.dot(pooled.astype(BF16), poolw_ref[g].astype(BF16), preferred_element_type=F32)
        out_ref[:, D_SSD + g * POOL_GROUP_DIM:D_SSD + (g + 1) * POOL_GROUP_DIM] = (
            mixed * pscale_ref[:, sl]).astype(out_ref.dtype)
    pu_ref[...] = u_cur


def _mixer(proj, dtp, conv_w, conv_b, dt_bias, a_log, d_skip, ssd_norm, pool_w, pool_scale, batch, seq):
    nchunk = seq // CHUNK
    row_map = lambda col: (lambda b, c: (b * nchunk + c, col))
    const2 = lambda b, c: (0, 0)
    pad = LANES - SSD_HEADS
    dtb = jnp.pad(dt_bias, (0, pad)).reshape(1, LANES)
    alog = jnp.pad(a_log, (0, pad)).reshape(1, LANES)
    dskip = jnp.repeat(d_skip, SSD_HEAD_DIM).reshape(1, D_SSD)
    return pl.pallas_call(
        _mixer_kernel,
        grid=(batch, nchunk),
        in_specs=[
            pl.BlockSpec((CHUNK, D_SSD), row_map(0)),
            pl.BlockSpec((CHUNK, D_SSD), row_map(1)),
            pl.BlockSpec((CHUNK, D_SSD), row_map(2)),
            pl.BlockSpec((CHUNK, D_POOL), row_map(3)),
            pl.BlockSpec((CHUNK, LANES), row_map(0)),
            pl.BlockSpec((CONV_WIDTH, D_CONV), const2),
            pl.BlockSpec((1, D_CONV), const2),
            pl.BlockSpec((1, LANES), const2),
            pl.BlockSpec((1, LANES), const2),
            pl.BlockSpec((1, D_SSD), const2),
            pl.BlockSpec((1, D_SSD), const2),
            pl.BlockSpec((len(POOL_WINDOWS), POOL_GROUP_DIM, POOL_GROUP_DIM), lambda b, c: (0, 0, 0)),
            pl.BlockSpec((1, D_POOL), const2),
        ],
        out_specs=pl.BlockSpec((CHUNK, D_MODEL), lambda b, c: (b * nchunk + c, 0)),
        out_shape=jax.ShapeDtypeStruct((batch * seq, D_MODEL), BF16),
        scratch_shapes=[pltpu.VMEM((CHUNK, D_SSD), F32), pltpu.VMEM((CHUNK, D_SSD), F32),
                        pltpu.VMEM((CHUNK, D_POOL), F32), pltpu.VMEM((SSD_HEADS * SSD_HEAD_DIM, D_STATE), F32)],
        compiler_params=_cparams(("arbitrary", "arbitrary")),
    )(proj, proj, proj, proj, dtp, conv_w, conv_b.reshape(1, D_CONV), dtb, alog, dskip,
      ssd_norm.reshape(1, D_SSD), pool_w, pool_scale.reshape(1, D_POOL))


def _attn_kernel(q_ref, k_ref, v_ref, o_ref):
    scale = XATTN_HEAD_DIM ** -0.5
    for h in range(XATTN_HEADS):
        sl = slice(h * XATTN_HEAD_DIM, (h + 1) * XATTN_HEAD_DIM)
        s = lax.dot_general(q_ref[:, sl], k_ref[:, sl], (((1,), (1,)), ((), ())),
                            preferred_element_type=F32) * scale
        s = s - jnp.max(s, axis=-1, keepdims=True)
        p = jnp.exp(s)
        p = p / jnp.sum(p, axis=-1, keepdims=True)
        o_ref[:, sl] = jnp.dot(p.astype(BF16), v_ref[:, sl], preferred_element_type=F32).astype(o_ref.dtype)


def _attn(q, kv, batch, seq, tq=512):
    nq = seq // tq
    return pl.pallas_call(
        _attn_kernel,
        grid=(batch, nq),
        in_specs=[pl.BlockSpec((tq, D_MODEL), lambda b, i: (b * nq + i, 0)),
                  pl.BlockSpec((N_MEM, D_MODEL), lambda b, i: (b, 0)),
                  pl.BlockSpec((N_MEM, D_MODEL), lambda b, i: (b, 1))],
        out_specs=pl.BlockSpec((tq, D_MODEL), lambda b, i: (b * nq + i, 0)),
        out_shape=jax.ShapeDtypeStruct((batch * seq, D_MODEL), BF16),
        compiler_params=_cparams(("arbitrary", "arbitrary")),
    )(q, kv, kv)


def _router_kernel(h_ref, nw_ref, wr_ref, br_ref, idx_ref, gate_ref, pos_ref, cnt_ref, run_ref):
    @pl.when(pl.program_id(0) == 0)
    def _():
        run_ref[...] = jnp.zeros_like(run_ref)

    hn = _rms(h_ref[...], nw_ref[...])
    logits = jnp.dot(hn, wr_ref[...], preferred_element_type=F32,
                     precision=lax.Precision.HIGHEST) + br_ref[...]
    tm = logits.shape[0]
    lane = lax.broadcasted_iota(jnp.int32, logits.shape, 1)
    vals = []
    hots = []
    idx_out = jnp.zeros(logits.shape, jnp.int32)
    for k in range(TOP_K):
        m = jnp.max(logits, axis=-1, keepdims=True)
        idx = jnp.min(jnp.where(logits == m, lane, LANES), axis=-1, keepdims=True)
        hit = lane == idx
        idx_out = jnp.where(lane == k, idx, idx_out)
        logits = jnp.where(hit, -jnp.inf, logits)
        vals.append(m)
        hots.append(jnp.where(hit, 1.0, 0.0))
    exps = [jnp.exp(v - vals[0]) for v in vals]
    den = exps[0] + exps[1] + exps[2] + exps[3]
    gate_out = jnp.zeros(logits.shape, F32)
    for k in range(TOP_K):
        gate_out = jnp.where(lane == k, exps[k] / den, gate_out)
    idx_ref[...] = idx_out
    gate_ref[...] = gate_out

    hot = hots[0] + hots[1] + hots[2] + hots[3]
    ri = lax.broadcasted_iota(jnp.int32, (tm, tm), 0)
    ci = lax.broadcasted_iota(jnp.int32, (tm, tm), 1)
    before = jnp.where(ri > ci, 1.0, 0.0).astype(BF16)
    ahead = jnp.dot(before, hot.astype(BF16), preferred_element_type=F32) + run_ref[...]
    pos_out = jnp.zeros(logits.shape, F32)
    for k in range(TOP_K):
        pos_out = jnp.where(lane == k, jnp.sum(ahead * hots[k], axis=-1, keepdims=True), pos_out)
    pos_ref[...] = pos_out
    run_ref[...] = run_ref[...] + jnp.sum(hot, axis=0, keepdims=True)
    cnt_ref[...] = run_ref[...]


def _router(h, norm_w, w_router, b_router):
    t = h.shape[0]
    tm = ROUTER_TOKENS
    pad = LANES - N_EXPERTS
    wr = jnp.pad(w_router, ((0, 0), (0, pad)))
    br = jnp.pad(b_router, (0, pad), constant_values=NEG_BIG).reshape(1, LANES)
    tok_spec = pl.BlockSpec((tm, LANES), lambda i: (i, 0))
    return pl.pallas_call(
        _router_kernel,
        grid=(t // tm,),
        in_specs=[pl.BlockSpec((tm, D_MODEL), lambda i: (i, 0)),
                  pl.BlockSpec((1, D_MODEL), lambda i: (0, 0)),
                  pl.BlockSpec((D_MODEL, LANES), lambda i: (0, 0)),
                  pl.BlockSpec((1, LANES), lambda i: (0, 0))],
        out_specs=[tok_spec, tok_spec, tok_spec, pl.BlockSpec((1, LANES), lambda i: (0, 0))],
        out_shape=[jax.ShapeDtypeStruct((t, LANES), jnp.int32),
                   jax.ShapeDtypeStruct((t, LANES), F32),
                   jax.ShapeDtypeStruct((t, LANES), F32),
                   jax.ShapeDtypeStruct((1, LANES), F32)],
        scratch_shapes=[pltpu.VMEM((1, LANES), F32)],
        compiler_params=_cparams(("arbitrary",)),
    )(h, norm_w.reshape(1, D_MODEL), wr, br)


def _dest_kernel(idx_ref, pos_ref, cnt_ref, dest_ref, bounds_ref):
    cnt = jnp.broadcast_to(cnt_ref[...], (SUBLANES, LANES))
    padded = jnp.floor((cnt + (MOE_BLOCK - 1)) * (1.0 / MOE_BLOCK)) * MOE_BLOCK
    ri = lax.broadcasted_iota(jnp.int32, (LANES, LANES), 0)
    ci = lax.broadcasted_iota(jnp.int32, (LANES, LANES), 1)
    upto = jnp.where(ri <= ci, 1.0, 0.0).astype(F32)
    seg_end = jnp.dot(padded, upto, preferred_element_type=F32, precision=lax.Precision.HIGHEST)
    seg_start = seg_end - padded
    row = lax.broadcasted_iota(jnp.int32, (SUBLANES, LANES), 0)
    bounds_ref[...] = jnp.where(row == 0, seg_start, jnp.where(row == 1, seg_end, 0.0)).astype(jnp.int32)

    idx = idx_ref[...]
    lane = lax.broadcasted_iota(jnp.int32, idx.shape, 1)
    out = pos_ref[...]
    for k in range(TOP_K):
        hit = lane == idx[:, k:k + 1]
        base = jnp.sum(jnp.where(hit, seg_start[0:1, :], 0.0), axis=-1, keepdims=True)
        out = jnp.where(lane == k, out + base, out)
    dest_ref[...] = out.T[0:SUBLANES, :].astype(jnp.int32)


def _dest_rows(idx_pad, pos_pad, cnt_pad):
    t = idx_pad.shape[0]
    tm = ROUTER_TOKENS
    tok_spec = pl.BlockSpec((tm, LANES), lambda i: (i, 0))
    return pl.pallas_call(
        _dest_kernel,
        grid=(t // tm,),
        in_specs=[tok_spec, tok_spec, pl.BlockSpec((1, LANES), lambda i: (0, 0))],
        out_specs=[pl.BlockSpec((SUBLANES, tm), lambda i: (0, i)),
                   pl.BlockSpec((SUBLANES, LANES), lambda i: (0, 0))],
        out_shape=[jax.ShapeDtypeStruct((SUBLANES, t), jnp.int32),
                   jax.ShapeDtypeStruct((SUBLANES, LANES), jnp.int32)],
        compiler_params=_cparams(("arbitrary",)),
    )(idx_pad, pos_pad, cnt_pad)


def _scatter_kernel(dest_ref, bounds_ref, h_ref, x_ref, zero_ref, zsem, sem):
    n_blocks = x_ref.shape[0] // MOE_BLOCK

    @pl.when(pl.program_id(0) == 0)
    def _():
        zero_ref[...] = jnp.zeros_like(zero_ref)
        n_used = bounds_ref[1, N_EXPERTS - 1] // MOE_BLOCK

        def zero_copy(b):
            return pltpu.make_async_copy(zero_ref, x_ref.at[pl.ds(pl.multiple_of(b * MOE_BLOCK, MOE_BLOCK),
                                                                   MOE_BLOCK)], zsem)

        def last_block(e):
            return bounds_ref[1, e] // MOE_BLOCK - 1, bounds_ref[1, e] > bounds_ref[0, e]

        def start_e(e, carry):
            b, nonempty = last_block(e)

            @pl.when(nonempty)
            def _():
                zero_copy(b).start()
            return carry

        def wait_e(e, carry):
            b, nonempty = last_block(e)

            @pl.when(nonempty)
            def _():
                zero_copy(b).wait()
            return carry

        lax.fori_loop(0, N_EXPERTS, start_e, 0)
        lax.fori_loop(n_used, n_blocks, lambda b, c: (zero_copy(b).start(), c)[1], 0)
        lax.fori_loop(0, N_EXPERTS, wait_e, 0)
        lax.fori_loop(n_used, n_blocks, lambda b, c: (zero_copy(b).wait(), c)[1], 0)

    def issue(r, carry):
        for k in range(TOP_K):
            d = dest_ref[k, r]
            pltpu.make_async_copy(h_ref.at[pl.ds(r, 1)], x_ref.at[pl.ds(d, 1)], sem).start(priority=k % 2)
        return carry

    lax.fori_loop(0, SCATTER_TOKENS, issue, 0)
    n_sent = SCATTER_TOKENS * TOP_K
    pltpu.make_async_copy(x_ref.at[pl.ds(0, n_sent)], x_ref.at[pl.ds(0, n_sent)], sem).wait()


def _scatter_rows(h, dest_t, bounds, n_rows):
    t = h.shape[0]
    steps = t // SCATTER_TOKENS
    return pl.pallas_call(
        _scatter_kernel,
        grid=(steps,),
        in_specs=[pl.BlockSpec((SUBLANES, SCATTER_TOKENS), lambda i: (0, i), memory_space=pltpu.SMEM),
                  pl.BlockSpec(memory_space=pltpu.SMEM),
                  pl.BlockSpec((SCATTER_TOKENS, D_MODEL), lambda i: (i, 0))],
        out_specs=pl.BlockSpec(memory_space=pl.ANY),
        out_shape=jax.ShapeDtypeStruct((n_rows, D_MODEL), F32),
        scratch_shapes=[pltpu.VMEM((MOE_BLOCK, D_MODEL), F32), pltpu.SemaphoreType.DMA(()),
                        pltpu.SemaphoreType.DMA(())],
        compiler_params=_cparams(("arbitrary",)),
    )(dest_t, bounds, h)


def _item_tables(bounds, n_j, n_items_max):
    first_e = bounds[0] // MOE_BLOCK * n_j
    end_e = bounds[1] // MOE_BLOCK * n_j
    blocks_e = (end_e - first_e) // n_j
    n_items = end_e[N_EXPERTS - 1]
    g = jnp.minimum(jnp.arange(n_items_max, dtype=jnp.int32), n_items - 1)
    e = jnp.sum((g[:, None] >= end_e[None, :N_EXPERTS]).astype(jnp.int32), axis=1)
    own = e[:, None] == jnp.arange(LANES, dtype=jnp.int32)[None, :]
    pick = lambda v: jnp.sum(jnp.where(own, v[None, :], 0), axis=1)
    first = pick(first_e)
    nb = jnp.maximum(pick(blocks_e), 1)
    q = g - first
    j = q // nb
    i = q % nb
    row = first // n_j + i
    meta = e * 8 + j * 2 + (i == 0).astype(jnp.int32)
    group_end = first + (j + 1) * nb
    return (meta.astype(jnp.int32), row.astype(jnp.int32), group_end.astype(jnp.int32),
            n_items.reshape(1).astype(jnp.int32))


def _item_loop(meta_ref, row_ref, gend_ref, n_ref, src_ref, dst_ref, w_refs, ring, ring_sem, wb_ref,
               in_buf, out_buf, in_sem, out_sem, compute):
    n = n_ref[0]
    n_in = in_buf.shape[0]
    n_out = out_buf.shape[0]
    tn = out_buf.shape[2]
    n_j = dst_ref.shape[1] // tn
    n_mats = len(w_refs)
    cpm = W_CHUNKS // n_mats
    kc = w_refs[0].shape[1] // cpm

    def rows(g):
        return pl.ds(pl.multiple_of(row_ref[g] * MOE_BLOCK, MOE_BLOCK), MOE_BLOCK)

    def cols(m):
        return pl.ds(pl.multiple_of(((m >> 1) & 3) * tn, tn), tn)

    def in_copy(g):
        slot = g % n_in
        return pltpu.make_async_copy(src_ref.at[rows(g)], in_buf.at[slot], in_sem.at[slot])

    def out_copy(g):
        slot = g % n_out
        return pltpu.make_async_copy(out_buf.at[slot], dst_ref.at[rows(g), cols(meta_ref[g])], out_sem.at[slot])

    def krows(c):
        r0 = (c % cpm) * kc
        return pl.ds(r0 if isinstance(c, int) else pl.multiple_of(r0, kc), kc)

    def chunk_copy(m, c, k):
        slot = c % W_RING
        return pltpu.make_async_copy(w_refs[k].at[m >> 3, krows(c), cols(m)], ring.at[slot], ring_sem.at[slot])

    def for_matrix(c, fn):
        if isinstance(c, int):
            return fn(c // cpm)
        for k in range(n_mats):
            @pl.when(c // cpm == k)
            def _(k=k):
                fn(k)

    def start_chunk(m, c):
        for_matrix(c, lambda k: chunk_copy(m, c, k).start())

    def finish_chunk(m, c, wslot):
        def fn(k):
            chunk_copy(m, c, k).wait()
            wb_ref[wslot, k, krows(c), :] = ring[c % W_RING].astype(BF16)
        for_matrix(c, fn)

        @pl.when(c + W_RING < W_CHUNKS)
        def _():
            start_chunk(m, c + W_RING)

    for c0 in range(W_RING):
        start_chunk(meta_ref[0], c0)
    for g0 in range(n_in - 1):
        @pl.when(g0 < n)
        def _():
            in_copy(g0).start(priority=ROW_DMA_PRIORITY)

    def body(g, carry):
        parity, done, nm = carry
        m = meta_ref[g]
        first = (m & 1) == 1

        @pl.when(first)
        def _():
            lax.fori_loop(done, W_CHUNKS, lambda c, z: (finish_chunk(m, c, 1 - parity), z)[1], 0)
            nxt = gend_ref[g]

            @pl.when(nxt < n)
            def _():
                for c0 in range(W_RING):
                    start_chunk(meta_ref[nxt], c0)

        nxt = gend_ref[g]
        parity = jnp.where(first, 1 - parity, parity)
        done = jnp.where(first, 0, done)
        nm = jnp.where(first, jnp.where(nxt < n, meta_ref[jnp.minimum(nxt, n - 1)], -1), nm)

        in_copy(g).wait()

        @pl.when(g + n_in - 1 < n)
        def _():
            in_copy(g + n_in - 1).start(priority=ROW_DMA_PRIORITY)

        @pl.when(g >= n_out)
        def _():
            out_copy(g - n_out).wait()

        out_buf[g % n_out] = compute(in_buf[g % n_in], wb_ref.at[parity], m).astype(out_buf.dtype)
        out_copy(g).start(priority=ROW_DMA_PRIORITY)

        active = (nm >= 0) & (done < W_CHUNKS)

        @pl.when(active)
        def _():
            finish_chunk(nm, done, 1 - parity)

        return parity, jnp.where(active, done + 1, done), nm

    lax.fori_loop(0, n, body, (jnp.int32(1), jnp.int32(0), meta_ref[0]))
    lax.fori_loop(jnp.maximum(n - n_out, 0), n, lambda g, c: (out_copy(g).wait(), c)[1], 0)

    n_blocks = dst_ref.shape[0] // MOE_BLOCK
    n_used = n // n_j
    out_buf[0] = jnp.zeros(out_buf.shape[1:], out_buf.dtype)
    for jt in range(n_j):
        def tail_copy(b, jt=jt):
            r = pl.ds(pl.multiple_of(b * MOE_BLOCK, MOE_BLOCK), MOE_BLOCK)
            return pltpu.make_async_copy(out_buf.at[0], dst_ref.at[r, jt * tn:(jt + 1) * tn], out_sem.at[0])

        lax.fori_loop(n_used, n_blocks, lambda b, c: (tail_copy(b).start(), c)[1], 0)
        lax.fori_loop(n_used, n_blocks, lambda b, c: (tail_copy(b).wait(), c)[1], 0)


def _expert_up_kernel(meta_ref, row_ref, gend_ref, n_ref, x_ref, nw_ref, bg_ref, bu_ref, wg_ref, wu_ref, act_ref,
                      ring, wb_ref, x_buf, a_buf, ring_sem, in_sem, out_sem):
    n_j = D_FF // MOE_UP_TN

    def compute(x, w, m):
        tile = (m >> 3) * n_j + ((m >> 1) & 3)
        xb = _rms(x, nw_ref[...]).astype(BF16)
        g = jnp.dot(xb, w[0], preferred_element_type=F32) + bg_ref[pl.ds(tile, 1), :]
        u = jnp.dot(xb, w[1], preferred_element_type=F32) + bu_ref[pl.ds(tile, 1), :]
        g = jnp.minimum(g, SWIGLU_LIMIT)
        u = jnp.clip(u, -SWIGLU_LIMIT, SWIGLU_LIMIT)
        return g * _sigmoid(SWIGLU_ALPHA * g) * (u + 1.0)

    _item_loop(meta_ref, row_ref, gend_ref, n_ref, x_ref, act_ref, (wg_ref, wu_ref), ring, ring_sem, wb_ref,
               x_buf, a_buf, in_sem, out_sem, compute)


def _expert_down_kernel(meta_ref, row_ref, gend_ref, n_ref, a_ref, bd_ref, wd_ref, y_ref,
                        ring, wb_ref, a_buf, y_buf, ring_sem, in_sem, out_sem):
    n_j = D_MODEL // MOE_DOWN_TN

    def compute(a, w, m):
        tile = (m >> 3) * n_j + ((m >> 1) & 3)
        return jnp.dot(a, w[0], preferred_element_type=F32) + bd_ref[pl.ds(tile, 1), :]

    _item_loop(meta_ref, row_ref, gend_ref, n_ref, a_ref, y_ref, (wd_ref,), ring, ring_sem, wb_ref,
               a_buf, y_buf, in_sem, out_sem, compute)


def _experts(x_rows, bounds, norm_w, w_gate, b_gate, w_up, b_up, w_down, b_down):
    n_rows = x_rows.shape[0]
    n_blocks = n_rows // MOE_BLOCK
    any_spec = pl.BlockSpec(memory_space=pl.ANY)
    sems = [pltpu.SemaphoreType.DMA((W_RING,)), pltpu.SemaphoreType.DMA((MOE_IN_BUFS,)),
            pltpu.SemaphoreType.DMA((MOE_OUT_BUFS,))]

    def full(shape):
        return pl.BlockSpec(shape, lambda i, *_: (0,) * len(shape))

    n_j = D_FF // MOE_UP_TN
    act = pl.pallas_call(
        _expert_up_kernel,
        grid_spec=pltpu.PrefetchScalarGridSpec(
            num_scalar_prefetch=4, grid=(1,),
            in_specs=[any_spec, full((1, D_MODEL)), full((N_EXPERTS * n_j, MOE_UP_TN)),
                      full((N_EXPERTS * n_j, MOE_UP_TN)), any_spec, any_spec],
            out_specs=any_spec,
            scratch_shapes=[pltpu.VMEM((W_RING, 2 * D_MODEL // W_CHUNKS, MOE_UP_TN), F32),
                            pltpu.VMEM((2, 2, D_MODEL, MOE_UP_TN), BF16),
                            pltpu.VMEM((MOE_IN_BUFS, MOE_BLOCK, D_MODEL), F32),
                            pltpu.VMEM((MOE_OUT_BUFS, MOE_BLOCK, MOE_UP_TN), BF16)] + sems),
        out_shape=jax.ShapeDtypeStruct((n_rows, D_FF), BF16),
        compiler_params=_cparams(("arbitrary",)),
    )(*_item_tables(bounds, n_j, n_blocks * n_j), x_rows, norm_w.reshape(1, D_MODEL),
      b_gate.reshape(N_EXPERTS * n_j, MOE_UP_TN), b_up.reshape(N_EXPERTS * n_j, MOE_UP_TN), w_gate, w_up)

    n_j = D_MODEL // MOE_DOWN_TN
    return pl.pallas_call(
        _expert_down_kernel,
        grid_spec=pltpu.PrefetchScalarGridSpec(
            num_scalar_prefetch=4, grid=(1,),
            in_specs=[any_spec, full((N_EXPERTS * n_j, MOE_DOWN_TN)), any_spec],
            out_specs=any_spec,
            scratch_shapes=[pltpu.VMEM((W_RING, D_FF // W_CHUNKS, MOE_DOWN_TN), F32),
                            pltpu.VMEM((2, 1, D_FF, MOE_DOWN_TN), BF16),
                            pltpu.VMEM((MOE_IN_BUFS, MOE_BLOCK, D_FF), BF16),
                            pltpu.VMEM((MOE_OUT_BUFS, MOE_BLOCK, MOE_DOWN_TN), F32)] + sems),
        out_shape=jax.ShapeDtypeStruct((n_rows, D_MODEL), F32),
        compiler_params=_cparams(("arbitrary",)),
    )(*_item_tables(bounds, n_j, n_blocks * n_j), act,
      b_down.reshape(N_EXPERTS * n_j, MOE_DOWN_TN), w_down)


def _combine_kernel(dest_ref, next_ref, h_ref, gate_ref, nw_ref, y_ref, o_ref, buf_ref, sem):
    i = pl.program_id(0)
    steps = pl.num_programs(0)

    def issue(d_ref, slot):
        def body(r, carry):
            for k in range(TOP_K):
                pltpu.make_async_copy(y_ref.at[pl.ds(d_ref[k, r], 1)], buf_ref.at[slot, k, pl.ds(r, 1)],
                                      sem.at[slot]).start(priority=k % 2)
            return carry
        lax.fori_loop(0, COMBINE_TOKENS, body, 0)

    @pl.when(i == 0)
    def _():
        issue(dest_ref, 0)

    @pl.when(i + 1 < steps)
    def _():
        issue(next_ref, (i + 1) % 2)

    slot = i % 2
    pltpu.make_async_copy(buf_ref.at[slot], buf_ref.at[slot], sem.at[slot]).wait()
    acc = h_ref[...]
    gate = gate_ref[...]
    for k in range(TOP_K):
        acc = acc + gate[:, k:k + 1] * buf_ref[slot, k]
    o_ref[...] = _rms(acc, nw_ref[...])


def _combine(h, y_rows, dest_t, gate_pad, norm_w):
    t = h.shape[0]
    steps = t // COMBINE_TOKENS
    return pl.pallas_call(
        _combine_kernel,
        grid=(steps,),
        in_specs=[pl.BlockSpec((SUBLANES, COMBINE_TOKENS), lambda i: (0, i), memory_space=pltpu.SMEM),
                  pl.BlockSpec((SUBLANES, COMBINE_TOKENS), lambda i: (0, jnp.minimum(i + 1, steps - 1)),
                               memory_space=pltpu.SMEM),
                  pl.BlockSpec((COMBINE_TOKENS, D_MODEL), lambda i: (i, 0)),
                  pl.BlockSpec((COMBINE_TOKENS, LANES), lambda i: (i, 0)),
                  pl.BlockSpec((1, D_MODEL), lambda i: (0, 0)),
                  pl.BlockSpec(memory_space=pl.ANY)],
        out_specs=pl.BlockSpec((COMBINE_TOKENS, D_MODEL), lambda i: (i, 0)),
        out_shape=jax.ShapeDtypeStruct((t, D_MODEL), F32),
        scratch_shapes=[pltpu.VMEM((2, TOP_K, COMBINE_TOKENS, D_MODEL), F32), pltpu.SemaphoreType.DMA((2,))],
        compiler_params=_cparams(("arbitrary",)),
    )(dest_t, dest_t, h, gate_pad, norm_w.reshape(1, D_MODEL), y_rows)


def _moe(h, norm_ffn, w_router, b_router, w_gate, b_gate, w_up, b_up, w_down, b_down, norm_final):
    t = h.shape[0]
    tk = t * TOP_K
    n_blocks = (tk + N_EXPERTS * (MOE_BLOCK - 1) + MOE_BLOCK - 1) // MOE_BLOCK
    n_rows = n_blocks * MOE_BLOCK
    idx_pad, gate_pad, pos_pad, cnt_pad = _router(h, norm_ffn, w_router, b_router)
    dest_t, bounds = _dest_rows(idx_pad, pos_pad, cnt_pad)
    x_rows = _scatter_rows(h, dest_t, bounds, n_rows)
    y_rows = _experts(x_rows, bounds, norm_ffn, w_gate, b_gate, w_up, b_up, w_down, b_down)
    return _combine(h, y_rows, dest_t, gate_pad, norm_final)


def kernel(x, mem, norm_mix, w_in, conv_w, conv_b, dt_bias, a_log, d_skip, ssd_norm, pool_w, pool_scale,
           w_out, norm_xattn, norm_mem, w_q, w_kv, w_o, norm_ffn, w_router, b_router, w_gate, b_gate,
           w_up, b_up, w_down, b_down, norm_final):
    batch, seq, d = x.shape
    t = batch * seq
    h = x.reshape(t, d)
    depth = norm_mix.shape[0]
    assert depth == 1, "the final norm is fused into the MoE combine of the single layer"
    for l in range(depth):
        dt_lo = D_SSD + D_CONV
        w_main = jnp.concatenate([w_in[l][:, :dt_lo], w_in[l][:, dt_lo + SSD_HEADS:]], axis=1)
        w_dt = jnp.pad(w_in[l][:, dt_lo:dt_lo + SSD_HEADS], ((0, 0), (0, LANES - SSD_HEADS)))
        proj = _mm(h, w_main, norm_w=norm_mix[l])
        dtp = _mm(h, w_dt, norm_w=norm_mix[l])
        mix = _mixer(proj, dtp, conv_w[l], conv_b[l], dt_bias[l], a_log[l], d_skip[l], ssd_norm[l],
                     pool_w[l], pool_scale[l], batch, seq)
        h = _mm(mix, w_out[l], res=h)

        q = _mm(h, w_q[l], norm_w=norm_xattn[l], out_dtype=BF16)
        kv = _mm(mem.reshape(batch * N_MEM, d), w_kv[l], norm_w=norm_mem[l], out_dtype=BF16)
        o = _attn(q, kv, batch, seq)
        h = _mm(o, w_o[l], res=h)

        out = _moe(h, norm_ffn[l], w_router[l], b_router[l], w_gate[l], b_gate[l], w_up[l], b_up[l],
                   w_down[l], b_down[l], norm_final)
    return out.reshape(batch, seq, d)
```

```python
import functools

import jax
import jax.numpy as jnp
from jax import lax
from jax.experimental import pallas as pl
from jax.experimental.pallas import tpu as pltpu

F32 = jnp.float32
BF16 = jnp.bfloat16

D_MODEL = 2048
D_SSD = 1024
SSD_HEAD_DIM = 64
SSD_HEADS = 16
SSD_GROUPS = 4
HEADS_PER_GROUP = SSD_HEADS // SSD_GROUPS
D_STATE = 128
CONV_WIDTH = 4
CHUNK = 128
D_CONV = D_SSD + 2 * SSD_GROUPS * D_STATE
D_POOL = 1024
POOL_WINDOWS = (2, 4, 8, 16)
POOL_GROUP_DIM = 256
N_MEM = 256
XATTN_HEADS = 4
XATTN_HEAD_DIM = 512
N_EXPERTS = 32
TOP_K = 4
D_FF = 2048
SWIGLU_LIMIT = 7.0
SWIGLU_ALPHA = 1.702
EPS = 1e-5

LANES = 128
SUBLANES = 8
VMEM_LIMIT = 56 * 1024 * 1024
NEG_BIG = -1e30

MOE_BLOCK = 256
MOE_UP_TN = 1024
MOE_DOWN_TN = 2048
MOE_IN_BUFS = 3
MOE_OUT_BUFS = 2
W_CHUNKS = 4
W_RING = 4
ROUTER_TOKENS = 512
SCATTER_TOKENS = 256
COMBINE_TOKENS = 128
ROW_DMA_PRIORITY = 1


def _cparams(sem):
    return pltpu.CompilerParams(dimension_semantics=sem, vmem_limit_bytes=VMEM_LIMIT)


def _rms(x, w):
    ms = jnp.mean(x * x, axis=-1, keepdims=True)
    return x * lax.rsqrt(ms + EPS) * w


def _mm_kernel(*refs, has_norm, has_res):
    it = iter(refs)
    a_ref = next(it)
    w_ref = next(it)
    nw_ref = next(it) if has_norm else None
    r_ref = next(it) if has_res else None
    o_ref = next(it)
    wb_ref = next(it)

    @pl.when(pl.program_id(1) == 0)
    def _():
        wb_ref[...] = w_ref[...].astype(BF16)

    a = a_ref[...]
    if has_norm:
        a = _rms(a.astype(F32), nw_ref[...])
    acc = jnp.dot(a.astype(BF16), wb_ref[...], preferred_element_type=F32)
    if has_res:
        acc = acc + r_ref[...]
    o_ref[...] = acc.astype(o_ref.dtype)


def _mm(a, w, norm_w=None, res=None, out_dtype=F32, tm=512, tn=1024):
    m, k = a.shape
    n = w.shape[1]
    tn = min(tn, n)
    tm = min(tm, m)
    assert m % tm == 0 and n % tn == 0
    in_specs = [pl.BlockSpec((tm, k), lambda j, i: (i, 0)),
                pl.BlockSpec((k, tn), lambda j, i: (0, j))]
    args = [a, w]
    if norm_w is not None:
        in_specs.append(pl.BlockSpec((1, k), lambda j, i: (0, 0)))
        args.append(norm_w.reshape(1, k))
    if res is not None:
        in_specs.append(pl.BlockSpec((tm, tn), lambda j, i: (i, j)))
        args.append(res)
    return pl.pallas_call(
        functools.partial(_mm_kernel, has_norm=norm_w is not None, has_res=res is not None),
        grid=(n // tn, m // tm),
        in_specs=in_specs,
        out_specs=pl.BlockSpec((tm, tn), lambda j, i: (i, j)),
        out_shape=jax.ShapeDtypeStruct((m, n), out_dtype),
        scratch_shapes=[pltpu.VMEM((k, tn), BF16)],
        compiler_params=_cparams(("arbitrary", "arbitrary")),
    )(*args)


def _sigmoid(x):
    return 1.0 / (1.0 + jnp.exp(-x))


def _shift_rows(cur, prev, j, rows):
    return jnp.where(rows < j, pltpu.roll(prev, j, 0), pltpu.roll(cur, j, 0))


def _conv_silu(cur, prev, w_ref, b_ref, col0, rows):
    width = cur.shape[1]
    acc = cur * w_ref[CONV_WIDTH - 1:CONV_WIDTH, col0:col0 + width] + b_ref[:, col0:col0 + width]
    for j in range(1, CONV_WIDTH):
        k = CONV_WIDTH - 1 - j
        acc = acc + _shift_rows(cur, prev, j, rows) * w_ref[k:k + 1, col0:col0 + width]
    return acc * _sigmoid(acc)


def _mixer_kernel(z_ref, x_ref, bc_ref, u_ref, dt_ref, convw_ref, convb_ref, dtb_ref, alog_ref,
                  dskip_ref, ssdn_ref, poolw_ref, pscale_ref, out_ref,
                  px_ref, pbc_ref, pu_ref, state_ref):
    c = pl.program_id(1)

    @pl.when(c == 0)
    def _():
        px_ref[...] = jnp.zeros_like(px_ref)
        pbc_ref[...] = jnp.zeros_like(pbc_ref)
        pu_ref[...] = jnp.zeros_like(pu_ref)
        state_ref[...] = jnp.zeros_like(state_ref)

    rows = lax.broadcasted_iota(jnp.int32, (CHUNK, D_SSD), 0)
    x_raw = x_ref[...]
    bc_raw = bc_ref[...]
    xs = _conv_silu(x_raw, px_ref[...], convw_ref, convb_ref, 0, rows)
    bcs = _conv_silu(bc_raw, pbc_ref[...], convw_ref, convb_ref, D_SSD, rows)
    px_ref[...] = x_raw
    pbc_ref[...] = bc_raw

    dt_in = dt_ref[...] + dtb_ref[...]
    dt = jnp.maximum(dt_in, 0.0) + jnp.log(1.0 + jnp.exp(-jnp.abs(dt_in)))
    a = -jnp.exp(alog_ref[...])
    dta = dt * a
    ri = lax.broadcasted_iota(jnp.int32, (CHUNK, CHUNK), 0)
    ci = lax.broadcasted_iota(jnp.int32, (CHUNK, CHUNK), 1)
    causal = ri >= ci
    tril = jnp.where(causal, 1.0, 0.0).astype(F32)
    acs = jnp.dot(tril, dta, preferred_element_type=F32, precision=lax.Precision.HIGHEST)
    acs_t = acs.T
    dt_t = dt.T
    lo = ci < SSD_HEAD_DIM

    y_pairs = []
    for g in range(SSD_GROUPS):
        bg = bcs[:, g * D_STATE:(g + 1) * D_STATE].astype(BF16)
        cg = bcs[:, SSD_GROUPS * D_STATE + g * D_STATE:SSD_GROUPS * D_STATE + (g + 1) * D_STATE].astype(BF16)
        cb = lax.dot_general(cg, bg, (((1,), (1,)), ((), ())), preferred_element_type=F32)
        gw = HEADS_PER_GROUP * SSD_HEAD_DIM
        prev_g = state_ref[g * gw:(g + 1) * gw, :]
        yoff = lax.dot_general(cg, prev_g.astype(BF16), (((1,), (1,)), ((), ())),
                               preferred_element_type=F32)
        xw_parts = []
        for pr in range(HEADS_PER_GROUP // 2):
            pair = g * (HEADS_PER_GROUP // 2) + pr
            xs_pair = xs[:, pair * LANES:(pair + 1) * LANES]
            y_pair = None
            ecols = []
            wcols = []
            for q in range(2):
                h = pair * 2 + q
                col = acs[:, h:h + 1]
                seg = col - acs_t[h:h + 1, :]
                decay = jnp.exp(jnp.where(causal, seg, NEG_BIG))
                mat = (cb * decay * dt_t[h:h + 1, :]).astype(BF16)
                keep = lo if q == 0 else jnp.logical_not(lo)
                xh = jnp.where(keep, xs_pair, 0.0).astype(BF16)
                yd = jnp.dot(mat, xh, preferred_element_type=F32)
                y_pair = yd if y_pair is None else y_pair + yd
                ecols.append(jnp.exp(col))
                wcols.append(dt[:, h:h + 1] * jnp.exp(acs[CHUNK - 1:CHUNK, h:h + 1] - col))
            e_pair = jnp.where(lo, ecols[0], ecols[1])
            w_pair = jnp.where(lo, wcols[0], wcols[1])
            y_pairs.append(y_pair + yoff[:, pr * LANES:(pr + 1) * LANES] * e_pair)
            xw_parts.append(xs_pair * w_pair)
        xw = jnp.concatenate(xw_parts, axis=1).astype(BF16)
        s_new = lax.dot_general(xw, bg, (((0,), (0,)), ((), ())), preferred_element_type=F32)
        for r in range(HEADS_PER_GROUP):
            h = g * HEADS_PER_GROUP + r
            dec = jnp.exp(acs[CHUNK - 1:CHUNK, h:h + 1])
            lo_r = h * SSD_HEAD_DIM
            state_ref[lo_r:lo_r + SSD_HEAD_DIM, :] = (
                state_ref[lo_r:lo_r + SSD_HEAD_DIM, :] * dec
                + s_new[r * SSD_HEAD_DIM:(r + 1) * SSD_HEAD_DIM, :])

    y = jnp.concatenate(y_pairs, axis=1)
    y = y + dskip_ref[...] * xs
    z = z_ref[...]
    y = y * (z * _sigmoid(z))
    gdim = D_SSD // SSD_GROUPS
    for g in range(SSD_GROUPS):
        yg = _rms(y[:, g * gdim:(g + 1) * gdim], ssdn_ref[:, g * gdim:(g + 1) * gdim])
        out_ref[:, g * gdim:(g + 1) * gdim] = yg.astype(out_ref.dtype)

    u_cur = u_ref[...]
    u_prev = pu_ref[...]
    pos = (c * CHUNK + lax.broadcasted_iota(jnp.int32, (CHUNK, POOL_GROUP_DIM), 0) + 1).astype(F32)
    for g, win in enumerate(POOL_WINDOWS):
        sl = slice(g * POOL_GROUP_DIM, (g + 1) * POOL_GROUP_DIM)
        ug = u_cur[:, sl]
        ext = jnp.concatenate([u_prev[:, sl], ug], axis=0)
        step = 1
        while step < win:
            ext = ext + pltpu.roll(ext, step, 0)
            step *= 2
        pooled = ext[CHUNK:, :] / jnp.minimum(pos, float(win)) - ug
        mixed = jnp.dot(pooled.astype(BF16), poolw_ref[g].astype(BF16), preferred_element_type=F32)
        out_ref[:, D_SSD + g * POOL_GROUP_DIM:D_SSD + (g + 1) * POOL_GROUP_DIM] = (
            mixed * pscale_ref[:, sl]).astype(out_ref.dtype)
    pu_ref[...] = u_cur


def _mixer(proj, dtp, conv_w, conv_b, dt_bias, a_log, d_skip, ssd_norm, pool_w, pool_scale, batch, seq):
    nchunk = seq // CHUNK
    row_map = lambda col: (lambda b, c: (b * nchunk + c, col))
    const2 = lambda b, c: (0, 0)
    pad = LANES - SSD_HEADS
    dtb = jnp.pad(dt_bias, (0, pad)).reshape(1, LANES)
    alog = jnp.pad(a_log, (0, pad)).reshape(1, LANES)
    dskip = jnp.repeat(d_skip, SSD_HEAD_DIM).reshape(1, D_SSD)
    return pl.pallas_call(
        _mixer_kernel,
        grid=(batch, nchunk),
        in_specs=[
            pl.BlockSpec((CHUNK, D_SSD), row_map(0)),
            pl.BlockSpec((CHUNK, D_SSD), row_map(1)),
            pl.BlockSpec((CHUNK, D_SSD), row_map(2)),
            pl.BlockSpec((CHUNK, D_POOL), row_map(3)),
            pl.BlockSpec((CHUNK, LANES), row_map(0)),
            pl.BlockSpec((CONV_WIDTH, D_CONV), const2),
            pl.BlockSpec((1, D_CONV), const2),
            pl.BlockSpec((1, LANES), const2),
            pl.BlockSpec((1, LANES), const2),
            pl.BlockSpec((1, D_SSD), const2),
            pl.BlockSpec((1, D_SSD), const2),
            pl.BlockSpec((len(POOL_WINDOWS), POOL_GROUP_DIM, POOL_GROUP_DIM), lambda b, c: (0, 0, 0)),
            pl.BlockSpec((1, D_POOL), const2),
        ],
        out_specs=pl.BlockSpec((CHUNK, D_MODEL), lambda b, c: (b * nchunk + c, 0)),
        out_shape=jax.ShapeDtypeStruct((batch * seq, D_MODEL), BF16),
        scratch_shapes=[pltpu.VMEM((CHUNK, D_SSD), F32), pltpu.VMEM((CHUNK, D_SSD), F32),
                        pltpu.VMEM((CHUNK, D_POOL), F32), pltpu.VMEM((SSD_HEADS * SSD_HEAD_DIM, D_STATE), F32)],
        compiler_params=_cparams(("arbitrary", "arbitrary")),
    )(proj, proj, proj, proj, dtp, conv_w, conv_b.reshape(1, D_CONV), dtb, alog, dskip,
      ssd_norm.reshape(1, D_SSD), pool_w, pool_scale.reshape(1, D_POOL))


def _attn_kernel(q_ref, k_ref, v_ref, o_ref):
    scale = XATTN_HEAD_DIM ** -0.5
    for h in range(XATTN_HEADS):
        sl = slice(h * XATTN_HEAD_DIM, (h + 1) * XATTN_HEAD_DIM)
        s = lax.dot_general(q_ref[:, sl], k_ref[:, sl], (((1,), (1,)), ((), ())),
                            preferred_element_type=F32) * scale
        s = s - jnp.max(s, axis=-1, keepdims=True)
        p = jnp.exp(s)
        p = p / jnp.sum(p, axis=-1, keepdims=True)
        o_ref[:, sl] = jnp.dot(p.astype(BF16), v_ref[:, sl], preferred_element_type=F32).astype(o_ref.dtype)


def _attn(q, kv, batch, seq, tq=512):
    nq = seq // tq
    return pl.pallas_call(
        _attn_kernel,
        grid=(batch, nq),
        in_specs=[pl.BlockSpec((tq, D_MODEL), lambda b, i: (b * nq + i, 0)),
                  pl.BlockSpec((N_MEM, D_MODEL), lambda b, i: (b, 0)),
                  pl.BlockSpec((N_MEM, D_MODEL), lambda b, i: (b, 1))],
        out_specs=pl.BlockSpec((tq, D_MODEL), lambda b, i: (b * nq + i, 0)),
        out_shape=jax.ShapeDtypeStruct((batch * seq, D_MODEL), BF16),
        compiler_params=_cparams(("arbitrary", "arbitrary")),
    )(q, kv, kv)


def _router_kernel(h_ref, nw_ref, wr_ref, br_ref, idx_ref, gate_ref, pos_ref, cnt_ref, run_ref):
    @pl.when(pl.program_id(0) == 0)
    def _():
        run_ref[...] = jnp.zeros_like(run_ref)

    hn = _rms(h_ref[...], nw_ref[...])
    w = wr_ref[...]
    h_hi = hn.astype(BF16)
    h_lo = (hn - h_hi.astype(F32)).astype(BF16)
    w_hi = w.astype(BF16)
    w_lo = (w - w_hi.astype(F32)).astype(BF16)
    logits = (jnp.dot(h_hi, w_hi, preferred_element_type=F32)
              + (jnp.dot(h_hi, w_lo, preferred_element_type=F32)
                 + jnp.dot(h_lo, w_hi, preferred_element_type=F32))) + br_ref[...]
    tm = logits.shape[0]
    lane = lax.broadcasted_iota(jnp.int32, logits.shape, 1)
    vals = []
    hots = []
    idx_out = jnp.zeros(logits.shape, jnp.int32)
    for k in range(TOP_K):
        m = jnp.max(logits, axis=-1, keepdims=True)
        idx = jnp.min(jnp.where(logits == m, lane, LANES), axis=-1, keepdims=True)
        hit = lane == idx
        idx_out = jnp.where(lane == k, idx, idx_out)
        logits = jnp.where(hit, -jnp.inf, logits)
        vals.append(m)
        hots.append(jnp.where(hit, 1.0, 0.0))
    exps = [jnp.exp(v - vals[0]) for v in vals]
    den = exps[0] + exps[1] + exps[2] + exps[3]
    gate_out = jnp.zeros(logits.shape, F32)
    for k in range(TOP_K):
        gate_out = jnp.where(lane == k, exps[k] / den, gate_out)
    idx_ref[...] = idx_out
    gate_ref[...] = gate_out

    hot = hots[0] + hots[1] + hots[2] + hots[3]
    ri = lax.broadcasted_iota(jnp.int32, (tm, tm), 0)
    ci = lax.broadcasted_iota(jnp.int32, (tm, tm), 1)
    before = jnp.where(ri > ci, 1.0, 0.0).astype(BF16)
    ahead = jnp.dot(before, hot.astype(BF16), preferred_element_type=F32) + run_ref[...]
    pos_out = jnp.zeros(logits.shape, F32)
    for k in range(TOP_K):
        pos_out = jnp.where(lane == k, jnp.sum(ahead * hots[k], axis=-1, keepdims=True), pos_out)
    pos_ref[...] = pos_out
    run_ref[...] = run_ref[...] + jnp.sum(hot, axis=0, keepdims=True)
    cnt_ref[...] = run_ref[...]


def _router(h, norm_w, w_router, b_router):
    t = h.shape[0]
    tm = ROUTER_TOKENS
    pad = LANES - N_EXPERTS
    wr = jnp.pad(w_router, ((0, 0), (0, pad)))
    br = jnp.pad(b_router, (0, pad), constant_values=NEG_BIG).reshape(1, LANES)
    tok_spec = pl.BlockSpec((tm, LANES), lambda i: (i, 0))
    return pl.pallas_call(
        _router_kernel,
        grid=(t // tm,),
        in_specs=[pl.BlockSpec((tm, D_MODEL), lambda i: (i, 0)),
                  pl.BlockSpec((1, D_MODEL), lambda i: (0, 0)),
                  pl.BlockSpec((D_MODEL, LANES), lambda i: (0, 0)),
                  pl.BlockSpec((1, LANES), lambda i: (0, 0))],
        out_specs=[tok_spec, tok_spec, tok_spec, pl.BlockSpec((1, LANES), lambda i: (0, 0))],
        out_shape=[jax.ShapeDtypeStruct((t, LANES), jnp.int32),
                   jax.ShapeDtypeStruct((t, LANES), F32),
                   jax.ShapeDtypeStruct((t, LANES), F32),
                   jax.ShapeDtypeStruct((1, LANES), F32)],
        scratch_shapes=[pltpu.VMEM((1, LANES), F32)],
        compiler_params=_cparams(("arbitrary",)),
    )(h, norm_w.reshape(1, D_MODEL), wr, br)


def _dest_kernel(idx_ref, pos_ref, cnt_ref, dest_ref, bounds_ref):
    cnt = jnp.broadcast_to(cnt_ref[...], (SUBLANES, LANES))
    padded = jnp.floor((cnt + (MOE_BLOCK - 1)) * (1.0 / MOE_BLOCK)) * MOE_BLOCK
    ri = lax.broadcasted_iota(jnp.int32, (LANES, LANES), 0)
    ci = lax.broadcasted_iota(jnp.int32, (LANES, LANES), 1)
    upto = jnp.where(ri <= ci, 1.0, 0.0).astype(F32)
    seg_end = jnp.dot(padded, upto, preferred_element_type=F32, precision=lax.Precision.HIGHEST)
    seg_start = seg_end - padded
    row = lax.broadcasted_iota(jnp.int32, (SUBLANES, LANES), 0)
    bounds_ref[...] = jnp.where(row == 0, seg_start, jnp.where(row == 1, seg_end, 0.0)).astype(jnp.int32)

    idx = idx_ref[...]
    lane = lax.broadcasted_iota(jnp.int32, idx.shape, 1)
    out = pos_ref[...]
    for k in range(TOP_K):
        hit = lane == idx[:, k:k + 1]
        base = jnp.sum(jnp.where(hit, seg_start[0:1, :], 0.0), axis=-1, keepdims=True)
        out = jnp.where(lane == k, out + base, out)
    dest_ref[...] = out.T[0:SUBLANES, :].astype(jnp.int32)


def _dest_rows(idx_pad, pos_pad, cnt_pad):
    t = idx_pad.shape[0]
    tm = ROUTER_TOKENS
    tok_spec = pl.BlockSpec((tm, LANES), lambda i: (i, 0))
    return pl.pallas_call(
        _dest_kernel,
        grid=(t // tm,),
        in_specs=[tok_spec, tok_spec, pl.BlockSpec((1, LANES), lambda i: (0, 0))],
        out_specs=[pl.BlockSpec((SUBLANES, tm), lambda i: (0, i)),
                   pl.BlockSpec((SUBLANES, LANES), lambda i: (0, 0))],
        out_shape=[jax.ShapeDtypeStruct((SUBLANES, t), jnp.int32),
                   jax.ShapeDtypeStruct((SUBLANES, LANES), jnp.int32)],
        compiler_params=_cparams(("arbitrary",)),
    )(idx_pad, pos_pad, cnt_pad)


def _pack_halves(x):
    c = x.shape[1] // 2
    lo = pltpu.bitcast(x[:, :c].astype(BF16).astype(F32), jnp.uint32)
    hi = pltpu.bitcast(x[:, c:].astype(BF16).astype(F32), jnp.uint32)
    return (lo >> 16) | hi


def _unpack_halves(w):
    lo = pltpu.bitcast(w << 16, F32)
    hi = pltpu.bitcast(w & jnp.uint32(0xFFFF0000), F32)
    return jnp.concatenate([lo, hi], axis=1).astype(BF16)


def _scatter_kernel(dest_ref, bounds_ref, h_ref, nw_ref, x_ref, zero_ref, packed_ref, zsem, sem):
    n_blocks = x_ref.shape[0] // MOE_BLOCK
    packed_ref[...] = _pack_halves(_rms(h_ref[...], nw_ref[...]))

    @pl.when(pl.program_id(0) == 0)
    def _():
        zero_ref[...] = jnp.zeros_like(zero_ref)
        n_used = bounds_ref[1, N_EXPERTS - 1] // MOE_BLOCK

        def zero_copy(b):
            return pltpu.make_async_copy(zero_ref, x_ref.at[pl.ds(pl.multiple_of(b * MOE_BLOCK, MOE_BLOCK),
                                                                   MOE_BLOCK)], zsem)

        def last_block(e):
            return bounds_ref[1, e] // MOE_BLOCK - 1, bounds_ref[1, e] > bounds_ref[0, e]

        def start_e(e, carry):
            b, nonempty = last_block(e)

            @pl.when(nonempty)
            def _():
                zero_copy(b).start()
            return carry

        def wait_e(e, carry):
            b, nonempty = last_block(e)

            @pl.when(nonempty)
            def _():
                zero_copy(b).wait()
            return carry

        lax.fori_loop(0, N_EXPERTS, start_e, 0)
        lax.fori_loop(n_used, n_blocks, lambda b, c: (zero_copy(b).start(), c)[1], 0)
        lax.fori_loop(0, N_EXPERTS, wait_e, 0)
        lax.fori_loop(n_used, n_blocks, lambda b, c: (zero_copy(b).wait(), c)[1], 0)

    def issue(r, carry):
        for k in range(TOP_K):
            d = dest_ref[k, r]
            pltpu.make_async_copy(packed_ref.at[pl.ds(r, 1)], x_ref.at[pl.ds(d, 1)], sem).start(priority=k % 2)
        return carry

    lax.fori_loop(0, SCATTER_TOKENS, issue, 0)
    n_sent = SCATTER_TOKENS * TOP_K
    pltpu.make_async_copy(x_ref.at[pl.ds(0, n_sent)], x_ref.at[pl.ds(0, n_sent)], sem).wait()


def _scatter_rows(h, norm_w, dest_t, bounds, n_rows):
    t = h.shape[0]
    steps = t // SCATTER_TOKENS
    half = D_MODEL // 2
    return pl.pallas_call(
        _scatter_kernel,
        grid=(steps,),
        in_specs=[pl.BlockSpec((SUBLANES, SCATTER_TOKENS), lambda i: (0, i), memory_space=pltpu.SMEM),
                  pl.BlockSpec(memory_space=pltpu.SMEM),
                  pl.BlockSpec((SCATTER_TOKENS, D_MODEL), lambda i: (i, 0)),
                  pl.BlockSpec((1, D_MODEL), lambda i: (0, 0))],
        out_specs=pl.BlockSpec(memory_space=pl.ANY),
        out_shape=jax.ShapeDtypeStruct((n_rows, half), jnp.uint32),
        scratch_shapes=[pltpu.VMEM((MOE_BLOCK, half), jnp.uint32), pltpu.VMEM((SCATTER_TOKENS, half), jnp.uint32),
                        pltpu.SemaphoreType.DMA(()), pltpu.SemaphoreType.DMA(())],
        compiler_params=_cparams(("arbitrary",)),
    )(dest_t, bounds, h, norm_w.reshape(1, D_MODEL))


def _item_tables(bounds, n_j, n_items_max):
    first_e = bounds[0] // MOE_BLOCK * n_j
    end_e = bounds[1] // MOE_BLOCK * n_j
    blocks_e = (end_e - first_e) // n_j
    n_items = end_e[N_EXPERTS - 1]
    g = jnp.minimum(jnp.arange(n_items_max, dtype=jnp.int32), n_items - 1)
    e = jnp.sum((g[:, None] >= end_e[None, :N_EXPERTS]).astype(jnp.int32), axis=1)
    own = e[:, None] == jnp.arange(LANES, dtype=jnp.int32)[None, :]
    pick = lambda v: jnp.sum(jnp.where(own, v[None, :], 0), axis=1)
    first = pick(first_e)
    nb = jnp.maximum(pick(blocks_e), 1)
    q = g - first
    j = q // nb
    i = q % nb
    row = first // n_j + i
    meta = e * 8 + j * 2 + (i == 0).astype(jnp.int32)
    group_end = first + (j + 1) * nb
    return (meta.astype(jnp.int32), row.astype(jnp.int32), group_end.astype(jnp.int32),
            n_items.reshape(1).astype(jnp.int32))


def _item_loop(meta_ref, row_ref, gend_ref, n_ref, src_ref, dst_ref, w_refs, ring, ring_sem, wb_ref,
               in_buf, out_buf, in_sem, out_sem, compute):
    n = n_ref[0]
    n_in = in_buf.shape[0]
    n_out = out_buf.shape[0]
    tn = out_buf.shape[2]
    n_j = dst_ref.shape[1] // tn
    n_mats = len(w_refs)
    cpm = W_CHUNKS // n_mats
    kc = w_refs[0].shape[1] // cpm

    def rows(g):
        return pl.ds(pl.multiple_of(row_ref[g] * MOE_BLOCK, MOE_BLOCK), MOE_BLOCK)

    def cols(m):
        return pl.ds(pl.multiple_of(((m >> 1) & 3) * tn, tn), tn)

    def in_copy(g):
        slot = g % n_in
        return pltpu.make_async_copy(src_ref.at[rows(g)], in_buf.at[slot], in_sem.at[slot])

    def out_copy(g):
        slot = g % n_out
        return pltpu.make_async_copy(out_buf.at[slot], dst_ref.at[rows(g), cols(meta_ref[g])], out_sem.at[slot])

    def krows(c):
        r0 = (c % cpm) * kc
        return pl.ds(r0 if isinstance(c, int) else pl.multiple_of(r0, kc), kc)

    def chunk_copy(m, c, k):
        slot = c % W_RING
        return pltpu.make_async_copy(w_refs[k].at[m >> 3, krows(c), cols(m)], ring.at[slot], ring_sem.at[slot])

    def for_matrix(c, fn):
        if isinstance(c, int):
            return fn(c // cpm)
        for k in range(n_mats):
            @pl.when(c // cpm == k)
            def _(k=k):
                fn(k)

    def start_chunk(m, c):
        for_matrix(c, lambda k: chunk_copy(m, c, k).start())

    def finish_chunk(m, c, wslot):
        def fn(k):
            chunk_copy(m, c, k).wait()
            wb_ref[wslot, k, krows(c), :] = ring[c % W_RING].astype(BF16)
        for_matrix(c, fn)

        @pl.when(c + W_RING < W_CHUNKS)
        def _():
            start_chunk(m, c + W_RING)

    for c0 in range(W_RING):
        start_chunk(meta_ref[0], c0)
    for g0 in range(n_in - 1):
        @pl.when(g0 < n)
        def _():
            in_copy(g0).start(priority=ROW_DMA_PRIORITY)

    def body(g, carry):
        parity, done, nm = carry
        m = meta_ref[g]
        first = (m & 1) == 1

        @pl.when(first)
        def _():
            lax.fori_loop(done, W_CHUNKS, lambda c, z: (finish_chunk(m, c, 1 - parity), z)[1], 0)
            nxt = gend_ref[g]

            @pl.when(nxt < n)
            def _():
                for c0 in range(W_RING):
                    start_chunk(meta_ref[nxt], c0)

        nxt = gend_ref[g]
        parity = jnp.where(first, 1 - parity, parity)
        done = jnp.where(first, 0, done)
        nm = jnp.where(first, jnp.where(nxt < n, meta_ref[jnp.minimum(nxt, n - 1)], -1), nm)

        in_copy(g).wait()

        @pl.when(g + n_in - 1 < n)
        def _():
            in_copy(g + n_in - 1).start(priority=ROW_DMA_PRIORITY)

        @pl.when(g >= n_out)
        def _():
            out_copy(g - n_out).wait()

        out_buf[g % n_out] = compute(in_buf[g % n_in], wb_ref.at[parity], m).astype(out_buf.dtype)
        out_copy(g).start(priority=ROW_DMA_PRIORITY)

        active = (nm >= 0) & (done < W_CHUNKS)

        @pl.when(active)
        def _():
            finish_chunk(nm, done, 1 - parity)

        return parity, jnp.where(active, done + 1, done), nm

    lax.fori_loop(0, n, body, (jnp.int32(1), jnp.int32(0), meta_ref[0]))
    lax.fori_loop(jnp.maximum(n - n_out, 0), n, lambda g, c: (out_copy(g).wait(), c)[1], 0)

    n_blocks = dst_ref.shape[0] // MOE_BLOCK
    n_used = n // n_j
    out_buf[0] = jnp.zeros(out_buf.shape[1:], out_buf.dtype)
    for jt in range(n_j):
        def tail_copy(b, jt=jt):
            r = pl.ds(pl.multiple_of(b * MOE_BLOCK, MOE_BLOCK), MOE_BLOCK)
            return pltpu.make_async_copy(out_buf.at[0], dst_ref.at[r, jt * tn:(jt + 1) * tn], out_sem.at[0])

        lax.fori_loop(n_used, n_blocks, lambda b, c: (tail_copy(b).start(), c)[1], 0)
        lax.fori_loop(n_used, n_blocks, lambda b, c: (tail_copy(b).wait(), c)[1], 0)


def _expert_up_kernel(meta_ref, row_ref, gend_ref, n_ref, x_ref, bg_ref, bu_ref, wg_ref, wu_ref, act_ref,
                      ring, wb_ref, x_buf, a_buf, ring_sem, in_sem, out_sem):
    n_j = D_FF // MOE_UP_TN

    def compute(x, w, m):
        tile = (m >> 3) * n_j + ((m >> 1) & 3)
        xb = _unpack_halves(x)
        g = jnp.dot(xb, w[0], preferred_element_type=F32) + bg_ref[pl.ds(tile, 1), :]
        u = jnp.dot(xb, w[1], preferred_element_type=F32) + bu_ref[pl.ds(tile, 1), :]
        g = jnp.minimum(g, SWIGLU_LIMIT)
        u = jnp.clip(u, -SWIGLU_LIMIT, SWIGLU_LIMIT)
        return g * _sigmoid(SWIGLU_ALPHA * g) * (u + 1.0)

    _item_loop(meta_ref, row_ref, gend_ref, n_ref, x_ref, act_ref, (wg_ref, wu_ref), ring, ring_sem, wb_ref,
               x_buf, a_buf, in_sem, out_sem, compute)


def _expert_down_kernel(meta_ref, row_ref, gend_ref, n_ref, a_ref, bd_ref, wd_ref, y_ref,
                        ring, wb_ref, a_buf, y_buf, ring_sem, in_sem, out_sem):
    n_j = D_MODEL // MOE_DOWN_TN

    def compute(a, w, m):
        tile = (m >> 3) * n_j + ((m >> 1) & 3)
        return jnp.dot(a, w[0], preferred_element_type=F32) + bd_ref[pl.ds(tile, 1), :]

    _item_loop(meta_ref, row_ref, gend_ref, n_ref, a_ref, y_ref, (wd_ref,), ring, ring_sem, wb_ref,
               a_buf, y_buf, in_sem, out_sem, compute)


def _experts(x_rows, bounds, w_gate, b_gate, w_up, b_up, w_down, b_down):
    n_rows = x_rows.shape[0]
    n_blocks = n_rows // MOE_BLOCK
    any_spec = pl.BlockSpec(memory_space=pl.ANY)
    sems = [pltpu.SemaphoreType.DMA((W_RING,)), pltpu.SemaphoreType.DMA((MOE_IN_BUFS,)),
            pltpu.SemaphoreType.DMA((MOE_OUT_BUFS,))]

    def full(shape):
        return pl.BlockSpec(shape, lambda i, *_: (0,) * len(shape))

    n_j = D_FF // MOE_UP_TN
    act = pl.pallas_call(
        _expert_up_kernel,
        grid_spec=pltpu.PrefetchScalarGridSpec(
            num_scalar_prefetch=4, grid=(1,),
            in_specs=[any_spec, full((N_EXPERTS * n_j, MOE_UP_TN)),
                      full((N_EXPERTS * n_j, MOE_UP_TN)), any_spec, any_spec],
            out_specs=any_spec,
            scratch_shapes=[pltpu.VMEM((W_RING, 2 * D_MODEL // W_CHUNKS, MOE_UP_TN), F32),
                            pltpu.VMEM((2, 2, D_MODEL, MOE_UP_TN), BF16),
                            pltpu.VMEM((MOE_IN_BUFS, MOE_BLOCK, D_MODEL // 2), jnp.uint32),
                            pltpu.VMEM((MOE_OUT_BUFS, MOE_BLOCK, MOE_UP_TN), BF16)] + sems),
        out_shape=jax.ShapeDtypeStruct((n_rows, D_FF), BF16),
        compiler_params=_cparams(("arbitrary",)),
    )(*_item_tables(bounds, n_j, n_blocks * n_j), x_rows,
      b_gate.reshape(N_EXPERTS * n_j, MOE_UP_TN), b_up.reshape(N_EXPERTS * n_j, MOE_UP_TN), w_gate, w_up)

    n_j = D_MODEL // MOE_DOWN_TN
    return pl.pallas_call(
        _expert_down_kernel,
        grid_spec=pltpu.PrefetchScalarGridSpec(
            num_scalar_prefetch=4, grid=(1,),
            in_specs=[any_spec, full((N_EXPERTS * n_j, MOE_DOWN_TN)), any_spec],
            out_specs=any_spec,
            scratch_shapes=[pltpu.VMEM((W_RING, D_FF // W_CHUNKS, MOE_DOWN_TN), F32),
                            pltpu.VMEM((2, 1, D_FF, MOE_DOWN_TN), BF16),
                            pltpu.VMEM((MOE_IN_BUFS, MOE_BLOCK, D_FF), BF16),
                            pltpu.VMEM((MOE_OUT_BUFS, MOE_BLOCK, MOE_DOWN_TN), F32)] + sems),
        out_shape=jax.ShapeDtypeStruct((n_rows, D_MODEL), F32),
        compiler_params=_cparams(("arbitrary",)),
    )(*_item_tables(bounds, n_j, n_blocks * n_j), act,
      b_down.reshape(N_EXPERTS * n_j, MOE_DOWN_TN), w_down)


def _combine_kernel(dest_ref, next_ref, h_ref, gate_ref, nw_ref, y_ref, o_ref, buf_ref, sem):
    i = pl.program_id(0)
    steps = pl.num_programs(0)

    def issue(d_ref, slot):
        def body(r, carry):
            for k in range(TOP_K):
                pltpu.make_async_copy(y_ref.at[pl.ds(d_ref[k, r], 1)], buf_ref.at[slot, k, pl.ds(r, 1)],
                                      sem.at[slot]).start(priority=k % 2)
            return carry
        lax.fori_loop(0, COMBINE_TOKENS, body, 0)

    @pl.when(i == 0)
    def _():
        issue(dest_ref, 0)

    @pl.when(i + 1 < steps)
    def _():
        issue(next_ref, (i + 1) % 2)

    slot = i % 2
    pltpu.make_async_copy(buf_ref.at[slot], buf_ref.at[slot], sem.at[slot]).wait()
    acc = h_ref[...]
    gate = gate_ref[...]
    for k in range(TOP_K):
        acc = acc + gate[:, k:k + 1] * buf_ref[slot, k]
    o_ref[...] = _rms(acc, nw_ref[...])


def _combine(h, y_rows, dest_t, gate_pad, norm_w):
    t = h.shape[0]
    steps = t // COMBINE_TOKENS
    return pl.pallas_call(
        _combine_kernel,
        grid=(steps,),
        in_specs=[pl.BlockSpec((SUBLANES, COMBINE_TOKENS), lambda i: (0, i), memory_space=pltpu.SMEM),
                  pl.BlockSpec((SUBLANES, COMBINE_TOKENS), lambda i: (0, jnp.minimum(i + 1, steps - 1)),
                               memory_space=pltpu.SMEM),
                  pl.BlockSpec((COMBINE_TOKENS, D_MODEL), lambda i: (i, 0)),
                  pl.BlockSpec((COMBINE_TOKENS, LANES), lambda i: (i, 0)),
                  pl.BlockSpec((1, D_MODEL), lambda i: (0, 0)),
                  pl.BlockSpec(memory_space=pl.ANY)],
        out_specs=pl.BlockSpec((COMBINE_TOKENS, D_MODEL), lambda i: (i, 0)),
        out_shape=jax.ShapeDtypeStruct((t, D_MODEL), F32),
        scratch_shapes=[pltpu.VMEM((2, TOP_K, COMBINE_TOKENS, D_MODEL), F32), pltpu.SemaphoreType.DMA((2,))],
        compiler_params=_cparams(("arbitrary",)),
    )(dest_t, dest_t, h, gate_pad, norm_w.reshape(1, D_MODEL), y_rows)


def _moe(h, norm_ffn, w_router, b_router, w_gate, b_gate, w_up, b_up, w_down, b_down, norm_final):
    t = h.shape[0]
    tk = t * TOP_K
    n_blocks = (tk + N_EXPERTS * (MOE_BLOCK - 1) + MOE_BLOCK - 1) // MOE_BLOCK
    n_rows = n_blocks * MOE_BLOCK
    idx_pad, gate_pad, pos_pad, cnt_pad = _router(h, norm_ffn, w_router, b_router)
    dest_t, bounds = _dest_rows(idx_pad, pos_pad, cnt_pad)
    x_rows = _scatter_rows(h, norm_ffn, dest_t, bounds, n_rows)
    y_rows = _experts(x_rows, bounds, w_gate, b_gate, w_up, b_up, w_down, b_down)
    return _combine(h, y_rows, dest_t, gate_pad, norm_final)


def kernel(x, mem, norm_mix, w_in, conv_w, conv_b, dt_bias, a_log, d_skip, ssd_norm, pool_w, pool_scale,
           w_out, norm_xattn, norm_mem, w_q, w_kv, w_o, norm_ffn, w_router, b_router, w_gate, b_gate,
           w_up, b_up, w_down, b_down, norm_final):
    batch, seq, d = x.shape
    t = batch * seq
    h = x.reshape(t, d)
    depth = norm_mix.shape[0]
    assert depth == 1, "the final norm is fused into the MoE combine of the single layer"
    for l in range(depth):
        dt_lo = D_SSD + D_CONV
        w_main = jnp.concatenate([w_in[l][:, :dt_lo], w_in[l][:, dt_lo + SSD_HEADS:]], axis=1)
        w_dt = jnp.pad(w_in[l][:, dt_lo:dt_lo + SSD_HEADS], ((0, 0), (0, LANES - SSD_HEADS)))
        proj = _mm(h, w_main, norm_w=norm_mix[l])
        dtp = _mm(h, w_dt, norm_w=norm_mix[l])
        mix = _mixer(proj, dtp, conv_w[l], conv_b[l], dt_bias[l], a_log[l], d_skip[l], ssd_norm[l],
                     pool_w[l], pool_scale[l], batch, seq)
        h = _mm(mix, w_out[l], res=h)

        q = _mm(h, w_q[l], norm_w=norm_xattn[l], out_dtype=BF16)
        kv = _mm(mem.reshape(batch * N_MEM, d), w_kv[l], norm_w=norm_mem[l], out_dtype=BF16)
        o = _attn(q, kv, batch, seq)
        h = _mm(o, w_o[l], res=h)

        out = _moe(h, norm_ffn[l], w_router[l], b_router[l], w_gate[l], b_gate[l], w_up[l], b_up[l],
                   w_down[l], b_down[l], norm_final)
    return out.reshape(batch, seq, d)
```

```python
import functools

import jax
import jax.numpy as jnp
from jax import lax
from jax.experimental import pallas as pl
from jax.experimental.pallas import tpu as pltpu

F32 = jnp.float32
BF16 = jnp.bfloat16

D_MODEL = 2048
D_SSD = 1024
SSD_HEAD_DIM = 64
SSD_HEADS = 16
SSD_GROUPS = 4
HEADS_PER_GROUP = SSD_HEADS // SSD_GROUPS
D_STATE = 128
CONV_WIDTH = 4
CHUNK = 128
D_CONV = D_SSD + 2 * SSD_GROUPS * D_STATE
D_POOL = 1024
POOL_WINDOWS = (2, 4, 8, 16)
POOL_GROUP_DIM = 256
N_MEM = 256
XATTN_HEADS = 4
XATTN_HEAD_DIM = 512
N_EXPERTS = 32
TOP_K = 4
D_FF = 2048
SWIGLU_LIMIT = 7.0
SWIGLU_ALPHA = 1.702
EPS = 1e-5

LANES = 128
SUBLANES = 8
VMEM_LIMIT = 56 * 1024 * 1024
NEG_BIG = -1e30

MOE_BLOCK = 512
MOE_UP_TN = 512
MOE_DOWN_TN = 1024
MOE_IN_BUFS = 3
MOE_OUT_BUFS = 2
W_CHUNKS = 4
W_RING = 4
ROUTER_TOKENS = 512
SCATTER_TOKENS = 256
COMBINE_TOKENS = 128
ROW_DMA_PRIORITY = 1


def _cparams(sem):
    return pltpu.CompilerParams(dimension_semantics=sem, vmem_limit_bytes=VMEM_LIMIT)


def _rms(x, w):
    ms = jnp.mean(x * x, axis=-1, keepdims=True)
    return x * lax.rsqrt(ms + EPS) * w


def _mm_kernel(*refs, has_norm, has_res):
    it = iter(refs)
    a_ref = next(it)
    w_ref = next(it)
    nw_ref = next(it) if has_norm else None
    r_ref = next(it) if has_res else None
    o_ref = next(it)
    wb_ref = next(it)

    @pl.when(pl.program_id(1) == 0)
    def _():
        wb_ref[...] = w_ref[...].astype(BF16)

    a = a_ref[...]
    if has_norm:
        a = _rms(a.astype(F32), nw_ref[...])
    acc = jnp.dot(a.astype(BF16), wb_ref[...], preferred_element_type=F32)
    if has_res:
        acc = acc + r_ref[...]
    o_ref[...] = acc.astype(o_ref.dtype)


def _mm(a, w, norm_w=None, res=None, out_dtype=F32, tm=512, tn=1024):
    m, k = a.shape
    n = w.shape[1]
    tn = min(tn, n)
    tm = min(tm, m)
    assert m % tm == 0 and n % tn == 0
    in_specs = [pl.BlockSpec((tm, k), lambda j, i: (i, 0)),
                pl.BlockSpec((k, tn), lambda j, i: (0, j))]
    args = [a, w]
    if norm_w is not None:
        in_specs.append(pl.BlockSpec((1, k), lambda j, i: (0, 0)))
        args.append(norm_w.reshape(1, k))
    if res is not None:
        in_specs.append(pl.BlockSpec((tm, tn), lambda j, i: (i, j)))
        args.append(res)
    return pl.pallas_call(
        functools.partial(_mm_kernel, has_norm=norm_w is not None, has_res=res is not None),
        grid=(n // tn, m // tm),
        in_specs=in_specs,
        out_specs=pl.BlockSpec((tm, tn), lambda j, i: (i, j)),
        out_shape=jax.ShapeDtypeStruct((m, n), out_dtype),
        scratch_shapes=[pltpu.VMEM((k, tn), BF16)],
        compiler_params=_cparams(("arbitrary", "arbitrary")),
    )(*args)


def _sigmoid(x):
    return 1.0 / (1.0 + jnp.exp(-x))


def _shift_rows(cur, prev, j, rows):
    return jnp.where(rows < j, pltpu.roll(prev, j, 0), pltpu.roll(cur, j, 0))


def _conv_silu(cur, prev, w_ref, b_ref, col0, rows):
    width = cur.shape[1]
    acc = cur * w_ref[CONV_WIDTH - 1:CONV_WIDTH, col0:col0 + width] + b_ref[:, col0:col0 + width]
    for j in range(1, CONV_WIDTH):
        k = CONV_WIDTH - 1 - j
        acc = acc + _shift_rows(cur, prev, j, rows) * w_ref[k:k + 1, col0:col0 + width]
    return acc * _sigmoid(acc)


def _mixer_kernel(z_ref, x_ref, bc_ref, u_ref, dt_ref, convw_ref, convb_ref, dtb_ref, alog_ref,
                  dskip_ref, ssdn_ref, poolw_ref, pscale_ref, out_ref,
                  px_ref, pbc_ref, pu_ref, state_ref):
    c = pl.program_id(1)

    @pl.when(c == 0)
    def _():
        px_ref[...] = jnp.zeros_like(px_ref)
        pbc_ref[...] = jnp.zeros_like(pbc_ref)
        pu_ref[...] = jnp.zeros_like(pu_ref)
        state_ref[...] = jnp.zeros_like(state_ref)

    rows = lax.broadcasted_iota(jnp.int32, (CHUNK, D_SSD), 0)
    x_raw = x_ref[...]
    bc_raw = bc_ref[...]
    xs = _conv_silu(x_raw, px_ref[...], convw_ref, convb_ref, 0, rows)
    bcs = _conv_silu(bc_raw, pbc_ref[...], convw_ref, convb_ref, D_SSD, rows)
    px_ref[...] = x_raw
    pbc_ref[...] = bc_raw

    dt_in = dt_ref[...] + dtb_ref[...]
    dt = jnp.maximum(dt_in, 0.0) + jnp.log(1.0 + jnp.exp(-jnp.abs(dt_in)))
    a = -jnp.exp(alog_ref[...])
    dta = dt * a
    ri = lax.broadcasted_iota(jnp.int32, (CHUNK, CHUNK), 0)
    ci = lax.broadcasted_iota(jnp.int32, (CHUNK, CHUNK), 1)
    causal = ri >= ci
    tril = jnp.where(causal, 1.0, 0.0).astype(F32)
    acs = jnp.dot(tril, dta, preferred_element_type=F32, precision=lax.Precision.HIGHEST)
    acs_t = acs.T
    dt_t = dt.T
    lo = ci < SSD_HEAD_DIM

    y_pairs = []
    for g in range(SSD_GROUPS):
        bg = bcs[:, g * D_STATE:(g + 1) * D_STATE].astype(BF16)
        cg = bcs[:, SSD_GROUPS * D_STATE + g * D_STATE:SSD_GROUPS * D_STATE + (g + 1) * D_STATE].astype(BF16)
        cb = lax.dot_general(cg, bg, (((1,), (1,)), ((), ())), preferred_element_type=F32)
        gw = HEADS_PER_GROUP * SSD_HEAD_DIM
        prev_g = state_ref[g * gw:(g + 1) * gw, :]
        yoff = lax.dot_general(cg, prev_g.astype(BF16), (((1,), (1,)), ((), ())),
                               preferred_element_type=F32)
        xw_parts = []
        for pr in range(HEADS_PER_GROUP // 2):
            pair = g * (HEADS_PER_GROUP // 2) + pr
            xs_pair = xs[:, pair * LANES:(pair + 1) * LANES]
            y_pair = None
            ecols = []
            wcols = []
            for q in range(2):
                h = pair * 2 + q
                col = acs[:, h:h + 1]
                seg = col - acs_t[h:h + 1, :]
                decay = jnp.exp(jnp.where(causal, seg, NEG_BIG))
                mat = (cb * decay * dt_t[h:h + 1, :]).astype(BF16)
                keep = lo if q == 0 else jnp.logical_not(lo)
                xh = jnp.where(keep, xs_pair, 0.0).astype(BF16)
                yd = jnp.dot(mat, xh, preferred_element_type=F32)
                y_pair = yd if y_pair is None else y_pair + yd
                ecols.append(jnp.exp(col))
                wcols.append(dt[:, h:h + 1] * jnp.exp(acs[CHUNK - 1:CHUNK, h:h + 1] - col))
            e_pair = jnp.where(lo, ecols[0], ecols[1])
            w_pair = jnp.where(lo, wcols[0], wcols[1])
            y_pairs.append(y_pair + yoff[:, pr * LANES:(pr + 1) * LANES] * e_pair)
            xw_parts.append(xs_pair * w_pair)
        xw = jnp.concatenate(xw_parts, axis=1).astype(BF16)
        s_new = lax.dot_general(xw, bg, (((0,), (0,)), ((), ())), preferred_element_type=F32)
        for r in range(HEADS_PER_GROUP):
            h = g * HEADS_PER_GROUP + r
            dec = jnp.exp(acs[CHUNK - 1:CHUNK, h:h + 1])
            lo_r = h * SSD_HEAD_DIM
            state_ref[lo_r:lo_r + SSD_HEAD_DIM, :] = (
                state_ref[lo_r:lo_r + SSD_HEAD_DIM, :] * dec
                + s_new[r * SSD_HEAD_DIM:(r + 1) * SSD_HEAD_DIM, :])

    y = jnp.concatenate(y_pairs, axis=1)
    y = y + dskip_ref[...] * xs
    z = z_ref[...]
    y = y * (z * _sigmoid(z))
    gdim = D_SSD // SSD_GROUPS
    for g in range(SSD_GROUPS):
        yg = _rms(y[:, g * gdim:(g + 1) * gdim], ssdn_ref[:, g * gdim:(g + 1) * gdim])
        out_ref[:, g * gdim:(g + 1) * gdim] = yg.astype(out_ref.dtype)

    u_cur = u_ref[...]
    u_prev = pu_ref[...]
    pos = (c * CHUNK + lax.broadcasted_iota(jnp.int32, (CHUNK, POOL_GROUP_DIM), 0) + 1).astype(F32)
    for g, win in enumerate(POOL_WINDOWS):
        sl = slice(g * POOL_GROUP_DIM, (g + 1) * POOL_GROUP_DIM)
        ug = u_cur[:, sl]
        ext = jnp.concatenate([u_prev[:, sl], ug], axis=0)
        step = 1
        while step < win:
            ext = ext + pltpu.roll(ext, step, 0)
            step *= 2
        pooled = ext[CHUNK:, :] / jnp.minimum(pos, float(win)) - ug
        mixed = jnp.dot(pooled.astype(BF16), poolw_ref[g].astype(BF16), preferred_element_type=F32)
        out_ref[:, D_SSD + g * POOL_GROUP_DIM:D_SSD + (g + 1) * POOL_GROUP_DIM] = (
            mixed * pscale_ref[:, sl]).astype(out_ref.dtype)
    pu_ref[...] = u_cur


def _mixer(proj, dtp, conv_w, conv_b, dt_bias, a_log, d_skip, ssd_norm, pool_w, pool_scale, batch, seq):
    nchunk = seq // CHUNK
    row_map = lambda col: (lambda b, c: (b * nchunk + c, col))
    const2 = lambda b, c: (0, 0)
    pad = LANES - SSD_HEADS
    dtb = jnp.pad(dt_bias, (0, pad)).reshape(1, LANES)
    alog = jnp.pad(a_log, (0, pad)).reshape(1, LANES)
    dskip = jnp.repeat(d_skip, SSD_HEAD_DIM).reshape(1, D_SSD)
    return pl.pallas_call(
        _mixer_kernel,
        grid=(batch, nchunk),
        in_specs=[
            pl.BlockSpec((CHUNK, D_SSD), row_map(0)),
            pl.BlockSpec((CHUNK, D_SSD), row_map(1)),
            pl.BlockSpec((CHUNK, D_SSD), row_map(2)),
            pl.BlockSpec((CHUNK, D_POOL), row_map(3)),
            pl.BlockSpec((CHUNK, LANES), row_map(0)),
            pl.BlockSpec((CONV_WIDTH, D_CONV), const2),
            pl.BlockSpec((1, D_CONV), const2),
            pl.BlockSpec((1, LANES), const2),
            pl.BlockSpec((1, LANES), const2),
            pl.BlockSpec((1, D_SSD), const2),
            pl.BlockSpec((1, D_SSD), const2),
            pl.BlockSpec((len(POOL_WINDOWS), POOL_GROUP_DIM, POOL_GROUP_DIM), lambda b, c: (0, 0, 0)),
            pl.BlockSpec((1, D_POOL), const2),
        ],
        out_specs=pl.BlockSpec((CHUNK, D_MODEL), lambda b, c: (b * nchunk + c, 0)),
        out_shape=jax.ShapeDtypeStruct((batch * seq, D_MODEL), BF16),
        scratch_shapes=[pltpu.VMEM((CHUNK, D_SSD), F32), pltpu.VMEM((CHUNK, D_SSD), F32),
                        pltpu.VMEM((CHUNK, D_POOL), F32), pltpu.VMEM((SSD_HEADS * SSD_HEAD_DIM, D_STATE), F32)],
        compiler_params=_cparams(("arbitrary", "arbitrary")),
    )(proj, proj, proj, proj, dtp, conv_w, conv_b.reshape(1, D_CONV), dtb, alog, dskip,
      ssd_norm.reshape(1, D_SSD), pool_w, pool_scale.reshape(1, D_POOL))


def _attn_kernel(q_ref, k_ref, v_ref, o_ref):
    scale = XATTN_HEAD_DIM ** -0.5
    for h in range(XATTN_HEADS):
        sl = slice(h * XATTN_HEAD_DIM, (h + 1) * XATTN_HEAD_DIM)
        s = lax.dot_general(q_ref[:, sl], k_ref[:, sl], (((1,), (1,)), ((), ())),
                            preferred_element_type=F32) * scale
        s = s - jnp.max(s, axis=-1, keepdims=True)
        p = jnp.exp(s)
        p = p / jnp.sum(p, axis=-1, keepdims=True)
        o_ref[:, sl] = jnp.dot(p.astype(BF16), v_ref[:, sl], preferred_element_type=F32).astype(o_ref.dtype)


def _attn(q, kv, batch, seq, tq=512):
    nq = seq // tq
    return pl.pallas_call(
        _attn_kernel,
        grid=(batch, nq),
        in_specs=[pl.BlockSpec((tq, D_MODEL), lambda b, i: (b * nq + i, 0)),
                  pl.BlockSpec((N_MEM, D_MODEL), lambda b, i: (b, 0)),
                  pl.BlockSpec((N_MEM, D_MODEL), lambda b, i: (b, 1))],
        out_specs=pl.BlockSpec((tq, D_MODEL), lambda b, i: (b * nq + i, 0)),
        out_shape=jax.ShapeDtypeStruct((batch * seq, D_MODEL), BF16),
        compiler_params=_cparams(("arbitrary", "arbitrary")),
    )(q, kv, kv)


def _router_kernel(h_ref, nw_ref, wr_ref, br_ref, idx_ref, gate_ref, pos_ref, cnt_ref, run_ref):
    @pl.when(pl.program_id(0) == 0)
    def _():
        run_ref[...] = jnp.zeros_like(run_ref)

    hn = _rms(h_ref[...], nw_ref[...])
    w = wr_ref[...]
    h_hi = hn.astype(BF16)
    h_lo = (hn - h_hi.astype(F32)).astype(BF16)
    w_hi = w.astype(BF16)
    w_lo = (w - w_hi.astype(F32)).astype(BF16)
    logits = (jnp.dot(h_hi, w_hi, preferred_element_type=F32)
              + (jnp.dot(h_hi, w_lo, preferred_element_type=F32)
                 + jnp.dot(h_lo, w_hi, preferred_element_type=F32))) + br_ref[...]
    tm = logits.shape[0]
    lane = lax.broadcasted_iota(jnp.int32, logits.shape, 1)
    vals = []
    hots = []
    idx_out = jnp.zeros(logits.shape, jnp.int32)
    for k in range(TOP_K):
        m = jnp.max(logits, axis=-1, keepdims=True)
        idx = jnp.min(jnp.where(logits == m, lane, LANES), axis=-1, keepdims=True)
        hit = lane == idx
        idx_out = jnp.where(lane == k, idx, idx_out)
        logits = jnp.where(hit, -jnp.inf, logits)
        vals.append(m)
        hots.append(jnp.where(hit, 1.0, 0.0))
    exps = [jnp.exp(v - vals[0]) for v in vals]
    den = exps[0] + exps[1] + exps[2] + exps[3]
    gate_out = jnp.zeros(logits.shape, F32)
    for k in range(TOP_K):
        gate_out = jnp.where(lane == k, exps[k] / den, gate_out)
    idx_ref[...] = idx_out
    gate_ref[...] = gate_out

    hot = hots[0] + hots[1] + hots[2] + hots[3]
    ri = lax.broadcasted_iota(jnp.int32, (tm, tm), 0)
    ci = lax.broadcasted_iota(jnp.int32, (tm, tm), 1)
    before = jnp.where(ri > ci, 1.0, 0.0).astype(BF16)
    ahead = jnp.dot(before, hot.astype(BF16), preferred_element_type=F32) + run_ref[...]
    pos_out = jnp.zeros(logits.shape, F32)
    for k in range(TOP_K):
        pos_out = jnp.where(lane == k, jnp.sum(ahead * hots[k], axis=-1, keepdims=True), pos_out)
    pos_ref[...] = pos_out
    run_ref[...] = run_ref[...] + jnp.sum(hot, axis=0, keepdims=True)
    cnt_ref[...] = run_ref[...]


def _router(h, norm_w, w_router, b_router):
    t = h.shape[0]
    tm = ROUTER_TOKENS
    pad = LANES - N_EXPERTS
    wr = jnp.pad(w_router, ((0, 0), (0, pad)))
    br = jnp.pad(b_router, (0, pad), constant_values=NEG_BIG).reshape(1, LANES)
    tok_spec = pl.BlockSpec((tm, LANES), lambda i: (i, 0))
    return pl.pallas_call(
        _router_kernel,
        grid=(t // tm,),
        in_specs=[pl.BlockSpec((tm, D_MODEL), lambda i: (i, 0)),
                  pl.BlockSpec((1, D_MODEL), lambda i: (0, 0)),
                  pl.BlockSpec((D_MODEL, LANES), lambda i: (0, 0)),
                  pl.BlockSpec((1, LANES), lambda i: (0, 0))],
        out_specs=[tok_spec, tok_spec, tok_spec, pl.BlockSpec((1, LANES), lambda i: (0, 0))],
        out_shape=[jax.ShapeDtypeStruct((t, LANES), jnp.int32),
                   jax.ShapeDtypeStruct((t, LANES), F32),
                   jax.ShapeDtypeStruct((t, LANES), F32),
                   jax.ShapeDtypeStruct((1, LANES), F32)],
        scratch_shapes=[pltpu.VMEM((1, LANES), F32)],
        compiler_params=_cparams(("arbitrary",)),
    )(h, norm_w.reshape(1, D_MODEL), wr, br)


def _dest_kernel(idx_ref, pos_ref, cnt_ref, dest_ref, bounds_ref):
    cnt = jnp.broadcast_to(cnt_ref[...], (SUBLANES, LANES))
    padded = jnp.floor((cnt + (MOE_BLOCK - 1)) * (1.0 / MOE_BLOCK)) * MOE_BLOCK
    ri = lax.broadcasted_iota(jnp.int32, (LANES, LANES), 0)
    ci = lax.broadcasted_iota(jnp.int32, (LANES, LANES), 1)
    upto = jnp.where(ri <= ci, 1.0, 0.0).astype(F32)
    seg_end = jnp.dot(padded, upto, preferred_element_type=F32, precision=lax.Precision.HIGHEST)
    seg_start = seg_end - padded
    row = lax.broadcasted_iota(jnp.int32, (SUBLANES, LANES), 0)
    bounds_ref[...] = jnp.where(row == 0, seg_start, jnp.where(row == 1, seg_end, 0.0)).astype(jnp.int32)

    idx = idx_ref[...]
    lane = lax.broadcasted_iota(jnp.int32, idx.shape, 1)
    out = pos_ref[...]
    for k in range(TOP_K):
        hit = lane == idx[:, k:k + 1]
        base = jnp.sum(jnp.where(hit, seg_start[0:1, :], 0.0), axis=-1, keepdims=True)
        out = jnp.where(lane == k, out + base, out)
    dest_ref[...] = out.T[0:SUBLANES, :].astype(jnp.int32)


def _dest_rows(idx_pad, pos_pad, cnt_pad):
    t = idx_pad.shape[0]
    tm = ROUTER_TOKENS
    tok_spec = pl.BlockSpec((tm, LANES), lambda i: (i, 0))
    return pl.pallas_call(
        _dest_kernel,
        grid=(t // tm,),
        in_specs=[tok_spec, tok_spec, pl.BlockSpec((1, LANES), lambda i: (0, 0))],
        out_specs=[pl.BlockSpec((SUBLANES, tm), lambda i: (0, i)),
                   pl.BlockSpec((SUBLANES, LANES), lambda i: (0, 0))],
        out_shape=[jax.ShapeDtypeStruct((SUBLANES, t), jnp.int32),
                   jax.ShapeDtypeStruct((SUBLANES, LANES), jnp.int32)],
        compiler_params=_cparams(("arbitrary",)),
    )(idx_pad, pos_pad, cnt_pad)


def _pack_halves(x):
    c = x.shape[1] // 2
    lo = pltpu.bitcast(x[:, :c].astype(BF16).astype(F32), jnp.uint32)
    hi = pltpu.bitcast(x[:, c:].astype(BF16).astype(F32), jnp.uint32)
    return (lo >> 16) | hi


def _unpack_halves(w):
    lo = pltpu.bitcast(w << 16, F32)
    hi = pltpu.bitcast(w & jnp.uint32(0xFFFF0000), F32)
    return jnp.concatenate([lo, hi], axis=1).astype(BF16)


def _scatter_kernel(dest_ref, bounds_ref, h_ref, nw_ref, x_ref, zero_ref, packed_ref, zsem, sem):
    n_blocks = x_ref.shape[0] // MOE_BLOCK
    packed_ref[...] = _pack_halves(_rms(h_ref[...], nw_ref[...]))

    @pl.when(pl.program_id(0) == 0)
    def _():
        zero_ref[...] = jnp.zeros_like(zero_ref)
        n_used = bounds_ref[1, N_EXPERTS - 1] // MOE_BLOCK

        def zero_copy(b):
            return pltpu.make_async_copy(zero_ref, x_ref.at[pl.ds(pl.multiple_of(b * MOE_BLOCK, MOE_BLOCK),
                                                                   MOE_BLOCK)], zsem)

        def last_block(e):
            return bounds_ref[1, e] // MOE_BLOCK - 1, bounds_ref[1, e] > bounds_ref[0, e]

        def start_e(e, carry):
            b, nonempty = last_block(e)

            @pl.when(nonempty)
            def _():
                zero_copy(b).start()
            return carry

        def wait_e(e, carry):
            b, nonempty = last_block(e)

            @pl.when(nonempty)
            def _():
                zero_copy(b).wait()
            return carry

        lax.fori_loop(0, N_EXPERTS, start_e, 0)
        lax.fori_loop(n_used, n_blocks, lambda b, c: (zero_copy(b).start(), c)[1], 0)
        lax.fori_loop(0, N_EXPERTS, wait_e, 0)
        lax.fori_loop(n_used, n_blocks, lambda b, c: (zero_copy(b).wait(), c)[1], 0)

    def issue(r, carry):
        for k in range(TOP_K):
            d = dest_ref[k, r]
            pltpu.make_async_copy(packed_ref.at[pl.ds(r, 1)], x_ref.at[pl.ds(d, 1)], sem).start(priority=k % 2)
        return carry

    lax.fori_loop(0, SCATTER_TOKENS, issue, 0)
    n_sent = SCATTER_TOKENS * TOP_K
    pltpu.make_async_copy(x_ref.at[pl.ds(0, n_sent)], x_ref.at[pl.ds(0, n_sent)], sem).wait()


def _scatter_rows(h, norm_w, dest_t, bounds, n_rows):
    t = h.shape[0]
    steps = t // SCATTER_TOKENS
    half = D_MODEL // 2
    return pl.pallas_call(
        _scatter_kernel,
        grid=(steps,),
        in_specs=[pl.BlockSpec((SUBLANES, SCATTER_TOKENS), lambda i: (0, i), memory_space=pltpu.SMEM),
                  pl.BlockSpec(memory_space=pltpu.SMEM),
                  pl.BlockSpec((SCATTER_TOKENS, D_MODEL), lambda i: (i, 0)),
                  pl.BlockSpec((1, D_MODEL), lambda i: (0, 0))],
        out_specs=pl.BlockSpec(memory_space=pl.ANY),
        out_shape=jax.ShapeDtypeStruct((n_rows, half), jnp.uint32),
        scratch_shapes=[pltpu.VMEM((MOE_BLOCK, half), jnp.uint32), pltpu.VMEM((SCATTER_TOKENS, half), jnp.uint32),
                        pltpu.SemaphoreType.DMA(()), pltpu.SemaphoreType.DMA(())],
        compiler_params=_cparams(("arbitrary",)),
    )(dest_t, bounds, h, norm_w.reshape(1, D_MODEL))


def _item_tables(bounds, n_j, n_items_max):
    first_e = bounds[0] // MOE_BLOCK * n_j
    end_e = bounds[1] // MOE_BLOCK * n_j
    blocks_e = (end_e - first_e) // n_j
    n_items = end_e[N_EXPERTS - 1]
    g = jnp.minimum(jnp.arange(n_items_max, dtype=jnp.int32), n_items - 1)
    e = jnp.sum((g[:, None] >= end_e[None, :N_EXPERTS]).astype(jnp.int32), axis=1)
    own = e[:, None] == jnp.arange(LANES, dtype=jnp.int32)[None, :]
    pick = lambda v: jnp.sum(jnp.where(own, v[None, :], 0), axis=1)
    first = pick(first_e)
    nb = jnp.maximum(pick(blocks_e), 1)
    q = g - first
    j = q // nb
    i = q % nb
    row = first // n_j + i
    meta = e * 8 + j * 2 + (i == 0).astype(jnp.int32)
    group_end = first + (j + 1) * nb
    return (meta.astype(jnp.int32), row.astype(jnp.int32), group_end.astype(jnp.int32),
            n_items.reshape(1).astype(jnp.int32))


def _item_loop(meta_ref, row_ref, gend_ref, n_ref, src_ref, dst_ref, w_refs, ring, ring_sem, wb_ref,
               in_buf, out_buf, in_sem, out_sem, compute):
    n = n_ref[0]
    n_in = in_buf.shape[0]
    n_out = out_buf.shape[0]
    tn = out_buf.shape[2]
    n_j = dst_ref.shape[1] // tn
    n_mats = len(w_refs)
    cpm = W_CHUNKS // n_mats
    kc = w_refs[0].shape[1] // cpm

    def rows(g):
        return pl.ds(pl.multiple_of(row_ref[g] * MOE_BLOCK, MOE_BLOCK), MOE_BLOCK)

    def cols(m):
        return pl.ds(pl.multiple_of(((m >> 1) & 3) * tn, tn), tn)

    def in_copy(g):
        slot = g % n_in
        return pltpu.make_async_copy(src_ref.at[rows(g)], in_buf.at[slot], in_sem.at[slot])

    def out_copy(g):
        slot = g % n_out
        return pltpu.make_async_copy(out_buf.at[slot], dst_ref.at[rows(g), cols(meta_ref[g])], out_sem.at[slot])

    def krows(c):
        r0 = (c % cpm) * kc
        return pl.ds(r0 if isinstance(c, int) else pl.multiple_of(r0, kc), kc)

    def chunk_copy(m, c, k):
        slot = c % W_RING
        return pltpu.make_async_copy(w_refs[k].at[m >> 3, krows(c), cols(m)], ring.at[slot], ring_sem.at[slot])

    def for_matrix(c, fn):
        if isinstance(c, int):
            return fn(c // cpm)
        for k in range(n_mats):
            @pl.when(c // cpm == k)
            def _(k=k):
                fn(k)

    def start_chunk(m, c):
        for_matrix(c, lambda k: chunk_copy(m, c, k).start())

    def finish_chunk(m, c, wslot):
        def fn(k):
            chunk_copy(m, c, k).wait()
            wb_ref[wslot, k, krows(c), :] = ring[c % W_RING].astype(BF16)
        for_matrix(c, fn)

        @pl.when(c + W_RING < W_CHUNKS)
        def _():
            start_chunk(m, c + W_RING)

    for c0 in range(W_RING):
        start_chunk(meta_ref[0], c0)
    for g0 in range(n_in - 1):
        @pl.when(g0 < n)
        def _():
            in_copy(g0).start(priority=ROW_DMA_PRIORITY)

    def body(g, carry):
        parity, done, nm = carry
        m = meta_ref[g]
        first = (m & 1) == 1

        @pl.when(first)
        def _():
            lax.fori_loop(done, W_CHUNKS, lambda c, z: (finish_chunk(m, c, 1 - parity), z)[1], 0)
            nxt = gend_ref[g]

            @pl.when(nxt < n)
            def _():
                for c0 in range(W_RING):
                    start_chunk(meta_ref[nxt], c0)

        nxt = gend_ref[g]
        parity = jnp.where(first, 1 - parity, parity)
        done = jnp.where(first, 0, done)
        nm = jnp.where(first, jnp.where(nxt < n, meta_ref[jnp.minimum(nxt, n - 1)], -1), nm)

        in_copy(g).wait()

        @pl.when(g + n_in - 1 < n)
        def _():
            in_copy(g + n_in - 1).start(priority=ROW_DMA_PRIORITY)

        @pl.when(g >= n_out)
        def _():
            out_copy(g - n_out).wait()

        out_buf[g % n_out] = compute(in_buf[g % n_in], wb_ref.at[parity], m).astype(out_buf.dtype)
        out_copy(g).start(priority=ROW_DMA_PRIORITY)

        active = (nm >= 0) & (done < W_CHUNKS)

        @pl.when(active)
        def _():
            finish_chunk(nm, done, 1 - parity)

        return parity, jnp.where(active, done + 1, done), nm

    lax.fori_loop(0, n, body, (jnp.int32(1), jnp.int32(0), meta_ref[0]))
    lax.fori_loop(jnp.maximum(n - n_out, 0), n, lambda g, c: (out_copy(g).wait(), c)[1], 0)

    n_blocks = dst_ref.shape[0] // MOE_BLOCK
    n_used = n // n_j
    out_buf[0] = jnp.zeros(out_buf.shape[1:], out_buf.dtype)
    for jt in range(n_j):
        def tail_copy(b, jt=jt):
            r = pl.ds(pl.multiple_of(b * MOE_BLOCK, MOE_BLOCK), MOE_BLOCK)
            return pltpu.make_async_copy(out_buf.at[0], dst_ref.at[r, jt * tn:(jt + 1) * tn], out_sem.at[0])

        lax.fori_loop(n_used, n_blocks, lambda b, c: (tail_copy(b).start(), c)[1], 0)
        lax.fori_loop(n_used, n_blocks, lambda b, c: (tail_copy(b).wait(), c)[1], 0)


def _expert_up_kernel(meta_ref, row_ref, gend_ref, n_ref, x_ref, bg_ref, bu_ref, wg_ref, wu_ref, act_ref,
                      ring, wb_ref, x_buf, a_buf, ring_sem, in_sem, out_sem):
    n_j = D_FF // MOE_UP_TN

    def compute(x, w, m):
        tile = (m >> 3) * n_j + ((m >> 1) & 3)
        xb = _unpack_halves(x)
        g = jnp.dot(xb, w[0], preferred_element_type=F32) + bg_ref[pl.ds(tile, 1), :]
        u = jnp.dot(xb, w[1], preferred_element_type=F32) + bu_ref[pl.ds(tile, 1), :]
        g = jnp.minimum(g, SWIGLU_LIMIT)
        u = jnp.clip(u, -SWIGLU_LIMIT, SWIGLU_LIMIT)
        return g * _sigmoid(SWIGLU_ALPHA * g) * (u + 1.0)

    _item_loop(meta_ref, row_ref, gend_ref, n_ref, x_ref, act_ref, (wg_ref, wu_ref), ring, ring_sem, wb_ref,
               x_buf, a_buf, in_sem, out_sem, compute)


def _expert_down_kernel(meta_ref, row_ref, gend_ref, n_ref, a_ref, bd_ref, wd_ref, y_ref,
                        ring, wb_ref, a_buf, y_buf, ring_sem, in_sem, out_sem):
    n_j = D_MODEL // MOE_DOWN_TN

    def compute(a, w, m):
        tile = (m >> 3) * n_j + ((m >> 1) & 3)
        return jnp.dot(a, w[0], preferred_element_type=F32) + bd_ref[pl.ds(tile, 1), :]

    _item_loop(meta_ref, row_ref, gend_ref, n_ref, a_ref, y_ref, (wd_ref,), ring, ring_sem, wb_ref,
               a_buf, y_buf, in_sem, out_sem, compute)


def _experts(x_rows, bounds, w_gate, b_gate, w_up, b_up, w_down, b_down):
    n_rows = x_rows.shape[0]
    n_blocks = n_rows // MOE_BLOCK
    any_spec = pl.BlockSpec(memory_space=pl.ANY)
    sems = [pltpu.SemaphoreType.DMA((W_RING,)), pltpu.SemaphoreType.DMA((MOE_IN_BUFS,)),
            pltpu.SemaphoreType.DMA((MOE_OUT_BUFS,))]

    def full(shape):
        return pl.BlockSpec(shape, lambda i, *_: (0,) * len(shape))

    n_j = D_FF // MOE_UP_TN
    act = pl.pallas_call(
        _expert_up_kernel,
        grid_spec=pltpu.PrefetchScalarGridSpec(
            num_scalar_prefetch=4, grid=(1,),
            in_specs=[any_spec, full((N_EXPERTS * n_j, MOE_UP_TN)),
                      full((N_EXPERTS * n_j, MOE_UP_TN)), any_spec, any_spec],
            out_specs=any_spec,
            scratch_shapes=[pltpu.VMEM((W_RING, 2 * D_MODEL // W_CHUNKS, MOE_UP_TN), F32),
                            pltpu.VMEM((2, 2, D_MODEL, MOE_UP_TN), BF16),
                            pltpu.VMEM((MOE_IN_BUFS, MOE_BLOCK, D_MODEL // 2), jnp.uint32),
                            pltpu.VMEM((MOE_OUT_BUFS, MOE_BLOCK, MOE_UP_TN), BF16)] + sems),
        out_shape=jax.ShapeDtypeStruct((n_rows, D_FF), BF16),
        compiler_params=_cparams(("arbitrary",)),
    )(*_item_tables(bounds, n_j, n_blocks * n_j), x_rows,
      b_gate.reshape(N_EXPERTS * n_j, MOE_UP_TN), b_up.reshape(N_EXPERTS * n_j, MOE_UP_TN), w_gate, w_up)

    n_j = D_MODEL // MOE_DOWN_TN
    return pl.pallas_call(
        _expert_down_kernel,
        grid_spec=pltpu.PrefetchScalarGridSpec(
            num_scalar_prefetch=4, grid=(1,),
            in_specs=[any_spec, full((N_EXPERTS * n_j, MOE_DOWN_TN)), any_spec],
            out_specs=any_spec,
            scratch_shapes=[pltpu.VMEM((W_RING, D_FF // W_CHUNKS, MOE_DOWN_TN), F32),
                            pltpu.VMEM((2, 1, D_FF, MOE_DOWN_TN), BF16),
                            pltpu.VMEM((MOE_IN_BUFS, MOE_BLOCK, D_FF), BF16),
                            pltpu.VMEM((MOE_OUT_BUFS, MOE_BLOCK, MOE_DOWN_TN), F32)] + sems),
        out_shape=jax.ShapeDtypeStruct((n_rows, D_MODEL), F32),
        compiler_params=_cparams(("arbitrary",)),
    )(*_item_tables(bounds, n_j, n_blocks * n_j), act,
      b_down.reshape(N_EXPERTS * n_j, MOE_DOWN_TN), w_down)


def _combine_kernel(dest_ref, next_ref, h_ref, gate_ref, nw_ref, y_ref, o_ref, buf_ref, sem):
    i = pl.program_id(0)
    steps = pl.num_programs(0)

    def issue(d_ref, slot):
        def body(r, carry):
            for k in range(TOP_K):
                pltpu.make_async_copy(y_ref.at[pl.ds(d_ref[k, r], 1)], buf_ref.at[slot, k, pl.ds(r, 1)],
                                      sem.at[slot]).start(priority=k % 2)
            return carry
        lax.fori_loop(0, COMBINE_TOKENS, body, 0)

    @pl.when(i == 0)
    def _():
        issue(dest_ref, 0)

    @pl.when(i + 1 < steps)
    def _():
        issue(next_ref, (i + 1) % 2)

    slot = i % 2
    pltpu.make_async_copy(buf_ref.at[slot], buf_ref.at[slot], sem.at[slot]).wait()
    acc = h_ref[...]
    gate = gate_ref[...]
    for k in range(TOP_K):
        acc = acc + gate[:, k:k + 1] * buf_ref[slot, k]
    o_ref[...] = _rms(acc, nw_ref[...])


def _combine(h, y_rows, dest_t, gate_pad, norm_w):
    t = h.shape[0]
    steps = t // COMBINE_TOKENS
    return pl.pallas_call(
        _combine_kernel,
        grid=(steps,),
        in_specs=[pl.BlockSpec((SUBLANES, COMBINE_TOKENS), lambda i: (0, i), memory_space=pltpu.SMEM),
                  pl.BlockSpec((SUBLANES, COMBINE_TOKENS), lambda i: (0, jnp.minimum(i + 1, steps - 1)),
                               memory_space=pltpu.SMEM),
                  pl.BlockSpec((COMBINE_TOKENS, D_MODEL), lambda i: (i, 0)),
                  pl.BlockSpec((COMBINE_TOKENS, LANES), lambda i: (i, 0)),
                  pl.BlockSpec((1, D_MODEL), lambda i: (0, 0)),
                  pl.BlockSpec(memory_space=pl.ANY)],
        out_specs=pl.BlockSpec((COMBINE_TOKENS, D_MODEL), lambda i: (i, 0)),
        out_shape=jax.ShapeDtypeStruct((t, D_MODEL), F32),
        scratch_shapes=[pltpu.VMEM((2, TOP_K, COMBINE_TOKENS, D_MODEL), F32), pltpu.SemaphoreType.DMA((2,))],
        compiler_params=_cparams(("arbitrary",)),
    )(dest_t, dest_t, h, gate_pad, norm_w.reshape(1, D_MODEL), y_rows)


def _moe(h, norm_ffn, w_router, b_router, w_gate, b_gate, w_up, b_up, w_down, b_down, norm_final):
    t = h.shape[0]
    tk = t * TOP_K
    n_blocks = (tk + N_EXPERTS * (MOE_BLOCK - 1) + MOE_BLOCK - 1) // MOE_BLOCK
    n_rows = n_blocks * MOE_BLOCK
    idx_pad, gate_pad, pos_pad, cnt_pad = _router(h, norm_ffn, w_router, b_router)
    dest_t, bounds = _dest_rows(idx_pad, pos_pad, cnt_pad)
    x_rows = _scatter_rows(h, norm_ffn, dest_t, bounds, n_rows)
    y_rows = _experts(x_rows, bounds, w_gate, b_gate, w_up, b_up, w_down, b_down)
    return _combine(h, y_rows, dest_t, gate_pad, norm_final)


def kernel(x, mem, norm_mix, w_in, conv_w, conv_b, dt_bias, a_log, d_skip, ssd_norm, pool_w, pool_scale,
           w_out, norm_xattn, norm_mem, w_q, w_kv, w_o, norm_ffn, w_router, b_router, w_gate, b_gate,
           w_up, b_up, w_down, b_down, norm_final):
    batch, seq, d = x.shape
    t = batch * seq
    h = x.reshape(t, d)
    depth = norm_mix.shape[0]
    assert depth == 1, "the final norm is fused into the MoE combine of the single layer"
    for l in range(depth):
        dt_lo = D_SSD + D_CONV
        w_main = jnp.concatenate([w_in[l][:, :dt_lo], w_in[l][:, dt_lo + SSD_HEADS:]], axis=1)
        w_dt = jnp.pad(w_in[l][:, dt_lo:dt_lo + SSD_HEADS], ((0, 0), (0, LANES - SSD_HEADS)))
        proj = _mm(h, w_main, norm_w=norm_mix[l])
        dtp = _mm(h, w_dt, norm_w=norm_mix[l])
        mix = _mixer(proj, dtp, conv_w[l], conv_b[l], dt_bias[l], a_log[l], d_skip[l], ssd_norm[l],
                     pool_w[l], pool_scale[l], batch, seq)
        h = _mm(mix, w_out[l], res=h)

        q = _mm(h, w_q[l], norm_w=norm_xattn[l], out_dtype=BF16)
        kv = _mm(mem.reshape(batch * N_MEM, d), w_kv[l], norm_w=norm_mem[l], out_dtype=BF16)
        o = _attn(q, kv, batch, seq)
        h = _mm(o, w_o[l], res=h)

        out = _moe(h, norm_ffn[l], w_router[l], b_router[l], w_gate[l], b_gate[l], w_up[l], b_up[l],
                   w_down[l], b_down[l], norm_final)
    return out.reshape(batch, seq, d)
```

```python
import functools

import jax
import jax.numpy as jnp
from jax import lax
from jax.experimental import pallas as pl
from jax.experimental.pallas import tpu as pltpu

F32 = jnp.float32
BF16 = jnp.bfloat16

D_MODEL = 2048
D_SSD = 1024
SSD_HEAD_DIM = 64
SSD_HEADS = 16
SSD_GROUPS = 4
HEADS_PER_GROUP = SSD_HEADS // SSD_GROUPS
D_STATE = 128
CONV_WIDTH = 4
CHUNK = 128
D_CONV = D_SSD + 2 * SSD_GROUPS * D_STATE
D_POOL = 1024
POOL_WINDOWS = (2, 4, 8, 16)
POOL_GROUP_DIM = 256
N_MEM = 256
XATTN_HEADS = 4
XATTN_HEAD_DIM = 512
N_EXPERTS = 32
TOP_K = 4
D_FF = 2048
SWIGLU_LIMIT = 7.0
SWIGLU_ALPHA = 1.702
EPS = 1e-5

LANES = 128
SUBLANES = 8
VMEM_LIMIT = 56 * 1024 * 1024
NEG_BIG = -1e30

MOE_BLOCK = 256
MOE_ITEM = 2 * MOE_BLOCK
MOE_UP_TN = 512
MOE_DOWN_TN = 1024
MOE_IN_BUFS = 3
MOE_OUT_BUFS = 2
W_CHUNKS = 4
W_RING = 4
ROUTER_TOKENS = 512
SCATTER_TOKENS = 256
COMBINE_TOKENS = 128
ROW_DMA_PRIORITY = 1


def _cparams(sem):
    return pltpu.CompilerParams(dimension_semantics=sem, vmem_limit_bytes=VMEM_LIMIT)


def _rms(x, w):
    ms = jnp.mean(x * x, axis=-1, keepdims=True)
    return x * lax.rsqrt(ms + EPS) * w


def _mm_kernel(*refs, has_norm, has_res):
    it = iter(refs)
    a_ref = next(it)
    w_ref = next(it)
    nw_ref = next(it) if has_norm else None
    r_ref = next(it) if has_res else None
    o_ref = next(it)
    wb_ref = next(it)

    @pl.when(pl.program_id(1) == 0)
    def _():
        wb_ref[...] = w_ref[...].astype(BF16)

    a = a_ref[...]
    if has_norm:
        a = _rms(a.astype(F32), nw_ref[...])
    acc = jnp.dot(a.astype(BF16), wb_ref[...], preferred_element_type=F32)
    if has_res:
        acc = acc + r_ref[...]
    o_ref[...] = acc.astype(o_ref.dtype)


def _mm(a, w, norm_w=None, res=None, out_dtype=F32, tm=512, tn=1024):
    m, k = a.shape
    n = w.shape[1]
    tn = min(tn, n)
    tm = min(tm, m)
    assert m % tm == 0 and n % tn == 0
    in_specs = [pl.BlockSpec((tm, k), lambda j, i: (i, 0)),
                pl.BlockSpec((k, tn), lambda j, i: (0, j))]
    args = [a, w]
    if norm_w is not None:
        in_specs.append(pl.BlockSpec((1, k), lambda j, i: (0, 0)))
        args.append(norm_w.reshape(1, k))
    if res is not None:
        in_specs.append(pl.BlockSpec((tm, tn), lambda j, i: (i, j)))
        args.append(res)
    return pl.pallas_call(
        functools.partial(_mm_kernel, has_norm=norm_w is not None, has_res=res is not None),
        grid=(n // tn, m // tm),
        in_specs=in_specs,
        out_specs=pl.BlockSpec((tm, tn), lambda j, i: (i, j)),
        out_shape=jax.ShapeDtypeStruct((m, n), out_dtype),
        scratch_shapes=[pltpu.VMEM((k, tn), BF16)],
        compiler_params=_cparams(("arbitrary", "arbitrary")),
    )(*args)


def _sigmoid(x):
    return 1.0 / (1.0 + jnp.exp(-x))


def _shift_rows(cur, prev, j, rows):
    return jnp.where(rows < j, pltpu.roll(prev, j, 0), pltpu.roll(cur, j, 0))


def _conv_silu(cur, prev, w_ref, b_ref, col0, rows):
    width = cur.shape[1]
    acc = cur * w_ref[CONV_WIDTH - 1:CONV_WIDTH, col0:col0 + width] + b_ref[:, col0:col0 + width]
    for j in range(1, CONV_WIDTH):
        k = CONV_WIDTH - 1 - j
        acc = acc + _shift_rows(cur, prev, j, rows) * w_ref[k:k + 1, col0:col0 + width]
    return acc * _sigmoid(acc)


def _mixer_kernel(z_ref, x_ref, bc_ref, u_ref, dt_ref, convw_ref, convb_ref, dtb_ref, alog_ref,
                  dskip_ref, ssdn_ref, poolw_ref, pscale_ref, out_ref,
                  px_ref, pbc_ref, pu_ref, state_ref):
    c = pl.program_id(1)

    @pl.when(c == 0)
    def _():
        px_ref[...] = jnp.zeros_like(px_ref)
        pbc_ref[...] = jnp.zeros_like(pbc_ref)
        pu_ref[...] = jnp.zeros_like(pu_ref)
        state_ref[...] = jnp.zeros_like(state_ref)

    rows = lax.broadcasted_iota(jnp.int32, (CHUNK, D_SSD), 0)
    x_raw = x_ref[...]
    bc_raw = bc_ref[...]
    xs = _conv_silu(x_raw, px_ref[...], convw_ref, convb_ref, 0, rows)
    bcs = _conv_silu(bc_raw, pbc_ref[...], convw_ref, convb_ref, D_SSD, rows)
    px_ref[...] = x_raw
    pbc_ref[...] = bc_raw

    dt_in = dt_ref[...] + dtb_ref[...]
    dt = jnp.maximum(dt_in, 0.0) + jnp.log(1.0 + jnp.exp(-jnp.abs(dt_in)))
    a = -jnp.exp(alog_ref[...])
    dta = dt * a
    ri = lax.broadcasted_iota(jnp.int32, (CHUNK, CHUNK), 0)
    ci = lax.broadcasted_iota(jnp.int32, (CHUNK, CHUNK), 1)
    causal = ri >= ci
    tril = jnp.where(causal, 1.0, 0.0).astype(F32)
    acs = jnp.dot(tril, dta, preferred_element_type=F32, precision=lax.Precision.HIGHEST)
    acs_t = acs.T
    dt_t = dt.T
    lo = ci < SSD_HEAD_DIM

    y_pairs = []
    for g in range(SSD_GROUPS):
        bg = bcs[:, g * D_STATE:(g + 1) * D_STATE].astype(BF16)
        cg = bcs[:, SSD_GROUPS * D_STATE + g * D_STATE:SSD_GROUPS * D_STATE + (g + 1) * D_STATE].astype(BF16)
        cb = lax.dot_general(cg, bg, (((1,), (1,)), ((), ())), preferred_element_type=F32)
        gw = HEADS_PER_GROUP * SSD_HEAD_DIM
        prev_g = state_ref[g * gw:(g + 1) * gw, :]
        yoff = lax.dot_general(cg, prev_g.astype(BF16), (((1,), (1,)), ((), ())),
                               preferred_element_type=F32)
        xw_parts = []
        for pr in range(HEADS_PER_GROUP // 2):
            pair = g * (HEADS_PER_GROUP // 2) + pr
            xs_pair = xs[:, pair * LANES:(pair + 1) * LANES]
            y_pair = None
            ecols = []
            wcols = []
            for q in range(2):
                h = pair * 2 + q
                col = acs[:, h:h + 1]
                seg = col - acs_t[h:h + 1, :]
                decay = jnp.exp(jnp.where(causal, seg, NEG_BIG))
                mat = (cb * decay * dt_t[h:h + 1, :]).astype(BF16)
                keep = lo if q == 0 else jnp.logical_not(lo)
                xh = jnp.where(keep, xs_pair, 0.0).astype(BF16)
                yd = jnp.dot(mat, xh, preferred_element_type=F32)
                y_pair = yd if y_pair is None else y_pair + yd
                ecols.append(jnp.exp(col))
                wcols.append(dt[:, h:h + 1] * jnp.exp(acs[CHUNK - 1:CHUNK, h:h + 1] - col))
            e_pair = jnp.where(lo, ecols[0], ecols[1])
            w_pair = jnp.where(lo, wcols[0], wcols[1])
            y_pairs.append(y_pair + yoff[:, pr * LANES:(pr + 1) * LANES] * e_pair)
            xw_parts.append(xs_pair * w_pair)
        xw = jnp.concatenate(xw_parts, axis=1).astype(BF16)
        s_new = lax.dot_general(xw, bg, (((0,), (0,)), ((), ())), preferred_element_type=F32)
        for r in range(HEADS_PER_GROUP):
            h = g * HEADS_PER_GROUP + r
            dec = jnp.exp(acs[CHUNK - 1:CHUNK, h:h + 1])
            lo_r = h * SSD_HEAD_DIM
            state_ref[lo_r:lo_r + SSD_HEAD_DIM, :] = (
                state_ref[lo_r:lo_r + SSD_HEAD_DIM, :] * dec
                + s_new[r * SSD_HEAD_DIM:(r + 1) * SSD_HEAD_DIM, :])

    y = jnp.concatenate(y_pairs, axis=1)
    y = y + dskip_ref[...] * xs
    z = z_ref[...]
    y = y * (z * _sigmoid(z))
    gdim = D_SSD // SSD_GROUPS
    for g in range(SSD_GROUPS):
        yg = _rms(y[:, g * gdim:(g + 1) * gdim], ssdn_ref[:, g * gdim:(g + 1) * gdim])
        out_ref[:, g * gdim:(g + 1) * gdim] = yg.astype(out_ref.dtype)

    u_cur = u_ref[...]
    u_prev = pu_ref[...]
    pos = (c * CHUNK + lax.broadcasted_iota(jnp.int32, (CHUNK, POOL_GROUP_DIM), 0) + 1).astype(F32)
    for g, win in enumerate(POOL_WINDOWS):
        sl = slice(g * POOL_GROUP_DIM, (g + 1) * POOL_GROUP_DIM)
        ug = u_cur[:, sl]
        ext = jnp.concatenate([u_prev[:, sl], ug], axis=0)
        step = 1
        while step < win:
            ext = ext + pltpu.roll(ext, step, 0)
            step *= 2
        pooled = ext[CHUNK:, :] / jnp.minimum(pos, float(win)) - ug
        mixed = jnp.dot(pooled.astype(BF16), poolw_ref[g].astype(BF16), preferred_element_type=F32)
        out_ref[:, D_SSD + g * POOL_GROUP_DIM:D_SSD + (g + 1) * POOL_GROUP_DIM] = (
            mixed * pscale_ref[:, sl]).astype(out_ref.dtype)
    pu_ref[...] = u_cur


def _mixer(proj, dtp, conv_w, conv_b, dt_bias, a_log, d_skip, ssd_norm, pool_w, pool_scale, batch, seq):
    nchunk = seq // CHUNK
    row_map = lambda col: (lambda b, c: (b * nchunk + c, col))
    const2 = lambda b, c: (0, 0)
    pad = LANES - SSD_HEADS
    dtb = jnp.pad(dt_bias, (0, pad)).reshape(1, LANES)
    alog = jnp.pad(a_log, (0, pad)).reshape(1, LANES)
    dskip = jnp.repeat(d_skip, SSD_HEAD_DIM).reshape(1, D_SSD)
    return pl.pallas_call(
        _mixer_kernel,
        grid=(batch, nchunk),
        in_specs=[
            pl.BlockSpec((CHUNK, D_SSD), row_map(0)),
            pl.BlockSpec((CHUNK, D_SSD), row_map(1)),
            pl.BlockSpec((CHUNK, D_SSD), row_map(2)),
            pl.BlockSpec((CHUNK, D_POOL), row_map(3)),
            pl.BlockSpec((CHUNK, LANES), row_map(0)),
            pl.BlockSpec((CONV_WIDTH, D_CONV), const2),
            pl.BlockSpec((1, D_CONV), const2),
            pl.BlockSpec((1, LANES), const2),
            pl.BlockSpec((1, LANES), const2),
            pl.BlockSpec((1, D_SSD), const2),
            pl.BlockSpec((1, D_SSD), const2),
            pl.BlockSpec((len(POOL_WINDOWS), POOL_GROUP_DIM, POOL_GROUP_DIM), lambda b, c: (0, 0, 0)),
            pl.BlockSpec((1, D_POOL), const2),
        ],
        out_specs=pl.BlockSpec((CHUNK, D_MODEL), lambda b, c: (b * nchunk + c, 0)),
        out_shape=jax.ShapeDtypeStruct((batch * seq, D_MODEL), BF16),
        scratch_shapes=[pltpu.VMEM((CHUNK, D_SSD), F32), pltpu.VMEM((CHUNK, D_SSD), F32),
                        pltpu.VMEM((CHUNK, D_POOL), F32), pltpu.VMEM((SSD_HEADS * SSD_HEAD_DIM, D_STATE), F32)],
        compiler_params=_cparams(("arbitrary", "arbitrary")),
    )(proj, proj, proj, proj, dtp, conv_w, conv_b.reshape(1, D_CONV), dtb, alog, dskip,
      ssd_norm.reshape(1, D_SSD), pool_w, pool_scale.reshape(1, D_POOL))


def _attn_kernel(q_ref, k_ref, v_ref, o_ref):
    scale = XATTN_HEAD_DIM ** -0.5
    for h in range(XATTN_HEADS):
        sl = slice(h * XATTN_HEAD_DIM, (h + 1) * XATTN_HEAD_DIM)
        s = lax.dot_general(q_ref[:, sl], k_ref[:, sl], (((1,), (1,)), ((), ())),
                            preferred_element_type=F32) * scale
        s = s - jnp.max(s, axis=-1, keepdims=True)
        p = jnp.exp(s)
        p = p / jnp.sum(p, axis=-1, keepdims=True)
        o_ref[:, sl] = jnp.dot(p.astype(BF16), v_ref[:, sl], preferred_element_type=F32).astype(o_ref.dtype)


def _attn(q, kv, batch, seq, tq=512):
    nq = seq // tq
    return pl.pallas_call(
        _attn_kernel,
        grid=(batch, nq),
        in_specs=[pl.BlockSpec((tq, D_MODEL), lambda b, i: (b * nq + i, 0)),
                  pl.BlockSpec((N_MEM, D_MODEL), lambda b, i: (b, 0)),
                  pl.BlockSpec((N_MEM, D_MODEL), lambda b, i: (b, 1))],
        out_specs=pl.BlockSpec((tq, D_MODEL), lambda b, i: (b * nq + i, 0)),
        out_shape=jax.ShapeDtypeStruct((batch * seq, D_MODEL), BF16),
        compiler_params=_cparams(("arbitrary", "arbitrary")),
    )(q, kv, kv)


def _router_kernel(h_ref, nw_ref, wr_ref, br_ref, idx_ref, gate_ref, pos_ref, cnt_ref, run_ref):
    @pl.when(pl.program_id(0) == 0)
    def _():
        run_ref[...] = jnp.zeros_like(run_ref)

    hn = _rms(h_ref[...], nw_ref[...])
    w = wr_ref[...]
    h_hi = hn.astype(BF16)
    h_lo = (hn - h_hi.astype(F32)).astype(BF16)
    w_hi = w.astype(BF16)
    w_lo = (w - w_hi.astype(F32)).astype(BF16)
    logits = (jnp.dot(h_hi, w_hi, preferred_element_type=F32)
              + (jnp.dot(h_hi, w_lo, preferred_element_type=F32)
                 + jnp.dot(h_lo, w_hi, preferred_element_type=F32))) + br_ref[...]
    tm = logits.shape[0]
    lane = lax.broadcasted_iota(jnp.int32, logits.shape, 1)
    vals = []
    hots = []
    idx_out = jnp.zeros(logits.shape, jnp.int32)
    for k in range(TOP_K):
        m = jnp.max(logits, axis=-1, keepdims=True)
        idx = jnp.min(jnp.where(logits == m, lane, LANES), axis=-1, keepdims=True)
        hit = lane == idx
        idx_out = jnp.where(lane == k, idx, idx_out)
        logits = jnp.where(hit, -jnp.inf, logits)
        vals.append(m)
        hots.append(jnp.where(hit, 1.0, 0.0))
    exps = [jnp.exp(v - vals[0]) for v in vals]
    den = exps[0] + exps[1] + exps[2] + exps[3]
    gate_out = jnp.zeros(logits.shape, F32)
    for k in range(TOP_K):
        gate_out = jnp.where(lane == k, exps[k] / den, gate_out)
    idx_ref[...] = idx_out
    gate_ref[...] = gate_out

    hot = hots[0] + hots[1] + hots[2] + hots[3]
    ri = lax.broadcasted_iota(jnp.int32, (tm, tm), 0)
    ci = lax.broadcasted_iota(jnp.int32, (tm, tm), 1)
    before = jnp.where(ri > ci, 1.0, 0.0).astype(BF16)
    ahead = jnp.dot(before, hot.astype(BF16), preferred_element_type=F32) + run_ref[...]
    pos_out = jnp.zeros(logits.shape, F32)
    for k in range(TOP_K):
        pos_out = jnp.where(lane == k, jnp.sum(ahead * hots[k], axis=-1, keepdims=True), pos_out)
    pos_ref[...] = pos_out
    run_ref[...] = run_ref[...] + jnp.sum(hot, axis=0, keepdims=True)
    cnt_ref[...] = run_ref[...]


def _router(h, norm_w, w_router, b_router):
    t = h.shape[0]
    tm = ROUTER_TOKENS
    pad = LANES - N_EXPERTS
    wr = jnp.pad(w_router, ((0, 0), (0, pad)))
    br = jnp.pad(b_router, (0, pad), constant_values=NEG_BIG).reshape(1, LANES)
    tok_spec = pl.BlockSpec((tm, LANES), lambda i: (i, 0))
    return pl.pallas_call(
        _router_kernel,
        grid=(t // tm,),
        in_specs=[pl.BlockSpec((tm, D_MODEL), lambda i: (i, 0)),
                  pl.BlockSpec((1, D_MODEL), lambda i: (0, 0)),
                  pl.BlockSpec((D_MODEL, LANES), lambda i: (0, 0)),
                  pl.BlockSpec((1, LANES), lambda i: (0, 0))],
        out_specs=[tok_spec, tok_spec, tok_spec, pl.BlockSpec((1, LANES), lambda i: (0, 0))],
        out_shape=[jax.ShapeDtypeStruct((t, LANES), jnp.int32),
                   jax.ShapeDtypeStruct((t, LANES), F32),
                   jax.ShapeDtypeStruct((t, LANES), F32),
                   jax.ShapeDtypeStruct((1, LANES), F32)],
        scratch_shapes=[pltpu.VMEM((1, LANES), F32)],
        compiler_params=_cparams(("arbitrary",)),
    )(h, norm_w.reshape(1, D_MODEL), wr, br)


def _dest_kernel(idx_ref, pos_ref, cnt_ref, dest_ref, bounds_ref):
    cnt = jnp.broadcast_to(cnt_ref[...], (SUBLANES, LANES))
    padded = jnp.floor((cnt + (MOE_BLOCK - 1)) * (1.0 / MOE_BLOCK)) * MOE_BLOCK
    ri = lax.broadcasted_iota(jnp.int32, (LANES, LANES), 0)
    ci = lax.broadcasted_iota(jnp.int32, (LANES, LANES), 1)
    upto = jnp.where(ri <= ci, 1.0, 0.0).astype(F32)
    seg_end = jnp.dot(padded, upto, preferred_element_type=F32, precision=lax.Precision.HIGHEST)
    seg_start = seg_end - padded
    blocks = padded * (1.0 / MOE_BLOCK)
    full = jnp.floor(blocks * (1.0 * MOE_BLOCK / MOE_ITEM))
    items = blocks - full
    items_end = jnp.dot(items, upto, preferred_element_type=F32, precision=lax.Precision.HIGHEST)
    row = lax.broadcasted_iota(jnp.int32, (SUBLANES, LANES), 0)
    table = jnp.zeros((SUBLANES, LANES), F32)
    for r, v in enumerate((seg_start, seg_end, items_end, items, full)):
        table = jnp.where(row == r, v, table)
    bounds_ref[...] = table.astype(jnp.int32)

    idx = idx_ref[...]
    lane = lax.broadcasted_iota(jnp.int32, idx.shape, 1)
    out = pos_ref[...]
    for k in range(TOP_K):
        hit = lane == idx[:, k:k + 1]
        base = jnp.sum(jnp.where(hit, seg_start[0:1, :], 0.0), axis=-1, keepdims=True)
        out = jnp.where(lane == k, out + base, out)
    dest_ref[...] = out.T[0:SUBLANES, :].astype(jnp.int32)


def _dest_rows(idx_pad, pos_pad, cnt_pad):
    t = idx_pad.shape[0]
    tm = ROUTER_TOKENS
    tok_spec = pl.BlockSpec((tm, LANES), lambda i: (i, 0))
    return pl.pallas_call(
        _dest_kernel,
        grid=(t // tm,),
        in_specs=[tok_spec, tok_spec, pl.BlockSpec((1, LANES), lambda i: (0, 0))],
        out_specs=[pl.BlockSpec((SUBLANES, tm), lambda i: (0, i)),
                   pl.BlockSpec((SUBLANES, LANES), lambda i: (0, 0))],
        out_shape=[jax.ShapeDtypeStruct((SUBLANES, t), jnp.int32),
                   jax.ShapeDtypeStruct((SUBLANES, LANES), jnp.int32)],
        compiler_params=_cparams(("arbitrary",)),
    )(idx_pad, pos_pad, cnt_pad)


def _pack_halves(x):
    c = x.shape[1] // 2
    lo = pltpu.bitcast(x[:, :c].astype(BF16).astype(F32), jnp.uint32)
    hi = pltpu.bitcast(x[:, c:].astype(BF16).astype(F32), jnp.uint32)
    return (lo >> 16) | hi


def _unpack_halves(w):
    lo = pltpu.bitcast(w << 16, F32)
    hi = pltpu.bitcast(w & jnp.uint32(0xFFFF0000), F32)
    return jnp.concatenate([lo, hi], axis=1).astype(BF16)


def _scatter_kernel(dest_ref, bounds_ref, h_ref, nw_ref, x_ref, zero_ref, packed_ref, zsem, sem):
    n_blocks = x_ref.shape[0] // MOE_BLOCK
    packed_ref[...] = _pack_halves(_rms(h_ref[...], nw_ref[...]))

    @pl.when(pl.program_id(0) == 0)
    def _():
        zero_ref[...] = jnp.zeros_like(zero_ref)
        n_used = bounds_ref[1, N_EXPERTS - 1] // MOE_BLOCK

        def zero_copy(b):
            return pltpu.make_async_copy(zero_ref, x_ref.at[pl.ds(pl.multiple_of(b * MOE_BLOCK, MOE_BLOCK),
                                                                   MOE_BLOCK)], zsem)

        def last_block(e):
            return bounds_ref[1, e] // MOE_BLOCK - 1, bounds_ref[1, e] > bounds_ref[0, e]

        def start_e(e, carry):
            b, nonempty = last_block(e)

            @pl.when(nonempty)
            def _():
                zero_copy(b).start()
            return carry

        def wait_e(e, carry):
            b, nonempty = last_block(e)

            @pl.when(nonempty)
            def _():
                zero_copy(b).wait()
            return carry

        lax.fori_loop(0, N_EXPERTS, start_e, 0)
        lax.fori_loop(n_used, n_blocks, lambda b, c: (zero_copy(b).start(), c)[1], 0)
        lax.fori_loop(0, N_EXPERTS, wait_e, 0)
        lax.fori_loop(n_used, n_blocks, lambda b, c: (zero_copy(b).wait(), c)[1], 0)

    def issue(r, carry):
        for k in range(TOP_K):
            d = dest_ref[k, r]
            pltpu.make_async_copy(packed_ref.at[pl.ds(r, 1)], x_ref.at[pl.ds(d, 1)], sem).start(priority=k % 2)
        return carry

    lax.fori_loop(0, SCATTER_TOKENS, issue, 0)
    n_sent = SCATTER_TOKENS * TOP_K
    pltpu.make_async_copy(x_ref.at[pl.ds(0, n_sent)], x_ref.at[pl.ds(0, n_sent)], sem).wait()


def _scatter_rows(h, norm_w, dest_t, bounds, n_rows):
    t = h.shape[0]
    steps = t // SCATTER_TOKENS
    half = D_MODEL // 2
    return pl.pallas_call(
        _scatter_kernel,
        grid=(steps,),
        in_specs=[pl.BlockSpec((SUBLANES, SCATTER_TOKENS), lambda i: (0, i), memory_space=pltpu.SMEM),
                  pl.BlockSpec(memory_space=pltpu.SMEM),
                  pl.BlockSpec((SCATTER_TOKENS, D_MODEL), lambda i: (i, 0)),
                  pl.BlockSpec((1, D_MODEL), lambda i: (0, 0))],
        out_specs=pl.BlockSpec(memory_space=pl.ANY),
        out_shape=jax.ShapeDtypeStruct((n_rows, half), jnp.uint32),
        scratch_shapes=[pltpu.VMEM((MOE_BLOCK, half), jnp.uint32), pltpu.VMEM((SCATTER_TOKENS, half), jnp.uint32),
                        pltpu.SemaphoreType.DMA(()), pltpu.SemaphoreType.DMA(())],
        compiler_params=_cparams(("arbitrary",)),
    )(dest_t, bounds, h, norm_w.reshape(1, D_MODEL))


def _item_tables(bounds, n_j, n_items_max):
    items_e = bounds[3]
    first_e = (bounds[2] - items_e) * n_j
    end_e = bounds[2] * n_j
    n_items = end_e[N_EXPERTS - 1]
    g = jnp.minimum(jnp.arange(n_items_max, dtype=jnp.int32), n_items - 1)
    e = jnp.sum((g[:, None] >= end_e[None, :N_EXPERTS]).astype(jnp.int32), axis=1)
    own = e[:, None] == jnp.arange(LANES, dtype=jnp.int32)[None, :]
    pick = lambda v: jnp.sum(jnp.where(own, v[None, :], 0), axis=1)
    first = pick(first_e)
    nb = jnp.maximum(pick(items_e), 1)
    q = g - first
    j = q // nb
    i = q % nb
    row = pick(bounds[0] // MOE_BLOCK) + i * (MOE_ITEM // MOE_BLOCK)
    single = (i >= pick(bounds[4])).astype(jnp.int32)
    meta = e * 16 + single * 8 + j * 2 + (i == 0).astype(jnp.int32)
    group_end = first + (j + 1) * nb
    counts = jnp.stack([n_items, bounds[1, N_EXPERTS - 1] // MOE_BLOCK])
    return meta.astype(jnp.int32), row.astype(jnp.int32), group_end.astype(jnp.int32), counts.astype(jnp.int32)


def _meta_expert(m):
    return m >> 4


def _meta_tile(m):
    return (m >> 1) & 3


def _item_loop(meta_ref, row_ref, gend_ref, n_ref, src_ref, dst_ref, w_refs, ring, ring_sem, wb_ref,
               in_buf, out_buf, in_sem, out_sem, compute):
    n = n_ref[0]
    n_in = in_buf.shape[0]
    n_out = out_buf.shape[0]
    tn = out_buf.shape[2]
    n_j = dst_ref.shape[1] // tn
    n_mats = len(w_refs)
    cpm = W_CHUNKS // n_mats
    kc = w_refs[0].shape[1] // cpm

    def by_size(m, fn):
        @pl.when(((m >> 3) & 1) == 0)
        def _():
            fn(MOE_ITEM)

        @pl.when(((m >> 3) & 1) == 1)
        def _():
            fn(MOE_BLOCK)

    def rows(g, size):
        return pl.ds(pl.multiple_of(row_ref[g] * MOE_BLOCK, MOE_BLOCK), size)

    def cols(m):
        return pl.ds(pl.multiple_of(_meta_tile(m) * tn, tn), tn)

    def in_copy(g, size):
        slot = g % n_in
        return pltpu.make_async_copy(src_ref.at[rows(g, size)], in_buf.at[slot, pl.ds(0, size)], in_sem.at[slot])

    def out_copy(g, size):
        slot = g % n_out
        return pltpu.make_async_copy(out_buf.at[slot, pl.ds(0, size)], dst_ref.at[rows(g, size), cols(meta_ref[g])],
                                     out_sem.at[slot])

    def start_in(g):
        by_size(meta_ref[g], lambda size: in_copy(g, size).start(priority=ROW_DMA_PRIORITY))

    def wait_in(g):
        by_size(meta_ref[g], lambda size: in_copy(g, size).wait())

    def start_out(g):
        by_size(meta_ref[g], lambda size: out_copy(g, size).start(priority=ROW_DMA_PRIORITY))

    def wait_out(g):
        by_size(meta_ref[g], lambda size: out_copy(g, size).wait())

    def krows(c):
        r0 = (c % cpm) * kc
        return pl.ds(r0 if isinstance(c, int) else pl.multiple_of(r0, kc), kc)

    def chunk_copy(m, c, k):
        slot = c % W_RING
        return pltpu.make_async_copy(w_refs[k].at[_meta_expert(m), krows(c), cols(m)], ring.at[slot],
                                     ring_sem.at[slot])

    def for_matrix(c, fn):
        if isinstance(c, int):
            return fn(c // cpm)
        for k in range(n_mats):
            @pl.when(c // cpm == k)
            def _(k=k):
                fn(k)

    def start_chunk(m, c):
        for_matrix(c, lambda k: chunk_copy(m, c, k).start())

    def finish_chunk(m, c, wslot):
        def fn(k):
            chunk_copy(m, c, k).wait()
            wb_ref[wslot, k, krows(c), :] = ring[c % W_RING].astype(BF16)
        for_matrix(c, fn)

        @pl.when(c + W_RING < W_CHUNKS)
        def _():
            start_chunk(m, c + W_RING)

    for c0 in range(W_RING):
        start_chunk(meta_ref[0], c0)
    for g0 in range(n_in - 1):
        @pl.when(g0 < n)
        def _():
            start_in(g0)

    def body(g, carry):
        parity, done, nm = carry
        m = meta_ref[g]
        first = (m & 1) == 1

        @pl.when(first)
        def _():
            lax.fori_loop(done, W_CHUNKS, lambda c, z: (finish_chunk(m, c, 1 - parity), z)[1], 0)
            nxt = gend_ref[g]

            @pl.when(nxt < n)
            def _():
                for c0 in range(W_RING):
                    start_chunk(meta_ref[nxt], c0)

        nxt = gend_ref[g]
        parity = jnp.where(first, 1 - parity, parity)
        done = jnp.where(first, 0, done)
        nm = jnp.where(first, jnp.where(nxt < n, meta_ref[jnp.minimum(nxt, n - 1)], -1), nm)

        wait_in(g)

        @pl.when(g + n_in - 1 < n)
        def _():
            start_in(g + n_in - 1)

        @pl.when(g >= n_out)
        def _():
            wait_out(g - n_out)

        def run(size):
            x = in_buf[g % n_in, pl.ds(0, size)]
            out_buf[g % n_out, pl.ds(0, size)] = compute(x, wb_ref.at[parity], m).astype(out_buf.dtype)

        by_size(m, run)
        start_out(g)

        active = (nm >= 0) & (done < W_CHUNKS)

        @pl.when(active)
        def _():
            finish_chunk(nm, done, 1 - parity)

        return parity, jnp.where(active, done + 1, done), nm

    lax.fori_loop(0, n, body, (jnp.int32(1), jnp.int32(0), meta_ref[0]))
    lax.fori_loop(jnp.maximum(n - n_out, 0), n, lambda g, c: (wait_out(g), c)[1], 0)

    n_blocks = dst_ref.shape[0] // MOE_BLOCK
    n_used = n_ref[1]
    out_buf[0] = jnp.zeros(out_buf.shape[1:], out_buf.dtype)
    for jt in range(n_j):
        def tail_copy(b, jt=jt):
            r = pl.ds(pl.multiple_of(b * MOE_BLOCK, MOE_BLOCK), MOE_BLOCK)
            return pltpu.make_async_copy(out_buf.at[0, pl.ds(0, MOE_BLOCK)], dst_ref.at[r, jt * tn:(jt + 1) * tn],
                                         out_sem.at[0])

        lax.fori_loop(n_used, n_blocks, lambda b, c: (tail_copy(b).start(), c)[1], 0)
        lax.fori_loop(n_used, n_blocks, lambda b, c: (tail_copy(b).wait(), c)[1], 0)


def _expert_up_kernel(meta_ref, row_ref, gend_ref, n_ref, x_ref, bg_ref, bu_ref, wg_ref, wu_ref, act_ref,
                      ring, wb_ref, x_buf, a_buf, ring_sem, in_sem, out_sem):
    n_j = D_FF // MOE_UP_TN

    def compute(x, w, m):
        tile = _meta_expert(m) * n_j + _meta_tile(m)
        xb = _unpack_halves(x)
        g = jnp.dot(xb, w[0], preferred_element_type=F32) + bg_ref[pl.ds(tile, 1), :]
        u = jnp.dot(xb, w[1], preferred_element_type=F32) + bu_ref[pl.ds(tile, 1), :]
        g = jnp.minimum(g, SWIGLU_LIMIT)
        u = jnp.clip(u, -SWIGLU_LIMIT, SWIGLU_LIMIT)
        return g * _sigmoid(SWIGLU_ALPHA * g) * (u + 1.0)

    _item_loop(meta_ref, row_ref, gend_ref, n_ref, x_ref, act_ref, (wg_ref, wu_ref), ring, ring_sem, wb_ref,
               x_buf, a_buf, in_sem, out_sem, compute)


def _expert_down_kernel(meta_ref, row_ref, gend_ref, n_ref, a_ref, bd_ref, wd_ref, y_ref,
                        ring, wb_ref, a_buf, y_buf, ring_sem, in_sem, out_sem):
    n_j = D_MODEL // MOE_DOWN_TN

    def compute(a, w, m):
        tile = _meta_expert(m) * n_j + _meta_tile(m)
        return jnp.dot(a, w[0], preferred_element_type=F32) + bd_ref[pl.ds(tile, 1), :]

    _item_loop(meta_ref, row_ref, gend_ref, n_ref, a_ref, y_ref, (wd_ref,), ring, ring_sem, wb_ref,
               a_buf, y_buf, in_sem, out_sem, compute)


def _experts(x_rows, bounds, w_gate, b_gate, w_up, b_up, w_down, b_down):
    n_rows = x_rows.shape[0]
    n_items_max = n_rows // MOE_ITEM + N_EXPERTS
    any_spec = pl.BlockSpec(memory_space=pl.ANY)
    sems = [pltpu.SemaphoreType.DMA((W_RING,)), pltpu.SemaphoreType.DMA((MOE_IN_BUFS,)),
            pltpu.SemaphoreType.DMA((MOE_OUT_BUFS,))]

    def full(shape):
        return pl.BlockSpec(shape, lambda i, *_: (0,) * len(shape))

    n_j = D_FF // MOE_UP_TN
    act = pl.pallas_call(
        _expert_up_kernel,
        grid_spec=pltpu.PrefetchScalarGridSpec(
            num_scalar_prefetch=4, grid=(1,),
            in_specs=[any_spec, full((N_EXPERTS * n_j, MOE_UP_TN)),
                      full((N_EXPERTS * n_j, MOE_UP_TN)), any_spec, any_spec],
            out_specs=any_spec,
            scratch_shapes=[pltpu.VMEM((W_RING, 2 * D_MODEL // W_CHUNKS, MOE_UP_TN), F32),
                            pltpu.VMEM((2, 2, D_MODEL, MOE_UP_TN), BF16),
                            pltpu.VMEM((MOE_IN_BUFS, MOE_ITEM, D_MODEL // 2), jnp.uint32),
                            pltpu.VMEM((MOE_OUT_BUFS, MOE_ITEM, MOE_UP_TN), BF16)] + sems),
        out_shape=jax.ShapeDtypeStruct((n_rows, D_FF), BF16),
        compiler_params=_cparams(("arbitrary",)),
    )(*_item_tables(bounds, n_j, n_items_max * n_j), x_rows,
      b_gate.reshape(N_EXPERTS * n_j, MOE_UP_TN), b_up.reshape(N_EXPERTS * n_j, MOE_UP_TN), w_gate, w_up)

    n_j = D_MODEL // MOE_DOWN_TN
    return pl.pallas_call(
        _expert_down_kernel,
        grid_spec=pltpu.PrefetchScalarGridSpec(
            num_scalar_prefetch=4, grid=(1,),
            in_specs=[any_spec, full((N_EXPERTS * n_j, MOE_DOWN_TN)), any_spec],
            out_specs=any_spec,
            scratch_shapes=[pltpu.VMEM((W_RING, D_FF // W_CHUNKS, MOE_DOWN_TN), F32),
                            pltpu.VMEM((2, 1, D_FF, MOE_DOWN_TN), BF16),
                            pltpu.VMEM((MOE_IN_BUFS, MOE_ITEM, D_FF), BF16),
                            pltpu.VMEM((MOE_OUT_BUFS, MOE_ITEM, MOE_DOWN_TN), F32)] + sems),
        out_shape=jax.ShapeDtypeStruct((n_rows, D_MODEL), F32),
        compiler_params=_cparams(("arbitrary",)),
    )(*_item_tables(bounds, n_j, n_items_max * n_j), act,
      b_down.reshape(N_EXPERTS * n_j, MOE_DOWN_TN), w_down)


def _combine_kernel(dest_ref, next_ref, h_ref, gate_ref, nw_ref, y_ref, o_ref, buf_ref, sem):
    i = pl.program_id(0)
    steps = pl.num_programs(0)

    def issue(d_ref, slot):
        def body(r, carry):
            for k in range(TOP_K):
                pltpu.make_async_copy(y_ref.at[pl.ds(d_ref[k, r], 1)], buf_ref.at[slot, k, pl.ds(r, 1)],
                                      sem.at[slot]).start(priority=k % 2)
            return carry
        lax.fori_loop(0, COMBINE_TOKENS, body, 0)

    @pl.when(i == 0)
    def _():
        issue(dest_ref, 0)

    @pl.when(i + 1 < steps)
    def _():
        issue(next_ref, (i + 1) % 2)

    slot = i % 2
    pltpu.make_async_copy(buf_ref.at[slot], buf_ref.at[slot], sem.at[slot]).wait()
    acc = h_ref[...]
    gate = gate_ref[...]
    for k in range(TOP_K):
        acc = acc + gate[:, k:k + 1] * buf_ref[slot, k]
    o_ref[...] = _rms(acc, nw_ref[...])


def _combine(h, y_rows, dest_t, gate_pad, norm_w):
    t = h.shape[0]
    steps = t // COMBINE_TOKENS
    return pl.pallas_call(
        _combine_kernel,
        grid=(steps,),
        in_specs=[pl.BlockSpec((SUBLANES, COMBINE_TOKENS), lambda i: (0, i), memory_space=pltpu.SMEM),
                  pl.BlockSpec((SUBLANES, COMBINE_TOKENS), lambda i: (0, jnp.minimum(i + 1, steps - 1)),
                               memory_space=pltpu.SMEM),
                  pl.BlockSpec((COMBINE_TOKENS, D_MODEL), lambda i: (i, 0)),
                  pl.BlockSpec((COMBINE_TOKENS, LANES), lambda i: (i, 0)),
                  pl.BlockSpec((1, D_MODEL), lambda i: (0, 0)),
                  pl.BlockSpec(memory_space=pl.ANY)],
        out_specs=pl.BlockSpec((COMBINE_TOKENS, D_MODEL), lambda i: (i, 0)),
        out_shape=jax.ShapeDtypeStruct((t, D_MODEL), F32),
        scratch_shapes=[pltpu.VMEM((2, TOP_K, COMBINE_TOKENS, D_MODEL), F32), pltpu.SemaphoreType.DMA((2,))],
        compiler_params=_cparams(("arbitrary",)),
    )(dest_t, dest_t, h, gate_pad, norm_w.reshape(1, D_MODEL), y_rows)


def _moe(h, norm_ffn, w_router, b_router, w_gate, b_gate, w_up, b_up, w_down, b_down, norm_final):
    t = h.shape[0]
    tk = t * TOP_K
    n_blocks = (tk + N_EXPERTS * (MOE_BLOCK - 1) + MOE_BLOCK - 1) // MOE_BLOCK
    n_rows = n_blocks * MOE_BLOCK
    idx_pad, gate_pad, pos_pad, cnt_pad = _router(h, norm_ffn, w_router, b_router)
    dest_t, bounds = _dest_rows(idx_pad, pos_pad, cnt_pad)
    x_rows = _scatter_rows(h, norm_ffn, dest_t, bounds, n_rows)
    y_rows = _experts(x_rows, bounds, w_gate, b_gate, w_up, b_up, w_down, b_down)
    return _combine(h, y_rows, dest_t, gate_pad, norm_final)


def kernel(x, mem, norm_mix, w_in, conv_w, conv_b, dt_bias, a_log, d_skip, ssd_norm, pool_w, pool_scale,
           w_out, norm_xattn, norm_mem, w_q, w_kv, w_o, norm_ffn, w_router, b_router, w_gate, b_gate,
           w_up, b_up, w_down, b_down, norm_final):
    batch, seq, d = x.shape
    t = batch * seq
    h = x.reshape(t, d)
    depth = norm_mix.shape[0]
    assert depth == 1, "the final norm is fused into the MoE combine of the single layer"
    for l in range(depth):
        dt_lo = D_SSD + D_CONV
        w_main = jnp.concatenate([w_in[l][:, :dt_lo], w_in[l][:, dt_lo + SSD_HEADS:]], axis=1)
        w_dt = jnp.pad(w_in[l][:, dt_lo:dt_lo + SSD_HEADS], ((0, 0), (0, LANES - SSD_HEADS)))
        proj = _mm(h, w_main, norm_w=norm_mix[l])
        dtp = _mm(h, w_dt, norm_w=norm_mix[l])
        mix = _mixer(proj, dtp, conv_w[l], conv_b[l], dt_bias[l], a_log[l], d_skip[l], ssd_norm[l],
                     pool_w[l], pool_scale[l], batch, seq)
        h = _mm(mix, w_out[l], res=h)

        q = _mm(h, w_q[l], norm_w=norm_xattn[l], out_dtype=BF16)
        kv = _mm(mem.reshape(batch * N_MEM, d), w_kv[l], norm_w=norm_mem[l], out_dtype=BF16)
        o = _attn(q, kv, batch, seq)
        h = _mm(o, w_o[l], res=h)

        out = _moe(h, norm_ffn[l], w_router[l], b_router[l], w_gate[l], b_gate[l], w_up[l], b_up[l],
                   w_down[l], b_down[l], norm_final)
    return out.reshape(batch, seq, d)
```

```python
import functools

import jax
import jax.numpy as jnp
from jax import lax
from jax.experimental import pallas as pl
from jax.experimental.pallas import tpu as pltpu

F32 = jnp.float32
BF16 = jnp.bfloat16

D_MODEL = 2048
D_SSD = 1024
SSD_HEAD_DIM = 64
SSD_HEADS = 16
SSD_GROUPS = 4
HEADS_PER_GROUP = SSD_HEADS // SSD_GROUPS
D_STATE = 128
CONV_WIDTH = 4
CHUNK = 128
D_CONV = D_SSD + 2 * SSD_GROUPS * D_STATE
D_POOL = 1024
POOL_WINDOWS = (2, 4, 8, 16)
POOL_GROUP_DIM = 256
N_MEM = 256
XATTN_HEADS = 4
XATTN_HEAD_DIM = 512
N_EXPERTS = 32
TOP_K = 4
D_FF = 2048
SWIGLU_LIMIT = 7.0
SWIGLU_ALPHA = 1.702
EPS = 1e-5

LANES = 128
SUBLANES = 8
VMEM_LIMIT = 56 * 1024 * 1024
NEG_BIG = -1e30

MOE_BLOCK = 256
MOE_ITEM = 2 * MOE_BLOCK
MOE_UP_TN = 512
MOE_DOWN_TN = 1024
MOE_IN_BUFS = 3
MOE_OUT_BUFS = 2
W_CHUNKS = 4
W_RING = 4
ROUTER_TOKENS = 512
SCATTER_TOKENS = 256
COMBINE_TOKENS = 128
ROW_DMA_PRIORITY = 1


def _cparams(sem):
    return pltpu.CompilerParams(dimension_semantics=sem, vmem_limit_bytes=VMEM_LIMIT)


def _rms(x, w):
    ms = jnp.mean(x * x, axis=-1, keepdims=True)
    return x * lax.rsqrt(ms + EPS) * w


def _mm_kernel(*refs, has_norm, has_res):
    it = iter(refs)
    a_ref = next(it)
    w_ref = next(it)
    nw_ref = next(it) if has_norm else None
    r_ref = next(it) if has_res else None
    o_ref = next(it)
    wb_ref = next(it)

    @pl.when(pl.program_id(1) == 0)
    def _():
        wb_ref[...] = w_ref[...].astype(BF16)

    a = a_ref[...]
    if has_norm:
        a = _rms(a.astype(F32), nw_ref[...])
    acc = jnp.dot(a.astype(BF16), wb_ref[...], preferred_element_type=F32)
    if has_res:
        acc = acc + r_ref[...]
    o_ref[...] = acc.astype(o_ref.dtype)


def _mm(a, w, norm_w=None, res=None, out_dtype=F32, tm=1024, tn=1024):
    m, k = a.shape
    n = w.shape[1]
    tn = min(tn, n)
    tm = min(tm, m)
    assert m % tm == 0 and n % tn == 0
    in_specs = [pl.BlockSpec((tm, k), lambda j, i: (i, 0)),
                pl.BlockSpec((k, tn), lambda j, i: (0, j))]
    args = [a, w]
    if norm_w is not None:
        in_specs.append(pl.BlockSpec((1, k), lambda j, i: (0, 0)))
        args.append(norm_w.reshape(1, k))
    if res is not None:
        in_specs.append(pl.BlockSpec((tm, tn), lambda j, i: (i, j)))
        args.append(res)
    return pl.pallas_call(
        functools.partial(_mm_kernel, has_norm=norm_w is not None, has_res=res is not None),
        grid=(n // tn, m // tm),
        in_specs=in_specs,
        out_specs=pl.BlockSpec((tm, tn), lambda j, i: (i, j)),
        out_shape=jax.ShapeDtypeStruct((m, n), out_dtype),
        scratch_shapes=[pltpu.VMEM((k, tn), BF16)],
        compiler_params=_cparams(("arbitrary", "arbitrary")),
    )(*args)


def _sigmoid(x):
    return 1.0 / (1.0 + jnp.exp(-x))


def _shift_rows(cur, prev, j, rows):
    return jnp.where(rows < j, pltpu.roll(prev, j, 0), pltpu.roll(cur, j, 0))


def _conv_silu(cur, prev, w_ref, b_ref, col0, rows):
    width = cur.shape[1]
    acc = cur * w_ref[CONV_WIDTH - 1:CONV_WIDTH, col0:col0 + width] + b_ref[:, col0:col0 + width]
    for j in range(1, CONV_WIDTH):
        k = CONV_WIDTH - 1 - j
        acc = acc + _shift_rows(cur, prev, j, rows) * w_ref[k:k + 1, col0:col0 + width]
    return acc * _sigmoid(acc)


def _mixer_kernel(z_ref, x_ref, bc_ref, u_ref, dt_ref, convw_ref, convb_ref, dtb_ref, alog_ref,
                  dskip_ref, ssdn_ref, poolw_ref, pscale_ref, out_ref,
                  px_ref, pbc_ref, pu_ref, state_ref):
    c = pl.program_id(1)

    @pl.when(c == 0)
    def _():
        px_ref[...] = jnp.zeros_like(px_ref)
        pbc_ref[...] = jnp.zeros_like(pbc_ref)
        pu_ref[...] = jnp.zeros_like(pu_ref)
        state_ref[...] = jnp.zeros_like(state_ref)

    rows = lax.broadcasted_iota(jnp.int32, (CHUNK, D_SSD), 0)
    x_raw = x_ref[...]
    bc_raw = bc_ref[...]
    xs = _conv_silu(x_raw, px_ref[...], convw_ref, convb_ref, 0, rows)
    bcs = _conv_silu(bc_raw, pbc_ref[...], convw_ref, convb_ref, D_SSD, rows)
    px_ref[...] = x_raw
    pbc_ref[...] = bc_raw

    dt_in = dt_ref[...] + dtb_ref[...]
    dt = jnp.maximum(dt_in, 0.0) + jnp.log(1.0 + jnp.exp(-jnp.abs(dt_in)))
    a = -jnp.exp(alog_ref[...])
    dta = dt * a
    ri = lax.broadcasted_iota(jnp.int32, (CHUNK, CHUNK), 0)
    ci = lax.broadcasted_iota(jnp.int32, (CHUNK, CHUNK), 1)
    causal = ri >= ci
    tril = jnp.where(causal, 1.0, 0.0).astype(F32)
    acs = jnp.dot(tril, dta, preferred_element_type=F32, precision=lax.Precision.HIGHEST)
    acs_t = acs.T
    dt_t = dt.T
    lo = ci < SSD_HEAD_DIM

    y_pairs = []
    for g in range(SSD_GROUPS):
        bg = bcs[:, g * D_STATE:(g + 1) * D_STATE].astype(BF16)
        cg = bcs[:, SSD_GROUPS * D_STATE + g * D_STATE:SSD_GROUPS * D_STATE + (g + 1) * D_STATE].astype(BF16)
        cb = lax.dot_general(cg, bg, (((1,), (1,)), ((), ())), preferred_element_type=F32)
        gw = HEADS_PER_GROUP * SSD_HEAD_DIM
        prev_g = state_ref[g * gw:(g + 1) * gw, :]
        yoff = lax.dot_general(cg, prev_g.astype(BF16), (((1,), (1,)), ((), ())),
                               preferred_element_type=F32)
        xw_parts = []
        for pr in range(HEADS_PER_GROUP // 2):
            pair = g * (HEADS_PER_GROUP // 2) + pr
            xs_pair = xs[:, pair * LANES:(pair + 1) * LANES]
            y_pair = None
            ecols = []
            wcols = []
            for q in range(2):
                h = pair * 2 + q
                col = acs[:, h:h + 1]
                seg = col - acs_t[h:h + 1, :]
                decay = jnp.exp(jnp.where(causal, seg, NEG_BIG))
                mat = (cb * decay * dt_t[h:h + 1, :]).astype(BF16)
                keep = lo if q == 0 else jnp.logical_not(lo)
                xh = jnp.where(keep, xs_pair, 0.0).astype(BF16)
                yd = jnp.dot(mat, xh, preferred_element_type=F32)
                y_pair = yd if y_pair is None else y_pair + yd
                ecols.append(jnp.exp(col))
                wcols.append(dt[:, h:h + 1] * jnp.exp(acs[CHUNK - 1:CHUNK, h:h + 1] - col))
            e_pair = jnp.where(lo, ecols[0], ecols[1])
            w_pair = jnp.where(lo, wcols[0], wcols[1])
            y_pairs.append(y_pair + yoff[:, pr * LANES:(pr + 1) * LANES] * e_pair)
            xw_parts.append(xs_pair * w_pair)
        xw = jnp.concatenate(xw_parts, axis=1).astype(BF16)
        s_new = lax.dot_general(xw, bg, (((0,), (0,)), ((), ())), preferred_element_type=F32)
        for r in range(HEADS_PER_GROUP):
            h = g * HEADS_PER_GROUP + r
            dec = jnp.exp(acs[CHUNK - 1:CHUNK, h:h + 1])
            lo_r = h * SSD_HEAD_DIM
            state_ref[lo_r:lo_r + SSD_HEAD_DIM, :] = (
                state_ref[lo_r:lo_r + SSD_HEAD_DIM, :] * dec
                + s_new[r * SSD_HEAD_DIM:(r + 1) * SSD_HEAD_DIM, :])

    y = jnp.concatenate(y_pairs, axis=1)
    y = y + dskip_ref[...] * xs
    z = z_ref[...]
    y = y * (z * _sigmoid(z))
    gdim = D_SSD // SSD_GROUPS
    for g in range(SSD_GROUPS):
        yg = _rms(y[:, g * gdim:(g + 1) * gdim], ssdn_ref[:, g * gdim:(g + 1) * gdim])
        out_ref[:, g * gdim:(g + 1) * gdim] = yg.astype(out_ref.dtype)

    u_cur = u_ref[...]
    u_prev = pu_ref[...]
    pos = (c * CHUNK + lax.broadcasted_iota(jnp.int32, (CHUNK, POOL_GROUP_DIM), 0) + 1).astype(F32)
    for g, win in enumerate(POOL_WINDOWS):
        sl = slice(g * POOL_GROUP_DIM, (g + 1) * POOL_GROUP_DIM)
        ug = u_cur[:, sl]
        ext = jnp.concatenate([u_prev[:, sl], ug], axis=0)
        step = 1
        while step < win:
            ext = ext + pltpu.roll(ext, step, 0)
            step *= 2
        pooled = ext[CHUNK:, :] / jnp.minimum(pos, float(win)) - ug
        mixed = jnp.dot(pooled.astype(BF16), poolw_ref[g].astype(BF16), preferred_element_type=F32)
        out_ref[:, D_SSD + g * POOL_GROUP_DIM:D_SSD + (g + 1) * POOL_GROUP_DIM] = (
            mixed * pscale_ref[:, sl]).astype(out_ref.dtype)
    pu_ref[...] = u_cur


def _mixer(proj, dtp, conv_w, conv_b, dt_bias, a_log, d_skip, ssd_norm, pool_w, pool_scale, batch, seq):
    nchunk = seq // CHUNK
    row_map = lambda col: (lambda b, c: (b * nchunk + c, col))
    const2 = lambda b, c: (0, 0)
    pad = LANES - SSD_HEADS
    dtb = jnp.pad(dt_bias, (0, pad)).reshape(1, LANES)
    alog = jnp.pad(a_log, (0, pad)).reshape(1, LANES)
    dskip = jnp.repeat(d_skip, SSD_HEAD_DIM).reshape(1, D_SSD)
    return pl.pallas_call(
        _mixer_kernel,
        grid=(batch, nchunk),
        in_specs=[
            pl.BlockSpec((CHUNK, D_SSD), row_map(0)),
            pl.BlockSpec((CHUNK, D_SSD), row_map(1)),
            pl.BlockSpec((CHUNK, D_SSD), row_map(2)),
            pl.BlockSpec((CHUNK, D_POOL), row_map(3)),
            pl.BlockSpec((CHUNK, LANES), row_map(0)),
            pl.BlockSpec((CONV_WIDTH, D_CONV), const2),
            pl.BlockSpec((1, D_CONV), const2),
            pl.BlockSpec((1, LANES), const2),
            pl.BlockSpec((1, LANES), const2),
            pl.BlockSpec((1, D_SSD), const2),
            pl.BlockSpec((1, D_SSD), const2),
            pl.BlockSpec((len(POOL_WINDOWS), POOL_GROUP_DIM, POOL_GROUP_DIM), lambda b, c: (0, 0, 0)),
            pl.BlockSpec((1, D_POOL), const2),
        ],
        out_specs=pl.BlockSpec((CHUNK, D_MODEL), lambda b, c: (b * nchunk + c, 0)),
        out_shape=jax.ShapeDtypeStruct((batch * seq, D_MODEL), BF16),
        scratch_shapes=[pltpu.VMEM((CHUNK, D_SSD), F32), pltpu.VMEM((CHUNK, D_SSD), F32),
                        pltpu.VMEM((CHUNK, D_POOL), F32), pltpu.VMEM((SSD_HEADS * SSD_HEAD_DIM, D_STATE), F32)],
        compiler_params=_cparams(("arbitrary", "arbitrary")),
    )(proj, proj, proj, proj, dtp, conv_w, conv_b.reshape(1, D_CONV), dtb, alog, dskip,
      ssd_norm.reshape(1, D_SSD), pool_w, pool_scale.reshape(1, D_POOL))


def _attn_kernel(q_ref, k_ref, v_ref, o_ref):
    scale = XATTN_HEAD_DIM ** -0.5
    for h in range(XATTN_HEADS):
        sl = slice(h * XATTN_HEAD_DIM, (h + 1) * XATTN_HEAD_DIM)
        s = lax.dot_general(q_ref[:, sl], k_ref[:, sl], (((1,), (1,)), ((), ())),
                            preferred_element_type=F32) * scale
        s = s - jnp.max(s, axis=-1, keepdims=True)
        p = jnp.exp(s)
        p = p / jnp.sum(p, axis=-1, keepdims=True)
        o_ref[:, sl] = jnp.dot(p.astype(BF16), v_ref[:, sl], preferred_element_type=F32).astype(o_ref.dtype)


def _attn(q, kv, batch, seq, tq=512):
    nq = seq // tq
    return pl.pallas_call(
        _attn_kernel,
        grid=(batch, nq),
        in_specs=[pl.BlockSpec((tq, D_MODEL), lambda b, i: (b * nq + i, 0)),
                  pl.BlockSpec((N_MEM, D_MODEL), lambda b, i: (b, 0)),
                  pl.BlockSpec((N_MEM, D_MODEL), lambda b, i: (b, 1))],
        out_specs=pl.BlockSpec((tq, D_MODEL), lambda b, i: (b * nq + i, 0)),
        out_shape=jax.ShapeDtypeStruct((batch * seq, D_MODEL), BF16),
        compiler_params=_cparams(("arbitrary", "arbitrary")),
    )(q, kv, kv)


def _router_kernel(h_ref, nw_ref, wr_ref, br_ref, idx_ref, gate_ref, pos_ref, cnt_ref, run_ref):
    @pl.when(pl.program_id(0) == 0)
    def _():
        run_ref[...] = jnp.zeros_like(run_ref)

    hn = _rms(h_ref[...], nw_ref[...])
    w = wr_ref[...]
    h_hi = hn.astype(BF16)
    h_lo = (hn - h_hi.astype(F32)).astype(BF16)
    w_hi = w.astype(BF16)
    w_lo = (w - w_hi.astype(F32)).astype(BF16)
    logits = (jnp.dot(h_hi, w_hi, preferred_element_type=F32)
              + (jnp.dot(h_hi, w_lo, preferred_element_type=F32)
                 + jnp.dot(h_lo, w_hi, preferred_element_type=F32))) + br_ref[...]
    tm = logits.shape[0]
    lane = lax.broadcasted_iota(jnp.int32, logits.shape, 1)
    vals = []
    hots = []
    idx_out = jnp.zeros(logits.shape, jnp.int32)
    for k in range(TOP_K):
        m = jnp.max(logits, axis=-1, keepdims=True)
        idx = jnp.min(jnp.where(logits == m, lane, LANES), axis=-1, keepdims=True)
        hit = lane == idx
        idx_out = jnp.where(lane == k, idx, idx_out)
        logits = jnp.where(hit, -jnp.inf, logits)
        vals.append(m)
        hots.append(jnp.where(hit, 1.0, 0.0))
    exps = [jnp.exp(v - vals[0]) for v in vals]
    den = exps[0] + exps[1] + exps[2] + exps[3]
    gate_out = jnp.zeros(logits.shape, F32)
    for k in range(TOP_K):
        gate_out = jnp.where(lane == k, exps[k] / den, gate_out)
    idx_ref[...] = idx_out
    gate_ref[...] = gate_out

    hot = hots[0] + hots[1] + hots[2] + hots[3]
    ri = lax.broadcasted_iota(jnp.int32, (tm, tm), 0)
    ci = lax.broadcasted_iota(jnp.int32, (tm, tm), 1)
    before = jnp.where(ri > ci, 1.0, 0.0).astype(BF16)
    ahead = jnp.dot(before, hot.astype(BF16), preferred_element_type=F32) + run_ref[...]
    pos_out = jnp.zeros(logits.shape, F32)
    for k in range(TOP_K):
        pos_out = jnp.where(lane == k, jnp.sum(ahead * hots[k], axis=-1, keepdims=True), pos_out)
    pos_ref[...] = pos_out
    run_ref[...] = run_ref[...] + jnp.sum(hot, axis=0, keepdims=True)
    cnt_ref[...] = run_ref[...]


def _router(h, norm_w, w_router, b_router):
    t = h.shape[0]
    tm = ROUTER_TOKENS
    pad = LANES - N_EXPERTS
    wr = jnp.pad(w_router, ((0, 0), (0, pad)))
    br = jnp.pad(b_router, (0, pad), constant_values=NEG_BIG).reshape(1, LANES)
    tok_spec = pl.BlockSpec((tm, LANES), lambda i: (i, 0))
    return pl.pallas_call(
        _router_kernel,
        grid=(t // tm,),
        in_specs=[pl.BlockSpec((tm, D_MODEL), lambda i: (i, 0)),
                  pl.BlockSpec((1, D_MODEL), lambda i: (0, 0)),
                  pl.BlockSpec((D_MODEL, LANES), lambda i: (0, 0)),
                  pl.BlockSpec((1, LANES), lambda i: (0, 0))],
        out_specs=[tok_spec, tok_spec, tok_spec, pl.BlockSpec((1, LANES), lambda i: (0, 0))],
        out_shape=[jax.ShapeDtypeStruct((t, LANES), jnp.int32),
                   jax.ShapeDtypeStruct((t, LANES), F32),
                   jax.ShapeDtypeStruct((t, LANES), F32),
                   jax.ShapeDtypeStruct((1, LANES), F32)],
        scratch_shapes=[pltpu.VMEM((1, LANES), F32)],
        compiler_params=_cparams(("arbitrary",)),
    )(h, norm_w.reshape(1, D_MODEL), wr, br)


def _dest_kernel(idx_ref, pos_ref, cnt_ref, dest_ref, bounds_ref):
    cnt = jnp.broadcast_to(cnt_ref[...], (SUBLANES, LANES))
    padded = jnp.floor((cnt + (MOE_BLOCK - 1)) * (1.0 / MOE_BLOCK)) * MOE_BLOCK
    ri = lax.broadcasted_iota(jnp.int32, (LANES, LANES), 0)
    ci = lax.broadcasted_iota(jnp.int32, (LANES, LANES), 1)
    upto = jnp.where(ri <= ci, 1.0, 0.0).astype(F32)
    seg_end = jnp.dot(padded, upto, preferred_element_type=F32, precision=lax.Precision.HIGHEST)
    seg_start = seg_end - padded
    blocks = padded * (1.0 / MOE_BLOCK)
    full = jnp.floor(blocks * (1.0 * MOE_BLOCK / MOE_ITEM))
    items = blocks - full
    items_end = jnp.dot(items, upto, preferred_element_type=F32, precision=lax.Precision.HIGHEST)
    row = lax.broadcasted_iota(jnp.int32, (SUBLANES, LANES), 0)
    table = jnp.zeros((SUBLANES, LANES), F32)
    for r, v in enumerate((seg_start, seg_end, items_end, items, full)):
        table = jnp.where(row == r, v, table)
    bounds_ref[...] = table.astype(jnp.int32)

    idx = idx_ref[...]
    lane = lax.broadcasted_iota(jnp.int32, idx.shape, 1)
    out = pos_ref[...]
    for k in range(TOP_K):
        hit = lane == idx[:, k:k + 1]
        base = jnp.sum(jnp.where(hit, seg_start[0:1, :], 0.0), axis=-1, keepdims=True)
        out = jnp.where(lane == k, out + base, out)
    dest_ref[...] = out.T[0:SUBLANES, :].astype(jnp.int32)


def _dest_rows(idx_pad, pos_pad, cnt_pad):
    t = idx_pad.shape[0]
    tm = ROUTER_TOKENS
    tok_spec = pl.BlockSpec((tm, LANES), lambda i: (i, 0))
    return pl.pallas_call(
        _dest_kernel,
        grid=(t // tm,),
        in_specs=[tok_spec, tok_spec, pl.BlockSpec((1, LANES), lambda i: (0, 0))],
        out_specs=[pl.BlockSpec((SUBLANES, tm), lambda i: (0, i)),
                   pl.BlockSpec((SUBLANES, LANES), lambda i: (0, 0))],
        out_shape=[jax.ShapeDtypeStruct((SUBLANES, t), jnp.int32),
                   jax.ShapeDtypeStruct((SUBLANES, LANES), jnp.int32)],
        compiler_params=_cparams(("arbitrary",)),
    )(idx_pad, pos_pad, cnt_pad)


def _pack_halves(x):
    c = x.shape[1] // 2
    lo = pltpu.bitcast(x[:, :c].astype(BF16).astype(F32), jnp.uint32)
    hi = pltpu.bitcast(x[:, c:].astype(BF16).astype(F32), jnp.uint32)
    return (lo >> 16) | hi


def _unpack_halves(w):
    lo = pltpu.bitcast(w << 16, F32)
    hi = pltpu.bitcast(w & jnp.uint32(0xFFFF0000), F32)
    return jnp.concatenate([lo, hi], axis=1).astype(BF16)


def _scatter_kernel(dest_ref, bounds_ref, h_ref, nw_ref, x_ref, zero_ref, packed_ref, zsem, sem):
    n_blocks = x_ref.shape[0] // MOE_BLOCK
    i = pl.program_id(0)
    slot = i % 2
    packed_ref[slot] = _pack_halves(_rms(h_ref[...], nw_ref[...]))

    @pl.when(i == 0)
    def _():
        zero_ref[...] = jnp.zeros_like(zero_ref)
        n_used = bounds_ref[1, N_EXPERTS - 1] // MOE_BLOCK

        def zero_copy(b):
            return pltpu.make_async_copy(zero_ref, x_ref.at[pl.ds(pl.multiple_of(b * MOE_BLOCK, MOE_BLOCK),
                                                                   MOE_BLOCK)], zsem)

        def last_block(e):
            return bounds_ref[1, e] // MOE_BLOCK - 1, bounds_ref[1, e] > bounds_ref[0, e]

        def start_e(e, carry):
            b, nonempty = last_block(e)

            @pl.when(nonempty)
            def _():
                zero_copy(b).start()
            return carry

        def wait_e(e, carry):
            b, nonempty = last_block(e)

            @pl.when(nonempty)
            def _():
                zero_copy(b).wait()
            return carry

        lax.fori_loop(0, N_EXPERTS, start_e, 0)
        lax.fori_loop(n_used, n_blocks, lambda b, c: (zero_copy(b).start(), c)[1], 0)
        lax.fori_loop(0, N_EXPERTS, wait_e, 0)
        lax.fori_loop(n_used, n_blocks, lambda b, c: (zero_copy(b).wait(), c)[1], 0)

    def issue(r, carry):
        for k in range(TOP_K):
            d = dest_ref[k, r]
            pltpu.make_async_copy(packed_ref.at[slot, pl.ds(r, 1)], x_ref.at[pl.ds(d, 1)],
                                  sem.at[slot]).start(priority=k % 2)
        return carry

    lax.fori_loop(0, SCATTER_TOKENS, issue, 0)

    def wait_rows(s):
        n_sent = SCATTER_TOKENS * TOP_K
        pltpu.make_async_copy(x_ref.at[pl.ds(0, n_sent)], x_ref.at[pl.ds(0, n_sent)], sem.at[s]).wait()

    @pl.when(i > 0)
    def _():
        wait_rows(1 - slot)

    @pl.when(i == pl.num_programs(0) - 1)
    def _():
        wait_rows(slot)


def _scatter_rows(h, norm_w, dest_t, bounds, n_rows):
    t = h.shape[0]
    steps = t // SCATTER_TOKENS
    half = D_MODEL // 2
    return pl.pallas_call(
        _scatter_kernel,
        grid=(steps,),
        in_specs=[pl.BlockSpec((SUBLANES, SCATTER_TOKENS), lambda i: (0, i), memory_space=pltpu.SMEM),
                  pl.BlockSpec(memory_space=pltpu.SMEM),
                  pl.BlockSpec((SCATTER_TOKENS, D_MODEL), lambda i: (i, 0)),
                  pl.BlockSpec((1, D_MODEL), lambda i: (0, 0))],
        out_specs=pl.BlockSpec(memory_space=pl.ANY),
        out_shape=jax.ShapeDtypeStruct((n_rows, half), jnp.uint32),
        scratch_shapes=[pltpu.VMEM((MOE_BLOCK, half), jnp.uint32), pltpu.VMEM((2, SCATTER_TOKENS, half), jnp.uint32),
                        pltpu.SemaphoreType.DMA(()), pltpu.SemaphoreType.DMA((2,))],
        compiler_params=_cparams(("arbitrary",)),
    )(dest_t, bounds, h, norm_w.reshape(1, D_MODEL))


def _item_tables(bounds, n_j, n_items_max):
    items_e = bounds[3]
    first_e = (bounds[2] - items_e) * n_j
    end_e = bounds[2] * n_j
    n_items = end_e[N_EXPERTS - 1]
    g = jnp.minimum(jnp.arange(n_items_max, dtype=jnp.int32), n_items - 1)
    e = jnp.sum((g[:, None] >= end_e[None, :N_EXPERTS]).astype(jnp.int32), axis=1)
    own = e[:, None] == jnp.arange(LANES, dtype=jnp.int32)[None, :]
    pick = lambda v: jnp.sum(jnp.where(own, v[None, :], 0), axis=1)
    first = pick(first_e)
    nb = jnp.maximum(pick(items_e), 1)
    q = g - first
    j = q // nb
    i = q % nb
    row = pick(bounds[0] // MOE_BLOCK) + i * (MOE_ITEM // MOE_BLOCK)
    single = (i >= pick(bounds[4])).astype(jnp.int32)
    meta = e * 16 + single * 8 + j * 2 + (i == 0).astype(jnp.int32)
    group_end = first + (j + 1) * nb
    counts = jnp.stack([n_items, bounds[1, N_EXPERTS - 1] // MOE_BLOCK])
    return meta.astype(jnp.int32), row.astype(jnp.int32), group_end.astype(jnp.int32), counts.astype(jnp.int32)


def _meta_expert(m):
    return m >> 4


def _meta_tile(m):
    return (m >> 1) & 3


def _item_loop(meta_ref, row_ref, gend_ref, n_ref, src_ref, dst_ref, w_refs, ring, ring_sem, wb_ref,
               in_buf, out_buf, in_sem, out_sem, compute):
    n = n_ref[0]
    n_in = in_buf.shape[0]
    n_out = out_buf.shape[0]
    tn = out_buf.shape[2]
    n_j = dst_ref.shape[1] // tn
    n_mats = len(w_refs)
    cpm = W_CHUNKS // n_mats
    kc = w_refs[0].shape[1] // cpm

    def by_size(m, fn):
        @pl.when(((m >> 3) & 1) == 0)
        def _():
            fn(MOE_ITEM)

        @pl.when(((m >> 3) & 1) == 1)
        def _():
            fn(MOE_BLOCK)

    def rows(g, size):
        return pl.ds(pl.multiple_of(row_ref[g] * MOE_BLOCK, MOE_BLOCK), size)

    def cols(m):
        return pl.ds(pl.multiple_of(_meta_tile(m) * tn, tn), tn)

    def in_copy(g, size):
        slot = g % n_in
        return pltpu.make_async_copy(src_ref.at[rows(g, size)], in_buf.at[slot, pl.ds(0, size)], in_sem.at[slot])

    def out_copy(g, size):
        slot = g % n_out
        return pltpu.make_async_copy(out_buf.at[slot, pl.ds(0, size)], dst_ref.at[rows(g, size), cols(meta_ref[g])],
                                     out_sem.at[slot])

    def start_in(g):
        by_size(meta_ref[g], lambda size: in_copy(g, size).start(priority=ROW_DMA_PRIORITY))

    def wait_in(g):
        by_size(meta_ref[g], lambda size: in_copy(g, size).wait())

    def start_out(g):
        by_size(meta_ref[g], lambda size: out_copy(g, size).start(priority=ROW_DMA_PRIORITY))

    def wait_out(g):
        by_size(meta_ref[g], lambda size: out_copy(g, size).wait())

    def krows(c):
        r0 = (c % cpm) * kc
        return pl.ds(r0 if isinstance(c, int) else pl.multiple_of(r0, kc), kc)

    def chunk_copy(m, c, k):
        slot = c % W_RING
        return pltpu.make_async_copy(w_refs[k].at[_meta_expert(m), krows(c), cols(m)], ring.at[slot],
                                     ring_sem.at[slot])

    def for_matrix(c, fn):
        if isinstance(c, int):
            return fn(c // cpm)
        for k in range(n_mats):
            @pl.when(c // cpm == k)
            def _(k=k):
                fn(k)

    def start_chunk(m, c):
        for_matrix(c, lambda k: chunk_copy(m, c, k).start())

    def finish_chunk(m, c, wslot):
        def fn(k):
            chunk_copy(m, c, k).wait()
            wb_ref[wslot, k, krows(c), :] = ring[c % W_RING].astype(BF16)
        for_matrix(c, fn)

        @pl.when(c + W_RING < W_CHUNKS)
        def _():
            start_chunk(m, c + W_RING)

    for c0 in range(W_RING):
        start_chunk(meta_ref[0], c0)
    for g0 in range(n_in - 1):
        @pl.when(g0 < n)
        def _():
            start_in(g0)

    def body(g, carry):
        parity, done, nm = carry
        m = meta_ref[g]
        first = (m & 1) == 1

        @pl.when(first)
        def _():
            lax.fori_loop(done, W_CHUNKS, lambda c, z: (finish_chunk(m, c, 1 - parity), z)[1], 0)
            nxt = gend_ref[g]

            @pl.when(nxt < n)
            def _():
                for c0 in range(W_RING):
                    start_chunk(meta_ref[nxt], c0)

        nxt = gend_ref[g]
        parity = jnp.where(first, 1 - parity, parity)
        done = jnp.where(first, 0, done)
        nm = jnp.where(first, jnp.where(nxt < n, meta_ref[jnp.minimum(nxt, n - 1)], -1), nm)

        wait_in(g)

        @pl.when(g + n_in - 1 < n)
        def _():
            start_in(g + n_in - 1)

        @pl.when(g >= n_out)
        def _():
            wait_out(g - n_out)

        def run(size):
            x = in_buf[g % n_in, pl.ds(0, size)]
            out_buf[g % n_out, pl.ds(0, size)] = compute(x, wb_ref.at[parity], m).astype(out_buf.dtype)

        by_size(m, run)
        start_out(g)

        active = (nm >= 0) & (done < W_CHUNKS)

        @pl.when(active)
        def _():
            finish_chunk(nm, done, 1 - parity)

        return parity, jnp.where(active, done + 1, done), nm

    lax.fori_loop(0, n, body, (jnp.int32(1), jnp.int32(0), meta_ref[0]))
    lax.fori_loop(jnp.maximum(n - n_out, 0), n, lambda g, c: (wait_out(g), c)[1], 0)

    n_blocks = dst_ref.shape[0] // MOE_BLOCK
    n_used = n_ref[1]
    out_buf[0] = jnp.zeros(out_buf.shape[1:], out_buf.dtype)
    for jt in range(n_j):
        def tail_copy(b, jt=jt):
            r = pl.ds(pl.multiple_of(b * MOE_BLOCK, MOE_BLOCK), MOE_BLOCK)
            return pltpu.make_async_copy(out_buf.at[0, pl.ds(0, MOE_BLOCK)], dst_ref.at[r, jt * tn:(jt + 1) * tn],
                                         out_sem.at[0])

        lax.fori_loop(n_used, n_blocks, lambda b, c: (tail_copy(b).start(), c)[1], 0)
        lax.fori_loop(n_used, n_blocks, lambda b, c: (tail_copy(b).wait(), c)[1], 0)


def _expert_up_kernel(meta_ref, row_ref, gend_ref, n_ref, x_ref, bg_ref, bu_ref, wg_ref, wu_ref, act_ref,
                      ring, wb_ref, x_buf, a_buf, ring_sem, in_sem, out_sem):
    n_j = D_FF // MOE_UP_TN

    def compute(x, w, m):
        tile = _meta_expert(m) * n_j + _meta_tile(m)
        xb = _unpack_halves(x)
        g = jnp.dot(xb, w[0], preferred_element_type=F32) + bg_ref[pl.ds(tile, 1), :]
        u = jnp.dot(xb, w[1], preferred_element_type=F32) + bu_ref[pl.ds(tile, 1), :]
        g = jnp.minimum(g, SWIGLU_LIMIT)
        u = jnp.clip(u, -SWIGLU_LIMIT, SWIGLU_LIMIT)
        return g * _sigmoid(SWIGLU_ALPHA * g) * (u + 1.0)

    _item_loop(meta_ref, row_ref, gend_ref, n_ref, x_ref, act_ref, (wg_ref, wu_ref), ring, ring_sem, wb_ref,
               x_buf, a_buf, in_sem, out_sem, compute)


def _expert_down_kernel(meta_ref, row_ref, gend_ref, n_ref, a_ref, bd_ref, wd_ref, y_ref,
                        ring, wb_ref, a_buf, y_buf, ring_sem, in_sem, out_sem):
    n_j = D_MODEL // MOE_DOWN_TN

    def compute(a, w, m):
        tile = _meta_expert(m) * n_j + _meta_tile(m)
        return jnp.dot(a, w[0], preferred_element_type=F32) + bd_ref[pl.ds(tile, 1), :]

    _item_loop(meta_ref, row_ref, gend_ref, n_ref, a_ref, y_ref, (wd_ref,), ring, ring_sem, wb_ref,
               a_buf, y_buf, in_sem, out_sem, compute)


def _experts(x_rows, bounds, w_gate, b_gate, w_up, b_up, w_down, b_down):
    n_rows = x_rows.shape[0]
    n_items_max = n_rows // MOE_ITEM + N_EXPERTS
    any_spec = pl.BlockSpec(memory_space=pl.ANY)
    sems = [pltpu.SemaphoreType.DMA((W_RING,)), pltpu.SemaphoreType.DMA((MOE_IN_BUFS,)),
            pltpu.SemaphoreType.DMA((MOE_OUT_BUFS,))]

    def full(shape):
        return pl.BlockSpec(shape, lambda i, *_: (0,) * len(shape))

    n_j = D_FF // MOE_UP_TN
    act = pl.pallas_call(
        _expert_up_kernel,
        grid_spec=pltpu.PrefetchScalarGridSpec(
            num_scalar_prefetch=4, grid=(1,),
            in_specs=[any_spec, full((N_EXPERTS * n_j, MOE_UP_TN)),
                      full((N_EXPERTS * n_j, MOE_UP_TN)), any_spec, any_spec],
            out_specs=any_spec,
            scratch_shapes=[pltpu.VMEM((W_RING, 2 * D_MODEL // W_CHUNKS, MOE_UP_TN), F32),
                            pltpu.VMEM((2, 2, D_MODEL, MOE_UP_TN), BF16),
                            pltpu.VMEM((MOE_IN_BUFS, MOE_ITEM, D_MODEL // 2), jnp.uint32),
                            pltpu.VMEM((MOE_OUT_BUFS, MOE_ITEM, MOE_UP_TN), BF16)] + sems),
        out_shape=jax.ShapeDtypeStruct((n_rows, D_FF), BF16),
        compiler_params=_cparams(("arbitrary",)),
    )(*_item_tables(bounds, n_j, n_items_max * n_j), x_rows,
      b_gate.reshape(N_EXPERTS * n_j, MOE_UP_TN), b_up.reshape(N_EXPERTS * n_j, MOE_UP_TN), w_gate, w_up)

    n_j = D_MODEL // MOE_DOWN_TN
    return pl.pallas_call(
        _expert_down_kernel,
        grid_spec=pltpu.PrefetchScalarGridSpec(
            num_scalar_prefetch=4, grid=(1,),
            in_specs=[any_spec, full((N_EXPERTS * n_j, MOE_DOWN_TN)), any_spec],
            out_specs=any_spec,
            scratch_shapes=[pltpu.VMEM((W_RING, D_FF // W_CHUNKS, MOE_DOWN_TN), F32),
                            pltpu.VMEM((2, 1, D_FF, MOE_DOWN_TN), BF16),
                            pltpu.VMEM((MOE_IN_BUFS, MOE_ITEM, D_FF), BF16),
                            pltpu.VMEM((MOE_OUT_BUFS, MOE_ITEM, MOE_DOWN_TN), F32)] + sems),
        out_shape=jax.ShapeDtypeStruct((n_rows, D_MODEL), F32),
        compiler_params=_cparams(("arbitrary",)),
    )(*_item_tables(bounds, n_j, n_items_max * n_j), act,
      b_down.reshape(N_EXPERTS * n_j, MOE_DOWN_TN), w_down)


def _combine_kernel(dest_ref, next_ref, h_ref, gate_ref, nw_ref, y_ref, o_ref, buf_ref, sem):
    i = pl.program_id(0)
    steps = pl.num_programs(0)

    def issue(d_ref, slot):
        def body(r, carry):
            for k in range(TOP_K):
                pltpu.make_async_copy(y_ref.at[pl.ds(d_ref[k, r], 1)], buf_ref.at[slot, k, pl.ds(r, 1)],
                                      sem.at[slot]).start(priority=k % 2)
            return carry
        lax.fori_loop(0, COMBINE_TOKENS, body, 0)

    @pl.when(i == 0)
    def _():
        issue(dest_ref, 0)

    @pl.when(i + 1 < steps)
    def _():
        issue(next_ref, (i + 1) % 2)

    slot = i % 2
    pltpu.make_async_copy(buf_ref.at[slot], buf_ref.at[slot], sem.at[slot]).wait()
    acc = h_ref[...]
    gate = gate_ref[...]
    for k in range(TOP_K):
        acc = acc + gate[:, k:k + 1] * buf_ref[slot, k]
    o_ref[...] = _rms(acc, nw_ref[...])


def _combine(h, y_rows, dest_t, gate_pad, norm_w):
    t = h.shape[0]
    steps = t // COMBINE_TOKENS
    return pl.pallas_call(
        _combine_kernel,
        grid=(steps,),
        in_specs=[pl.BlockSpec((SUBLANES, COMBINE_TOKENS), lambda i: (0, i), memory_space=pltpu.SMEM),
                  pl.BlockSpec((SUBLANES, COMBINE_TOKENS), lambda i: (0, jnp.minimum(i + 1, steps - 1)),
                               memory_space=pltpu.SMEM),
                  pl.BlockSpec((COMBINE_TOKENS, D_MODEL), lambda i: (i, 0)),
                  pl.BlockSpec((COMBINE_TOKENS, LANES), lambda i: (i, 0)),
                  pl.BlockSpec((1, D_MODEL), lambda i: (0, 0)),
                  pl.BlockSpec(memory_space=pl.ANY)],
        out_specs=pl.BlockSpec((COMBINE_TOKENS, D_MODEL), lambda i: (i, 0)),
        out_shape=jax.ShapeDtypeStruct((t, D_MODEL), F32),
        scratch_shapes=[pltpu.VMEM((2, TOP_K, COMBINE_TOKENS, D_MODEL), F32), pltpu.SemaphoreType.DMA((2,))],
        compiler_params=_cparams(("arbitrary",)),
    )(dest_t, dest_t, h, gate_pad, norm_w.reshape(1, D_MODEL), y_rows)


def _moe(h, norm_ffn, w_router, b_router, w_gate, b_gate, w_up, b_up, w_down, b_down, norm_final):
    t = h.shape[0]
    tk = t * TOP_K
    n_blocks = (tk + N_EXPERTS * (MOE_BLOCK - 1) + MOE_BLOCK - 1) // MOE_BLOCK
    n_rows = n_blocks * MOE_BLOCK
    idx_pad, gate_pad, pos_pad, cnt_pad = _router(h, norm_ffn, w_router, b_router)
    dest_t, bounds = _dest_rows(idx_pad, pos_pad, cnt_pad)
    x_rows = _scatter_rows(h, norm_ffn, dest_t, bounds, n_rows)
    y_rows = _experts(x_rows, bounds, w_gate, b_gate, w_up, b_up, w_down, b_down)
    return _combine(h, y_rows, dest_t, gate_pad, norm_final)


def kernel(x, mem, norm_mix, w_in, conv_w, conv_b, dt_bias, a_log, d_skip, ssd_norm, pool_w, pool_scale,
           w_out, norm_xattn, norm_mem, w_q, w_kv, w_o, norm_ffn, w_router, b_router, w_gate, b_gate,
           w_up, b_up, w_down, b_down, norm_final):
    batch, seq, d = x.shape
    t = batch * seq
    h = x.reshape(t, d)
    depth = norm_mix.shape[0]
    assert depth == 1, "the final norm is fused into the MoE combine of the single layer"
    for l in range(depth):
        dt_lo = D_SSD + D_CONV
        w_main = jnp.concatenate([w_in[l][:, :dt_lo], w_in[l][:, dt_lo + SSD_HEADS:]], axis=1)
        w_dt = jnp.pad(w_in[l][:, dt_lo:dt_lo + SSD_HEADS], ((0, 0), (0, LANES - SSD_HEADS)))
        proj = _mm(h, w_main, norm_w=norm_mix[l])
        dtp = _mm(h, w_dt, norm_w=norm_mix[l])
        mix = _mixer(proj, dtp, conv_w[l], conv_b[l], dt_bias[l], a_log[l], d_skip[l], ssd_norm[l],
                     pool_w[l], pool_scale[l], batch, seq)
        h = _mm(mix, w_out[l], res=h)

        q = _mm(h, w_q[l], norm_w=norm_xattn[l], out_dtype=BF16)
        kv = _mm(mem.reshape(batch * N_MEM, d), w_kv[l], norm_w=norm_mem[l], out_dtype=BF16)
        o = _attn(q, kv, batch, seq)
        h = _mm(o, w_o[l], res=h)

        out = _moe(h, norm_ffn[l], w_router[l], b_router[l], w_gate[l], b_gate[l], w_up[l], b_up[l],
                   w_down[l], b_down[l], norm_final)
    return out.reshape(batch, seq, d)
```

```python
import functools

import jax
import jax.numpy as jnp
from jax import lax
from jax.experimental import pallas as pl
from jax.experimental.pallas import tpu as pltpu

F32 = jnp.float32
BF16 = jnp.bfloat16

D_MODEL = 2048
D_SSD = 1024
SSD_HEAD_DIM = 64
SSD_HEADS = 16
SSD_GROUPS = 4
HEADS_PER_GROUP = SSD_HEADS // SSD_GROUPS
D_STATE = 128
CONV_WIDTH = 4
CHUNK = 128
D_CONV = D_SSD + 2 * SSD_GROUPS * D_STATE
D_POOL = 1024
POOL_WINDOWS = (2, 4, 8, 16)
POOL_GROUP_DIM = 256
N_MEM = 256
XATTN_HEADS = 4
XATTN_HEAD_DIM = 512
N_EXPERTS = 32
TOP_K = 4
D_FF = 2048
SWIGLU_LIMIT = 7.0
SWIGLU_ALPHA = 1.702
EPS = 1e-5

LANES = 128
SUBLANES = 8
VMEM_LIMIT = 56 * 1024 * 1024
NEG_BIG = -1e30

MOE_BLOCK = 256
MOE_ITEM = 2 * MOE_BLOCK
MOE_UP_TN = 512
MOE_DOWN_TN = 1024
MOE_IN_BUFS = 3
MOE_OUT_BUFS = 2
W_CHUNKS = 2
W_RING = 2
ROUTER_TOKENS = 512
SCATTER_TOKENS = 256
COMBINE_TOKENS = 128
ROW_DMA_PRIORITY = 1


def _cparams(sem):
    return pltpu.CompilerParams(dimension_semantics=sem, vmem_limit_bytes=VMEM_LIMIT)


def _rms(x, w):
    ms = jnp.mean(x * x, axis=-1, keepdims=True)
    return x * lax.rsqrt(ms + EPS) * w


def _mm_kernel(*refs, has_norm, has_res):
    it = iter(refs)
    a_ref = next(it)
    w_ref = next(it)
    nw_ref = next(it) if has_norm else None
    r_ref = next(it) if has_res else None
    o_ref = next(it)
    wb_ref = next(it)

    @pl.when(pl.program_id(1) == 0)
    def _():
        wb_ref[...] = w_ref[...].astype(BF16)

    a = a_ref[...]
    if has_norm:
        a = _rms(a.astype(F32), nw_ref[...])
    acc = jnp.dot(a.astype(BF16), wb_ref[...], preferred_element_type=F32)
    if has_res:
        acc = acc + r_ref[...]
    o_ref[...] = acc.astype(o_ref.dtype)


def _mm(a, w, norm_w=None, res=None, out_dtype=F32, tm=1024, tn=1024, n_cols=None):
    m, k = a.shape
    n = w.shape[1] if n_cols is None else n_cols
    tn = min(tn, n)
    tm = min(tm, m)
    assert m % tm == 0 and n % tn == 0
    in_specs = [pl.BlockSpec((tm, k), lambda j, i: (i, 0)),
                pl.BlockSpec((k, tn), lambda j, i: (0, j))]
    args = [a, w]
    if norm_w is not None:
        in_specs.append(pl.BlockSpec((1, k), lambda j, i: (0, 0)))
        args.append(norm_w.reshape(1, k))
    if res is not None:
        in_specs.append(pl.BlockSpec((tm, tn), lambda j, i: (i, j)))
        args.append(res)
    return pl.pallas_call(
        functools.partial(_mm_kernel, has_norm=norm_w is not None, has_res=res is not None),
        grid=(n // tn, m // tm),
        in_specs=in_specs,
        out_specs=pl.BlockSpec((tm, tn), lambda j, i: (i, j)),
        out_shape=jax.ShapeDtypeStruct((m, n), out_dtype),
        scratch_shapes=[pltpu.VMEM((k, tn), BF16)],
        compiler_params=_cparams(("arbitrary", "arbitrary")),
    )(*args)


def _sigmoid(x):
    return 1.0 / (1.0 + jnp.exp(-x))


def _shift_rows(cur, prev, j, rows):
    return jnp.where(rows < j, pltpu.roll(prev, j, 0), pltpu.roll(cur, j, 0))


def _conv_silu(cur, prev, w_ref, b_ref, col0, rows):
    width = cur.shape[1]
    acc = cur * w_ref[CONV_WIDTH - 1:CONV_WIDTH, col0:col0 + width] + b_ref[:, col0:col0 + width]
    for j in range(1, CONV_WIDTH):
        k = CONV_WIDTH - 1 - j
        acc = acc + _shift_rows(cur, prev, j, rows) * w_ref[k:k + 1, col0:col0 + width]
    return acc * _sigmoid(acc)


def _mixer_kernel(z_ref, x_ref, bc_ref, u_ref, dt_ref, convw_ref, convb_ref, dtb_ref, alog_ref,
                  dskip_ref, ssdn_ref, poolw_ref, pscale_ref, out_ref,
                  px_ref, pbc_ref, pu_ref, state_ref):
    c = pl.program_id(1)

    @pl.when(c == 0)
    def _():
        px_ref[...] = jnp.zeros_like(px_ref)
        pbc_ref[...] = jnp.zeros_like(pbc_ref)
        pu_ref[...] = jnp.zeros_like(pu_ref)
        state_ref[...] = jnp.zeros_like(state_ref)

    rows = lax.broadcasted_iota(jnp.int32, (CHUNK, D_SSD), 0)
    x_raw = x_ref[...]
    bc_raw = bc_ref[...]
    xs = _conv_silu(x_raw, px_ref[...], convw_ref, convb_ref, 0, rows)
    bcs = _conv_silu(bc_raw, pbc_ref[...], convw_ref, convb_ref, D_SSD, rows)
    px_ref[...] = x_raw
    pbc_ref[...] = bc_raw

    dt_in = dt_ref[...] + dtb_ref[...]
    dt = jnp.maximum(dt_in, 0.0) + jnp.log(1.0 + jnp.exp(-jnp.abs(dt_in)))
    a = -jnp.exp(alog_ref[...])
    dta = dt * a
    ri = lax.broadcasted_iota(jnp.int32, (CHUNK, CHUNK), 0)
    ci = lax.broadcasted_iota(jnp.int32, (CHUNK, CHUNK), 1)
    causal = ri >= ci
    tril = jnp.where(causal, 1.0, 0.0).astype(F32)
    acs = jnp.dot(tril, dta, preferred_element_type=F32, precision=lax.Precision.HIGHEST)
    acs_t = acs.T
    dt_t = dt.T
    lo = ci < SSD_HEAD_DIM

    y_pairs = []
    for g in range(SSD_GROUPS):
        bg = bcs[:, g * D_STATE:(g + 1) * D_STATE].astype(BF16)
        cg = bcs[:, SSD_GROUPS * D_STATE + g * D_STATE:SSD_GROUPS * D_STATE + (g + 1) * D_STATE].astype(BF16)
        cb = lax.dot_general(cg, bg, (((1,), (1,)), ((), ())), preferred_element_type=F32)
        gw = HEADS_PER_GROUP * SSD_HEAD_DIM
        prev_g = state_ref[g * gw:(g + 1) * gw, :]
        yoff = lax.dot_general(cg, prev_g.astype(BF16), (((1,), (1,)), ((), ())),
                               preferred_element_type=F32)
        xw_parts = []
        for pr in range(HEADS_PER_GROUP // 2):
            pair = g * (HEADS_PER_GROUP // 2) + pr
            xs_pair = xs[:, pair * LANES:(pair + 1) * LANES]
            y_pair = None
            ecols = []
            wcols = []
            for q in range(2):
                h = pair * 2 + q
                col = acs[:, h:h + 1]
                seg = col - acs_t[h:h + 1, :]
                decay = jnp.exp(jnp.where(causal, seg, NEG_BIG))
                mat = (cb * decay * dt_t[h:h + 1, :]).astype(BF16)
                keep = lo if q == 0 else jnp.logical_not(lo)
                xh = jnp.where(keep, xs_pair, 0.0).astype(BF16)
                yd = jnp.dot(mat, xh, preferred_element_type=F32)
                y_pair = yd if y_pair is None else y_pair + yd
                ecols.append(jnp.exp(col))
                wcols.append(dt[:, h:h + 1] * jnp.exp(acs[CHUNK - 1:CHUNK, h:h + 1] - col))
            e_pair = jnp.where(lo, ecols[0], ecols[1])
            w_pair = jnp.where(lo, wcols[0], wcols[1])
            y_pairs.append(y_pair + yoff[:, pr * LANES:(pr + 1) * LANES] * e_pair)
            xw_parts.append(xs_pair * w_pair)
        xw = jnp.concatenate(xw_parts, axis=1).astype(BF16)
        s_new = lax.dot_general(xw, bg, (((0,), (0,)), ((), ())), preferred_element_type=F32)
        for r in range(HEADS_PER_GROUP):
            h = g * HEADS_PER_GROUP + r
            dec = jnp.exp(acs[CHUNK - 1:CHUNK, h:h + 1])
            lo_r = h * SSD_HEAD_DIM
            state_ref[lo_r:lo_r + SSD_HEAD_DIM, :] = (
                state_ref[lo_r:lo_r + SSD_HEAD_DIM, :] * dec
                + s_new[r * SSD_HEAD_DIM:(r + 1) * SSD_HEAD_DIM, :])

    y = jnp.concatenate(y_pairs, axis=1)
    y = y + dskip_ref[...] * xs
    z = z_ref[...]
    y = y * (z * _sigmoid(z))
    gdim = D_SSD // SSD_GROUPS
    for g in range(SSD_GROUPS):
        yg = _rms(y[:, g * gdim:(g + 1) * gdim], ssdn_ref[:, g * gdim:(g + 1) * gdim])
        out_ref[:, g * gdim:(g + 1) * gdim] = yg.astype(out_ref.dtype)

    u_cur = u_ref[...]
    u_prev = pu_ref[...]
    pos = (c * CHUNK + lax.broadcasted_iota(jnp.int32, (CHUNK, POOL_GROUP_DIM), 0) + 1).astype(F32)
    for g, win in enumerate(POOL_WINDOWS):
        sl = slice(g * POOL_GROUP_DIM, (g + 1) * POOL_GROUP_DIM)
        ug = u_cur[:, sl]
        ext = jnp.concatenate([u_prev[:, sl], ug], axis=0)
        step = 1
        while step < win:
            ext = ext + pltpu.roll(ext, step, 0)
            step *= 2
        pooled = ext[CHUNK:, :] / jnp.minimum(pos, float(win)) - ug
        mixed = jnp.dot(pooled.astype(BF16), poolw_ref[g].astype(BF16), preferred_element_type=F32)
        out_ref[:, D_SSD + g * POOL_GROUP_DIM:D_SSD + (g + 1) * POOL_GROUP_DIM] = (
            mixed * pscale_ref[:, sl]).astype(out_ref.dtype)
    pu_ref[...] = u_cur


def _mixer(proj, proj_u, dtp, conv_w, conv_b, dt_bias, a_log, d_skip, ssd_norm, pool_w, pool_scale, batch, seq):
    nchunk = seq // CHUNK
    row_map = lambda col: (lambda b, c: (b * nchunk + c, col))
    const2 = lambda b, c: (0, 0)
    pad = LANES - SSD_HEADS
    dtb = jnp.pad(dt_bias, (0, pad)).reshape(1, LANES)
    alog = jnp.pad(a_log, (0, pad)).reshape(1, LANES)
    dskip = jnp.repeat(d_skip, SSD_HEAD_DIM).reshape(1, D_SSD)
    return pl.pallas_call(
        _mixer_kernel,
        grid=(batch, nchunk),
        in_specs=[
            pl.BlockSpec((CHUNK, D_SSD), row_map(0)),
            pl.BlockSpec((CHUNK, D_SSD), row_map(1)),
            pl.BlockSpec((CHUNK, D_SSD), row_map(2)),
            pl.BlockSpec((CHUNK, D_POOL), row_map(0)),
            pl.BlockSpec((CHUNK, LANES), row_map(0)),
            pl.BlockSpec((CONV_WIDTH, D_CONV), const2),
            pl.BlockSpec((1, D_CONV), const2),
            pl.BlockSpec((1, LANES), const2),
            pl.BlockSpec((1, LANES), const2),
            pl.BlockSpec((1, D_SSD), const2),
            pl.BlockSpec((1, D_SSD), const2),
            pl.BlockSpec((len(POOL_WINDOWS), POOL_GROUP_DIM, POOL_GROUP_DIM), lambda b, c: (0, 0, 0)),
            pl.BlockSpec((1, D_POOL), const2),
        ],
        out_specs=pl.BlockSpec((CHUNK, D_MODEL), lambda b, c: (b * nchunk + c, 0)),
        out_shape=jax.ShapeDtypeStruct((batch * seq, D_MODEL), BF16),
        scratch_shapes=[pltpu.VMEM((CHUNK, D_SSD), F32), pltpu.VMEM((CHUNK, D_SSD), F32),
                        pltpu.VMEM((CHUNK, D_POOL), F32), pltpu.VMEM((SSD_HEADS * SSD_HEAD_DIM, D_STATE), F32)],
        compiler_params=_cparams(("arbitrary", "arbitrary")),
    )(proj, proj, proj, proj_u, dtp, conv_w, conv_b.reshape(1, D_CONV), dtb, alog, dskip,
      ssd_norm.reshape(1, D_SSD), pool_w, pool_scale.reshape(1, D_POOL))


def _attn_kernel(q_ref, k_ref, v_ref, o_ref):
    scale = XATTN_HEAD_DIM ** -0.5
    for h in range(XATTN_HEADS):
        sl = slice(h * XATTN_HEAD_DIM, (h + 1) * XATTN_HEAD_DIM)
        s = lax.dot_general(q_ref[:, sl], k_ref[:, sl], (((1,), (1,)), ((), ())),
                            preferred_element_type=F32) * scale
        s = s - jnp.max(s, axis=-1, keepdims=True)
        p = jnp.exp(s)
        p = p / jnp.sum(p, axis=-1, keepdims=True)
        o_ref[:, sl] = jnp.dot(p.astype(BF16), v_ref[:, sl], preferred_element_type=F32).astype(o_ref.dtype)


def _attn(q, kv, batch, seq, tq=512):
    nq = seq // tq
    return pl.pallas_call(
        _attn_kernel,
        grid=(batch, nq),
        in_specs=[pl.BlockSpec((tq, D_MODEL), lambda b, i: (b * nq + i, 0)),
                  pl.BlockSpec((N_MEM, D_MODEL), lambda b, i: (b, 0)),
                  pl.BlockSpec((N_MEM, D_MODEL), lambda b, i: (b, 1))],
        out_specs=pl.BlockSpec((tq, D_MODEL), lambda b, i: (b * nq + i, 0)),
        out_shape=jax.ShapeDtypeStruct((batch * seq, D_MODEL), BF16),
        compiler_params=_cparams(("arbitrary", "arbitrary")),
    )(q, kv, kv)


def _router_kernel(h_ref, nw_ref, wr_ref, br_ref, idx_ref, gate_ref, pos_ref, cnt_ref, run_ref):
    @pl.when(pl.program_id(0) == 0)
    def _():
        run_ref[...] = jnp.zeros_like(run_ref)

    hn = _rms(h_ref[...], nw_ref[...])
    w = wr_ref[...]
    h_hi = hn.astype(BF16)
    h_lo = (hn - h_hi.astype(F32)).astype(BF16)
    w_hi = w.astype(BF16)
    w_lo = (w - w_hi.astype(F32)).astype(BF16)
    logits = (jnp.dot(h_hi, w_hi, preferred_element_type=F32)
              + (jnp.dot(h_hi, w_lo, preferred_element_type=F32)
                 + jnp.dot(h_lo, w_hi, preferred_element_type=F32))) + br_ref[...]
    tm = logits.shape[0]
    lane = lax.broadcasted_iota(jnp.int32, logits.shape, 1)
    vals = []
    hots = []
    idx_out = jnp.zeros(logits.shape, jnp.int32)
    for k in range(TOP_K):
        m = jnp.max(logits, axis=-1, keepdims=True)
        idx = jnp.min(jnp.where(logits == m, lane, LANES), axis=-1, keepdims=True)
        hit = lane == idx
        idx_out = jnp.where(lane == k, idx, idx_out)
        logits = jnp.where(hit, -jnp.inf, logits)
        vals.append(m)
        hots.append(jnp.where(hit, 1.0, 0.0))
    exps = [jnp.exp(v - vals[0]) for v in vals]
    den = exps[0] + exps[1] + exps[2] + exps[3]
    gate_out = jnp.zeros(logits.shape, F32)
    for k in range(TOP_K):
        gate_out = jnp.where(lane == k, exps[k] / den, gate_out)
    idx_ref[...] = idx_out
    gate_ref[...] = gate_out

    hot = hots[0] + hots[1] + hots[2] + hots[3]
    ri = lax.broadcasted_iota(jnp.int32, (tm, tm), 0)
    ci = lax.broadcasted_iota(jnp.int32, (tm, tm), 1)
    before = jnp.where(ri > ci, 1.0, 0.0).astype(BF16)
    ahead = jnp.dot(before, hot.astype(BF16), preferred_element_type=F32) + run_ref[...]
    pos_out = jnp.zeros(logits.shape, F32)
    for k in range(TOP_K):
        pos_out = jnp.where(lane == k, jnp.sum(ahead * hots[k], axis=-1, keepdims=True), pos_out)
    pos_ref[...] = pos_out
    run_ref[...] = run_ref[...] + jnp.sum(hot, axis=0, keepdims=True)
    cnt_ref[...] = run_ref[...]


def _router(h, norm_w, w_router, b_router):
    t = h.shape[0]
    tm = ROUTER_TOKENS
    pad = LANES - N_EXPERTS
    wr = jnp.pad(w_router, ((0, 0), (0, pad)))
    br = jnp.pad(b_router, (0, pad), constant_values=NEG_BIG).reshape(1, LANES)
    tok_spec = pl.BlockSpec((tm, LANES), lambda i: (i, 0))
    return pl.pallas_call(
        _router_kernel,
        grid=(t // tm,),
        in_specs=[pl.BlockSpec((tm, D_MODEL), lambda i: (i, 0)),
                  pl.BlockSpec((1, D_MODEL), lambda i: (0, 0)),
                  pl.BlockSpec((D_MODEL, LANES), lambda i: (0, 0)),
                  pl.BlockSpec((1, LANES), lambda i: (0, 0))],
        out_specs=[tok_spec, tok_spec, tok_spec, pl.BlockSpec((1, LANES), lambda i: (0, 0))],
        out_shape=[jax.ShapeDtypeStruct((t, LANES), jnp.int32),
                   jax.ShapeDtypeStruct((t, LANES), F32),
                   jax.ShapeDtypeStruct((t, LANES), F32),
                   jax.ShapeDtypeStruct((1, LANES), F32)],
        scratch_shapes=[pltpu.VMEM((1, LANES), F32)],
        compiler_params=_cparams(("arbitrary",)),
    )(h, norm_w.reshape(1, D_MODEL), wr, br)


def _dest_kernel(idx_ref, pos_ref, cnt_ref, dest_ref, bounds_ref):
    cnt = jnp.broadcast_to(cnt_ref[...], (SUBLANES, LANES))
    padded = jnp.floor((cnt + (MOE_BLOCK - 1)) * (1.0 / MOE_BLOCK)) * MOE_BLOCK
    ri = lax.broadcasted_iota(jnp.int32, (LANES, LANES), 0)
    ci = lax.broadcasted_iota(jnp.int32, (LANES, LANES), 1)
    upto = jnp.where(ri <= ci, 1.0, 0.0).astype(F32)
    seg_end = jnp.dot(padded, upto, preferred_element_type=F32, precision=lax.Precision.HIGHEST)
    seg_start = seg_end - padded
    blocks = padded * (1.0 / MOE_BLOCK)
    full = jnp.floor(blocks * (1.0 * MOE_BLOCK / MOE_ITEM))
    items = blocks - full
    items_end = jnp.dot(items, upto, preferred_element_type=F32, precision=lax.Precision.HIGHEST)
    row = lax.broadcasted_iota(jnp.int32, (SUBLANES, LANES), 0)
    table = jnp.zeros((SUBLANES, LANES), F32)
    for r, v in enumerate((seg_start, seg_end, items_end, items, full)):
        table = jnp.where(row == r, v, table)
    bounds_ref[...] = table.astype(jnp.int32)

    idx = idx_ref[...]
    lane = lax.broadcasted_iota(jnp.int32, idx.shape, 1)
    out = pos_ref[...]
    for k in range(TOP_K):
        hit = lane == idx[:, k:k + 1]
        base = jnp.sum(jnp.where(hit, seg_start[0:1, :], 0.0), axis=-1, keepdims=True)
        out = jnp.where(lane == k, out + base, out)
    dest_ref[...] = out.T[0:SUBLANES, :].astype(jnp.int32)


def _dest_rows(idx_pad, pos_pad, cnt_pad):
    t = idx_pad.shape[0]
    tm = ROUTER_TOKENS
    tok_spec = pl.BlockSpec((tm, LANES), lambda i: (i, 0))
    return pl.pallas_call(
        _dest_kernel,
        grid=(t // tm,),
        in_specs=[tok_spec, tok_spec, pl.BlockSpec((1, LANES), lambda i: (0, 0))],
        out_specs=[pl.BlockSpec((SUBLANES, tm), lambda i: (0, i)),
                   pl.BlockSpec((SUBLANES, LANES), lambda i: (0, 0))],
        out_shape=[jax.ShapeDtypeStruct((SUBLANES, t), jnp.int32),
                   jax.ShapeDtypeStruct((SUBLANES, LANES), jnp.int32)],
        compiler_params=_cparams(("arbitrary",)),
    )(idx_pad, pos_pad, cnt_pad)


def _pack_halves(x):
    c = x.shape[1] // 2
    lo = pltpu.bitcast(x[:, :c].astype(BF16).astype(F32), jnp.uint32)
    hi = pltpu.bitcast(x[:, c:].astype(BF16).astype(F32), jnp.uint32)
    return (lo >> 16) | hi


def _unpack_halves(w):
    lo = pltpu.bitcast(w << 16, F32)
    hi = pltpu.bitcast(w & jnp.uint32(0xFFFF0000), F32)
    return jnp.concatenate([lo, hi], axis=1).astype(BF16)


def _scatter_kernel(dest_ref, bounds_ref, h_ref, nw_ref, x_ref, zero_ref, packed_ref, zsem, sem):
    n_blocks = x_ref.shape[0] // MOE_BLOCK
    i = pl.program_id(0)
    slot = i % 2
    packed_ref[slot] = _pack_halves(_rms(h_ref[...], nw_ref[...]))

    @pl.when(i == 0)
    def _():
        zero_ref[...] = jnp.zeros_like(zero_ref)
        n_used = bounds_ref[1, N_EXPERTS - 1] // MOE_BLOCK

        def zero_copy(b):
            return pltpu.make_async_copy(zero_ref, x_ref.at[pl.ds(pl.multiple_of(b * MOE_BLOCK, MOE_BLOCK),
                                                                   MOE_BLOCK)], zsem)

        def last_block(e):
            return bounds_ref[1, e] // MOE_BLOCK - 1, bounds_ref[1, e] > bounds_ref[0, e]

        def start_e(e, carry):
            b, nonempty = last_block(e)

            @pl.when(nonempty)
            def _():
                zero_copy(b).start()
            return carry

        def wait_e(e, carry):
            b, nonempty = last_block(e)

            @pl.when(nonempty)
            def _():
                zero_copy(b).wait()
            return carry

        lax.fori_loop(0, N_EXPERTS, start_e, 0)
        lax.fori_loop(n_used, n_blocks, lambda b, c: (zero_copy(b).start(), c)[1], 0)
        lax.fori_loop(0, N_EXPERTS, wait_e, 0)
        lax.fori_loop(n_used, n_blocks, lambda b, c: (zero_copy(b).wait(), c)[1], 0)

    def issue(r, carry):
        for k in range(TOP_K):
            d = dest_ref[k, r]
            pltpu.make_async_copy(packed_ref.at[slot, pl.ds(r, 1)], x_ref.at[pl.ds(d, 1)],
                                  sem.at[slot]).start(priority=k % 2)
        return carry

    lax.fori_loop(0, SCATTER_TOKENS, issue, 0)

    def wait_rows(s):
        n_sent = SCATTER_TOKENS * TOP_K
        pltpu.make_async_copy(x_ref.at[pl.ds(0, n_sent)], x_ref.at[pl.ds(0, n_sent)], sem.at[s]).wait()

    @pl.when(i > 0)
    def _():
        wait_rows(1 - slot)

    @pl.when(i == pl.num_programs(0) - 1)
    def _():
        wait_rows(slot)


def _scatter_rows(h, norm_w, dest_t, bounds, n_rows):
    t = h.shape[0]
    steps = t // SCATTER_TOKENS
    half = D_MODEL // 2
    return pl.pallas_call(
        _scatter_kernel,
        grid=(steps,),
        in_specs=[pl.BlockSpec((SUBLANES, SCATTER_TOKENS), lambda i: (0, i), memory_space=pltpu.SMEM),
                  pl.BlockSpec(memory_space=pltpu.SMEM),
                  pl.BlockSpec((SCATTER_TOKENS, D_MODEL), lambda i: (i, 0)),
                  pl.BlockSpec((1, D_MODEL), lambda i: (0, 0))],
        out_specs=pl.BlockSpec(memory_space=pl.ANY),
        out_shape=jax.ShapeDtypeStruct((n_rows, half), jnp.uint32),
        scratch_shapes=[pltpu.VMEM((MOE_BLOCK, half), jnp.uint32), pltpu.VMEM((2, SCATTER_TOKENS, half), jnp.uint32),
                        pltpu.SemaphoreType.DMA(()), pltpu.SemaphoreType.DMA((2,))],
        compiler_params=_cparams(("arbitrary",)),
    )(dest_t, bounds, h, norm_w.reshape(1, D_MODEL))


def _item_tables(bounds, n_j, n_items_max):
    items_e = bounds[3]
    first_e = (bounds[2] - items_e) * n_j
    end_e = bounds[2] * n_j
    n_items = end_e[N_EXPERTS - 1]
    g = jnp.minimum(jnp.arange(n_items_max, dtype=jnp.int32), n_items - 1)
    e = jnp.sum((g[:, None] >= end_e[None, :N_EXPERTS]).astype(jnp.int32), axis=1)
    own = e[:, None] == jnp.arange(LANES, dtype=jnp.int32)[None, :]
    pick = lambda v: jnp.sum(jnp.where(own, v[None, :], 0), axis=1)
    first = pick(first_e)
    nb = jnp.maximum(pick(items_e), 1)
    q = g - first
    j = q // nb
    i = q % nb
    row = pick(bounds[0] // MOE_BLOCK) + i * (MOE_ITEM // MOE_BLOCK)
    single = (i >= pick(bounds[4])).astype(jnp.int32)
    meta = e * 16 + single * 8 + j * 2 + (i == 0).astype(jnp.int32)
    group_end = first + (j + 1) * nb
    counts = jnp.stack([n_items, bounds[1, N_EXPERTS - 1] // MOE_BLOCK])
    return meta.astype(jnp.int32), row.astype(jnp.int32), group_end.astype(jnp.int32), counts.astype(jnp.int32)


def _meta_expert(m):
    return m >> 4


def _meta_tile(m):
    return (m >> 1) & 3


def _item_loop(meta_ref, row_ref, gend_ref, n_ref, src_ref, dst_ref, w_refs, ring, ring_sem, wb_ref,
               in_buf, out_buf, in_sem, out_sem, compute):
    n = n_ref[0]
    n_in = in_buf.shape[0]
    n_out = out_buf.shape[0]
    tn = out_buf.shape[2]
    n_j = dst_ref.shape[1] // tn
    n_mats = len(w_refs)
    cpm = W_CHUNKS // n_mats
    kc = w_refs[0].shape[1] // cpm

    def by_size(m, fn):
        @pl.when(((m >> 3) & 1) == 0)
        def _():
            fn(MOE_ITEM)

        @pl.when(((m >> 3) & 1) == 1)
        def _():
            fn(MOE_BLOCK)

    def rows(g, size):
        return pl.ds(pl.multiple_of(row_ref[g] * MOE_BLOCK, MOE_BLOCK), size)

    def cols(m):
        return pl.ds(pl.multiple_of(_meta_tile(m) * tn, tn), tn)

    def in_copy(g, size):
        slot = g % n_in
        return pltpu.make_async_copy(src_ref.at[rows(g, size)], in_buf.at[slot, pl.ds(0, size)], in_sem.at[slot])

    def out_copy(g, size):
        slot = g % n_out
        return pltpu.make_async_copy(out_buf.at[slot, pl.ds(0, size)], dst_ref.at[rows(g, size), cols(meta_ref[g])],
                                     out_sem.at[slot])

    def start_in(g):
        by_size(meta_ref[g], lambda size: in_copy(g, size).start(priority=ROW_DMA_PRIORITY))

    def wait_in(g):
        by_size(meta_ref[g], lambda size: in_copy(g, size).wait())

    def start_out(g):
        by_size(meta_ref[g], lambda size: out_copy(g, size).start(priority=ROW_DMA_PRIORITY))

    def wait_out(g):
        by_size(meta_ref[g], lambda size: out_copy(g, size).wait())

    def krows(c):
        r0 = (c % cpm) * kc
        return pl.ds(r0 if isinstance(c, int) else pl.multiple_of(r0, kc), kc)

    def chunk_copy(m, c, k):
        slot = c % W_RING
        return pltpu.make_async_copy(w_refs[k].at[_meta_expert(m), krows(c), cols(m)], ring.at[slot],
                                     ring_sem.at[slot])

    def for_matrix(c, fn):
        if isinstance(c, int):
            return fn(c // cpm)
        for k in range(n_mats):
            @pl.when(c // cpm == k)
            def _(k=k):
                fn(k)

    def start_chunk(m, c):
        for_matrix(c, lambda k: chunk_copy(m, c, k).start())

    def finish_chunk(m, c, wslot):
        def fn(k):
            chunk_copy(m, c, k).wait()
            wb_ref[wslot, k, krows(c), :] = ring[c % W_RING].astype(BF16)
        for_matrix(c, fn)

        @pl.when(c + W_RING < W_CHUNKS)
        def _():
            start_chunk(m, c + W_RING)

    for c0 in range(W_RING):
        start_chunk(meta_ref[0], c0)
    for g0 in range(n_in - 1):
        @pl.when(g0 < n)
        def _():
            start_in(g0)

    def body(g, carry):
        parity, done, nm = carry
        m = meta_ref[g]
        first = (m & 1) == 1

        @pl.when(first)
        def _():
            lax.fori_loop(done, W_CHUNKS, lambda c, z: (finish_chunk(m, c, 1 - parity), z)[1], 0)
            nxt = gend_ref[g]

            @pl.when(nxt < n)
            def _():
                for c0 in range(W_RING):
                    start_chunk(meta_ref[nxt], c0)

        nxt = gend_ref[g]
        parity = jnp.where(first, 1 - parity, parity)
        done = jnp.where(first, 0, done)
        nm = jnp.where(first, jnp.where(nxt < n, meta_ref[jnp.minimum(nxt, n - 1)], -1), nm)

        wait_in(g)

        @pl.when(g + n_in - 1 < n)
        def _():
            start_in(g + n_in - 1)

        @pl.when(g >= n_out)
        def _():
            wait_out(g - n_out)

        active = (nm >= 0) & (done < W_CHUNKS)

        @pl.when(active)
        def _():
            chunk_copy(nm, done, 0).wait()

        cast_mat = jnp.minimum(done // cpm, n_mats - 1)
        cast_row = pl.multiple_of(jnp.where(active, (done % cpm) * kc, w_refs[0].shape[1]), kc)

        def run(size):
            x = in_buf[g % n_in, pl.ds(0, size)]
            out_buf[g % n_out, pl.ds(0, size)] = compute(x, wb_ref.at[parity], m).astype(out_buf.dtype)
            wb_ref[1 - parity, cast_mat, pl.ds(cast_row, kc), :] = ring[done % W_RING].astype(BF16)

        by_size(m, run)
        start_out(g)

        @pl.when(active & (done + W_RING < W_CHUNKS))
        def _():
            start_chunk(nm, done + W_RING)

        return parity, jnp.where(active, done + 1, done), nm

    lax.fori_loop(0, n, body, (jnp.int32(1), jnp.int32(0), meta_ref[0]))
    lax.fori_loop(jnp.maximum(n - n_out, 0), n, lambda g, c: (wait_out(g), c)[1], 0)

    n_blocks = dst_ref.shape[0] // MOE_BLOCK
    n_used = n_ref[1]
    out_buf[0] = jnp.zeros(out_buf.shape[1:], out_buf.dtype)
    for jt in range(n_j):
        def tail_copy(b, jt=jt):
            r = pl.ds(pl.multiple_of(b * MOE_BLOCK, MOE_BLOCK), MOE_BLOCK)
            return pltpu.make_async_copy(out_buf.at[0, pl.ds(0, MOE_BLOCK)], dst_ref.at[r, jt * tn:(jt + 1) * tn],
                                         out_sem.at[0])

        lax.fori_loop(n_used, n_blocks, lambda b, c: (tail_copy(b).start(), c)[1], 0)
        lax.fori_loop(n_used, n_blocks, lambda b, c: (tail_copy(b).wait(), c)[1], 0)


def _expert_up_kernel(meta_ref, row_ref, gend_ref, n_ref, x_ref, bg_ref, bu_ref, wg_ref, wu_ref, act_ref,
                      ring, wb_ref, x_buf, a_buf, ring_sem, in_sem, out_sem):
    n_j = D_FF // MOE_UP_TN

    def compute(x, w, m):
        tile = _meta_expert(m) * n_j + _meta_tile(m)
        xb = _unpack_halves(x)
        g = jnp.dot(xb, w[0, pl.ds(0, D_MODEL), :], preferred_element_type=F32) + bg_ref[pl.ds(tile, 1), :]
        u = jnp.dot(xb, w[1, pl.ds(0, D_MODEL), :], preferred_element_type=F32) + bu_ref[pl.ds(tile, 1), :]
        g = jnp.minimum(g, SWIGLU_LIMIT)
        u = jnp.clip(u, -SWIGLU_LIMIT, SWIGLU_LIMIT)
        return g * _sigmoid(SWIGLU_ALPHA * g) * (u + 1.0)

    _item_loop(meta_ref, row_ref, gend_ref, n_ref, x_ref, act_ref, (wg_ref, wu_ref), ring, ring_sem, wb_ref,
               x_buf, a_buf, in_sem, out_sem, compute)


def _expert_down_kernel(meta_ref, row_ref, gend_ref, n_ref, a_ref, bd_ref, wd_ref, y_ref,
                        ring, wb_ref, a_buf, y_buf, ring_sem, in_sem, out_sem):
    n_j = D_MODEL // MOE_DOWN_TN

    def compute(a, w, m):
        tile = _meta_expert(m) * n_j + _meta_tile(m)
        return jnp.dot(a, w[0, pl.ds(0, D_FF), :], preferred_element_type=F32) + bd_ref[pl.ds(tile, 1), :]

    _item_loop(meta_ref, row_ref, gend_ref, n_ref, a_ref, y_ref, (wd_ref,), ring, ring_sem, wb_ref,
               a_buf, y_buf, in_sem, out_sem, compute)


def _experts(x_rows, bounds, w_gate, b_gate, w_up, b_up, w_down, b_down):
    n_rows = x_rows.shape[0]
    n_items_max = n_rows // MOE_ITEM + N_EXPERTS
    any_spec = pl.BlockSpec(memory_space=pl.ANY)
    sems = [pltpu.SemaphoreType.DMA((W_RING,)), pltpu.SemaphoreType.DMA((MOE_IN_BUFS,)),
            pltpu.SemaphoreType.DMA((MOE_OUT_BUFS,))]

    def full(shape):
        return pl.BlockSpec(shape, lambda i, *_: (0,) * len(shape))

    n_j = D_FF // MOE_UP_TN
    act = pl.pallas_call(
        _expert_up_kernel,
        grid_spec=pltpu.PrefetchScalarGridSpec(
            num_scalar_prefetch=4, grid=(1,),
            in_specs=[any_spec, full((N_EXPERTS * n_j, MOE_UP_TN)),
                      full((N_EXPERTS * n_j, MOE_UP_TN)), any_spec, any_spec],
            out_specs=any_spec,
            scratch_shapes=[pltpu.VMEM((W_RING, 2 * D_MODEL // W_CHUNKS, MOE_UP_TN), F32),
                            pltpu.VMEM((2, 2, D_MODEL + 2 * D_MODEL // W_CHUNKS, MOE_UP_TN), BF16),
                            pltpu.VMEM((MOE_IN_BUFS, MOE_ITEM, D_MODEL // 2), jnp.uint32),
                            pltpu.VMEM((MOE_OUT_BUFS, MOE_ITEM, MOE_UP_TN), BF16)] + sems),
        out_shape=jax.ShapeDtypeStruct((n_rows, D_FF), BF16),
        compiler_params=_cparams(("arbitrary",)),
    )(*_item_tables(bounds, n_j, n_items_max * n_j), x_rows,
      b_gate.reshape(N_EXPERTS * n_j, MOE_UP_TN), b_up.reshape(N_EXPERTS * n_j, MOE_UP_TN), w_gate, w_up)

    n_j = D_MODEL // MOE_DOWN_TN
    return pl.pallas_call(
        _expert_down_kernel,
        grid_spec=pltpu.PrefetchScalarGridSpec(
            num_scalar_prefetch=4, grid=(1,),
            in_specs=[any_spec, full((N_EXPERTS * n_j, MOE_DOWN_TN)), any_spec],
            out_specs=any_spec,
            scratch_shapes=[pltpu.VMEM((W_RING, D_FF // W_CHUNKS, MOE_DOWN_TN), F32),
                            pltpu.VMEM((2, 1, D_FF + D_FF // W_CHUNKS, MOE_DOWN_TN), BF16),
                            pltpu.VMEM((MOE_IN_BUFS, MOE_ITEM, D_FF), BF16),
                            pltpu.VMEM((MOE_OUT_BUFS, MOE_ITEM, MOE_DOWN_TN), F32)] + sems),
        out_shape=jax.ShapeDtypeStruct((n_rows, D_MODEL), F32),
        compiler_params=_cparams(("arbitrary",)),
    )(*_item_tables(bounds, n_j, n_items_max * n_j), act,
      b_down.reshape(N_EXPERTS * n_j, MOE_DOWN_TN), w_down)


def _combine_kernel(dest_ref, next_ref, h_ref, gate_ref, nw_ref, y_ref, o_ref, buf_ref, sem):
    i = pl.program_id(0)
    steps = pl.num_programs(0)

    def issue(d_ref, slot):
        def body(r, carry):
            for k in range(TOP_K):
                pltpu.make_async_copy(y_ref.at[pl.ds(d_ref[k, r], 1)], buf_ref.at[slot, k, pl.ds(r, 1)],
                                      sem.at[slot]).start(priority=k % 2)
            return carry
        lax.fori_loop(0, COMBINE_TOKENS, body, 0)

    @pl.when(i == 0)
    def _():
        issue(dest_ref, 0)

    @pl.when(i + 1 < steps)
    def _():
        issue(next_ref, (i + 1) % 2)

    slot = i % 2
    pltpu.make_async_copy(buf_ref.at[slot], buf_ref.at[slot], sem.at[slot]).wait()
    acc = h_ref[...]
    gate = gate_ref[...]
    for k in range(TOP_K):
        acc = acc + gate[:, k:k + 1] * buf_ref[slot, k]
    o_ref[...] = _rms(acc, nw_ref[...])


def _combine(h, y_rows, dest_t, gate_pad, norm_w):
    t = h.shape[0]
    steps = t // COMBINE_TOKENS
    return pl.pallas_call(
        _combine_kernel,
        grid=(steps,),
        in_specs=[pl.BlockSpec((SUBLANES, COMBINE_TOKENS), lambda i: (0, i), memory_space=pltpu.SMEM),
                  pl.BlockSpec((SUBLANES, COMBINE_TOKENS), lambda i: (0, jnp.minimum(i + 1, steps - 1)),
                               memory_space=pltpu.SMEM),
                  pl.BlockSpec((COMBINE_TOKENS, D_MODEL), lambda i: (i, 0)),
                  pl.BlockSpec((COMBINE_TOKENS, LANES), lambda i: (i, 0)),
                  pl.BlockSpec((1, D_MODEL), lambda i: (0, 0)),
                  pl.BlockSpec(memory_space=pl.ANY)],
        out_specs=pl.BlockSpec((COMBINE_TOKENS, D_MODEL), lambda i: (i, 0)),
        out_shape=jax.ShapeDtypeStruct((t, D_MODEL), F32),
        scratch_shapes=[pltpu.VMEM((2, TOP_K, COMBINE_TOKENS, D_MODEL), F32), pltpu.SemaphoreType.DMA((2,))],
        compiler_params=_cparams(("arbitrary",)),
    )(dest_t, dest_t, h, gate_pad, norm_w.reshape(1, D_MODEL), y_rows)


def _moe(h, norm_ffn, w_router, b_router, w_gate, b_gate, w_up, b_up, w_down, b_down, norm_final):
    t = h.shape[0]
    tk = t * TOP_K
    n_blocks = (tk + N_EXPERTS * (MOE_BLOCK - 1) + MOE_BLOCK - 1) // MOE_BLOCK
    n_rows = n_blocks * MOE_BLOCK
    idx_pad, gate_pad, pos_pad, cnt_pad = _router(h, norm_ffn, w_router, b_router)
    dest_t, bounds = _dest_rows(idx_pad, pos_pad, cnt_pad)
    x_rows = _scatter_rows(h, norm_ffn, dest_t, bounds, n_rows)
    y_rows = _experts(x_rows, bounds, w_gate, b_gate, w_up, b_up, w_down, b_down)
    return _combine(h, y_rows, dest_t, gate_pad, norm_final)


def kernel(x, mem, norm_mix, w_in, conv_w, conv_b, dt_bias, a_log, d_skip, ssd_norm, pool_w, pool_scale,
           w_out, norm_xattn, norm_mem, w_q, w_kv, w_o, norm_ffn, w_router, b_router, w_gate, b_gate,
           w_up, b_up, w_down, b_down, norm_final):
    batch, seq, d = x.shape
    t = batch * seq
    h = x.reshape(t, d)
    depth = norm_mix.shape[0]
    assert depth == 1, "the final norm is fused into the MoE combine of the single layer"
    for l in range(depth):
        dt_lo = D_SSD + D_CONV
        w_dt = jnp.pad(w_in[l][:, dt_lo:dt_lo + SSD_HEADS], ((0, 0), (0, LANES - SSD_HEADS)))
        proj = _mm(h, w_in[l], norm_w=norm_mix[l], n_cols=dt_lo)
        proj_u = _mm(h, w_in[l][:, dt_lo + SSD_HEADS:], norm_w=norm_mix[l])
        dtp = _mm(h, w_dt, norm_w=norm_mix[l])
        mix = _mixer(proj, proj_u, dtp, conv_w[l], conv_b[l], dt_bias[l], a_log[l], d_skip[l], ssd_norm[l],
                     pool_w[l], pool_scale[l], batch, seq)
        h = _mm(mix, w_out[l], res=h)

        q = _mm(h, w_q[l], norm_w=norm_xattn[l], out_dtype=BF16)
        kv = _mm(mem.reshape(batch * N_MEM, d), w_kv[l], norm_w=norm_mem[l], out_dtype=BF16)
        o = _attn(q, kv, batch, seq)
        h = _mm(o, w_o[l], res=h)

        out = _moe(h, norm_ffn[l], w_router[l], b_router[l], w_gate[l], b_gate[l], w_up[l], b_up[l],
                   w_down[l], b_down[l], norm_final)
    return out.reshape(batch, seq, d)
```

```python
import functools

import jax
import jax.numpy as jnp
from jax import lax
from jax.experimental import pallas as pl
from jax.experimental.pallas import tpu as pltpu

F32 = jnp.float32
BF16 = jnp.bfloat16

D_MODEL = 2048
D_SSD = 1024
SSD_HEAD_DIM = 64
SSD_HEADS = 16
SSD_GROUPS = 4
HEADS_PER_GROUP = SSD_HEADS // SSD_GROUPS
D_STATE = 128
CONV_WIDTH = 4
CHUNK = 128
D_CONV = D_SSD + 2 * SSD_GROUPS * D_STATE
D_POOL = 1024
POOL_WINDOWS = (2, 4, 8, 16)
POOL_GROUP_DIM = 256
N_MEM = 256
XATTN_HEADS = 4
XATTN_HEAD_DIM = 512
N_EXPERTS = 32
TOP_K = 4
D_FF = 2048
SWIGLU_LIMIT = 7.0
SWIGLU_ALPHA = 1.702
EPS = 1e-5

LANES = 128
SUBLANES = 8
VMEM_LIMIT = 56 * 1024 * 1024
NEG_BIG = -1e30

MOE_BLOCK = 256
MOE_ITEM = 2 * MOE_BLOCK
MOE_UP_TN = 512
MOE_DOWN_TN = 1024
MOE_IN_BUFS = 3
MOE_OUT_BUFS = 2
W_CHUNKS = 4
W_RING = 4
ROUTER_TOKENS = 512
SCATTER_TOKENS = 512
COMBINE_TOKENS = 256
ROW_DMA_PRIORITY = 1


def _cparams(sem):
    return pltpu.CompilerParams(dimension_semantics=sem, vmem_limit_bytes=VMEM_LIMIT)


def _rms(x, w):
    ms = jnp.mean(x * x, axis=-1, keepdims=True)
    return x * lax.rsqrt(ms + EPS) * w


def _mm_kernel(*refs, has_norm, has_res):
    it = iter(refs)
    a_ref = next(it)
    w_ref = next(it)
    nw_ref = next(it) if has_norm else None
    r_ref = next(it) if has_res else None
    o_ref = next(it)
    wb_ref = next(it)

    @pl.when(pl.program_id(1) == 0)
    def _():
        wb_ref[...] = w_ref[...].astype(BF16)

    a = a_ref[...]
    if has_norm:
        a = _rms(a.astype(F32), nw_ref[...])
    acc = jnp.dot(a.astype(BF16), wb_ref[...], preferred_element_type=F32)
    if has_res:
        acc = acc + r_ref[...]
    o_ref[...] = acc.astype(o_ref.dtype)


def _mm(a, w, norm_w=None, res=None, out_dtype=F32, tm=1024, tn=1024, n_cols=None, layer=None):
    m, k = a.shape
    n = w.shape[-1] if n_cols is None else n_cols
    tn = min(tn, n)
    tm = min(tm, m)
    assert m % tm == 0 and n % tn == 0
    w_spec = (pl.BlockSpec((k, tn), lambda j, i: (0, j)) if layer is None
              else pl.BlockSpec((None, k, tn), lambda j, i: (layer, 0, j)))
    in_specs = [pl.BlockSpec((tm, k), lambda j, i: (i, 0)), w_spec]
    args = [a, w]
    if norm_w is not None:
        in_specs.append(pl.BlockSpec((1, k), lambda j, i: (0, 0)))
        args.append(norm_w.reshape(1, k))
    if res is not None:
        in_specs.append(pl.BlockSpec((tm, tn), lambda j, i: (i, j)))
        args.append(res)
    return pl.pallas_call(
        functools.partial(_mm_kernel, has_norm=norm_w is not None, has_res=res is not None),
        grid=(n // tn, m // tm),
        in_specs=in_specs,
        out_specs=pl.BlockSpec((tm, tn), lambda j, i: (i, j)),
        out_shape=jax.ShapeDtypeStruct((m, n), out_dtype),
        scratch_shapes=[pltpu.VMEM((k, tn), BF16)],
        compiler_params=_cparams(("arbitrary", "arbitrary")),
    )(*args)


def _sigmoid(x):
    return 1.0 / (1.0 + jnp.exp(-x))


def _shift_rows(cur, prev, j, rows):
    return jnp.where(rows < j, pltpu.roll(prev, j, 0), pltpu.roll(cur, j, 0))


def _conv_silu(cur, prev, w_ref, b_ref, col0, rows):
    width = cur.shape[1]
    acc = cur * w_ref[CONV_WIDTH - 1:CONV_WIDTH, col0:col0 + width] + b_ref[:, col0:col0 + width]
    for j in range(1, CONV_WIDTH):
        k = CONV_WIDTH - 1 - j
        acc = acc + _shift_rows(cur, prev, j, rows) * w_ref[k:k + 1, col0:col0 + width]
    return acc * _sigmoid(acc)


def _mixer_kernel(z_ref, x_ref, bc_ref, u_ref, dt_ref, convw_ref, convb_ref, dtb_ref, alog_ref,
                  dskip_ref, ssdn_ref, poolw_ref, pscale_ref, out_ref,
                  px_ref, pbc_ref, pu_ref, state_ref):
    c = pl.program_id(1)

    @pl.when(c == 0)
    def _():
        px_ref[...] = jnp.zeros_like(px_ref)
        pbc_ref[...] = jnp.zeros_like(pbc_ref)
        pu_ref[...] = jnp.zeros_like(pu_ref)
        state_ref[...] = jnp.zeros_like(state_ref)

    rows = lax.broadcasted_iota(jnp.int32, (CHUNK, D_SSD), 0)
    x_raw = x_ref[...]
    bc_raw = bc_ref[...]
    xs = _conv_silu(x_raw, px_ref[...], convw_ref, convb_ref, 0, rows)
    bcs = _conv_silu(bc_raw, pbc_ref[...], convw_ref, convb_ref, D_SSD, rows)
    px_ref[...] = x_raw
    pbc_ref[...] = bc_raw

    dt_in = dt_ref[...] + dtb_ref[...]
    dt = jnp.maximum(dt_in, 0.0) + jnp.log(1.0 + jnp.exp(-jnp.abs(dt_in)))
    a = -jnp.exp(alog_ref[...])
    dta = dt * a
    ri = lax.broadcasted_iota(jnp.int32, (CHUNK, CHUNK), 0)
    ci = lax.broadcasted_iota(jnp.int32, (CHUNK, CHUNK), 1)
    causal = ri >= ci
    tril = jnp.where(causal, 1.0, 0.0).astype(F32)
    acs = jnp.dot(tril, dta, preferred_element_type=F32, precision=lax.Precision.HIGHEST)
    acs_t = acs.T
    dt_t = dt.T
    lo = ci < SSD_HEAD_DIM

    y_pairs = []
    for g in range(SSD_GROUPS):
        bg = bcs[:, g * D_STATE:(g + 1) * D_STATE].astype(BF16)
        cg = bcs[:, SSD_GROUPS * D_STATE + g * D_STATE:SSD_GROUPS * D_STATE + (g + 1) * D_STATE].astype(BF16)
        cb = lax.dot_general(cg, bg, (((1,), (1,)), ((), ())), preferred_element_type=F32)
        gw = HEADS_PER_GROUP * SSD_HEAD_DIM
        prev_g = state_ref[g * gw:(g + 1) * gw, :]
        yoff = lax.dot_general(cg, prev_g.astype(BF16), (((1,), (1,)), ((), ())),
                               preferred_element_type=F32)
        xw_parts = []
        for pr in range(HEADS_PER_GROUP // 2):
            pair = g * (HEADS_PER_GROUP // 2) + pr
            xs_pair = xs[:, pair * LANES:(pair + 1) * LANES]
            y_pair = None
            ecols = []
            wcols = []
            for q in range(2):
                h = pair * 2 + q
                col = acs[:, h:h + 1]
                seg = col - acs_t[h:h + 1, :]
                decay = jnp.exp(jnp.where(causal, seg, NEG_BIG))
                mat = (cb * decay * dt_t[h:h + 1, :]).astype(BF16)
                keep = lo if q == 0 else jnp.logical_not(lo)
                xh = jnp.where(keep, xs_pair, 0.0).astype(BF16)
                yd = jnp.dot(mat, xh, preferred_element_type=F32)
                y_pair = yd if y_pair is None else y_pair + yd
                ecols.append(jnp.exp(col))
                wcols.append(dt[:, h:h + 1] * jnp.exp(acs[CHUNK - 1:CHUNK, h:h + 1] - col))
            e_pair = jnp.where(lo, ecols[0], ecols[1])
            w_pair = jnp.where(lo, wcols[0], wcols[1])
            y_pairs.append(y_pair + yoff[:, pr * LANES:(pr + 1) * LANES] * e_pair)
            xw_parts.append(xs_pair * w_pair)
        xw = jnp.concatenate(xw_parts, axis=1).astype(BF16)
        s_new = lax.dot_general(xw, bg, (((0,), (0,)), ((), ())), preferred_element_type=F32)
        for r in range(HEADS_PER_GROUP):
            h = g * HEADS_PER_GROUP + r
            dec = jnp.exp(acs[CHUNK - 1:CHUNK, h:h + 1])
            lo_r = h * SSD_HEAD_DIM
            state_ref[lo_r:lo_r + SSD_HEAD_DIM, :] = (
                state_ref[lo_r:lo_r + SSD_HEAD_DIM, :] * dec
                + s_new[r * SSD_HEAD_DIM:(r + 1) * SSD_HEAD_DIM, :])

    y = jnp.concatenate(y_pairs, axis=1)
    y = y + dskip_ref[...] * xs
    z = z_ref[...]
    y = y * (z * _sigmoid(z))
    gdim = D_SSD // SSD_GROUPS
    for g in range(SSD_GROUPS):
        yg = _rms(y[:, g * gdim:(g + 1) * gdim], ssdn_ref[:, g * gdim:(g + 1) * gdim])
        out_ref[:, g * gdim:(g + 1) * gdim] = yg.astype(out_ref.dtype)

    u_cur = u_ref[...]
    u_prev = pu_ref[...]
    pos = (c * CHUNK + lax.broadcasted_iota(jnp.int32, (CHUNK, POOL_GROUP_DIM), 0) + 1).astype(F32)
    for g, win in enumerate(POOL_WINDOWS):
        sl = slice(g * POOL_GROUP_DIM, (g + 1) * POOL_GROUP_DIM)
        ug = u_cur[:, sl]
        ext = jnp.concatenate([u_prev[:, sl], ug], axis=0)
        step = 1
        while step < win:
            ext = ext + pltpu.roll(ext, step, 0)
            step *= 2
        pooled = ext[CHUNK:, :] / jnp.minimum(pos, float(win)) - ug
        mixed = jnp.dot(pooled.astype(BF16), poolw_ref[g].astype(BF16), preferred_element_type=F32)
        out_ref[:, D_SSD + g * POOL_GROUP_DIM:D_SSD + (g + 1) * POOL_GROUP_DIM] = (
            mixed * pscale_ref[:, sl]).astype(out_ref.dtype)
    pu_ref[...] = u_cur


def _mixer(proj, proj_u, dtp, conv_w, conv_b, dt_bias, a_log, d_skip, ssd_norm, pool_w, pool_scale, batch, seq):
    nchunk = seq // CHUNK
    row_map = lambda col: (lambda b, c: (b * nchunk + c, col))
    const2 = lambda b, c: (0, 0)
    pad = LANES - SSD_HEADS
    dtb = jnp.pad(dt_bias, (0, pad)).reshape(1, LANES)
    alog = jnp.pad(a_log, (0, pad)).reshape(1, LANES)
    dskip = jnp.repeat(d_skip, SSD_HEAD_DIM).reshape(1, D_SSD)
    return pl.pallas_call(
        _mixer_kernel,
        grid=(batch, nchunk),
        in_specs=[
            pl.BlockSpec((CHUNK, D_SSD), row_map(0)),
            pl.BlockSpec((CHUNK, D_SSD), row_map(1)),
            pl.BlockSpec((CHUNK, D_SSD), row_map(2)),
            pl.BlockSpec((CHUNK, D_POOL), row_map(0)),
            pl.BlockSpec((CHUNK, LANES), row_map(0)),
            pl.BlockSpec((CONV_WIDTH, D_CONV), const2),
            pl.BlockSpec((1, D_CONV), const2),
            pl.BlockSpec((1, LANES), const2),
            pl.BlockSpec((1, LANES), const2),
            pl.BlockSpec((1, D_SSD), const2),
            pl.BlockSpec((1, D_SSD), const2),
            pl.BlockSpec((len(POOL_WINDOWS), POOL_GROUP_DIM, POOL_GROUP_DIM), lambda b, c: (0, 0, 0)),
            pl.BlockSpec((1, D_POOL), const2),
        ],
        out_specs=pl.BlockSpec((CHUNK, D_MODEL), lambda b, c: (b * nchunk + c, 0)),
        out_shape=jax.ShapeDtypeStruct((batch * seq, D_MODEL), BF16),
        scratch_shapes=[pltpu.VMEM((CHUNK, D_SSD), F32), pltpu.VMEM((CHUNK, D_SSD), F32),
                        pltpu.VMEM((CHUNK, D_POOL), F32), pltpu.VMEM((SSD_HEADS * SSD_HEAD_DIM, D_STATE), F32)],
        compiler_params=_cparams(("arbitrary", "arbitrary")),
    )(proj, proj, proj, proj_u, dtp, conv_w, conv_b.reshape(1, D_CONV), dtb, alog, dskip,
      ssd_norm.reshape(1, D_SSD), pool_w, pool_scale.reshape(1, D_POOL))


def _attn_kernel(q_ref, k_ref, v_ref, o_ref):
    scale = XATTN_HEAD_DIM ** -0.5
    for h in range(XATTN_HEADS):
        sl = slice(h * XATTN_HEAD_DIM, (h + 1) * XATTN_HEAD_DIM)
        s = lax.dot_general(q_ref[:, sl], k_ref[:, sl], (((1,), (1,)), ((), ())),
                            preferred_element_type=F32) * scale
        s = s - jnp.max(s, axis=-1, keepdims=True)
        p = jnp.exp(s)
        p = p / jnp.sum(p, axis=-1, keepdims=True)
        o_ref[:, sl] = jnp.dot(p.astype(BF16), v_ref[:, sl], preferred_element_type=F32).astype(o_ref.dtype)


def _attn(q, kv, batch, seq, tq=512):
    nq = seq // tq
    return pl.pallas_call(
        _attn_kernel,
        grid=(batch, nq),
        in_specs=[pl.BlockSpec((tq, D_MODEL), lambda b, i: (b * nq + i, 0)),
                  pl.BlockSpec((N_MEM, D_MODEL), lambda b, i: (b, 0)),
                  pl.BlockSpec((N_MEM, D_MODEL), lambda b, i: (b, 1))],
        out_specs=pl.BlockSpec((tq, D_MODEL), lambda b, i: (b * nq + i, 0)),
        out_shape=jax.ShapeDtypeStruct((batch * seq, D_MODEL), BF16),
        compiler_params=_cparams(("arbitrary", "arbitrary")),
    )(q, kv, kv)


def _router_kernel(h_ref, nw_ref, wr_ref, br_ref, idx_ref, gate_ref, pos_ref, cnt_ref, run_ref):
    @pl.when(pl.program_id(0) == 0)
    def _():
        run_ref[...] = jnp.zeros_like(run_ref)

    hn = _rms(h_ref[...], nw_ref[...])
    w = wr_ref[...]
    h_hi = hn.astype(BF16)
    h_lo = (hn - h_hi.astype(F32)).astype(BF16)
    w_hi = w.astype(BF16)
    w_lo = (w - w_hi.astype(F32)).astype(BF16)
    logits = (jnp.dot(h_hi, w_hi, preferred_element_type=F32)
              + (jnp.dot(h_hi, w_lo, preferred_element_type=F32)
                 + jnp.dot(h_lo, w_hi, preferred_element_type=F32))) + br_ref[...]
    tm = logits.shape[0]
    lane = lax.broadcasted_iota(jnp.int32, logits.shape, 1)
    vals = []
    hots = []
    idx_out = jnp.zeros(logits.shape, jnp.int32)
    for k in range(TOP_K):
        m = jnp.max(logits, axis=-1, keepdims=True)
        idx = jnp.min(jnp.where(logits == m, lane, LANES), axis=-1, keepdims=True)
        hit = lane == idx
        idx_out = jnp.where(lane == k, idx, idx_out)
        logits = jnp.where(hit, -jnp.inf, logits)
        vals.append(m)
        hots.append(jnp.where(hit, 1.0, 0.0))
    exps = [jnp.exp(v - vals[0]) for v in vals]
    den = exps[0] + exps[1] + exps[2] + exps[3]
    gate_out = jnp.zeros(logits.shape, F32)
    for k in range(TOP_K):
        gate_out = jnp.where(lane == k, exps[k] / den, gate_out)
    idx_ref[...] = idx_out
    gate_ref[...] = gate_out

    hot = hots[0] + hots[1] + hots[2] + hots[3]
    ri = lax.broadcasted_iota(jnp.int32, (tm, tm), 0)
    ci = lax.broadcasted_iota(jnp.int32, (tm, tm), 1)
    before = jnp.where(ri > ci, 1.0, 0.0).astype(BF16)
    ahead = jnp.dot(before, hot.astype(BF16), preferred_element_type=F32) + run_ref[...]
    pos_out = jnp.zeros(logits.shape, F32)
    for k in range(TOP_K):
        pos_out = jnp.where(lane == k, jnp.sum(ahead * hots[k], axis=-1, keepdims=True), pos_out)
    pos_ref[...] = pos_out
    run_ref[...] = run_ref[...] + jnp.sum(hot, axis=0, keepdims=True)
    cnt_ref[...] = run_ref[...]


def _router(h, norm_w, w_router, b_router):
    t = h.shape[0]
    tm = ROUTER_TOKENS
    pad = LANES - N_EXPERTS
    wr = jnp.pad(w_router, ((0, 0), (0, pad)))
    br = jnp.pad(b_router, (0, pad), constant_values=NEG_BIG).reshape(1, LANES)
    tok_spec = pl.BlockSpec((tm, LANES), lambda i: (i, 0))
    return pl.pallas_call(
        _router_kernel,
        grid=(t // tm,),
        in_specs=[pl.BlockSpec((tm, D_MODEL), lambda i: (i, 0)),
                  pl.BlockSpec((1, D_MODEL), lambda i: (0, 0)),
                  pl.BlockSpec((D_MODEL, LANES), lambda i: (0, 0)),
                  pl.BlockSpec((1, LANES), lambda i: (0, 0))],
        out_specs=[tok_spec, tok_spec, tok_spec, pl.BlockSpec((1, LANES), lambda i: (0, 0))],
        out_shape=[jax.ShapeDtypeStruct((t, LANES), jnp.int32),
                   jax.ShapeDtypeStruct((t, LANES), F32),
                   jax.ShapeDtypeStruct((t, LANES), F32),
                   jax.ShapeDtypeStruct((1, LANES), F32)],
        scratch_shapes=[pltpu.VMEM((1, LANES), F32)],
        compiler_params=_cparams(("arbitrary",)),
    )(h, norm_w.reshape(1, D_MODEL), wr, br)


def _dest_kernel(idx_ref, pos_ref, cnt_ref, dest_ref, bounds_ref):
    cnt = jnp.broadcast_to(cnt_ref[...], (SUBLANES, LANES))
    padded = jnp.floor((cnt + (MOE_BLOCK - 1)) * (1.0 / MOE_BLOCK)) * MOE_BLOCK
    ri = lax.broadcasted_iota(jnp.int32, (LANES, LANES), 0)
    ci = lax.broadcasted_iota(jnp.int32, (LANES, LANES), 1)
    upto = jnp.where(ri <= ci, 1.0, 0.0).astype(F32)
    seg_end = jnp.dot(padded, upto, preferred_element_type=F32, precision=lax.Precision.HIGHEST)
    seg_start = seg_end - padded
    blocks = padded * (1.0 / MOE_BLOCK)
    full = jnp.floor(blocks * (1.0 * MOE_BLOCK / MOE_ITEM))
    items = blocks - full
    items_end = jnp.dot(items, upto, preferred_element_type=F32, precision=lax.Precision.HIGHEST)
    row = lax.broadcasted_iota(jnp.int32, (SUBLANES, LANES), 0)
    table = jnp.zeros((SUBLANES, LANES), F32)
    for r, v in enumerate((seg_start, seg_end, items_end, items, full)):
        table = jnp.where(row == r, v, table)
    bounds_ref[...] = table.astype(jnp.int32)

    idx = idx_ref[...]
    lane = lax.broadcasted_iota(jnp.int32, idx.shape, 1)
    out = pos_ref[...]
    for k in range(TOP_K):
        hit = lane == idx[:, k:k + 1]
        base = jnp.sum(jnp.where(hit, seg_start[0:1, :], 0.0), axis=-1, keepdims=True)
        out = jnp.where(lane == k, out + base, out)
    dest_ref[...] = out.T[0:SUBLANES, :].astype(jnp.int32)


def _dest_rows(idx_pad, pos_pad, cnt_pad):
    t = idx_pad.shape[0]
    tm = ROUTER_TOKENS
    tok_spec = pl.BlockSpec((tm, LANES), lambda i: (i, 0))
    return pl.pallas_call(
        _dest_kernel,
        grid=(t // tm,),
        in_specs=[tok_spec, tok_spec, pl.BlockSpec((1, LANES), lambda i: (0, 0))],
        out_specs=[pl.BlockSpec((SUBLANES, tm), lambda i: (0, i)),
                   pl.BlockSpec((SUBLANES, LANES), lambda i: (0, 0))],
        out_shape=[jax.ShapeDtypeStruct((SUBLANES, t), jnp.int32),
                   jax.ShapeDtypeStruct((SUBLANES, LANES), jnp.int32)],
        compiler_params=_cparams(("arbitrary",)),
    )(idx_pad, pos_pad, cnt_pad)


def _pack_halves(x):
    c = x.shape[1] // 2
    lo = pltpu.bitcast(x[:, :c].astype(BF16).astype(F32), jnp.uint32)
    hi = pltpu.bitcast(x[:, c:].astype(BF16).astype(F32), jnp.uint32)
    return (lo >> 16) | hi


def _unpack_halves(w):
    lo = pltpu.bitcast(w << 16, F32)
    hi = pltpu.bitcast(w & jnp.uint32(0xFFFF0000), F32)
    return jnp.concatenate([lo, hi], axis=1).astype(BF16)


def _scatter_kernel(dest_ref, bounds_ref, h_ref, nw_ref, x_ref, zero_ref, packed_ref, zsem, sem):
    n_blocks = x_ref.shape[0] // MOE_BLOCK
    i = pl.program_id(0)
    slot = i % 2
    packed_ref[slot] = _pack_halves(_rms(h_ref[...], nw_ref[...]))

    @pl.when(i == 0)
    def _():
        zero_ref[...] = jnp.zeros_like(zero_ref)
        n_used = bounds_ref[1, N_EXPERTS - 1] // MOE_BLOCK

        def zero_copy(b):
            return pltpu.make_async_copy(zero_ref, x_ref.at[pl.ds(pl.multiple_of(b * MOE_BLOCK, MOE_BLOCK),
                                                                   MOE_BLOCK)], zsem)

        def last_block(e):
            return bounds_ref[1, e] // MOE_BLOCK - 1, bounds_ref[1, e] > bounds_ref[0, e]

        def start_e(e, carry):
            b, nonempty = last_block(e)

            @pl.when(nonempty)
            def _():
                zero_copy(b).start()
            return carry

        def wait_e(e, carry):
            b, nonempty = last_block(e)

            @pl.when(nonempty)
            def _():
                zero_copy(b).wait()
            return carry

        lax.fori_loop(0, N_EXPERTS, start_e, 0)
        lax.fori_loop(n_used, n_blocks, lambda b, c: (zero_copy(b).start(), c)[1], 0)
        lax.fori_loop(0, N_EXPERTS, wait_e, 0)
        lax.fori_loop(n_used, n_blocks, lambda b, c: (zero_copy(b).wait(), c)[1], 0)

    def issue(r, carry):
        for k in range(TOP_K):
            d = dest_ref[k, r]
            pltpu.make_async_copy(packed_ref.at[slot, pl.ds(r, 1)], x_ref.at[pl.ds(d, 1)],
                                  sem.at[slot]).start(priority=k % 2)
        return carry

    lax.fori_loop(0, SCATTER_TOKENS, issue, 0)

    def wait_rows(s):
        n_sent = SCATTER_TOKENS * TOP_K
        pltpu.make_async_copy(x_ref.at[pl.ds(0, n_sent)], x_ref.at[pl.ds(0, n_sent)], sem.at[s]).wait()

    @pl.when(i > 0)
    def _():
        wait_rows(1 - slot)

    @pl.when(i == pl.num_programs(0) - 1)
    def _():
        wait_rows(slot)


def _scatter_rows(h, norm_w, dest_t, bounds, n_rows):
    t = h.shape[0]
    steps = t // SCATTER_TOKENS
    half = D_MODEL // 2
    return pl.pallas_call(
        _scatter_kernel,
        grid=(steps,),
        in_specs=[pl.BlockSpec((SUBLANES, SCATTER_TOKENS), lambda i: (0, i), memory_space=pltpu.SMEM),
                  pl.BlockSpec(memory_space=pltpu.SMEM),
                  pl.BlockSpec((SCATTER_TOKENS, D_MODEL), lambda i: (i, 0)),
                  pl.BlockSpec((1, D_MODEL), lambda i: (0, 0))],
        out_specs=pl.BlockSpec(memory_space=pl.ANY),
        out_shape=jax.ShapeDtypeStruct((n_rows, half), jnp.uint32),
        scratch_shapes=[pltpu.VMEM((MOE_BLOCK, half), jnp.uint32), pltpu.VMEM((2, SCATTER_TOKENS, half), jnp.uint32),
                        pltpu.SemaphoreType.DMA(()), pltpu.SemaphoreType.DMA((2,))],
        compiler_params=_cparams(("arbitrary",)),
    )(dest_t, bounds, h, norm_w.reshape(1, D_MODEL))


def _item_tables(bounds, n_j, n_items_max):
    items_e = bounds[3]
    first_e = (bounds[2] - items_e) * n_j
    end_e = bounds[2] * n_j
    n_items = end_e[N_EXPERTS - 1]
    g = jnp.minimum(jnp.arange(n_items_max, dtype=jnp.int32), n_items - 1)
    e = jnp.sum((g[:, None] >= end_e[None, :N_EXPERTS]).astype(jnp.int32), axis=1)
    own = e[:, None] == jnp.arange(LANES, dtype=jnp.int32)[None, :]
    pick = lambda v: jnp.sum(jnp.where(own, v[None, :], 0), axis=1)
    first = pick(first_e)
    nb = jnp.maximum(pick(items_e), 1)
    q = g - first
    j = q // nb
    i = q % nb
    row = pick(bounds[0] // MOE_BLOCK) + i * (MOE_ITEM // MOE_BLOCK)
    single = (i >= pick(bounds[4])).astype(jnp.int32)
    meta = e * 16 + single * 8 + j * 2 + (i == 0).astype(jnp.int32)
    group_end = first + (j + 1) * nb
    counts = jnp.stack([n_items, bounds[1, N_EXPERTS - 1] // MOE_BLOCK])
    return meta.astype(jnp.int32), row.astype(jnp.int32), group_end.astype(jnp.int32), counts.astype(jnp.int32)


def _meta_expert(m):
    return m >> 4


def _meta_tile(m):
    return (m >> 1) & 3


def _item_loop(meta_ref, row_ref, gend_ref, n_ref, src_ref, dst_ref, w_refs, ring, ring_sem, wb_ref,
               in_buf, out_buf, in_sem, out_sem, compute):
    n = n_ref[0]
    n_in = in_buf.shape[0]
    n_out = out_buf.shape[0]
    tn = out_buf.shape[2]
    n_j = dst_ref.shape[1] // tn
    n_mats = len(w_refs)
    cpm = W_CHUNKS // n_mats
    kc = w_refs[0].shape[1] // cpm

    def by_size(m, fn):
        @pl.when(((m >> 3) & 1) == 0)
        def _():
            fn(MOE_ITEM)

        @pl.when(((m >> 3) & 1) == 1)
        def _():
            fn(MOE_BLOCK)

    def rows(g, size):
        return pl.ds(pl.multiple_of(row_ref[g] * MOE_BLOCK, MOE_BLOCK), size)

    def cols(m):
        return pl.ds(pl.multiple_of(_meta_tile(m) * tn, tn), tn)

    def in_copy(g, size):
        slot = g % n_in
        return pltpu.make_async_copy(src_ref.at[rows(g, size)], in_buf.at[slot, pl.ds(0, size)], in_sem.at[slot])

    def out_copy(g, size):
        slot = g % n_out
        return pltpu.make_async_copy(out_buf.at[slot, pl.ds(0, size)], dst_ref.at[rows(g, size), cols(meta_ref[g])],
                                     out_sem.at[slot])

    def start_in(g):
        by_size(meta_ref[g], lambda size: in_copy(g, size).start(priority=ROW_DMA_PRIORITY))

    def wait_in(g):
        by_size(meta_ref[g], lambda size: in_copy(g, size).wait())

    def start_out(g):
        by_size(meta_ref[g], lambda size: out_copy(g, size).start(priority=ROW_DMA_PRIORITY))

    def wait_out(g):
        by_size(meta_ref[g], lambda size: out_copy(g, size).wait())

    def krows(c):
        r0 = (c % cpm) * kc
        return pl.ds(r0 if isinstance(c, int) else pl.multiple_of(r0, kc), kc)

    def chunk_copy(m, c, k):
        slot = c % W_RING
        return pltpu.make_async_copy(w_refs[k].at[_meta_expert(m), krows(c), cols(m)], ring.at[slot],
                                     ring_sem.at[slot])

    def for_matrix(c, fn):
        if isinstance(c, int):
            return fn(c // cpm)
        for k in range(n_mats):
            @pl.when(c // cpm == k)
            def _(k=k):
                fn(k)

    def start_chunk(m, c):
        for_matrix(c, lambda k: chunk_copy(m, c, k).start())

    def finish_chunk(m, c, wslot):
        def fn(k):
            chunk_copy(m, c, k).wait()
            wb_ref[wslot, k, krows(c), :] = ring[c % W_RING].astype(BF16)
        for_matrix(c, fn)

        @pl.when(c + W_RING < W_CHUNKS)
        def _():
            start_chunk(m, c + W_RING)

    for c0 in range(W_RING):
        start_chunk(meta_ref[0], c0)
    for g0 in range(n_in - 1):
        @pl.when(g0 < n)
        def _():
            start_in(g0)

    def body(g, carry):
        parity, done, nm = carry
        m = meta_ref[g]
        first = (m & 1) == 1

        @pl.when(first)
        def _():
            lax.fori_loop(done, W_CHUNKS, lambda c, z: (finish_chunk(m, c, 1 - parity), z)[1], 0)
            nxt = gend_ref[g]

            @pl.when(nxt < n)
            def _():
                for c0 in range(W_RING):
                    start_chunk(meta_ref[nxt], c0)

        nxt = gend_ref[g]
        parity = jnp.where(first, 1 - parity, parity)
        done = jnp.where(first, 0, done)
        nm = jnp.where(first, jnp.where(nxt < n, meta_ref[jnp.minimum(nxt, n - 1)], -1), nm)

        wait_in(g)

        @pl.when(g + n_in - 1 < n)
        def _():
            start_in(g + n_in - 1)

        @pl.when(g >= n_out)
        def _():
            wait_out(g - n_out)

        def run(size):
            x = in_buf[g % n_in, pl.ds(0, size)]
            out_buf[g % n_out, pl.ds(0, size)] = compute(x, wb_ref.at[parity], m).astype(out_buf.dtype)

        by_size(m, run)
        start_out(g)

        active = (nm >= 0) & (done < W_CHUNKS)

        @pl.when(active)
        def _():
            finish_chunk(nm, done, 1 - parity)

        return parity, jnp.where(active, done + 1, done), nm

    lax.fori_loop(0, n, body, (jnp.int32(1), jnp.int32(0), meta_ref[0]))
    lax.fori_loop(jnp.maximum(n - n_out, 0), n, lambda g, c: (wait_out(g), c)[1], 0)

    n_blocks = dst_ref.shape[0] // MOE_BLOCK
    n_used = n_ref[1]
    out_buf[0] = jnp.zeros(out_buf.shape[1:], out_buf.dtype)
    for jt in range(n_j):
        def tail_copy(b, jt=jt):
            r = pl.ds(pl.multiple_of(b * MOE_BLOCK, MOE_BLOCK), MOE_BLOCK)
            return pltpu.make_async_copy(out_buf.at[0, pl.ds(0, MOE_BLOCK)], dst_ref.at[r, jt * tn:(jt + 1) * tn],
                                         out_sem.at[0])

        lax.fori_loop(n_used, n_blocks, lambda b, c: (tail_copy(b).start(), c)[1], 0)
        lax.fori_loop(n_used, n_blocks, lambda b, c: (tail_copy(b).wait(), c)[1], 0)


def _expert_up_kernel(meta_ref, row_ref, gend_ref, n_ref, x_ref, bg_ref, bu_ref, wg_ref, wu_ref, act_ref,
                      ring, wb_ref, x_buf, a_buf, ring_sem, in_sem, out_sem):
    n_j = D_FF // MOE_UP_TN

    def compute(x, w, m):
        tile = _meta_expert(m) * n_j + _meta_tile(m)
        xb = _unpack_halves(x)
        g = jnp.dot(xb, w[0], preferred_element_type=F32) + bg_ref[pl.ds(tile, 1), :]
        u = jnp.dot(xb, w[1], preferred_element_type=F32) + bu_ref[pl.ds(tile, 1), :]
        g = jnp.minimum(g, SWIGLU_LIMIT)
        u = jnp.clip(u, -SWIGLU_LIMIT, SWIGLU_LIMIT)
        return g * _sigmoid(SWIGLU_ALPHA * g) * (u + 1.0)

    _item_loop(meta_ref, row_ref, gend_ref, n_ref, x_ref, act_ref, (wg_ref, wu_ref), ring, ring_sem, wb_ref,
               x_buf, a_buf, in_sem, out_sem, compute)


def _expert_down_kernel(meta_ref, row_ref, gend_ref, n_ref, a_ref, bd_ref, wd_ref, y_ref,
                        ring, wb_ref, a_buf, y_buf, ring_sem, in_sem, out_sem):
    n_j = D_MODEL // MOE_DOWN_TN

    def compute(a, w, m):
        tile = _meta_expert(m) * n_j + _meta_tile(m)
        return jnp.dot(a, w[0], preferred_element_type=F32) + bd_ref[pl.ds(tile, 1), :]

    _item_loop(meta_ref, row_ref, gend_ref, n_ref, a_ref, y_ref, (wd_ref,), ring, ring_sem, wb_ref,
               a_buf, y_buf, in_sem, out_sem, compute)


def _experts(x_rows, bounds, w_gate, b_gate, w_up, b_up, w_down, b_down):
    n_rows = x_rows.shape[0]
    n_items_max = n_rows // MOE_ITEM + N_EXPERTS
    any_spec = pl.BlockSpec(memory_space=pl.ANY)
    sems = [pltpu.SemaphoreType.DMA((W_RING,)), pltpu.SemaphoreType.DMA((MOE_IN_BUFS,)),
            pltpu.SemaphoreType.DMA((MOE_OUT_BUFS,))]

    def full(shape):
        return pl.BlockSpec(shape, lambda i, *_: (0,) * len(shape))

    n_j = D_FF // MOE_UP_TN
    act = pl.pallas_call(
        _expert_up_kernel,
        grid_spec=pltpu.PrefetchScalarGridSpec(
            num_scalar_prefetch=4, grid=(1,),
            in_specs=[any_spec, full((N_EXPERTS * n_j, MOE_UP_TN)),
                      full((N_EXPERTS * n_j, MOE_UP_TN)), any_spec, any_spec],
            out_specs=any_spec,
            scratch_shapes=[pltpu.VMEM((W_RING, 2 * D_MODEL // W_CHUNKS, MOE_UP_TN), F32),
                            pltpu.VMEM((2, 2, D_MODEL, MOE_UP_TN), BF16),
                            pltpu.VMEM((MOE_IN_BUFS, MOE_ITEM, D_MODEL // 2), jnp.uint32),
                            pltpu.VMEM((MOE_OUT_BUFS, MOE_ITEM, MOE_UP_TN), BF16)] + sems),
        out_shape=jax.ShapeDtypeStruct((n_rows, D_FF), BF16),
        compiler_params=_cparams(("arbitrary",)),
    )(*_item_tables(bounds, n_j, n_items_max * n_j), x_rows,
      b_gate.reshape(N_EXPERTS * n_j, MOE_UP_TN), b_up.reshape(N_EXPERTS * n_j, MOE_UP_TN), w_gate, w_up)

    n_j = D_MODEL // MOE_DOWN_TN
    return pl.pallas_call(
        _expert_down_kernel,
        grid_spec=pltpu.PrefetchScalarGridSpec(
            num_scalar_prefetch=4, grid=(1,),
            in_specs=[any_spec, full((N_EXPERTS * n_j, MOE_DOWN_TN)), any_spec],
            out_specs=any_spec,
            scratch_shapes=[pltpu.VMEM((W_RING, D_FF // W_CHUNKS, MOE_DOWN_TN), F32),
                            pltpu.VMEM((2, 1, D_FF, MOE_DOWN_TN), BF16),
                            pltpu.VMEM((MOE_IN_BUFS, MOE_ITEM, D_FF), BF16),
                            pltpu.VMEM((MOE_OUT_BUFS, MOE_ITEM, MOE_DOWN_TN), F32)] + sems),
        out_shape=jax.ShapeDtypeStruct((n_rows, D_MODEL), F32),
        compiler_params=_cparams(("arbitrary",)),
    )(*_item_tables(bounds, n_j, n_items_max * n_j), act,
      b_down.reshape(N_EXPERTS * n_j, MOE_DOWN_TN), w_down)


def _combine_kernel(dest_ref, next_ref, h_ref, gate_ref, nw_ref, y_ref, o_ref, buf_ref, sem):
    i = pl.program_id(0)
    steps = pl.num_programs(0)

    def issue(d_ref, slot):
        def body(r, carry):
            for k in range(TOP_K):
                pltpu.make_async_copy(y_ref.at[pl.ds(d_ref[k, r], 1)], buf_ref.at[slot, k, pl.ds(r, 1)],
                                      sem.at[slot]).start(priority=k % 2)
            return carry
        lax.fori_loop(0, COMBINE_TOKENS, body, 0)

    @pl.when(i == 0)
    def _():
        issue(dest_ref, 0)

    @pl.when(i + 1 < steps)
    def _():
        issue(next_ref, (i + 1) % 2)

    slot = i % 2
    pltpu.make_async_copy(buf_ref.at[slot], buf_ref.at[slot], sem.at[slot]).wait()
    acc = h_ref[...]
    gate = gate_ref[...]
    for k in range(TOP_K):
        acc = acc + gate[:, k:k + 1] * buf_ref[slot, k]
    o_ref[...] = _rms(acc, nw_ref[...])


def _combine(h, y_rows, dest_t, gate_pad, norm_w):
    t = h.shape[0]
    steps = t // COMBINE_TOKENS
    return pl.pallas_call(
        _combine_kernel,
        grid=(steps,),
        in_specs=[pl.BlockSpec((SUBLANES, COMBINE_TOKENS), lambda i: (0, i), memory_space=pltpu.SMEM),
                  pl.BlockSpec((SUBLANES, COMBINE_TOKENS), lambda i: (0, jnp.minimum(i + 1, steps - 1)),
                               memory_space=pltpu.SMEM),
                  pl.BlockSpec((COMBINE_TOKENS, D_MODEL), lambda i: (i, 0)),
                  pl.BlockSpec((COMBINE_TOKENS, LANES), lambda i: (i, 0)),
                  pl.BlockSpec((1, D_MODEL), lambda i: (0, 0)),
                  pl.BlockSpec(memory_space=pl.ANY)],
        out_specs=pl.BlockSpec((COMBINE_TOKENS, D_MODEL), lambda i: (i, 0)),
        out_shape=jax.ShapeDtypeStruct((t, D_MODEL), F32),
        scratch_shapes=[pltpu.VMEM((2, TOP_K, COMBINE_TOKENS, D_MODEL), F32), pltpu.SemaphoreType.DMA((2,))],
        compiler_params=_cparams(("arbitrary",)),
    )(dest_t, dest_t, h, gate_pad, norm_w.reshape(1, D_MODEL), y_rows)


def _moe(h, norm_ffn, w_router, b_router, w_gate, b_gate, w_up, b_up, w_down, b_down, norm_final):
    t = h.shape[0]
    tk = t * TOP_K
    n_blocks = (tk + N_EXPERTS * (MOE_BLOCK - 1) + MOE_BLOCK - 1) // MOE_BLOCK
    n_rows = n_blocks * MOE_BLOCK
    idx_pad, gate_pad, pos_pad, cnt_pad = _router(h, norm_ffn, w_router, b_router)
    dest_t, bounds = _dest_rows(idx_pad, pos_pad, cnt_pad)
    x_rows = _scatter_rows(h, norm_ffn, dest_t, bounds, n_rows)
    y_rows = _experts(x_rows, bounds, w_gate, b_gate, w_up, b_up, w_down, b_down)
    return _combine(h, y_rows, dest_t, gate_pad, norm_final)


def kernel(x, mem, norm_mix, w_in, conv_w, conv_b, dt_bias, a_log, d_skip, ssd_norm, pool_w, pool_scale,
           w_out, norm_xattn, norm_mem, w_q, w_kv, w_o, norm_ffn, w_router, b_router, w_gate, b_gate,
           w_up, b_up, w_down, b_down, norm_final):
    batch, seq, d = x.shape
    t = batch * seq
    h = x.reshape(t, d)
    depth = norm_mix.shape[0]
    assert depth == 1, "the final norm is fused into the MoE combine of the single layer"
    for l in range(depth):
        dt_lo = D_SSD + D_CONV
        w_dt = jnp.pad(w_in[l][:, dt_lo:dt_lo + SSD_HEADS], ((0, 0), (0, LANES - SSD_HEADS)))
        proj = _mm(h, w_in, norm_w=norm_mix[l], n_cols=dt_lo, layer=l)
        proj_u = _mm(h, w_in[l][:, dt_lo + SSD_HEADS:], norm_w=norm_mix[l])
        dtp = _mm(h, w_dt, norm_w=norm_mix[l])
        mix = _mixer(proj, proj_u, dtp, conv_w[l], conv_b[l], dt_bias[l], a_log[l], d_skip[l], ssd_norm[l],
                     pool_w[l], pool_scale[l], batch, seq)
        h = _mm(mix, w_out[l], res=h)

        q = _mm(h, w_q[l], norm_w=norm_xattn[l], out_dtype=BF16)
        kv = _mm(mem.reshape(batch * N_MEM, d), w_kv[l], norm_w=norm_mem[l], out_dtype=BF16)
        o = _attn(q, kv, batch, seq)
        h = _mm(o, w_o[l], res=h)

        out = _moe(h, norm_ffn[l], w_router[l], b_router[l], w_gate[l], b_gate[l], w_up[l], b_up[l],
                   w_down[l], b_down[l], norm_final)
    return out.reshape(batch, seq, d)
```

```python
import functools

import jax
import jax.numpy as jnp
from jax import lax
from jax.experimental import pallas as pl
from jax.experimental.pallas import tpu as pltpu

F32 = jnp.float32
BF16 = jnp.bfloat16

D_MODEL = 2048
D_SSD = 1024
SSD_HEAD_DIM = 64
SSD_HEADS = 16
SSD_GROUPS = 4
HEADS_PER_GROUP = SSD_HEADS // SSD_GROUPS
D_STATE = 128
CONV_WIDTH = 4
CHUNK = 128
D_CONV = D_SSD + 2 * SSD_GROUPS * D_STATE
D_POOL = 1024
POOL_WINDOWS = (2, 4, 8, 16)
POOL_GROUP_DIM = 256
N_MEM = 256
XATTN_HEADS = 4
XATTN_HEAD_DIM = 512
N_EXPERTS = 32
TOP_K = 4
D_FF = 2048
SWIGLU_LIMIT = 7.0
SWIGLU_ALPHA = 1.702
EPS = 1e-5

LANES = 128
SUBLANES = 8
VMEM_LIMIT = 56 * 1024 * 1024
NEG_BIG = -1e30

MOE_BLOCK = 128
MOE_ITEM_SIZES = (4 * MOE_BLOCK, 2 * MOE_BLOCK, MOE_BLOCK)
MOE_ITEM = MOE_ITEM_SIZES[0]
MOE_UP_TN = 512
MOE_DOWN_TN = 1024
MOE_IN_BUFS = 3
MOE_OUT_BUFS = 2
W_CHUNKS = 4
W_RING = 4
ROUTER_TOKENS = 512
SCATTER_TOKENS = 512
COMBINE_TOKENS = 256
ROW_DMA_PRIORITY = 1


def _cparams(sem):
    return pltpu.CompilerParams(dimension_semantics=sem, vmem_limit_bytes=VMEM_LIMIT)


def _rms(x, w):
    ms = jnp.mean(x * x, axis=-1, keepdims=True)
    return x * lax.rsqrt(ms + EPS) * w


def _mm_kernel(*refs, has_norm, has_res):
    it = iter(refs)
    a_ref = next(it)
    w_ref = next(it)
    nw_ref = next(it) if has_norm else None
    r_ref = next(it) if has_res else None
    o_ref = next(it)
    wb_ref = next(it)

    @pl.when(pl.program_id(1) == 0)
    def _():
        wb_ref[...] = w_ref[...].astype(BF16)

    a = a_ref[...]
    if has_norm:
        a = _rms(a.astype(F32), nw_ref[...])
    acc = jnp.dot(a.astype(BF16), wb_ref[...], preferred_element_type=F32)
    if has_res:
        acc = acc + r_ref[...]
    o_ref[...] = acc.astype(o_ref.dtype)


def _mm(a, w, norm_w=None, res=None, out_dtype=F32, tm=1024, tn=1024, n_cols=None, layer=None):
    m, k = a.shape
    n = w.shape[-1] if n_cols is None else n_cols
    tn = min(tn, n)
    tm = min(tm, m)
    assert m % tm == 0 and n % tn == 0
    w_spec = (pl.BlockSpec((k, tn), lambda j, i: (0, j)) if layer is None
              else pl.BlockSpec((None, k, tn), lambda j, i: (layer, 0, j)))
    in_specs = [pl.BlockSpec((tm, k), lambda j, i: (i, 0)), w_spec]
    args = [a, w]
    if norm_w is not None:
        in_specs.append(pl.BlockSpec((1, k), lambda j, i: (0, 0)))
        args.append(norm_w.reshape(1, k))
    if res is not None:
        in_specs.append(pl.BlockSpec((tm, tn), lambda j, i: (i, j)))
        args.append(res)
    return pl.pallas_call(
        functools.partial(_mm_kernel, has_norm=norm_w is not None, has_res=res is not None),
        grid=(n // tn, m // tm),
        in_specs=in_specs,
        out_specs=pl.BlockSpec((tm, tn), lambda j, i: (i, j)),
        out_shape=jax.ShapeDtypeStruct((m, n), out_dtype),
        scratch_shapes=[pltpu.VMEM((k, tn), BF16)],
        compiler_params=_cparams(("arbitrary", "arbitrary")),
    )(*args)


def _sigmoid(x):
    return 1.0 / (1.0 + jnp.exp(-x))


def _shift_rows(cur, prev, j, rows):
    return jnp.where(rows < j, pltpu.roll(prev, j, 0), pltpu.roll(cur, j, 0))


def _conv_silu(cur, prev, w_ref, b_ref, col0, rows):
    width = cur.shape[1]
    acc = cur * w_ref[CONV_WIDTH - 1:CONV_WIDTH, col0:col0 + width] + b_ref[:, col0:col0 + width]
    for j in range(1, CONV_WIDTH):
        k = CONV_WIDTH - 1 - j
        acc = acc + _shift_rows(cur, prev, j, rows) * w_ref[k:k + 1, col0:col0 + width]
    return acc * _sigmoid(acc)


def _mixer_kernel(z_ref, x_ref, bc_ref, u_ref, dt_ref, convw_ref, convb_ref, dtb_ref, alog_ref,
                  dskip_ref, ssdn_ref, poolw_ref, pscale_ref, out_ref,
                  px_ref, pbc_ref, pu_ref, state_ref):
    c = pl.program_id(1)

    @pl.when(c == 0)
    def _():
        px_ref[...] = jnp.zeros_like(px_ref)
        pbc_ref[...] = jnp.zeros_like(pbc_ref)
        pu_ref[...] = jnp.zeros_like(pu_ref)
        state_ref[...] = jnp.zeros_like(state_ref)

    rows = lax.broadcasted_iota(jnp.int32, (CHUNK, D_SSD), 0)
    x_raw = x_ref[...]
    bc_raw = bc_ref[...]
    xs = _conv_silu(x_raw, px_ref[...], convw_ref, convb_ref, 0, rows)
    bcs = _conv_silu(bc_raw, pbc_ref[...], convw_ref, convb_ref, D_SSD, rows)
    px_ref[...] = x_raw
    pbc_ref[...] = bc_raw

    dt_in = dt_ref[...] + dtb_ref[...]
    dt = jnp.maximum(dt_in, 0.0) + jnp.log(1.0 + jnp.exp(-jnp.abs(dt_in)))
    a = -jnp.exp(alog_ref[...])
    dta = dt * a
    ri = lax.broadcasted_iota(jnp.int32, (CHUNK, CHUNK), 0)
    ci = lax.broadcasted_iota(jnp.int32, (CHUNK, CHUNK), 1)
    causal = ri >= ci
    tril = jnp.where(causal, 1.0, 0.0).astype(F32)
    acs = jnp.dot(tril, dta, preferred_element_type=F32, precision=lax.Precision.HIGHEST)
    acs_t = acs.T
    dt_t = dt.T
    lo = ci < SSD_HEAD_DIM

    y_pairs = []
    for g in range(SSD_GROUPS):
        bg = bcs[:, g * D_STATE:(g + 1) * D_STATE].astype(BF16)
        cg = bcs[:, SSD_GROUPS * D_STATE + g * D_STATE:SSD_GROUPS * D_STATE + (g + 1) * D_STATE].astype(BF16)
        cb = lax.dot_general(cg, bg, (((1,), (1,)), ((), ())), preferred_element_type=F32)
        gw = HEADS_PER_GROUP * SSD_HEAD_DIM
        prev_g = state_ref[g * gw:(g + 1) * gw, :]
        yoff = lax.dot_general(cg, prev_g.astype(BF16), (((1,), (1,)), ((), ())),
                               preferred_element_type=F32)
        xw_parts = []
        for pr in range(HEADS_PER_GROUP // 2):
            pair = g * (HEADS_PER_GROUP // 2) + pr
            xs_pair = xs[:, pair * LANES:(pair + 1) * LANES]
            y_pair = None
            ecols = []
            wcols = []
            for q in range(2):
                h = pair * 2 + q
                col = acs[:, h:h + 1]
                seg = col - acs_t[h:h + 1, :]
                decay = jnp.exp(jnp.where(causal, seg, NEG_BIG))
                mat = (cb * decay * dt_t[h:h + 1, :]).astype(BF16)
                keep = lo if q == 0 else jnp.logical_not(lo)
                xh = jnp.where(keep, xs_pair, 0.0).astype(BF16)
                yd = jnp.dot(mat, xh, preferred_element_type=F32)
                y_pair = yd if y_pair is None else y_pair + yd
                ecols.append(jnp.exp(col))
                wcols.append(dt[:, h:h + 1] * jnp.exp(acs[CHUNK - 1:CHUNK, h:h + 1] - col))
            e_pair = jnp.where(lo, ecols[0], ecols[1])
            w_pair = jnp.where(lo, wcols[0], wcols[1])
            y_pairs.append(y_pair + yoff[:, pr * LANES:(pr + 1) * LANES] * e_pair)
            xw_parts.append(xs_pair * w_pair)
        xw = jnp.concatenate(xw_parts, axis=1).astype(BF16)
        s_new = lax.dot_general(xw, bg, (((0,), (0,)), ((), ())), preferred_element_type=F32)
        for r in range(HEADS_PER_GROUP):
            h = g * HEADS_PER_GROUP + r
            dec = jnp.exp(acs[CHUNK - 1:CHUNK, h:h + 1])
            lo_r = h * SSD_HEAD_DIM
            state_ref[lo_r:lo_r + SSD_HEAD_DIM, :] = (
                state_ref[lo_r:lo_r + SSD_HEAD_DIM, :] * dec
                + s_new[r * SSD_HEAD_DIM:(r + 1) * SSD_HEAD_DIM, :])

    y = jnp.concatenate(y_pairs, axis=1)
    y = y + dskip_ref[...] * xs
    z = z_ref[...]
    y = y * (z * _sigmoid(z))
    gdim = D_SSD // SSD_GROUPS
    for g in range(SSD_GROUPS):
        yg = _rms(y[:, g * gdim:(g + 1) * gdim], ssdn_ref[:, g * gdim:(g + 1) * gdim])
        out_ref[:, g * gdim:(g + 1) * gdim] = yg.astype(out_ref.dtype)

    u_cur = u_ref[...]
    u_prev = pu_ref[...]
    pos = (c * CHUNK + lax.broadcasted_iota(jnp.int32, (CHUNK, POOL_GROUP_DIM), 0) + 1).astype(F32)
    for g, win in enumerate(POOL_WINDOWS):
        sl = slice(g * POOL_GROUP_DIM, (g + 1) * POOL_GROUP_DIM)
        ug = u_cur[:, sl]
        ext = jnp.concatenate([u_prev[:, sl], ug], axis=0)
        step = 1
        while step < win:
            ext = ext + pltpu.roll(ext, step, 0)
            step *= 2
        pooled = ext[CHUNK:, :] / jnp.minimum(pos, float(win)) - ug
        mixed = jnp.dot(pooled.astype(BF16), poolw_ref[g].astype(BF16), preferred_element_type=F32)
        out_ref[:, D_SSD + g * POOL_GROUP_DIM:D_SSD + (g + 1) * POOL_GROUP_DIM] = (
            mixed * pscale_ref[:, sl]).astype(out_ref.dtype)
    pu_ref[...] = u_cur


def _mixer(proj, proj_u, dtp, conv_w, conv_b, dt_bias, a_log, d_skip, ssd_norm, pool_w, pool_scale, batch, seq):
    nchunk = seq // CHUNK
    row_map = lambda col: (lambda b, c: (b * nchunk + c, col))
    const2 = lambda b, c: (0, 0)
    pad = LANES - SSD_HEADS
    dtb = jnp.pad(dt_bias, (0, pad)).reshape(1, LANES)
    alog = jnp.pad(a_log, (0, pad)).reshape(1, LANES)
    dskip = jnp.repeat(d_skip, SSD_HEAD_DIM).reshape(1, D_SSD)
    return pl.pallas_call(
        _mixer_kernel,
        grid=(batch, nchunk),
        in_specs=[
            pl.BlockSpec((CHUNK, D_SSD), row_map(0)),
            pl.BlockSpec((CHUNK, D_SSD), row_map(1)),
            pl.BlockSpec((CHUNK, D_SSD), row_map(2)),
            pl.BlockSpec((CHUNK, D_POOL), row_map(0)),
            pl.BlockSpec((CHUNK, LANES), row_map(0)),
            pl.BlockSpec((CONV_WIDTH, D_CONV), const2),
            pl.BlockSpec((1, D_CONV), const2),
            pl.BlockSpec((1, LANES), const2),
            pl.BlockSpec((1, LANES), const2),
            pl.BlockSpec((1, D_SSD), const2),
            pl.BlockSpec((1, D_SSD), const2),
            pl.BlockSpec((len(POOL_WINDOWS), POOL_GROUP_DIM, POOL_GROUP_DIM), lambda b, c: (0, 0, 0)),
            pl.BlockSpec((1, D_POOL), const2),
        ],
        out_specs=pl.BlockSpec((CHUNK, D_MODEL), lambda b, c: (b * nchunk + c, 0)),
        out_shape=jax.ShapeDtypeStruct((batch * seq, D_MODEL), BF16),
        scratch_shapes=[pltpu.VMEM((CHUNK, D_SSD), F32), pltpu.VMEM((CHUNK, D_SSD), F32),
                        pltpu.VMEM((CHUNK, D_POOL), F32), pltpu.VMEM((SSD_HEADS * SSD_HEAD_DIM, D_STATE), F32)],
        compiler_params=_cparams(("arbitrary", "arbitrary")),
    )(proj, proj, proj, proj_u, dtp, conv_w, conv_b.reshape(1, D_CONV), dtb, alog, dskip,
      ssd_norm.reshape(1, D_SSD), pool_w, pool_scale.reshape(1, D_POOL))


def _attn_kernel(q_ref, k_ref, v_ref, o_ref):
    scale = XATTN_HEAD_DIM ** -0.5
    for h in range(XATTN_HEADS):
        sl = slice(h * XATTN_HEAD_DIM, (h + 1) * XATTN_HEAD_DIM)
        s = lax.dot_general(q_ref[:, sl], k_ref[:, sl], (((1,), (1,)), ((), ())),
                            preferred_element_type=F32) * scale
        s = s - jnp.max(s, axis=-1, keepdims=True)
        p = jnp.exp(s)
        p = p / jnp.sum(p, axis=-1, keepdims=True)
        o_ref[:, sl] = jnp.dot(p.astype(BF16), v_ref[:, sl], preferred_element_type=F32).astype(o_ref.dtype)


def _attn(q, kv, batch, seq, tq=512):
    nq = seq // tq
    return pl.pallas_call(
        _attn_kernel,
        grid=(batch, nq),
        in_specs=[pl.BlockSpec((tq, D_MODEL), lambda b, i: (b * nq + i, 0)),
                  pl.BlockSpec((N_MEM, D_MODEL), lambda b, i: (b, 0)),
                  pl.BlockSpec((N_MEM, D_MODEL), lambda b, i: (b, 1))],
        out_specs=pl.BlockSpec((tq, D_MODEL), lambda b, i: (b * nq + i, 0)),
        out_shape=jax.ShapeDtypeStruct((batch * seq, D_MODEL), BF16),
        compiler_params=_cparams(("arbitrary", "arbitrary")),
    )(q, kv, kv)


def _router_kernel(h_ref, nw_ref, wr_ref, br_ref, idx_ref, gate_ref, pos_ref, cnt_ref, run_ref):
    @pl.when(pl.program_id(0) == 0)
    def _():
        run_ref[...] = jnp.zeros_like(run_ref)

    hn = _rms(h_ref[...], nw_ref[...])
    w = wr_ref[...]
    h_hi = hn.astype(BF16)
    h_lo = (hn - h_hi.astype(F32)).astype(BF16)
    w_hi = w.astype(BF16)
    w_lo = (w - w_hi.astype(F32)).astype(BF16)
    logits = (jnp.dot(h_hi, w_hi, preferred_element_type=F32)
              + (jnp.dot(h_hi, w_lo, preferred_element_type=F32)
                 + jnp.dot(h_lo, w_hi, preferred_element_type=F32))) + br_ref[...]
    tm = logits.shape[0]
    lane = lax.broadcasted_iota(jnp.int32, logits.shape, 1)
    vals = []
    hots = []
    idx_out = jnp.zeros(logits.shape, jnp.int32)
    for k in range(TOP_K):
        m = jnp.max(logits, axis=-1, keepdims=True)
        idx = jnp.min(jnp.where(logits == m, lane, LANES), axis=-1, keepdims=True)
        hit = lane == idx
        idx_out = jnp.where(lane == k, idx, idx_out)
        logits = jnp.where(hit, -jnp.inf, logits)
        vals.append(m)
        hots.append(jnp.where(hit, 1.0, 0.0))
    exps = [jnp.exp(v - vals[0]) for v in vals]
    den = exps[0] + exps[1] + exps[2] + exps[3]
    gate_out = jnp.zeros(logits.shape, F32)
    for k in range(TOP_K):
        gate_out = jnp.where(lane == k, exps[k] / den, gate_out)
    idx_ref[...] = idx_out
    gate_ref[...] = gate_out

    hot = hots[0] + hots[1] + hots[2] + hots[3]
    ri = lax.broadcasted_iota(jnp.int32, (tm, tm), 0)
    ci = lax.broadcasted_iota(jnp.int32, (tm, tm), 1)
    before = jnp.where(ri > ci, 1.0, 0.0).astype(BF16)
    ahead = jnp.dot(before, hot.astype(BF16), preferred_element_type=F32) + run_ref[...]
    pos_out = jnp.zeros(logits.shape, F32)
    for k in range(TOP_K):
        pos_out = jnp.where(lane == k, jnp.sum(ahead * hots[k], axis=-1, keepdims=True), pos_out)
    pos_ref[...] = pos_out
    run_ref[...] = run_ref[...] + jnp.sum(hot, axis=0, keepdims=True)
    cnt_ref[...] = run_ref[...]


def _router(h, norm_w, w_router, b_router):
    t = h.shape[0]
    tm = ROUTER_TOKENS
    pad = LANES - N_EXPERTS
    wr = jnp.pad(w_router, ((0, 0), (0, pad)))
    br = jnp.pad(b_router, (0, pad), constant_values=NEG_BIG).reshape(1, LANES)
    tok_spec = pl.BlockSpec((tm, LANES), lambda i: (i, 0))
    return pl.pallas_call(
        _router_kernel,
        grid=(t // tm,),
        in_specs=[pl.BlockSpec((tm, D_MODEL), lambda i: (i, 0)),
                  pl.BlockSpec((1, D_MODEL), lambda i: (0, 0)),
                  pl.BlockSpec((D_MODEL, LANES), lambda i: (0, 0)),
                  pl.BlockSpec((1, LANES), lambda i: (0, 0))],
        out_specs=[tok_spec, tok_spec, tok_spec, pl.BlockSpec((1, LANES), lambda i: (0, 0))],
        out_shape=[jax.ShapeDtypeStruct((t, LANES), jnp.int32),
                   jax.ShapeDtypeStruct((t, LANES), F32),
                   jax.ShapeDtypeStruct((t, LANES), F32),
                   jax.ShapeDtypeStruct((1, LANES), F32)],
        scratch_shapes=[pltpu.VMEM((1, LANES), F32)],
        compiler_params=_cparams(("arbitrary",)),
    )(h, norm_w.reshape(1, D_MODEL), wr, br)


def _dest_kernel(idx_ref, pos_ref, cnt_ref, dest_ref, bounds_ref):
    cnt = jnp.broadcast_to(cnt_ref[...], (SUBLANES, LANES))
    padded = jnp.floor((cnt + (MOE_BLOCK - 1)) * (1.0 / MOE_BLOCK)) * MOE_BLOCK
    ri = lax.broadcasted_iota(jnp.int32, (LANES, LANES), 0)
    ci = lax.broadcasted_iota(jnp.int32, (LANES, LANES), 1)
    upto = jnp.where(ri <= ci, 1.0, 0.0).astype(F32)
    seg_end = jnp.dot(padded, upto, preferred_element_type=F32, precision=lax.Precision.HIGHEST)
    seg_start = seg_end - padded
    blocks = padded * (1.0 / MOE_BLOCK)
    full = jnp.floor(blocks * (1.0 * MOE_BLOCK / MOE_ITEM))
    rem = blocks - full * (MOE_ITEM // MOE_BLOCK)
    mid = jnp.floor(rem * 0.5)
    items = full + mid + (rem - 2.0 * mid)
    items_end = jnp.dot(items, upto, preferred_element_type=F32, precision=lax.Precision.HIGHEST)
    row = lax.broadcasted_iota(jnp.int32, (SUBLANES, LANES), 0)
    table = jnp.zeros((SUBLANES, LANES), F32)
    for r, v in enumerate((seg_start, seg_end, items_end, items, full, mid)):
        table = jnp.where(row == r, v, table)
    bounds_ref[...] = table.astype(jnp.int32)

    idx = idx_ref[...]
    lane = lax.broadcasted_iota(jnp.int32, idx.shape, 1)
    out = pos_ref[...]
    for k in range(TOP_K):
        hit = lane == idx[:, k:k + 1]
        base = jnp.sum(jnp.where(hit, seg_start[0:1, :], 0.0), axis=-1, keepdims=True)
        out = jnp.where(lane == k, out + base, out)
    dest_ref[...] = out.T[0:SUBLANES, :].astype(jnp.int32)


def _dest_rows(idx_pad, pos_pad, cnt_pad):
    t = idx_pad.shape[0]
    tm = ROUTER_TOKENS
    tok_spec = pl.BlockSpec((tm, LANES), lambda i: (i, 0))
    return pl.pallas_call(
        _dest_kernel,
        grid=(t // tm,),
        in_specs=[tok_spec, tok_spec, pl.BlockSpec((1, LANES), lambda i: (0, 0))],
        out_specs=[pl.BlockSpec((SUBLANES, tm), lambda i: (0, i)),
                   pl.BlockSpec((SUBLANES, LANES), lambda i: (0, 0))],
        out_shape=[jax.ShapeDtypeStruct((SUBLANES, t), jnp.int32),
                   jax.ShapeDtypeStruct((SUBLANES, LANES), jnp.int32)],
        compiler_params=_cparams(("arbitrary",)),
    )(idx_pad, pos_pad, cnt_pad)


def _pack_halves(x):
    c = x.shape[1] // 2
    lo = pltpu.bitcast(x[:, :c].astype(BF16).astype(F32), jnp.uint32)
    hi = pltpu.bitcast(x[:, c:].astype(BF16).astype(F32), jnp.uint32)
    return (lo >> 16) | hi


def _unpack_halves(w):
    lo = pltpu.bitcast(w << 16, F32)
    hi = pltpu.bitcast(w & jnp.uint32(0xFFFF0000), F32)
    return jnp.concatenate([lo, hi], axis=1).astype(BF16)


def _scatter_kernel(dest_ref, bounds_ref, h_ref, nw_ref, x_ref, zero_ref, packed_ref, zsem, sem):
    n_blocks = x_ref.shape[0] // MOE_BLOCK
    i = pl.program_id(0)
    slot = i % 2
    packed_ref[slot] = _pack_halves(_rms(h_ref[...], nw_ref[...]))

    @pl.when(i == 0)
    def _():
        zero_ref[...] = jnp.zeros_like(zero_ref)
        n_used = bounds_ref[1, N_EXPERTS - 1] // MOE_BLOCK

        def zero_copy(b):
            return pltpu.make_async_copy(zero_ref, x_ref.at[pl.ds(pl.multiple_of(b * MOE_BLOCK, MOE_BLOCK),
                                                                   MOE_BLOCK)], zsem)

        def last_block(e):
            return bounds_ref[1, e] // MOE_BLOCK - 1, bounds_ref[1, e] > bounds_ref[0, e]

        def start_e(e, carry):
            b, nonempty = last_block(e)

            @pl.when(nonempty)
            def _():
                zero_copy(b).start()
            return carry

        def wait_e(e, carry):
            b, nonempty = last_block(e)

            @pl.when(nonempty)
            def _():
                zero_copy(b).wait()
            return carry

        lax.fori_loop(0, N_EXPERTS, start_e, 0)
        lax.fori_loop(n_used, n_blocks, lambda b, c: (zero_copy(b).start(), c)[1], 0)
        lax.fori_loop(0, N_EXPERTS, wait_e, 0)
        lax.fori_loop(n_used, n_blocks, lambda b, c: (zero_copy(b).wait(), c)[1], 0)

    def issue(r, carry):
        for k in range(TOP_K):
            d = dest_ref[k, r]
            pltpu.make_async_copy(packed_ref.at[slot, pl.ds(r, 1)], x_ref.at[pl.ds(d, 1)],
                                  sem.at[slot]).start(priority=k % 2)
        return carry

    lax.fori_loop(0, SCATTER_TOKENS, issue, 0)

    def wait_rows(s):
        n_sent = SCATTER_TOKENS * TOP_K
        pltpu.make_async_copy(x_ref.at[pl.ds(0, n_sent)], x_ref.at[pl.ds(0, n_sent)], sem.at[s]).wait()

    @pl.when(i > 0)
    def _():
        wait_rows(1 - slot)

    @pl.when(i == pl.num_programs(0) - 1)
    def _():
        wait_rows(slot)


def _scatter_rows(h, norm_w, dest_t, bounds, n_rows):
    t = h.shape[0]
    steps = t // SCATTER_TOKENS
    half = D_MODEL // 2
    return pl.pallas_call(
        _scatter_kernel,
        grid=(steps,),
        in_specs=[pl.BlockSpec((SUBLANES, SCATTER_TOKENS), lambda i: (0, i), memory_space=pltpu.SMEM),
                  pl.BlockSpec(memory_space=pltpu.SMEM),
                  pl.BlockSpec((SCATTER_TOKENS, D_MODEL), lambda i: (i, 0)),
                  pl.BlockSpec((1, D_MODEL), lambda i: (0, 0))],
        out_specs=pl.BlockSpec(memory_space=pl.ANY),
        out_shape=jax.ShapeDtypeStruct((n_rows, half), jnp.uint32),
        scratch_shapes=[pltpu.VMEM((MOE_BLOCK, half), jnp.uint32), pltpu.VMEM((2, SCATTER_TOKENS, half), jnp.uint32),
                        pltpu.SemaphoreType.DMA(()), pltpu.SemaphoreType.DMA((2,))],
        compiler_params=_cparams(("arbitrary",)),
    )(dest_t, bounds, h, norm_w.reshape(1, D_MODEL))


def _item_tables(bounds, n_j, n_items_max):
    items_e = bounds[3]
    first_e = (bounds[2] - items_e) * n_j
    end_e = bounds[2] * n_j
    n_items = end_e[N_EXPERTS - 1]
    g = jnp.minimum(jnp.arange(n_items_max, dtype=jnp.int32), n_items - 1)
    e = jnp.sum((g[:, None] >= end_e[None, :N_EXPERTS]).astype(jnp.int32), axis=1)
    own = e[:, None] == jnp.arange(LANES, dtype=jnp.int32)[None, :]
    pick = lambda v: jnp.sum(jnp.where(own, v[None, :], 0), axis=1)
    first = pick(first_e)
    nb = jnp.maximum(pick(items_e), 1)
    q = g - first
    j = q // nb
    i = q % nb
    full = pick(bounds[4])
    mid = pick(bounds[5])
    code = jnp.where(i < full, 0, jnp.where((i == full) & (mid > 0), 1, 2))
    per_full = MOE_ITEM // MOE_BLOCK
    offset = jnp.where(code == 0, i * per_full, full * per_full + jnp.where(code == 2, mid * (per_full // 2), 0))
    row = pick(bounds[0] // MOE_BLOCK) + offset
    meta = e * 32 + code * 8 + j * 2 + (i == 0).astype(jnp.int32)
    group_end = first + (j + 1) * nb
    counts = jnp.stack([n_items, bounds[1, N_EXPERTS - 1] // MOE_BLOCK])
    return meta.astype(jnp.int32), row.astype(jnp.int32), group_end.astype(jnp.int32), counts.astype(jnp.int32)


def _meta_expert(m):
    return m >> 5


def _meta_tile(m):
    return (m >> 1) & 3


def _item_loop(meta_ref, row_ref, gend_ref, n_ref, src_ref, dst_ref, w_refs, ring, ring_sem, wb_ref,
               in_buf, out_buf, in_sem, out_sem, compute):
    n = n_ref[0]
    n_in = in_buf.shape[0]
    n_out = out_buf.shape[0]
    tn = out_buf.shape[2]
    n_j = dst_ref.shape[1] // tn
    n_mats = len(w_refs)
    cpm = W_CHUNKS // n_mats
    kc = w_refs[0].shape[1] // cpm

    def by_size(m, fn):
        for code, size in enumerate(MOE_ITEM_SIZES):
            @pl.when(((m >> 3) & 3) == code)
            def _(size=size):
                fn(size)

    def rows(g, size):
        return pl.ds(pl.multiple_of(row_ref[g] * MOE_BLOCK, MOE_BLOCK), size)

    def cols(m):
        return pl.ds(pl.multiple_of(_meta_tile(m) * tn, tn), tn)

    def in_copy(g, size):
        slot = g % n_in
        return pltpu.make_async_copy(src_ref.at[rows(g, size)], in_buf.at[slot, pl.ds(0, size)], in_sem.at[slot])

    def out_copy(g, size):
        slot = g % n_out
        return pltpu.make_async_copy(out_buf.at[slot, pl.ds(0, size)], dst_ref.at[rows(g, size), cols(meta_ref[g])],
                                     out_sem.at[slot])

    def start_in(g):
        by_size(meta_ref[g], lambda size: in_copy(g, size).start(priority=ROW_DMA_PRIORITY))

    def wait_in(g):
        by_size(meta_ref[g], lambda size: in_copy(g, size).wait())

    def start_out(g):
        by_size(meta_ref[g], lambda size: out_copy(g, size).start(priority=ROW_DMA_PRIORITY))

    def wait_out(g):
        by_size(meta_ref[g], lambda size: out_copy(g, size).wait())

    def krows(c):
        r0 = (c % cpm) * kc
        return pl.ds(r0 if isinstance(c, int) else pl.multiple_of(r0, kc), kc)

    def chunk_copy(m, c, k):
        slot = c % W_RING
        return pltpu.make_async_copy(w_refs[k].at[_meta_expert(m), krows(c), cols(m)], ring.at[slot],
                                     ring_sem.at[slot])

    def for_matrix(c, fn):
        if isinstance(c, int):
            return fn(c // cpm)
        for k in range(n_mats):
            @pl.when(c // cpm == k)
            def _(k=k):
                fn(k)

    def start_chunk(m, c):
        for_matrix(c, lambda k: chunk_copy(m, c, k).start())

    def finish_chunk(m, c, wslot):
        def fn(k):
            chunk_copy(m, c, k).wait()
            wb_ref[wslot, k, krows(c), :] = ring[c % W_RING].astype(BF16)
        for_matrix(c, fn)

        @pl.when(c + W_RING < W_CHUNKS)
        def _():
            start_chunk(m, c + W_RING)

    for c0 in range(W_RING):
        start_chunk(meta_ref[0], c0)
    for g0 in range(n_in - 1):
        @pl.when(g0 < n)
        def _():
            start_in(g0)

    def body(g, carry):
        parity, done, nm = carry
        m = meta_ref[g]
        first = (m & 1) == 1

        @pl.when(first)
        def _():
            lax.fori_loop(done, W_CHUNKS, lambda c, z: (finish_chunk(m, c, 1 - parity), z)[1], 0)
            nxt = gend_ref[g]

            @pl.when(nxt < n)
            def _():
                for c0 in range(W_RING):
                    start_chunk(meta_ref[nxt], c0)

        nxt = gend_ref[g]
        parity = jnp.where(first, 1 - parity, parity)
        done = jnp.where(first, 0, done)
        nm = jnp.where(first, jnp.where(nxt < n, meta_ref[jnp.minimum(nxt, n - 1)], -1), nm)

        wait_in(g)

        @pl.when(g + n_in - 1 < n)
        def _():
            start_in(g + n_in - 1)

        @pl.when(g >= n_out)
        def _():
            wait_out(g - n_out)

        def run(size):
            x = in_buf[g % n_in, pl.ds(0, size)]
            out_buf[g % n_out, pl.ds(0, size)] = compute(x, wb_ref.at[parity], m).astype(out_buf.dtype)

        by_size(m, run)
        start_out(g)

        active = (nm >= 0) & (done < W_CHUNKS)

        @pl.when(active)
        def _():
            finish_chunk(nm, done, 1 - parity)

        return parity, jnp.where(active, done + 1, done), nm

    lax.fori_loop(0, n, body, (jnp.int32(1), jnp.int32(0), meta_ref[0]))
    lax.fori_loop(jnp.maximum(n - n_out, 0), n, lambda g, c: (wait_out(g), c)[1], 0)

    n_blocks = dst_ref.shape[0] // MOE_BLOCK
    n_used = n_ref[1]
    out_buf[0] = jnp.zeros(out_buf.shape[1:], out_buf.dtype)
    for jt in range(n_j):
        def tail_copy(b, jt=jt):
            r = pl.ds(pl.multiple_of(b * MOE_BLOCK, MOE_BLOCK), MOE_BLOCK)
            return pltpu.make_async_copy(out_buf.at[0, pl.ds(0, MOE_BLOCK)], dst_ref.at[r, jt * tn:(jt + 1) * tn],
                                         out_sem.at[0])

        lax.fori_loop(n_used, n_blocks, lambda b, c: (tail_copy(b).start(), c)[1], 0)
        lax.fori_loop(n_used, n_blocks, lambda b, c: (tail_copy(b).wait(), c)[1], 0)


def _expert_up_kernel(meta_ref, row_ref, gend_ref, n_ref, x_ref, bg_ref, bu_ref, wg_ref, wu_ref, act_ref,
                      ring, wb_ref, x_buf, a_buf, ring_sem, in_sem, out_sem):
    n_j = D_FF // MOE_UP_TN

    def compute(x, w, m):
        tile = _meta_expert(m) * n_j + _meta_tile(m)
        xb = _unpack_halves(x)
        g = jnp.dot(xb, w[0], preferred_element_type=F32) + bg_ref[pl.ds(tile, 1), :]
        u = jnp.dot(xb, w[1], preferred_element_type=F32) + bu_ref[pl.ds(tile, 1), :]
        g = jnp.minimum(g, SWIGLU_LIMIT)
        u = jnp.clip(u, -SWIGLU_LIMIT, SWIGLU_LIMIT)
        return g * _sigmoid(SWIGLU_ALPHA * g) * (u + 1.0)

    _item_loop(meta_ref, row_ref, gend_ref, n_ref, x_ref, act_ref, (wg_ref, wu_ref), ring, ring_sem, wb_ref,
               x_buf, a_buf, in_sem, out_sem, compute)


def _expert_down_kernel(meta_ref, row_ref, gend_ref, n_ref, a_ref, bd_ref, wd_ref, y_ref,
                        ring, wb_ref, a_buf, y_buf, ring_sem, in_sem, out_sem):
    n_j = D_MODEL // MOE_DOWN_TN

    def compute(a, w, m):
        tile = _meta_expert(m) * n_j + _meta_tile(m)
        return jnp.dot(a, w[0], preferred_element_type=F32) + bd_ref[pl.ds(tile, 1), :]

    _item_loop(meta_ref, row_ref, gend_ref, n_ref, a_ref, y_ref, (wd_ref,), ring, ring_sem, wb_ref,
               a_buf, y_buf, in_sem, out_sem, compute)


def _experts(x_rows, bounds, w_gate, b_gate, w_up, b_up, w_down, b_down):
    n_rows = x_rows.shape[0]
    n_items_max = n_rows // MOE_ITEM + (len(MOE_ITEM_SIZES) - 1) * N_EXPERTS
    any_spec = pl.BlockSpec(memory_space=pl.ANY)
    sems = [pltpu.SemaphoreType.DMA((W_RING,)), pltpu.SemaphoreType.DMA((MOE_IN_BUFS,)),
            pltpu.SemaphoreType.DMA((MOE_OUT_BUFS,))]

    def full(shape):
        return pl.BlockSpec(shape, lambda i, *_: (0,) * len(shape))

    n_j = D_FF // MOE_UP_TN
    act = pl.pallas_call(
        _expert_up_kernel,
        grid_spec=pltpu.PrefetchScalarGridSpec(
            num_scalar_prefetch=4, grid=(1,),
            in_specs=[any_spec, full((N_EXPERTS * n_j, MOE_UP_TN)),
                      full((N_EXPERTS * n_j, MOE_UP_TN)), any_spec, any_spec],
            out_specs=any_spec,
            scratch_shapes=[pltpu.VMEM((W_RING, 2 * D_MODEL // W_CHUNKS, MOE_UP_TN), F32),
                            pltpu.VMEM((2, 2, D_MODEL, MOE_UP_TN), BF16),
                            pltpu.VMEM((MOE_IN_BUFS, MOE_ITEM, D_MODEL // 2), jnp.uint32),
                            pltpu.VMEM((MOE_OUT_BUFS, MOE_ITEM, MOE_UP_TN), BF16)] + sems),
        out_shape=jax.ShapeDtypeStruct((n_rows, D_FF), BF16),
        compiler_params=_cparams(("arbitrary",)),
    )(*_item_tables(bounds, n_j, n_items_max * n_j), x_rows,
      b_gate.reshape(N_EXPERTS * n_j, MOE_UP_TN), b_up.reshape(N_EXPERTS * n_j, MOE_UP_TN), w_gate, w_up)

    n_j = D_MODEL // MOE_DOWN_TN
    return pl.pallas_call(
        _expert_down_kernel,
        grid_spec=pltpu.PrefetchScalarGridSpec(
            num_scalar_prefetch=4, grid=(1,),
            in_specs=[any_spec, full((N_EXPERTS * n_j, MOE_DOWN_TN)), any_spec],
            out_specs=any_spec,
            scratch_shapes=[pltpu.VMEM((W_RING, D_FF // W_CHUNKS, MOE_DOWN_TN), F32),
                            pltpu.VMEM((2, 1, D_FF, MOE_DOWN_TN), BF16),
                            pltpu.VMEM((MOE_IN_BUFS, MOE_ITEM, D_FF), BF16),
                            pltpu.VMEM((MOE_OUT_BUFS, MOE_ITEM, MOE_DOWN_TN), F32)] + sems),
        out_shape=jax.ShapeDtypeStruct((n_rows, D_MODEL), F32),
        compiler_params=_cparams(("arbitrary",)),
    )(*_item_tables(bounds, n_j, n_items_max * n_j), act,
      b_down.reshape(N_EXPERTS * n_j, MOE_DOWN_TN), w_down)


def _combine_kernel(dest_ref, next_ref, h_ref, gate_ref, nw_ref, y_ref, o_ref, buf_ref, sem):
    i = pl.program_id(0)
    steps = pl.num_programs(0)

    def issue(d_ref, slot):
        def body(r, carry):
            for k in range(TOP_K):
                pltpu.make_async_copy(y_ref.at[pl.ds(d_ref[k, r], 1)], buf_ref.at[slot, k, pl.ds(r, 1)],
                                      sem.at[slot]).start(priority=k % 2)
            return carry
        lax.fori_loop(0, COMBINE_TOKENS, body, 0)

    @pl.when(i == 0)
    def _():
        issue(dest_ref, 0)

    @pl.when(i + 1 < steps)
    def _():
        issue(next_ref, (i + 1) % 2)

    slot = i % 2
    pltpu.make_async_copy(buf_ref.at[slot], buf_ref.at[slot], sem.at[slot]).wait()
    acc = h_ref[...]
    gate = gate_ref[...]
    for k in range(TOP_K):
        acc = acc + gate[:, k:k + 1] * buf_ref[slot, k]
    o_ref[...] = _rms(acc, nw_ref[...])


def _combine(h, y_rows, dest_t, gate_pad, norm_w):
    t = h.shape[0]
    steps = t // COMBINE_TOKENS
    return pl.pallas_call(
        _combine_kernel,
        grid=(steps,),
        in_specs=[pl.BlockSpec((SUBLANES, COMBINE_TOKENS), lambda i: (0, i), memory_space=pltpu.SMEM),
                  pl.BlockSpec((SUBLANES, COMBINE_TOKENS), lambda i: (0, jnp.minimum(i + 1, steps - 1)),
                               memory_space=pltpu.SMEM),
                  pl.BlockSpec((COMBINE_TOKENS, D_MODEL), lambda i: (i, 0)),
                  pl.BlockSpec((COMBINE_TOKENS, LANES), lambda i: (i, 0)),
                  pl.BlockSpec((1, D_MODEL), lambda i: (0, 0)),
                  pl.BlockSpec(memory_space=pl.ANY)],
        out_specs=pl.BlockSpec((COMBINE_TOKENS, D_MODEL), lambda i: (i, 0)),
        out_shape=jax.ShapeDtypeStruct((t, D_MODEL), F32),
        scratch_shapes=[pltpu.VMEM((2, TOP_K, COMBINE_TOKENS, D_MODEL), F32), pltpu.SemaphoreType.DMA((2,))],
        compiler_params=_cparams(("arbitrary",)),
    )(dest_t, dest_t, h, gate_pad, norm_w.reshape(1, D_MODEL), y_rows)


def _moe(h, norm_ffn, w_router, b_router, w_gate, b_gate, w_up, b_up, w_down, b_down, norm_final):
    t = h.shape[0]
    tk = t * TOP_K
    n_blocks = (tk + N_EXPERTS * (MOE_BLOCK - 1) + MOE_BLOCK - 1) // MOE_BLOCK
    n_rows = n_blocks * MOE_BLOCK
    idx_pad, gate_pad, pos_pad, cnt_pad = _router(h, norm_ffn, w_router, b_router)
    dest_t, bounds = _dest_rows(idx_pad, pos_pad, cnt_pad)
    x_rows = _scatter_rows(h, norm_ffn, dest_t, bounds, n_rows)
    y_rows = _experts(x_rows, bounds, w_gate, b_gate, w_up, b_up, w_down, b_down)
    return _combine(h, y_rows, dest_t, gate_pad, norm_final)


def kernel(x, mem, norm_mix, w_in, conv_w, conv_b, dt_bias, a_log, d_skip, ssd_norm, pool_w, pool_scale,
           w_out, norm_xattn, norm_mem, w_q, w_kv, w_o, norm_ffn, w_router, b_router, w_gate, b_gate,
           w_up, b_up, w_down, b_down, norm_final):
    batch, seq, d = x.shape
    t = batch * seq
    h = x.reshape(t, d)
    depth = norm_mix.shape[0]
    assert depth == 1, "the final norm is fused into the MoE combine of the single layer"
    for l in range(depth):
        dt_lo = D_SSD + D_CONV
        w_dt = jnp.pad(w_in[l][:, dt_lo:dt_lo + SSD_HEADS], ((0, 0), (0, LANES - SSD_HEADS)))
        proj = _mm(h, w_in, norm_w=norm_mix[l], n_cols=dt_lo, layer=l)
        proj_u = _mm(h, w_in[l][:, dt_lo + SSD_HEADS:], norm_w=norm_mix[l])
        dtp = _mm(h, w_dt, norm_w=norm_mix[l])
        mix = _mixer(proj, proj_u, dtp, conv_w[l], conv_b[l], dt_bias[l], a_log[l], d_skip[l], ssd_norm[l],
                     pool_w[l], pool_scale[l], batch, seq)
        h = _mm(mix, w_out[l], res=h)

        q = _mm(h, w_q[l], norm_w=norm_xattn[l], out_dtype=BF16)
        kv = _mm(mem.reshape(batch * N_MEM, d), w_kv[l], norm_w=norm_mem[l], out_dtype=BF16)
        o = _attn(q, kv, batch, seq)
        h = _mm(o, w_o[l], res=h)

        out = _moe(h, norm_ffn[l], w_router[l], b_router[l], w_gate[l], b_gate[l], w_up[l], b_up[l],
                   w_down[l], b_down[l], norm_final)
    return out.reshape(batch, seq, d)
```

```python
import functools

import jax
import jax.numpy as jnp
from jax import lax
from jax.experimental import pallas as pl
from jax.experimental.pallas import tpu as pltpu

F32 = jnp.float32
BF16 = jnp.bfloat16

D_MODEL = 2048
D_SSD = 1024
SSD_HEAD_DIM = 64
SSD_HEADS = 16
SSD_GROUPS = 4
HEADS_PER_GROUP = SSD_HEADS // SSD_GROUPS
D_STATE = 128
CONV_WIDTH = 4
CHUNK = 128
D_CONV = D_SSD + 2 * SSD_GROUPS * D_STATE
D_POOL = 1024
POOL_WINDOWS = (2, 4, 8, 16)
POOL_GROUP_DIM = 256
N_MEM = 256
XATTN_HEADS = 4
XATTN_HEAD_DIM = 512
N_EXPERTS = 32
TOP_K = 4
D_FF = 2048
SWIGLU_LIMIT = 7.0
SWIGLU_ALPHA = 1.702
EPS = 1e-5

LANES = 128
SUBLANES = 8
VMEM_LIMIT = 56 * 1024 * 1024
NEG_BIG = -1e30

MOE_BLOCK = 128
MOE_ITEM_SIZES = (4 * MOE_BLOCK, 2 * MOE_BLOCK, MOE_BLOCK)
MOE_ITEM = MOE_ITEM_SIZES[0]
MOE_UP_TN = 512
MOE_DOWN_TN = 1024
MOE_IN_BUFS = 3
MOE_OUT_BUFS = 2
W_CHUNKS = 4
W_RING = 4
ROUTER_TOKENS = 512
SCATTER_TOKENS = 512
COMBINE_TOKENS = 256
ROW_DMA_PRIORITY = 1


def _cparams(sem):
    return pltpu.CompilerParams(dimension_semantics=sem, vmem_limit_bytes=VMEM_LIMIT)


def _rms(x, w):
    ms = jnp.mean(x * x, axis=-1, keepdims=True)
    return x * lax.rsqrt(ms + EPS) * w


def _mm_kernel(*refs, has_norm, has_res, w_transposed):
    it = iter(refs)
    a_ref = next(it)
    w_ref = next(it)
    nw_ref = next(it) if has_norm else None
    r_ref = next(it) if has_res else None
    o_ref = next(it)
    wb_ref = next(it)

    @pl.when(pl.program_id(1) == 0)
    def _():
        w = w_ref[...]
        wb_ref[...] = (w.T if w_transposed else w).astype(BF16)

    a = a_ref[...]
    if has_norm:
        a = _rms(a.astype(F32), nw_ref[...])
    acc = jnp.dot(a.astype(BF16), wb_ref[...], preferred_element_type=F32)
    if has_res:
        acc = acc + r_ref[...]
    o_ref[...] = acc.astype(o_ref.dtype)


def _mm(a, w, norm_w=None, res=None, out_dtype=F32, tm=1024, tn=1024, n_cols=None, w_transposed=False):
    m, k = a.shape
    n = w.shape[0 if w_transposed else 1] if n_cols is None else n_cols
    tn = min(tn, n)
    tm = min(tm, m)
    assert m % tm == 0 and n % tn == 0
    w_spec = (pl.BlockSpec((tn, k), lambda j, i: (j, 0)) if w_transposed
              else pl.BlockSpec((k, tn), lambda j, i: (0, j)))
    in_specs = [pl.BlockSpec((tm, k), lambda j, i: (i, 0)), w_spec]
    args = [a, w]
    if norm_w is not None:
        in_specs.append(pl.BlockSpec((1, k), lambda j, i: (0, 0)))
        args.append(norm_w.reshape(1, k))
    if res is not None:
        in_specs.append(pl.BlockSpec((tm, tn), lambda j, i: (i, j)))
        args.append(res)
    return pl.pallas_call(
        functools.partial(_mm_kernel, has_norm=norm_w is not None, has_res=res is not None,
                          w_transposed=w_transposed),
        grid=(n // tn, m // tm),
        in_specs=in_specs,
        out_specs=pl.BlockSpec((tm, tn), lambda j, i: (i, j)),
        out_shape=jax.ShapeDtypeStruct((m, n), out_dtype),
        scratch_shapes=[pltpu.VMEM((k, tn), BF16)],
        compiler_params=_cparams(("arbitrary", "arbitrary")),
    )(*args)


def _sigmoid(x):
    return 1.0 / (1.0 + jnp.exp(-x))


def _shift_rows(cur, prev, j, rows):
    return jnp.where(rows < j, pltpu.roll(prev, j, 0), pltpu.roll(cur, j, 0))


def _conv_silu(cur, prev, w_ref, b_ref, col0, rows):
    width = cur.shape[1]
    acc = cur * w_ref[CONV_WIDTH - 1:CONV_WIDTH, col0:col0 + width] + b_ref[:, col0:col0 + width]
    for j in range(1, CONV_WIDTH):
        k = CONV_WIDTH - 1 - j
        acc = acc + _shift_rows(cur, prev, j, rows) * w_ref[k:k + 1, col0:col0 + width]
    return acc * _sigmoid(acc)


def _mixer_kernel(z_ref, x_ref, bc_ref, u_ref, dt_ref, convw_ref, convb_ref, dtb_ref, alog_ref,
                  dskip_ref, ssdn_ref, poolw_ref, pscale_ref, out_ref,
                  px_ref, pbc_ref, pu_ref, state_ref):
    c = pl.program_id(1)

    @pl.when(c == 0)
    def _():
        px_ref[...] = jnp.zeros_like(px_ref)
        pbc_ref[...] = jnp.zeros_like(pbc_ref)
        pu_ref[...] = jnp.zeros_like(pu_ref)
        state_ref[...] = jnp.zeros_like(state_ref)

    rows = lax.broadcasted_iota(jnp.int32, (CHUNK, D_SSD), 0)
    x_raw = x_ref[...]
    bc_raw = bc_ref[...]
    xs = _conv_silu(x_raw, px_ref[...], convw_ref, convb_ref, 0, rows)
    bcs = _conv_silu(bc_raw, pbc_ref[...], convw_ref, convb_ref, D_SSD, rows)
    px_ref[...] = x_raw
    pbc_ref[...] = bc_raw

    dt_in = dt_ref[...] + dtb_ref[...]
    dt = jnp.maximum(dt_in, 0.0) + jnp.log(1.0 + jnp.exp(-jnp.abs(dt_in)))
    a = -jnp.exp(alog_ref[...])
    dta = dt * a
    ri = lax.broadcasted_iota(jnp.int32, (CHUNK, CHUNK), 0)
    ci = lax.broadcasted_iota(jnp.int32, (CHUNK, CHUNK), 1)
    causal = ri >= ci
    tril = jnp.where(causal, 1.0, 0.0).astype(F32)
    acs = jnp.dot(tril, dta, preferred_element_type=F32, precision=lax.Precision.HIGHEST)
    acs_t = acs.T
    dt_t = dt.T
    lo = ci < SSD_HEAD_DIM

    y_pairs = []
    for g in range(SSD_GROUPS):
        bg = bcs[:, g * D_STATE:(g + 1) * D_STATE].astype(BF16)
        cg = bcs[:, SSD_GROUPS * D_STATE + g * D_STATE:SSD_GROUPS * D_STATE + (g + 1) * D_STATE].astype(BF16)
        cb = lax.dot_general(cg, bg, (((1,), (1,)), ((), ())), preferred_element_type=F32)
        gw = HEADS_PER_GROUP * SSD_HEAD_DIM
        prev_g = state_ref[g * gw:(g + 1) * gw, :]
        yoff = lax.dot_general(cg, prev_g.astype(BF16), (((1,), (1,)), ((), ())),
                               preferred_element_type=F32)
        xw_parts = []
        for pr in range(HEADS_PER_GROUP // 2):
            pair = g * (HEADS_PER_GROUP // 2) + pr
            xs_pair = xs[:, pair * LANES:(pair + 1) * LANES]
            y_pair = None
            ecols = []
            wcols = []
            for q in range(2):
                h = pair * 2 + q
                col = acs[:, h:h + 1]
                seg = col - acs_t[h:h + 1, :]
                decay = jnp.exp(jnp.where(causal, seg, NEG_BIG))
                mat = (cb * decay * dt_t[h:h + 1, :]).astype(BF16)
                keep = lo if q == 0 else jnp.logical_not(lo)
                xh = jnp.where(keep, xs_pair, 0.0).astype(BF16)
                yd = jnp.dot(mat, xh, preferred_element_type=F32)
                y_pair = yd if y_pair is None else y_pair + yd
                ecols.append(jnp.exp(col))
                wcols.append(dt[:, h:h + 1] * jnp.exp(acs[CHUNK - 1:CHUNK, h:h + 1] - col))
            e_pair = jnp.where(lo, ecols[0], ecols[1])
            w_pair = jnp.where(lo, wcols[0], wcols[1])
            y_pairs.append(y_pair + yoff[:, pr * LANES:(pr + 1) * LANES] * e_pair)
            xw_parts.append(xs_pair * w_pair)
        xw = jnp.concatenate(xw_parts, axis=1).astype(BF16)
        s_new = lax.dot_general(xw, bg, (((0,), (0,)), ((), ())), preferred_element_type=F32)
        for r in range(HEADS_PER_GROUP):
            h = g * HEADS_PER_GROUP + r
            dec = jnp.exp(acs[CHUNK - 1:CHUNK, h:h + 1])
            lo_r = h * SSD_HEAD_DIM
            state_ref[lo_r:lo_r + SSD_HEAD_DIM, :] = (
                state_ref[lo_r:lo_r + SSD_HEAD_DIM, :] * dec
                + s_new[r * SSD_HEAD_DIM:(r + 1) * SSD_HEAD_DIM, :])

    y = jnp.concatenate(y_pairs, axis=1)
    y = y + dskip_ref[...] * xs
    z = z_ref[...]
    y = y * (z * _sigmoid(z))
    gdim = D_SSD // SSD_GROUPS
    for g in range(SSD_GROUPS):
        yg = _rms(y[:, g * gdim:(g + 1) * gdim], ssdn_ref[:, g * gdim:(g + 1) * gdim])
        out_ref[:, g * gdim:(g + 1) * gdim] = yg.astype(out_ref.dtype)

    u_cur = u_ref[...]
    u_prev = pu_ref[...]
    pos = (c * CHUNK + lax.broadcasted_iota(jnp.int32, (CHUNK, POOL_GROUP_DIM), 0) + 1).astype(F32)
    for g, win in enumerate(POOL_WINDOWS):
        sl = slice(g * POOL_GROUP_DIM, (g + 1) * POOL_GROUP_DIM)
        ug = u_cur[:, sl]
        ext = jnp.concatenate([u_prev[:, sl], ug], axis=0)
        step = 1
        while step < win:
            ext = ext + pltpu.roll(ext, step, 0)
            step *= 2
        pooled = ext[CHUNK:, :] / jnp.minimum(pos, float(win)) - ug
        mixed = jnp.dot(pooled.astype(BF16), poolw_ref[g].astype(BF16), preferred_element_type=F32)
        out_ref[:, D_SSD + g * POOL_GROUP_DIM:D_SSD + (g + 1) * POOL_GROUP_DIM] = (
            mixed * pscale_ref[:, sl]).astype(out_ref.dtype)
    pu_ref[...] = u_cur


def _mixer(proj, proj_u, dtp, conv_w, conv_b, dt_bias, a_log, d_skip, ssd_norm, pool_w, pool_scale, batch, seq):
    nchunk = seq // CHUNK
    row_map = lambda col: (lambda b, c: (b * nchunk + c, col))
    const2 = lambda b, c: (0, 0)
    pad = LANES - SSD_HEADS
    dtb = jnp.pad(dt_bias, (0, pad)).reshape(1, LANES)
    alog = jnp.pad(a_log, (0, pad)).reshape(1, LANES)
    dskip = jnp.repeat(d_skip, SSD_HEAD_DIM).reshape(1, D_SSD)
    return pl.pallas_call(
        _mixer_kernel,
        grid=(batch, nchunk),
        in_specs=[
            pl.BlockSpec((CHUNK, D_SSD), row_map(0)),
            pl.BlockSpec((CHUNK, D_SSD), row_map(1)),
            pl.BlockSpec((CHUNK, D_SSD), row_map(2)),
            pl.BlockSpec((CHUNK, D_POOL), row_map(0)),
            pl.BlockSpec((CHUNK, LANES), row_map(0)),
            pl.BlockSpec((CONV_WIDTH, D_CONV), const2),
            pl.BlockSpec((1, D_CONV), const2),
            pl.BlockSpec((1, LANES), const2),
            pl.BlockSpec((1, LANES), const2),
            pl.BlockSpec((1, D_SSD), const2),
            pl.BlockSpec((1, D_SSD), const2),
            pl.BlockSpec((len(POOL_WINDOWS), POOL_GROUP_DIM, POOL_GROUP_DIM), lambda b, c: (0, 0, 0)),
            pl.BlockSpec((1, D_POOL), const2),
        ],
        out_specs=pl.BlockSpec((CHUNK, D_MODEL), lambda b, c: (b * nchunk + c, 0)),
        out_shape=jax.ShapeDtypeStruct((batch * seq, D_MODEL), BF16),
        scratch_shapes=[pltpu.VMEM((CHUNK, D_SSD), F32), pltpu.VMEM((CHUNK, D_SSD), F32),
                        pltpu.VMEM((CHUNK, D_POOL), F32), pltpu.VMEM((SSD_HEADS * SSD_HEAD_DIM, D_STATE), F32)],
        compiler_params=_cparams(("arbitrary", "arbitrary")),
    )(proj, proj, proj, proj_u, dtp, conv_w, conv_b.reshape(1, D_CONV), dtb, alog, dskip,
      ssd_norm.reshape(1, D_SSD), pool_w, pool_scale.reshape(1, D_POOL))


def _attn_kernel(q_ref, k_ref, v_ref, o_ref):
    scale = XATTN_HEAD_DIM ** -0.5
    for h in range(XATTN_HEADS):
        sl = slice(h * XATTN_HEAD_DIM, (h + 1) * XATTN_HEAD_DIM)
        s = lax.dot_general(q_ref[:, sl], k_ref[:, sl], (((1,), (1,)), ((), ())),
                            preferred_element_type=F32) * scale
        s = s - jnp.max(s, axis=-1, keepdims=True)
        p = jnp.exp(s)
        p = p / jnp.sum(p, axis=-1, keepdims=True)
        o_ref[:, sl] = jnp.dot(p.astype(BF16), v_ref[:, sl], preferred_element_type=F32).astype(o_ref.dtype)


def _attn(q, kv, batch, seq, tq=512):
    nq = seq // tq
    return pl.pallas_call(
        _attn_kernel,
        grid=(batch, nq),
        in_specs=[pl.BlockSpec((tq, D_MODEL), lambda b, i: (b * nq + i, 0)),
                  pl.BlockSpec((N_MEM, D_MODEL), lambda b, i: (b, 0)),
                  pl.BlockSpec((N_MEM, D_MODEL), lambda b, i: (b, 1))],
        out_specs=pl.BlockSpec((tq, D_MODEL), lambda b, i: (b * nq + i, 0)),
        out_shape=jax.ShapeDtypeStruct((batch * seq, D_MODEL), BF16),
        compiler_params=_cparams(("arbitrary", "arbitrary")),
    )(q, kv, kv)


def _router_kernel(h_ref, nw_ref, wr_ref, br_ref, idx_ref, gate_ref, pos_ref, cnt_ref, run_ref):
    @pl.when(pl.program_id(0) == 0)
    def _():
        run_ref[...] = jnp.zeros_like(run_ref)

    hn = _rms(h_ref[...], nw_ref[...])
    w = wr_ref[...]
    h_hi = hn.astype(BF16)
    h_lo = (hn - h_hi.astype(F32)).astype(BF16)
    w_hi = w.astype(BF16)
    w_lo = (w - w_hi.astype(F32)).astype(BF16)
    logits = (jnp.dot(h_hi, w_hi, preferred_element_type=F32)
              + (jnp.dot(h_hi, w_lo, preferred_element_type=F32)
                 + jnp.dot(h_lo, w_hi, preferred_element_type=F32))) + br_ref[...]
    tm = logits.shape[0]
    lane = lax.broadcasted_iota(jnp.int32, logits.shape, 1)
    vals = []
    hots = []
    idx_out = jnp.zeros(logits.shape, jnp.int32)
    for k in range(TOP_K):
        m = jnp.max(logits, axis=-1, keepdims=True)
        idx = jnp.min(jnp.where(logits == m, lane, LANES), axis=-1, keepdims=True)
        hit = lane == idx
        idx_out = jnp.where(lane == k, idx, idx_out)
        logits = jnp.where(hit, -jnp.inf, logits)
        vals.append(m)
        hots.append(jnp.where(hit, 1.0, 0.0))
    exps = [jnp.exp(v - vals[0]) for v in vals]
    den = exps[0] + exps[1] + exps[2] + exps[3]
    gate_out = jnp.zeros(logits.shape, F32)
    for k in range(TOP_K):
        gate_out = jnp.where(lane == k, exps[k] / den, gate_out)
    idx_ref[...] = idx_out
    gate_ref[...] = gate_out

    hot = hots[0] + hots[1] + hots[2] + hots[3]
    ri = lax.broadcasted_iota(jnp.int32, (tm, tm), 0)
    ci = lax.broadcasted_iota(jnp.int32, (tm, tm), 1)
    before = jnp.where(ri > ci, 1.0, 0.0).astype(BF16)
    ahead = jnp.dot(before, hot.astype(BF16), preferred_element_type=F32) + run_ref[...]
    pos_out = jnp.zeros(logits.shape, F32)
    for k in range(TOP_K):
        pos_out = jnp.where(lane == k, jnp.sum(ahead * hots[k], axis=-1, keepdims=True), pos_out)
    pos_ref[...] = pos_out
    run_ref[...] = run_ref[...] + jnp.sum(hot, axis=0, keepdims=True)
    cnt_ref[...] = run_ref[...]


def _router(h, norm_w, w_router, b_router):
    t = h.shape[0]
    tm = ROUTER_TOKENS
    pad = LANES - N_EXPERTS
    wr = jnp.pad(w_router, ((0, 0), (0, pad)))
    br = jnp.pad(b_router, (0, pad), constant_values=NEG_BIG).reshape(1, LANES)
    tok_spec = pl.BlockSpec((tm, LANES), lambda i: (i, 0))
    return pl.pallas_call(
        _router_kernel,
        grid=(t // tm,),
        in_specs=[pl.BlockSpec((tm, D_MODEL), lambda i: (i, 0)),
                  pl.BlockSpec((1, D_MODEL), lambda i: (0, 0)),
                  pl.BlockSpec((D_MODEL, LANES), lambda i: (0, 0)),
                  pl.BlockSpec((1, LANES), lambda i: (0, 0))],
        out_specs=[tok_spec, tok_spec, tok_spec, pl.BlockSpec((1, LANES), lambda i: (0, 0))],
        out_shape=[jax.ShapeDtypeStruct((t, LANES), jnp.int32),
                   jax.ShapeDtypeStruct((t, LANES), F32),
                   jax.ShapeDtypeStruct((t, LANES), F32),
                   jax.ShapeDtypeStruct((1, LANES), F32)],
        scratch_shapes=[pltpu.VMEM((1, LANES), F32)],
        compiler_params=_cparams(("arbitrary",)),
    )(h, norm_w.reshape(1, D_MODEL), wr, br)


def _dest_kernel(idx_ref, pos_ref, cnt_ref, dest_ref, bounds_ref):
    cnt = jnp.broadcast_to(cnt_ref[...], (SUBLANES, LANES))
    padded = jnp.floor((cnt + (MOE_BLOCK - 1)) * (1.0 / MOE_BLOCK)) * MOE_BLOCK
    ri = lax.broadcasted_iota(jnp.int32, (LANES, LANES), 0)
    ci = lax.broadcasted_iota(jnp.int32, (LANES, LANES), 1)
    upto = jnp.where(ri <= ci, 1.0, 0.0).astype(F32)
    seg_end = jnp.dot(padded, upto, preferred_element_type=F32, precision=lax.Precision.HIGHEST)
    seg_start = seg_end - padded
    blocks = padded * (1.0 / MOE_BLOCK)
    full = jnp.floor(blocks * (1.0 * MOE_BLOCK / MOE_ITEM))
    rem = blocks - full * (MOE_ITEM // MOE_BLOCK)
    mid = jnp.floor(rem * 0.5)
    items = full + mid + (rem - 2.0 * mid)
    items_end = jnp.dot(items, upto, preferred_element_type=F32, precision=lax.Precision.HIGHEST)
    row = lax.broadcasted_iota(jnp.int32, (SUBLANES, LANES), 0)
    table = jnp.zeros((SUBLANES, LANES), F32)
    for r, v in enumerate((seg_start, seg_end, items_end, items, full, mid)):
        table = jnp.where(row == r, v, table)
    bounds_ref[...] = table.astype(jnp.int32)

    idx = idx_ref[...]
    lane = lax.broadcasted_iota(jnp.int32, idx.shape, 1)
    out = pos_ref[...]
    for k in range(TOP_K):
        hit = lane == idx[:, k:k + 1]
        base = jnp.sum(jnp.where(hit, seg_start[0:1, :], 0.0), axis=-1, keepdims=True)
        out = jnp.where(lane == k, out + base, out)
    dest_ref[...] = out.T[0:SUBLANES, :].astype(jnp.int32)


def _dest_rows(idx_pad, pos_pad, cnt_pad):
    t = idx_pad.shape[0]
    tm = ROUTER_TOKENS
    tok_spec = pl.BlockSpec((tm, LANES), lambda i: (i, 0))
    return pl.pallas_call(
        _dest_kernel,
        grid=(t // tm,),
        in_specs=[tok_spec, tok_spec, pl.BlockSpec((1, LANES), lambda i: (0, 0))],
        out_specs=[pl.BlockSpec((SUBLANES, tm), lambda i: (0, i)),
                   pl.BlockSpec((SUBLANES, LANES), lambda i: (0, 0))],
        out_shape=[jax.ShapeDtypeStruct((SUBLANES, t), jnp.int32),
                   jax.ShapeDtypeStruct((SUBLANES, LANES), jnp.int32)],
        compiler_params=_cparams(("arbitrary",)),
    )(idx_pad, pos_pad, cnt_pad)


def _pack_halves(x):
    c = x.shape[1] // 2
    lo = pltpu.bitcast(x[:, :c].astype(BF16).astype(F32), jnp.uint32)
    hi = pltpu.bitcast(x[:, c:].astype(BF16).astype(F32), jnp.uint32)
    return (lo >> 16) | hi


def _unpack_halves(w):
    lo = pltpu.bitcast(w << 16, F32)
    hi = pltpu.bitcast(w & jnp.uint32(0xFFFF0000), F32)
    return jnp.concatenate([lo, hi], axis=1).astype(BF16)


def _scatter_kernel(dest_ref, bounds_ref, h_ref, nw_ref, x_ref, zero_ref, packed_ref, zsem, sem):
    n_blocks = x_ref.shape[0] // MOE_BLOCK
    i = pl.program_id(0)
    slot = i % 2
    packed_ref[slot] = _pack_halves(_rms(h_ref[...], nw_ref[...]))

    @pl.when(i == 0)
    def _():
        zero_ref[...] = jnp.zeros_like(zero_ref)
        n_used = bounds_ref[1, N_EXPERTS - 1] // MOE_BLOCK

        def zero_copy(b):
            return pltpu.make_async_copy(zero_ref, x_ref.at[pl.ds(pl.multiple_of(b * MOE_BLOCK, MOE_BLOCK),
                                                                   MOE_BLOCK)], zsem)

        def last_block(e):
            return bounds_ref[1, e] // MOE_BLOCK - 1, bounds_ref[1, e] > bounds_ref[0, e]

        def start_e(e, carry):
            b, nonempty = last_block(e)

            @pl.when(nonempty)
            def _():
                zero_copy(b).start()
            return carry

        def wait_e(e, carry):
            b, nonempty = last_block(e)

            @pl.when(nonempty)
            def _():
                zero_copy(b).wait()
            return carry

        lax.fori_loop(0, N_EXPERTS, start_e, 0)
        lax.fori_loop(n_used, n_blocks, lambda b, c: (zero_copy(b).start(), c)[1], 0)
        lax.fori_loop(0, N_EXPERTS, wait_e, 0)
        lax.fori_loop(n_used, n_blocks, lambda b, c: (zero_copy(b).wait(), c)[1], 0)

    def issue(r, carry):
        for k in range(TOP_K):
            d = dest_ref[k, r]
            pltpu.make_async_copy(packed_ref.at[slot, pl.ds(r, 1)], x_ref.at[pl.ds(d, 1)],
                                  sem.at[slot]).start(priority=k % 2)
        return carry

    lax.fori_loop(0, SCATTER_TOKENS, issue, 0)

    def wait_rows(s):
        n_sent = SCATTER_TOKENS * TOP_K
        pltpu.make_async_copy(x_ref.at[pl.ds(0, n_sent)], x_ref.at[pl.ds(0, n_sent)], sem.at[s]).wait()

    @pl.when(i > 0)
    def _():
        wait_rows(1 - slot)

    @pl.when(i == pl.num_programs(0) - 1)
    def _():
        wait_rows(slot)


def _scatter_rows(h, norm_w, dest_t, bounds, n_rows):
    t = h.shape[0]
    steps = t // SCATTER_TOKENS
    half = D_MODEL // 2
    return pl.pallas_call(
        _scatter_kernel,
        grid=(steps,),
        in_specs=[pl.BlockSpec((SUBLANES, SCATTER_TOKENS), lambda i: (0, i), memory_space=pltpu.SMEM),
                  pl.BlockSpec(memory_space=pltpu.SMEM),
                  pl.BlockSpec((SCATTER_TOKENS, D_MODEL), lambda i: (i, 0)),
                  pl.BlockSpec((1, D_MODEL), lambda i: (0, 0))],
        out_specs=pl.BlockSpec(memory_space=pl.ANY),
        out_shape=jax.ShapeDtypeStruct((n_rows, half), jnp.uint32),
        scratch_shapes=[pltpu.VMEM((MOE_BLOCK, half), jnp.uint32), pltpu.VMEM((2, SCATTER_TOKENS, half), jnp.uint32),
                        pltpu.SemaphoreType.DMA(()), pltpu.SemaphoreType.DMA((2,))],
        compiler_params=_cparams(("arbitrary",)),
    )(dest_t, bounds, h, norm_w.reshape(1, D_MODEL))


def _item_tables(bounds, n_j, n_items_max):
    items_e = bounds[3]
    first_e = (bounds[2] - items_e) * n_j
    end_e = bounds[2] * n_j
    n_items = end_e[N_EXPERTS - 1]
    g = jnp.minimum(jnp.arange(n_items_max, dtype=jnp.int32), n_items - 1)
    e = jnp.sum((g[:, None] >= end_e[None, :N_EXPERTS]).astype(jnp.int32), axis=1)
    own = e[:, None] == jnp.arange(LANES, dtype=jnp.int32)[None, :]
    pick = lambda v: jnp.sum(jnp.where(own, v[None, :], 0), axis=1)
    first = pick(first_e)
    nb = jnp.maximum(pick(items_e), 1)
    q = g - first
    j = q // nb
    i = q % nb
    full = pick(bounds[4])
    mid = pick(bounds[5])
    code = jnp.where(i < full, 0, jnp.where((i == full) & (mid > 0), 1, 2))
    per_full = MOE_ITEM // MOE_BLOCK
    offset = jnp.where(code == 0, i * per_full, full * per_full + jnp.where(code == 2, mid * (per_full // 2), 0))
    row = pick(bounds[0] // MOE_BLOCK) + offset
    meta = e * 32 + code * 8 + j * 2 + (i == 0).astype(jnp.int32)
    group_end = first + (j + 1) * nb
    counts = jnp.stack([n_items, bounds[1, N_EXPERTS - 1] // MOE_BLOCK])
    return meta.astype(jnp.int32), row.astype(jnp.int32), group_end.astype(jnp.int32), counts.astype(jnp.int32)


def _meta_expert(m):
    return m >> 5


def _meta_tile(m):
    return (m >> 1) & 3


def _item_loop(meta_ref, row_ref, gend_ref, n_ref, src_ref, dst_ref, w_refs, ring, ring_sem, wb_ref,
               in_buf, out_buf, in_sem, out_sem, compute):
    n = n_ref[0]
    n_in = in_buf.shape[0]
    n_out = out_buf.shape[0]
    tn = out_buf.shape[2]
    n_j = dst_ref.shape[1] // tn
    n_mats = len(w_refs)
    cpm = W_CHUNKS // n_mats
    kc = w_refs[0].shape[1] // cpm

    def by_size(m, fn):
        for code, size in enumerate(MOE_ITEM_SIZES):
            @pl.when(((m >> 3) & 3) == code)
            def _(size=size):
                fn(size)

    def rows(g, size):
        return pl.ds(pl.multiple_of(row_ref[g] * MOE_BLOCK, MOE_BLOCK), size)

    def cols(m):
        return pl.ds(pl.multiple_of(_meta_tile(m) * tn, tn), tn)

    def in_copy(g, size):
        slot = g % n_in
        return pltpu.make_async_copy(src_ref.at[rows(g, size)], in_buf.at[slot, pl.ds(0, size)], in_sem.at[slot])

    def out_copy(g, size):
        slot = g % n_out
        return pltpu.make_async_copy(out_buf.at[slot, pl.ds(0, size)], dst_ref.at[rows(g, size), cols(meta_ref[g])],
                                     out_sem.at[slot])

    def start_in(g):
        by_size(meta_ref[g], lambda size: in_copy(g, size).start(priority=ROW_DMA_PRIORITY))

    def wait_in(g):
        by_size(meta_ref[g], lambda size: in_copy(g, size).wait())

    def start_out(g):
        by_size(meta_ref[g], lambda size: out_copy(g, size).start(priority=ROW_DMA_PRIORITY))

    def wait_out(g):
        by_size(meta_ref[g], lambda size: out_copy(g, size).wait())

    def krows(c):
        r0 = (c % cpm) * kc
        return pl.ds(r0 if isinstance(c, int) else pl.multiple_of(r0, kc), kc)

    def chunk_copy(m, c, k):
        slot = c % W_RING
        return pltpu.make_async_copy(w_refs[k].at[_meta_expert(m), krows(c), cols(m)], ring.at[slot],
                                     ring_sem.at[slot])

    def for_matrix(c, fn):
        if isinstance(c, int):
            return fn(c // cpm)
        for k in range(n_mats):
            @pl.when(c // cpm == k)
            def _(k=k):
                fn(k)

    def start_chunk(m, c):
        for_matrix(c, lambda k: chunk_copy(m, c, k).start())

    def finish_chunk(m, c, wslot):
        def fn(k):
            chunk_copy(m, c, k).wait()
            wb_ref[wslot, k, krows(c), :] = ring[c % W_RING].astype(BF16)
        for_matrix(c, fn)

        @pl.when(c + W_RING < W_CHUNKS)
        def _():
            start_chunk(m, c + W_RING)

    for c0 in range(W_RING):
        start_chunk(meta_ref[0], c0)
    for g0 in range(n_in - 1):
        @pl.when(g0 < n)
        def _():
            start_in(g0)

    def body(g, carry):
        parity, done, nm = carry
        m = meta_ref[g]
        first = (m & 1) == 1

        @pl.when(first)
        def _():
            lax.fori_loop(done, W_CHUNKS, lambda c, z: (finish_chunk(m, c, 1 - parity), z)[1], 0)
            nxt = gend_ref[g]

            @pl.when(nxt < n)
            def _():
                for c0 in range(W_RING):
                    start_chunk(meta_ref[nxt], c0)

        nxt = gend_ref[g]
        parity = jnp.where(first, 1 - parity, parity)
        done = jnp.where(first, 0, done)
        nm = jnp.where(first, jnp.where(nxt < n, meta_ref[jnp.minimum(nxt, n - 1)], -1), nm)

        wait_in(g)

        @pl.when(g + n_in - 1 < n)
        def _():
            start_in(g + n_in - 1)

        @pl.when(g >= n_out)
        def _():
            wait_out(g - n_out)

        def run(size):
            x = in_buf[g % n_in, pl.ds(0, size)]
            out_buf[g % n_out, pl.ds(0, size)] = compute(x, wb_ref.at[parity], m).astype(out_buf.dtype)

        by_size(m, run)
        start_out(g)

        active = (nm >= 0) & (done < W_CHUNKS)

        @pl.when(active)
        def _():
            finish_chunk(nm, done, 1 - parity)

        return parity, jnp.where(active, done + 1, done), nm

    lax.fori_loop(0, n, body, (jnp.int32(1), jnp.int32(0), meta_ref[0]))
    lax.fori_loop(jnp.maximum(n - n_out, 0), n, lambda g, c: (wait_out(g), c)[1], 0)

    n_blocks = dst_ref.shape[0] // MOE_BLOCK
    n_used = n_ref[1]
    out_buf[0] = jnp.zeros(out_buf.shape[1:], out_buf.dtype)
    for jt in range(n_j):
        def tail_copy(b, jt=jt):
            r = pl.ds(pl.multiple_of(b * MOE_BLOCK, MOE_BLOCK), MOE_BLOCK)
            return pltpu.make_async_copy(out_buf.at[0, pl.ds(0, MOE_BLOCK)], dst_ref.at[r, jt * tn:(jt + 1) * tn],
                                         out_sem.at[0])

        lax.fori_loop(n_used, n_blocks, lambda b, c: (tail_copy(b).start(), c)[1], 0)
        lax.fori_loop(n_used, n_blocks, lambda b, c: (tail_copy(b).wait(), c)[1], 0)


def _expert_up_kernel(meta_ref, row_ref, gend_ref, n_ref, x_ref, bg_ref, bu_ref, wg_ref, wu_ref, act_ref,
                      ring, wb_ref, x_buf, a_buf, ring_sem, in_sem, out_sem):
    n_j = D_FF // MOE_UP_TN

    def compute(x, w, m):
        tile = _meta_expert(m) * n_j + _meta_tile(m)
        xb = _unpack_halves(x)
        g = jnp.dot(xb, w[0], preferred_element_type=F32) + bg_ref[pl.ds(tile, 1), :]
        u = jnp.dot(xb, w[1], preferred_element_type=F32) + bu_ref[pl.ds(tile, 1), :]
        g = jnp.minimum(g, SWIGLU_LIMIT)
        u = jnp.clip(u, -SWIGLU_LIMIT, SWIGLU_LIMIT)
        return g * _sigmoid(SWIGLU_ALPHA * g) * (u + 1.0)

    _item_loop(meta_ref, row_ref, gend_ref, n_ref, x_ref, act_ref, (wg_ref, wu_ref), ring, ring_sem, wb_ref,
               x_buf, a_buf, in_sem, out_sem, compute)


def _expert_down_kernel(meta_ref, row_ref, gend_ref, n_ref, a_ref, bd_ref, wd_ref, y_ref,
                        ring, wb_ref, a_buf, y_buf, ring_sem, in_sem, out_sem):
    n_j = D_MODEL // MOE_DOWN_TN

    def compute(a, w, m):
        tile = _meta_expert(m) * n_j + _meta_tile(m)
        return jnp.dot(a, w[0], preferred_element_type=F32) + bd_ref[pl.ds(tile, 1), :]

    _item_loop(meta_ref, row_ref, gend_ref, n_ref, a_ref, y_ref, (wd_ref,), ring, ring_sem, wb_ref,
               a_buf, y_buf, in_sem, out_sem, compute)


def _experts(x_rows, bounds, w_gate, b_gate, w_up, b_up, w_down, b_down):
    n_rows = x_rows.shape[0]
    n_items_max = n_rows // MOE_ITEM + (len(MOE_ITEM_SIZES) - 1) * N_EXPERTS
    any_spec = pl.BlockSpec(memory_space=pl.ANY)
    sems = [pltpu.SemaphoreType.DMA((W_RING,)), pltpu.SemaphoreType.DMA((MOE_IN_BUFS,)),
            pltpu.SemaphoreType.DMA((MOE_OUT_BUFS,))]

    def full(shape):
        return pl.BlockSpec(shape, lambda i, *_: (0,) * len(shape))

    n_j = D_FF // MOE_UP_TN
    act = pl.pallas_call(
        _expert_up_kernel,
        grid_spec=pltpu.PrefetchScalarGridSpec(
            num_scalar_prefetch=4, grid=(1,),
            in_specs=[any_spec, full((N_EXPERTS * n_j, MOE_UP_TN)),
                      full((N_EXPERTS * n_j, MOE_UP_TN)), any_spec, any_spec],
            out_specs=any_spec,
            scratch_shapes=[pltpu.VMEM((W_RING, 2 * D_MODEL // W_CHUNKS, MOE_UP_TN), F32),
                            pltpu.VMEM((2, 2, D_MODEL, MOE_UP_TN), BF16),
                            pltpu.VMEM((MOE_IN_BUFS, MOE_ITEM, D_MODEL // 2), jnp.uint32),
                            pltpu.VMEM((MOE_OUT_BUFS, MOE_ITEM, MOE_UP_TN), BF16)] + sems),
        out_shape=jax.ShapeDtypeStruct((n_rows, D_FF), BF16),
        compiler_params=_cparams(("arbitrary",)),
    )(*_item_tables(bounds, n_j, n_items_max * n_j), x_rows,
      b_gate.reshape(N_EXPERTS * n_j, MOE_UP_TN), b_up.reshape(N_EXPERTS * n_j, MOE_UP_TN), w_gate, w_up)

    n_j = D_MODEL // MOE_DOWN_TN
    return pl.pallas_call(
        _expert_down_kernel,
        grid_spec=pltpu.PrefetchScalarGridSpec(
            num_scalar_prefetch=4, grid=(1,),
            in_specs=[any_spec, full((N_EXPERTS * n_j, MOE_DOWN_TN)), any_spec],
            out_specs=any_spec,
            scratch_shapes=[pltpu.VMEM((W_RING, D_FF // W_CHUNKS, MOE_DOWN_TN), F32),
                            pltpu.VMEM((2, 1, D_FF, MOE_DOWN_TN), BF16),
                            pltpu.VMEM((MOE_IN_BUFS, MOE_ITEM, D_FF), BF16),
                            pltpu.VMEM((MOE_OUT_BUFS, MOE_ITEM, MOE_DOWN_TN), F32)] + sems),
        out_shape=jax.ShapeDtypeStruct((n_rows, D_MODEL), F32),
        compiler_params=_cparams(("arbitrary",)),
    )(*_item_tables(bounds, n_j, n_items_max * n_j), act,
      b_down.reshape(N_EXPERTS * n_j, MOE_DOWN_TN), w_down)


def _combine_kernel(dest_ref, next_ref, h_ref, gate_ref, nw_ref, y_ref, o_ref, buf_ref, sem):
    i = pl.program_id(0)
    steps = pl.num_programs(0)

    def issue(d_ref, slot):
        def body(r, carry):
            for k in range(TOP_K):
                pltpu.make_async_copy(y_ref.at[pl.ds(d_ref[k, r], 1)], buf_ref.at[slot, k, pl.ds(r, 1)],
                                      sem.at[slot]).start(priority=k % 2)
            return carry
        lax.fori_loop(0, COMBINE_TOKENS, body, 0)

    @pl.when(i == 0)
    def _():
        issue(dest_ref, 0)

    @pl.when(i + 1 < steps)
    def _():
        issue(next_ref, (i + 1) % 2)

    slot = i % 2
    pltpu.make_async_copy(buf_ref.at[slot], buf_ref.at[slot], sem.at[slot]).wait()
    acc = h_ref[...]
    gate = gate_ref[...]
    for k in range(TOP_K):
        acc = acc + gate[:, k:k + 1] * buf_ref[slot, k]
    o_ref[...] = _rms(acc, nw_ref[...])


def _combine(h, y_rows, dest_t, gate_pad, norm_w):
    t = h.shape[0]
    steps = t // COMBINE_TOKENS
    return pl.pallas_call(
        _combine_kernel,
        grid=(steps,),
        in_specs=[pl.BlockSpec((SUBLANES, COMBINE_TOKENS), lambda i: (0, i), memory_space=pltpu.SMEM),
                  pl.BlockSpec((SUBLANES, COMBINE_TOKENS), lambda i: (0, jnp.minimum(i + 1, steps - 1)),
                               memory_space=pltpu.SMEM),
                  pl.BlockSpec((COMBINE_TOKENS, D_MODEL), lambda i: (i, 0)),
                  pl.BlockSpec((COMBINE_TOKENS, LANES), lambda i: (i, 0)),
                  pl.BlockSpec((1, D_MODEL), lambda i: (0, 0)),
                  pl.BlockSpec(memory_space=pl.ANY)],
        out_specs=pl.BlockSpec((COMBINE_TOKENS, D_MODEL), lambda i: (i, 0)),
        out_shape=jax.ShapeDtypeStruct((t, D_MODEL), F32),
        scratch_shapes=[pltpu.VMEM((2, TOP_K, COMBINE_TOKENS, D_MODEL), F32), pltpu.SemaphoreType.DMA((2,))],
        compiler_params=_cparams(("arbitrary",)),
    )(dest_t, dest_t, h, gate_pad, norm_w.reshape(1, D_MODEL), y_rows)


def _moe(h, norm_ffn, w_router, b_router, w_gate, b_gate, w_up, b_up, w_down, b_down, norm_final):
    t = h.shape[0]
    tk = t * TOP_K
    n_blocks = (tk + N_EXPERTS * (MOE_BLOCK - 1) + MOE_BLOCK - 1) // MOE_BLOCK
    n_rows = n_blocks * MOE_BLOCK
    idx_pad, gate_pad, pos_pad, cnt_pad = _router(h, norm_ffn, w_router, b_router)
    dest_t, bounds = _dest_rows(idx_pad, pos_pad, cnt_pad)
    x_rows = _scatter_rows(h, norm_ffn, dest_t, bounds, n_rows)
    y_rows = _experts(x_rows, bounds, w_gate, b_gate, w_up, b_up, w_down, b_down)
    return _combine(h, y_rows, dest_t, gate_pad, norm_final)


def kernel(x, mem, norm_mix, w_in, conv_w, conv_b, dt_bias, a_log, d_skip, ssd_norm, pool_w, pool_scale,
           w_out, norm_xattn, norm_mem, w_q, w_kv, w_o, norm_ffn, w_router, b_router, w_gate, b_gate,
           w_up, b_up, w_down, b_down, norm_final):
    batch, seq, d = x.shape
    t = batch * seq
    h = x.reshape(t, d)
    depth = norm_mix.shape[0]
    assert depth == 1, "the final norm is fused into the MoE combine of the single layer"
    for l in range(depth):
        dt_lo = D_SSD + D_CONV
        w_in_t = jnp.swapaxes(w_in[l], 0, 1)
        w_dt_t = jnp.pad(w_in_t[dt_lo:dt_lo + SSD_HEADS], ((0, LANES - SSD_HEADS), (0, 0)))
        proj = _mm(h, w_in_t, norm_w=norm_mix[l], n_cols=dt_lo, w_transposed=True)
        proj_u = _mm(h, w_in_t[dt_lo + SSD_HEADS:], norm_w=norm_mix[l], w_transposed=True)
        dtp = _mm(h, w_dt_t, norm_w=norm_mix[l], w_transposed=True)
        mix = _mixer(proj, proj_u, dtp, conv_w[l], conv_b[l], dt_bias[l], a_log[l], d_skip[l], ssd_norm[l],
                     pool_w[l], pool_scale[l], batch, seq)
        h = _mm(mix, w_out[l], res=h)

        q = _mm(h, w_q[l], norm_w=norm_xattn[l], out_dtype=BF16)
        kv = _mm(mem.reshape(batch * N_MEM, d), w_kv[l], norm_w=norm_mem[l], out_dtype=BF16)
        o = _attn(q, kv, batch, seq)
        h = _mm(o, w_o[l], res=h)

        out = _moe(h, norm_ffn[l], w_router[l], b_router[l], w_gate[l], b_gate[l], w_up[l], b_up[l],
                   w_down[l], b_down[l], norm_final)
    return out.reshape(batch, seq, d)
```

```python
import functools

import jax
import jax.numpy as jnp
from jax import lax
from jax.experimental import pallas as pl
from jax.experimental.pallas import tpu as pltpu

F32 = jnp.float32
BF16 = jnp.bfloat16

D_MODEL = 2048
D_SSD = 1024
SSD_HEAD_DIM = 64
SSD_HEADS = 16
SSD_GROUPS = 4
HEADS_PER_GROUP = SSD_HEADS // SSD_GROUPS
D_STATE = 128
CONV_WIDTH = 4
CHUNK = 128
D_CONV = D_SSD + 2 * SSD_GROUPS * D_STATE
D_POOL = 1024
POOL_WINDOWS = (2, 4, 8, 16)
POOL_GROUP_DIM = 256
N_MEM = 256
XATTN_HEADS = 4
XATTN_HEAD_DIM = 512
N_EXPERTS = 32
TOP_K = 4
D_FF = 2048
SWIGLU_LIMIT = 7.0
SWIGLU_ALPHA = 1.702
EPS = 1e-5

LANES = 128
SUBLANES = 8
VMEM_LIMIT = 56 * 1024 * 1024
NEG_BIG = -1e30

MOE_BLOCK = 128
MOE_ITEM_SIZES = (4 * MOE_BLOCK, 2 * MOE_BLOCK, MOE_BLOCK)
MOE_ITEM = MOE_ITEM_SIZES[0]
MOE_UP_TN = 1024
MOE_DOWN_TN = 2048
MOE_IN_BUFS = 3
MOE_OUT_BUFS = 2
W_CHUNKS = 4
W_RING = 2
ROUTER_TOKENS = 512
SCATTER_TOKENS = 512
COMBINE_TOKENS = 256
ROW_DMA_PRIORITY = 1


def _cparams(sem):
    return pltpu.CompilerParams(dimension_semantics=sem, vmem_limit_bytes=VMEM_LIMIT)


def _rms(x, w):
    ms = jnp.mean(x * x, axis=-1, keepdims=True)
    return x * lax.rsqrt(ms + EPS) * w


def _mm_kernel(*refs, has_norm, has_res, w_transposed):
    it = iter(refs)
    a_ref = next(it)
    w_ref = next(it)
    nw_ref = next(it) if has_norm else None
    r_ref = next(it) if has_res else None
    o_ref = next(it)
    wb_ref = next(it)

    @pl.when(pl.program_id(1) == 0)
    def _():
        w = w_ref[...]
        wb_ref[...] = (w.T if w_transposed else w).astype(BF16)

    a = a_ref[...]
    if has_norm:
        a = _rms(a.astype(F32), nw_ref[...])
    acc = jnp.dot(a.astype(BF16), wb_ref[...], preferred_element_type=F32)
    if has_res:
        acc = acc + r_ref[...]
    o_ref[...] = acc.astype(o_ref.dtype)


def _mm(a, w, norm_w=None, res=None, out_dtype=F32, tm=1024, tn=1024, n_cols=None, w_transposed=False):
    m, k = a.shape
    n = w.shape[0 if w_transposed else 1] if n_cols is None else n_cols
    tn = min(tn, n)
    tm = min(tm, m)
    assert m % tm == 0 and n % tn == 0
    w_spec = (pl.BlockSpec((tn, k), lambda j, i: (j, 0)) if w_transposed
              else pl.BlockSpec((k, tn), lambda j, i: (0, j)))
    in_specs = [pl.BlockSpec((tm, k), lambda j, i: (i, 0)), w_spec]
    args = [a, w]
    if norm_w is not None:
        in_specs.append(pl.BlockSpec((1, k), lambda j, i: (0, 0)))
        args.append(norm_w.reshape(1, k))
    if res is not None:
        in_specs.append(pl.BlockSpec((tm, tn), lambda j, i: (i, j)))
        args.append(res)
    return pl.pallas_call(
        functools.partial(_mm_kernel, has_norm=norm_w is not None, has_res=res is not None,
                          w_transposed=w_transposed),
        grid=(n // tn, m // tm),
        in_specs=in_specs,
        out_specs=pl.BlockSpec((tm, tn), lambda j, i: (i, j)),
        out_shape=jax.ShapeDtypeStruct((m, n), out_dtype),
        scratch_shapes=[pltpu.VMEM((k, tn), BF16)],
        compiler_params=_cparams(("arbitrary", "arbitrary")),
    )(*args)


def _sigmoid(x):
    return 1.0 / (1.0 + jnp.exp(-x))


def _shift_rows(cur, prev, j, rows):
    return jnp.where(rows < j, pltpu.roll(prev, j, 0), pltpu.roll(cur, j, 0))


def _conv_silu(cur, prev, w_ref, b_ref, col0, rows):
    width = cur.shape[1]
    acc = cur * w_ref[CONV_WIDTH - 1:CONV_WIDTH, col0:col0 + width] + b_ref[:, col0:col0 + width]
    for j in range(1, CONV_WIDTH):
        k = CONV_WIDTH - 1 - j
        acc = acc + _shift_rows(cur, prev, j, rows) * w_ref[k:k + 1, col0:col0 + width]
    return acc * _sigmoid(acc)


def _mixer_kernel(z_ref, x_ref, bc_ref, u_ref, dt_ref, convw_ref, convb_ref, dtb_ref, alog_ref,
                  dskip_ref, ssdn_ref, poolw_ref, pscale_ref, out_ref,
                  px_ref, pbc_ref, pu_ref, state_ref):
    c = pl.program_id(1)

    @pl.when(c == 0)
    def _():
        px_ref[...] = jnp.zeros_like(px_ref)
        pbc_ref[...] = jnp.zeros_like(pbc_ref)
        pu_ref[...] = jnp.zeros_like(pu_ref)
        state_ref[...] = jnp.zeros_like(state_ref)

    rows = lax.broadcasted_iota(jnp.int32, (CHUNK, D_SSD), 0)
    x_raw = x_ref[...]
    bc_raw = bc_ref[...]
    xs = _conv_silu(x_raw, px_ref[...], convw_ref, convb_ref, 0, rows)
    bcs = _conv_silu(bc_raw, pbc_ref[...], convw_ref, convb_ref, D_SSD, rows)
    px_ref[...] = x_raw
    pbc_ref[...] = bc_raw

    dt_in = dt_ref[...] + dtb_ref[...]
    dt = jnp.maximum(dt_in, 0.0) + jnp.log(1.0 + jnp.exp(-jnp.abs(dt_in)))
    a = -jnp.exp(alog_ref[...])
    dta = dt * a
    ri = lax.broadcasted_iota(jnp.int32, (CHUNK, CHUNK), 0)
    ci = lax.broadcasted_iota(jnp.int32, (CHUNK, CHUNK), 1)
    causal = ri >= ci
    tril = jnp.where(causal, 1.0, 0.0).astype(F32)
    acs = jnp.dot(tril, dta, preferred_element_type=F32, precision=lax.Precision.HIGHEST)
    acs_t = acs.T
    dt_t = dt.T
    lo = ci < SSD_HEAD_DIM

    y_pairs = []
    for g in range(SSD_GROUPS):
        bg = bcs[:, g * D_STATE:(g + 1) * D_STATE].astype(BF16)
        cg = bcs[:, SSD_GROUPS * D_STATE + g * D_STATE:SSD_GROUPS * D_STATE + (g + 1) * D_STATE].astype(BF16)
        cb = lax.dot_general(cg, bg, (((1,), (1,)), ((), ())), preferred_element_type=F32)
        gw = HEADS_PER_GROUP * SSD_HEAD_DIM
        prev_g = state_ref[g * gw:(g + 1) * gw, :]
        yoff = lax.dot_general(cg, prev_g.astype(BF16), (((1,), (1,)), ((), ())),
                               preferred_element_type=F32)
        xw_parts = []
        for pr in range(HEADS_PER_GROUP // 2):
            pair = g * (HEADS_PER_GROUP // 2) + pr
            xs_pair = xs[:, pair * LANES:(pair + 1) * LANES]
            y_pair = None
            ecols = []
            wcols = []
            for q in range(2):
                h = pair * 2 + q
                col = acs[:, h:h + 1]
                seg = col - acs_t[h:h + 1, :]
                decay = jnp.exp(jnp.where(causal, seg, NEG_BIG))
                mat = (cb * decay * dt_t[h:h + 1, :]).astype(BF16)
                keep = lo if q == 0 else jnp.logical_not(lo)
                xh = jnp.where(keep, xs_pair, 0.0).astype(BF16)
                yd = jnp.dot(mat, xh, preferred_element_type=F32)
                y_pair = yd if y_pair is None else y_pair + yd
                ecols.append(jnp.exp(col))
                wcols.append(dt[:, h:h + 1] * jnp.exp(acs[CHUNK - 1:CHUNK, h:h + 1] - col))
            e_pair = jnp.where(lo, ecols[0], ecols[1])
            w_pair = jnp.where(lo, wcols[0], wcols[1])
            y_pairs.append(y_pair + yoff[:, pr * LANES:(pr + 1) * LANES] * e_pair)
            xw_parts.append(xs_pair * w_pair)
        xw = jnp.concatenate(xw_parts, axis=1).astype(BF16)
        s_new = lax.dot_general(xw, bg, (((0,), (0,)), ((), ())), preferred_element_type=F32)
        for r in range(HEADS_PER_GROUP):
            h = g * HEADS_PER_GROUP + r
            dec = jnp.exp(acs[CHUNK - 1:CHUNK, h:h + 1])
            lo_r = h * SSD_HEAD_DIM
            state_ref[lo_r:lo_r + SSD_HEAD_DIM, :] = (
                state_ref[lo_r:lo_r + SSD_HEAD_DIM, :] * dec
                + s_new[r * SSD_HEAD_DIM:(r + 1) * SSD_HEAD_DIM, :])

    y = jnp.concatenate(y_pairs, axis=1)
    y = y + dskip_ref[...] * xs
    z = z_ref[...]
    y = y * (z * _sigmoid(z))
    gdim = D_SSD // SSD_GROUPS
    for g in range(SSD_GROUPS):
        yg = _rms(y[:, g * gdim:(g + 1) * gdim], ssdn_ref[:, g * gdim:(g + 1) * gdim])
        out_ref[:, g * gdim:(g + 1) * gdim] = yg.astype(out_ref.dtype)

    u_cur = u_ref[...]
    u_prev = pu_ref[...]
    pos = (c * CHUNK + lax.broadcasted_iota(jnp.int32, (CHUNK, POOL_GROUP_DIM), 0) + 1).astype(F32)
    for g, win in enumerate(POOL_WINDOWS):
        sl = slice(g * POOL_GROUP_DIM, (g + 1) * POOL_GROUP_DIM)
        ug = u_cur[:, sl]
        ext = jnp.concatenate([u_prev[:, sl], ug], axis=0)
        step = 1
        while step < win:
            ext = ext + pltpu.roll(ext, step, 0)
            step *= 2
        pooled = ext[CHUNK:, :] / jnp.minimum(pos, float(win)) - ug
        mixed = jnp.dot(pooled.astype(BF16), poolw_ref[g].astype(BF16), preferred_element_type=F32)
        out_ref[:, D_SSD + g * POOL_GROUP_DIM:D_SSD + (g + 1) * POOL_GROUP_DIM] = (
            mixed * pscale_ref[:, sl]).astype(out_ref.dtype)
    pu_ref[...] = u_cur


def _mixer(proj, proj_u, dtp, conv_w, conv_b, dt_bias, a_log, d_skip, ssd_norm, pool_w, pool_scale, batch, seq):
    nchunk = seq // CHUNK
    row_map = lambda col: (lambda b, c: (b * nchunk + c, col))
    const2 = lambda b, c: (0, 0)
    pad = LANES - SSD_HEADS
    dtb = jnp.pad(dt_bias, (0, pad)).reshape(1, LANES)
    alog = jnp.pad(a_log, (0, pad)).reshape(1, LANES)
    dskip = jnp.repeat(d_skip, SSD_HEAD_DIM).reshape(1, D_SSD)
    return pl.pallas_call(
        _mixer_kernel,
        grid=(batch, nchunk),
        in_specs=[
            pl.BlockSpec((CHUNK, D_SSD), row_map(0)),
            pl.BlockSpec((CHUNK, D_SSD), row_map(1)),
            pl.BlockSpec((CHUNK, D_SSD), row_map(2)),
            pl.BlockSpec((CHUNK, D_POOL), row_map(0)),
            pl.BlockSpec((CHUNK, LANES), row_map(0)),
            pl.BlockSpec((CONV_WIDTH, D_CONV), const2),
            pl.BlockSpec((1, D_CONV), const2),
            pl.BlockSpec((1, LANES), const2),
            pl.BlockSpec((1, LANES), const2),
            pl.BlockSpec((1, D_SSD), const2),
            pl.BlockSpec((1, D_SSD), const2),
            pl.BlockSpec((len(POOL_WINDOWS), POOL_GROUP_DIM, POOL_GROUP_DIM), lambda b, c: (0, 0, 0)),
            pl.BlockSpec((1, D_POOL), const2),
        ],
        out_specs=pl.BlockSpec((CHUNK, D_MODEL), lambda b, c: (b * nchunk + c, 0)),
        out_shape=jax.ShapeDtypeStruct((batch * seq, D_MODEL), BF16),
        scratch_shapes=[pltpu.VMEM((CHUNK, D_SSD), F32), pltpu.VMEM((CHUNK, D_SSD), F32),
                        pltpu.VMEM((CHUNK, D_POOL), F32), pltpu.VMEM((SSD_HEADS * SSD_HEAD_DIM, D_STATE), F32)],
        compiler_params=_cparams(("arbitrary", "arbitrary")),
    )(proj, proj, proj, proj_u, dtp, conv_w, conv_b.reshape(1, D_CONV), dtb, alog, dskip,
      ssd_norm.reshape(1, D_SSD), pool_w, pool_scale.reshape(1, D_POOL))


def _attn_kernel(q_ref, k_ref, v_ref, o_ref):
    scale = XATTN_HEAD_DIM ** -0.5
    for h in range(XATTN_HEADS):
        sl = slice(h * XATTN_HEAD_DIM, (h + 1) * XATTN_HEAD_DIM)
        s = lax.dot_general(q_ref[:, sl], k_ref[:, sl], (((1,), (1,)), ((), ())),
                            preferred_element_type=F32) * scale
        s = s - jnp.max(s, axis=-1, keepdims=True)
        p = jnp.exp(s)
        p = p / jnp.sum(p, axis=-1, keepdims=True)
        o_ref[:, sl] = jnp.dot(p.astype(BF16), v_ref[:, sl], preferred_element_type=F32).astype(o_ref.dtype)


def _attn(q, kv, batch, seq, tq=512):
    nq = seq // tq
    return pl.pallas_call(
        _attn_kernel,
        grid=(batch, nq),
        in_specs=[pl.BlockSpec((tq, D_MODEL), lambda b, i: (b * nq + i, 0)),
                  pl.BlockSpec((N_MEM, D_MODEL), lambda b, i: (b, 0)),
                  pl.BlockSpec((N_MEM, D_MODEL), lambda b, i: (b, 1))],
        out_specs=pl.BlockSpec((tq, D_MODEL), lambda b, i: (b * nq + i, 0)),
        out_shape=jax.ShapeDtypeStruct((batch * seq, D_MODEL), BF16),
        compiler_params=_cparams(("arbitrary", "arbitrary")),
    )(q, kv, kv)


def _router_kernel(h_ref, nw_ref, wr_ref, br_ref, idx_ref, gate_ref, pos_ref, cnt_ref, run_ref):
    @pl.when(pl.program_id(0) == 0)
    def _():
        run_ref[...] = jnp.zeros_like(run_ref)

    hn = _rms(h_ref[...], nw_ref[...])
    w = wr_ref[...]
    h_hi = hn.astype(BF16)
    h_lo = (hn - h_hi.astype(F32)).astype(BF16)
    w_hi = w.astype(BF16)
    w_lo = (w - w_hi.astype(F32)).astype(BF16)
    logits = (jnp.dot(h_hi, w_hi, preferred_element_type=F32)
              + (jnp.dot(h_hi, w_lo, preferred_element_type=F32)
                 + jnp.dot(h_lo, w_hi, preferred_element_type=F32))) + br_ref[...]
    tm = logits.shape[0]
    lane = lax.broadcasted_iota(jnp.int32, logits.shape, 1)
    vals = []
    hots = []
    idx_out = jnp.zeros(logits.shape, jnp.int32)
    for k in range(TOP_K):
        m = jnp.max(logits, axis=-1, keepdims=True)
        idx = jnp.min(jnp.where(logits == m, lane, LANES), axis=-1, keepdims=True)
        hit = lane == idx
        idx_out = jnp.where(lane == k, idx, idx_out)
        logits = jnp.where(hit, -jnp.inf, logits)
        vals.append(m)
        hots.append(jnp.where(hit, 1.0, 0.0))
    exps = [jnp.exp(v - vals[0]) for v in vals]
    den = exps[0] + exps[1] + exps[2] + exps[3]
    gate_out = jnp.zeros(logits.shape, F32)
    for k in range(TOP_K):
        gate_out = jnp.where(lane == k, exps[k] / den, gate_out)
    idx_ref[...] = idx_out
    gate_ref[...] = gate_out

    hot = hots[0] + hots[1] + hots[2] + hots[3]
    ri = lax.broadcasted_iota(jnp.int32, (tm, tm), 0)
    ci = lax.broadcasted_iota(jnp.int32, (tm, tm), 1)
    before = jnp.where(ri > ci, 1.0, 0.0).astype(BF16)
    ahead = jnp.dot(before, hot.astype(BF16), preferred_element_type=F32) + run_ref[...]
    pos_out = jnp.zeros(logits.shape, F32)
    for k in range(TOP_K):
        pos_out = jnp.where(lane == k, jnp.sum(ahead * hots[k], axis=-1, keepdims=True), pos_out)
    pos_ref[...] = pos_out
    run_ref[...] = run_ref[...] + jnp.sum(hot, axis=0, keepdims=True)
    cnt_ref[...] = run_ref[...]


def _router(h, norm_w, w_router, b_router):
    t = h.shape[0]
    tm = ROUTER_TOKENS
    pad = LANES - N_EXPERTS
    wr = jnp.pad(w_router, ((0, 0), (0, pad)))
    br = jnp.pad(b_router, (0, pad), constant_values=NEG_BIG).reshape(1, LANES)
    tok_spec = pl.BlockSpec((tm, LANES), lambda i: (i, 0))
    return pl.pallas_call(
        _router_kernel,
        grid=(t // tm,),
        in_specs=[pl.BlockSpec((tm, D_MODEL), lambda i: (i, 0)),
                  pl.BlockSpec((1, D_MODEL), lambda i: (0, 0)),
                  pl.BlockSpec((D_MODEL, LANES), lambda i: (0, 0)),
                  pl.BlockSpec((1, LANES), lambda i: (0, 0))],
        out_specs=[tok_spec, tok_spec, tok_spec, pl.BlockSpec((1, LANES), lambda i: (0, 0))],
        out_shape=[jax.ShapeDtypeStruct((t, LANES), jnp.int32),
                   jax.ShapeDtypeStruct((t, LANES), F32),
                   jax.ShapeDtypeStruct((t, LANES), F32),
                   jax.ShapeDtypeStruct((1, LANES), F32)],
        scratch_shapes=[pltpu.VMEM((1, LANES), F32)],
        compiler_params=_cparams(("arbitrary",)),
    )(h, norm_w.reshape(1, D_MODEL), wr, br)


def _dest_kernel(idx_ref, pos_ref, cnt_ref, dest_ref, bounds_ref):
    cnt = jnp.broadcast_to(cnt_ref[...], (SUBLANES, LANES))
    padded = jnp.floor((cnt + (MOE_BLOCK - 1)) * (1.0 / MOE_BLOCK)) * MOE_BLOCK
    ri = lax.broadcasted_iota(jnp.int32, (LANES, LANES), 0)
    ci = lax.broadcasted_iota(jnp.int32, (LANES, LANES), 1)
    upto = jnp.where(ri <= ci, 1.0, 0.0).astype(F32)
    seg_end = jnp.dot(padded, upto, preferred_element_type=F32, precision=lax.Precision.HIGHEST)
    seg_start = seg_end - padded
    blocks = padded * (1.0 / MOE_BLOCK)
    full = jnp.floor(blocks * (1.0 * MOE_BLOCK / MOE_ITEM))
    rem = blocks - full * (MOE_ITEM // MOE_BLOCK)
    mid = jnp.floor(rem * 0.5)
    items = full + mid + (rem - 2.0 * mid)
    items_end = jnp.dot(items, upto, preferred_element_type=F32, precision=lax.Precision.HIGHEST)
    row = lax.broadcasted_iota(jnp.int32, (SUBLANES, LANES), 0)
    table = jnp.zeros((SUBLANES, LANES), F32)
    for r, v in enumerate((seg_start, seg_end, items_end, items, full, mid)):
        table = jnp.where(row == r, v, table)
    bounds_ref[...] = table.astype(jnp.int32)

    idx = idx_ref[...]
    lane = lax.broadcasted_iota(jnp.int32, idx.shape, 1)
    out = pos_ref[...]
    for k in range(TOP_K):
        hit = lane == idx[:, k:k + 1]
        base = jnp.sum(jnp.where(hit, seg_start[0:1, :], 0.0), axis=-1, keepdims=True)
        out = jnp.where(lane == k, out + base, out)
    dest_ref[...] = out.T[0:SUBLANES, :].astype(jnp.int32)


def _dest_rows(idx_pad, pos_pad, cnt_pad):
    t = idx_pad.shape[0]
    tm = ROUTER_TOKENS
    tok_spec = pl.BlockSpec((tm, LANES), lambda i: (i, 0))
    return pl.pallas_call(
        _dest_kernel,
        grid=(t // tm,),
        in_specs=[tok_spec, tok_spec, pl.BlockSpec((1, LANES), lambda i: (0, 0))],
        out_specs=[pl.BlockSpec((SUBLANES, tm), lambda i: (0, i)),
                   pl.BlockSpec((SUBLANES, LANES), lambda i: (0, 0))],
        out_shape=[jax.ShapeDtypeStruct((SUBLANES, t), jnp.int32),
                   jax.ShapeDtypeStruct((SUBLANES, LANES), jnp.int32)],
        compiler_params=_cparams(("arbitrary",)),
    )(idx_pad, pos_pad, cnt_pad)


def _pack_halves(x):
    c = x.shape[1] // 2
    lo = pltpu.bitcast(x[:, :c].astype(BF16).astype(F32), jnp.uint32)
    hi = pltpu.bitcast(x[:, c:].astype(BF16).astype(F32), jnp.uint32)
    return (lo >> 16) | hi


def _unpack_halves(w):
    lo = pltpu.bitcast(w << 16, F32)
    hi = pltpu.bitcast(w & jnp.uint32(0xFFFF0000), F32)
    return jnp.concatenate([lo, hi], axis=1).astype(BF16)


def _scatter_kernel(dest_ref, bounds_ref, h_ref, nw_ref, x_ref, zero_ref, packed_ref, zsem, sem):
    n_blocks = x_ref.shape[0] // MOE_BLOCK
    i = pl.program_id(0)
    slot = i % 2
    packed_ref[slot] = _pack_halves(_rms(h_ref[...], nw_ref[...]))

    @pl.when(i == 0)
    def _():
        zero_ref[...] = jnp.zeros_like(zero_ref)
        n_used = bounds_ref[1, N_EXPERTS - 1] // MOE_BLOCK

        def zero_copy(b):
            return pltpu.make_async_copy(zero_ref, x_ref.at[pl.ds(pl.multiple_of(b * MOE_BLOCK, MOE_BLOCK),
                                                                   MOE_BLOCK)], zsem)

        def last_block(e):
            return bounds_ref[1, e] // MOE_BLOCK - 1, bounds_ref[1, e] > bounds_ref[0, e]

        def start_e(e, carry):
            b, nonempty = last_block(e)

            @pl.when(nonempty)
            def _():
                zero_copy(b).start()
            return carry

        def wait_e(e, carry):
            b, nonempty = last_block(e)

            @pl.when(nonempty)
            def _():
                zero_copy(b).wait()
            return carry

        lax.fori_loop(0, N_EXPERTS, start_e, 0)
        lax.fori_loop(n_used, n_blocks, lambda b, c: (zero_copy(b).start(), c)[1], 0)
        lax.fori_loop(0, N_EXPERTS, wait_e, 0)
        lax.fori_loop(n_used, n_blocks, lambda b, c: (zero_copy(b).wait(), c)[1], 0)

    def issue(r, carry):
        for k in range(TOP_K):
            d = dest_ref[k, r]
            pltpu.make_async_copy(packed_ref.at[slot, pl.ds(r, 1)], x_ref.at[pl.ds(d, 1)],
                                  sem.at[slot]).start(priority=k % 2)
        return carry

    lax.fori_loop(0, SCATTER_TOKENS, issue, 0)

    def wait_rows(s):
        n_sent = SCATTER_TOKENS * TOP_K
        pltpu.make_async_copy(x_ref.at[pl.ds(0, n_sent)], x_ref.at[pl.ds(0, n_sent)], sem.at[s]).wait()

    @pl.when(i > 0)
    def _():
        wait_rows(1 - slot)

    @pl.when(i == pl.num_programs(0) - 1)
    def _():
        wait_rows(slot)


def _scatter_rows(h, norm_w, dest_t, bounds, n_rows):
    t = h.shape[0]
    steps = t // SCATTER_TOKENS
    half = D_MODEL // 2
    return pl.pallas_call(
        _scatter_kernel,
        grid=(steps,),
        in_specs=[pl.BlockSpec((SUBLANES, SCATTER_TOKENS), lambda i: (0, i), memory_space=pltpu.SMEM),
                  pl.BlockSpec(memory_space=pltpu.SMEM),
                  pl.BlockSpec((SCATTER_TOKENS, D_MODEL), lambda i: (i, 0)),
                  pl.BlockSpec((1, D_MODEL), lambda i: (0, 0))],
        out_specs=pl.BlockSpec(memory_space=pl.ANY),
        out_shape=jax.ShapeDtypeStruct((n_rows, half), jnp.uint32),
        scratch_shapes=[pltpu.VMEM((MOE_BLOCK, half), jnp.uint32), pltpu.VMEM((2, SCATTER_TOKENS, half), jnp.uint32),
                        pltpu.SemaphoreType.DMA(()), pltpu.SemaphoreType.DMA((2,))],
        compiler_params=_cparams(("arbitrary",)),
    )(dest_t, bounds, h, norm_w.reshape(1, D_MODEL))


def _item_tables(bounds, n_j, n_items_max):
    items_e = bounds[3]
    first_e = (bounds[2] - items_e) * n_j
    end_e = bounds[2] * n_j
    n_items = end_e[N_EXPERTS - 1]
    g = jnp.minimum(jnp.arange(n_items_max, dtype=jnp.int32), n_items - 1)
    e = jnp.sum((g[:, None] >= end_e[None, :N_EXPERTS]).astype(jnp.int32), axis=1)
    own = e[:, None] == jnp.arange(LANES, dtype=jnp.int32)[None, :]
    pick = lambda v: jnp.sum(jnp.where(own, v[None, :], 0), axis=1)
    first = pick(first_e)
    nb = jnp.maximum(pick(items_e), 1)
    q = g - first
    j = q // nb
    i = q % nb
    full = pick(bounds[4])
    mid = pick(bounds[5])
    code = jnp.where(i < full, 0, jnp.where((i == full) & (mid > 0), 1, 2))
    per_full = MOE_ITEM // MOE_BLOCK
    offset = jnp.where(code == 0, i * per_full, full * per_full + jnp.where(code == 2, mid * (per_full // 2), 0))
    row = pick(bounds[0] // MOE_BLOCK) + offset
    meta = e * 32 + code * 8 + j * 2 + (i == 0).astype(jnp.int32)
    group_end = first + (j + 1) * nb
    counts = jnp.stack([n_items, bounds[1, N_EXPERTS - 1] // MOE_BLOCK])
    return meta.astype(jnp.int32), row.astype(jnp.int32), group_end.astype(jnp.int32), counts.astype(jnp.int32)


def _meta_expert(m):
    return m >> 5


def _meta_tile(m):
    return (m >> 1) & 3


def _item_loop(meta_ref, row_ref, gend_ref, n_ref, src_ref, dst_ref, w_refs, ring, ring_sem, wb_ref,
               in_buf, out_buf, in_sem, out_sem, compute):
    n = n_ref[0]
    n_in = in_buf.shape[0]
    n_out = out_buf.shape[0]
    tn = out_buf.shape[2]
    n_j = dst_ref.shape[1] // tn
    n_mats = len(w_refs)
    cpm = W_CHUNKS // n_mats
    kc = w_refs[0].shape[1] // cpm

    def by_size(m, fn):
        for code, size in enumerate(MOE_ITEM_SIZES):
            @pl.when(((m >> 3) & 3) == code)
            def _(size=size):
                fn(size)

    def rows(g, size):
        return pl.ds(pl.multiple_of(row_ref[g] * MOE_BLOCK, MOE_BLOCK), size)

    def cols(m):
        return pl.ds(pl.multiple_of(_meta_tile(m) * tn, tn), tn)

    def in_copy(g, size):
        slot = g % n_in
        return pltpu.make_async_copy(src_ref.at[rows(g, size)], in_buf.at[slot, pl.ds(0, size)], in_sem.at[slot])

    def out_copy(g, size):
        slot = g % n_out
        return pltpu.make_async_copy(out_buf.at[slot, pl.ds(0, size)], dst_ref.at[rows(g, size), cols(meta_ref[g])],
                                     out_sem.at[slot])

    def start_in(g):
        by_size(meta_ref[g], lambda size: in_copy(g, size).start(priority=ROW_DMA_PRIORITY))

    def wait_in(g):
        by_size(meta_ref[g], lambda size: in_copy(g, size).wait())

    def start_out(g):
        by_size(meta_ref[g], lambda size: out_copy(g, size).start(priority=ROW_DMA_PRIORITY))

    def wait_out(g):
        by_size(meta_ref[g], lambda size: out_copy(g, size).wait())

    def krows(c):
        r0 = (c % cpm) * kc
        return pl.ds(r0 if isinstance(c, int) else pl.multiple_of(r0, kc), kc)

    def chunk_copy(m, c, k):
        slot = c % W_RING
        return pltpu.make_async_copy(w_refs[k].at[_meta_expert(m), krows(c), cols(m)], ring.at[slot],
                                     ring_sem.at[slot])

    def for_matrix(c, fn):
        if isinstance(c, int):
            return fn(c // cpm)
        for k in range(n_mats):
            @pl.when(c // cpm == k)
            def _(k=k):
                fn(k)

    def start_chunk(m, c):
        for_matrix(c, lambda k: chunk_copy(m, c, k).start())

    def finish_chunk(m, c, wslot):
        def fn(k):
            chunk_copy(m, c, k).wait()
            wb_ref[wslot, k, krows(c), :] = ring[c % W_RING].astype(BF16)
        for_matrix(c, fn)

        @pl.when(c + W_RING < W_CHUNKS)
        def _():
            start_chunk(m, c + W_RING)

    for c0 in range(W_RING):
        start_chunk(meta_ref[0], c0)
    for g0 in range(n_in - 1):
        @pl.when(g0 < n)
        def _():
            start_in(g0)

    def body(g, carry):
        parity, done, nm = carry
        m = meta_ref[g]
        first = (m & 1) == 1

        @pl.when(first)
        def _():
            lax.fori_loop(done, W_CHUNKS, lambda c, z: (finish_chunk(m, c, 1 - parity), z)[1], 0)
            nxt = gend_ref[g]

            @pl.when(nxt < n)
            def _():
                for c0 in range(W_RING):
                    start_chunk(meta_ref[nxt], c0)

        nxt = gend_ref[g]
        parity = jnp.where(first, 1 - parity, parity)
        done = jnp.where(first, 0, done)
        nm = jnp.where(first, jnp.where(nxt < n, meta_ref[jnp.minimum(nxt, n - 1)], -1), nm)

        wait_in(g)

        @pl.when(g + n_in - 1 < n)
        def _():
            start_in(g + n_in - 1)

        @pl.when(g >= n_out)
        def _():
            wait_out(g - n_out)

        def run(size):
            x = in_buf[g % n_in, pl.ds(0, size)]
            out_buf[g % n_out, pl.ds(0, size)] = compute(x, wb_ref.at[parity], m).astype(out_buf.dtype)

        by_size(m, run)
        start_out(g)

        active = (nm >= 0) & (done < W_CHUNKS)

        @pl.when(active)
        def _():
            finish_chunk(nm, done, 1 - parity)

        return parity, jnp.where(active, done + 1, done), nm

    lax.fori_loop(0, n, body, (jnp.int32(1), jnp.int32(0), meta_ref[0]))
    lax.fori_loop(jnp.maximum(n - n_out, 0), n, lambda g, c: (wait_out(g), c)[1], 0)

    n_blocks = dst_ref.shape[0] // MOE_BLOCK
    n_used = n_ref[1]
    out_buf[0] = jnp.zeros(out_buf.shape[1:], out_buf.dtype)
    for jt in range(n_j):
        def tail_copy(b, jt=jt):
            r = pl.ds(pl.multiple_of(b * MOE_BLOCK, MOE_BLOCK), MOE_BLOCK)
            return pltpu.make_async_copy(out_buf.at[0, pl.ds(0, MOE_BLOCK)], dst_ref.at[r, jt * tn:(jt + 1) * tn],
                                         out_sem.at[0])

        lax.fori_loop(n_used, n_blocks, lambda b, c: (tail_copy(b).start(), c)[1], 0)
        lax.fori_loop(n_used, n_blocks, lambda b, c: (tail_copy(b).wait(), c)[1], 0)


def _expert_up_kernel(meta_ref, row_ref, gend_ref, n_ref, x_ref, bg_ref, bu_ref, wg_ref, wu_ref, act_ref,
                      ring, wb_ref, x_buf, a_buf, ring_sem, in_sem, out_sem):
    n_j = D_FF // MOE_UP_TN

    def compute(x, w, m):
        tile = _meta_expert(m) * n_j + _meta_tile(m)
        xb = _unpack_halves(x)
        g = jnp.dot(xb, w[0], preferred_element_type=F32) + bg_ref[pl.ds(tile, 1), :]
        u = jnp.dot(xb, w[1], preferred_element_type=F32) + bu_ref[pl.ds(tile, 1), :]
        g = jnp.minimum(g, SWIGLU_LIMIT)
        u = jnp.clip(u, -SWIGLU_LIMIT, SWIGLU_LIMIT)
        return g * _sigmoid(SWIGLU_ALPHA * g) * (u + 1.0)

    _item_loop(meta_ref, row_ref, gend_ref, n_ref, x_ref, act_ref, (wg_ref, wu_ref), ring, ring_sem, wb_ref,
               x_buf, a_buf, in_sem, out_sem, compute)


def _expert_down_kernel(meta_ref, row_ref, gend_ref, n_ref, a_ref, bd_ref, wd_ref, y_ref,
                        ring, wb_ref, a_buf, y_buf, ring_sem, in_sem, out_sem):
    n_j = D_MODEL // MOE_DOWN_TN

    def compute(a, w, m):
        tile = _meta_expert(m) * n_j + _meta_tile(m)
        return jnp.dot(a, w[0], preferred_element_type=F32) + bd_ref[pl.ds(tile, 1), :]

    _item_loop(meta_ref, row_ref, gend_ref, n_ref, a_ref, y_ref, (wd_ref,), ring, ring_sem, wb_ref,
               a_buf, y_buf, in_sem, out_sem, compute)


def _experts(x_rows, bounds, w_gate, b_gate, w_up, b_up, w_down, b_down):
    n_rows = x_rows.shape[0]
    n_items_max = n_rows // MOE_ITEM + (len(MOE_ITEM_SIZES) - 1) * N_EXPERTS
    any_spec = pl.BlockSpec(memory_space=pl.ANY)
    sems = [pltpu.SemaphoreType.DMA((W_RING,)), pltpu.SemaphoreType.DMA((MOE_IN_BUFS,)),
            pltpu.SemaphoreType.DMA((MOE_OUT_BUFS,))]

    def full(shape):
        return pl.BlockSpec(shape, lambda i, *_: (0,) * len(shape))

    n_j = D_FF // MOE_UP_TN
    act = pl.pallas_call(
        _expert_up_kernel,
        grid_spec=pltpu.PrefetchScalarGridSpec(
            num_scalar_prefetch=4, grid=(1,),
            in_specs=[any_spec, full((N_EXPERTS * n_j, MOE_UP_TN)),
                      full((N_EXPERTS * n_j, MOE_UP_TN)), any_spec, any_spec],
            out_specs=any_spec,
            scratch_shapes=[pltpu.VMEM((W_RING, 2 * D_MODEL // W_CHUNKS, MOE_UP_TN), F32),
                            pltpu.VMEM((2, 2, D_MODEL, MOE_UP_TN), BF16),
                            pltpu.VMEM((MOE_IN_BUFS, MOE_ITEM, D_MODEL // 2), jnp.uint32),
                            pltpu.VMEM((MOE_OUT_BUFS, MOE_ITEM, MOE_UP_TN), BF16)] + sems),
        out_shape=jax.ShapeDtypeStruct((n_rows, D_FF), BF16),
        compiler_params=_cparams(("arbitrary",)),
    )(*_item_tables(bounds, n_j, n_items_max * n_j), x_rows,
      b_gate.reshape(N_EXPERTS * n_j, MOE_UP_TN), b_up.reshape(N_EXPERTS * n_j, MOE_UP_TN), w_gate, w_up)

    n_j = D_MODEL // MOE_DOWN_TN
    return pl.pallas_call(
        _expert_down_kernel,
        grid_spec=pltpu.PrefetchScalarGridSpec(
            num_scalar_prefetch=4, grid=(1,),
            in_specs=[any_spec, full((N_EXPERTS * n_j, MOE_DOWN_TN)), any_spec],
            out_specs=any_spec,
            scratch_shapes=[pltpu.VMEM((W_RING, D_FF // W_CHUNKS, MOE_DOWN_TN), F32),
                            pltpu.VMEM((2, 1, D_FF, MOE_DOWN_TN), BF16),
                            pltpu.VMEM((MOE_IN_BUFS, MOE_ITEM, D_FF), BF16),
                            pltpu.VMEM((MOE_OUT_BUFS, MOE_ITEM, MOE_DOWN_TN), F32)] + sems),
        out_shape=jax.ShapeDtypeStruct((n_rows, D_MODEL), F32),
        compiler_params=_cparams(("arbitrary",)),
    )(*_item_tables(bounds, n_j, n_items_max * n_j), act,
      b_down.reshape(N_EXPERTS * n_j, MOE_DOWN_TN), w_down)


def _combine_kernel(dest_ref, next_ref, h_ref, gate_ref, nw_ref, y_ref, o_ref, buf_ref, sem):
    i = pl.program_id(0)
    steps = pl.num_programs(0)

    def issue(d_ref, slot):
        def body(r, carry):
            for k in range(TOP_K):
                pltpu.make_async_copy(y_ref.at[pl.ds(d_ref[k, r], 1)], buf_ref.at[slot, k, pl.ds(r, 1)],
                                      sem.at[slot]).start(priority=k % 2)
            return carry
        lax.fori_loop(0, COMBINE_TOKENS, body, 0)

    @pl.when(i == 0)
    def _():
        issue(dest_ref, 0)

    @pl.when(i + 1 < steps)
    def _():
        issue(next_ref, (i + 1) % 2)

    slot = i % 2
    pltpu.make_async_copy(buf_ref.at[slot], buf_ref.at[slot], sem.at[slot]).wait()
    acc = h_ref[...]
    gate = gate_ref[...]
    for k in range(TOP_K):
        acc = acc + gate[:, k:k + 1] * buf_ref[slot, k]
    o_ref[...] = _rms(acc, nw_ref[...])


def _combine(h, y_rows, dest_t, gate_pad, norm_w):
    t = h.shape[0]
    steps = t // COMBINE_TOKENS
    return pl.pallas_call(
        _combine_kernel,
        grid=(steps,),
        in_specs=[pl.BlockSpec((SUBLANES, COMBINE_TOKENS), lambda i: (0, i), memory_space=pltpu.SMEM),
                  pl.BlockSpec((SUBLANES, COMBINE_TOKENS), lambda i: (0, jnp.minimum(i + 1, steps - 1)),
                               memory_space=pltpu.SMEM),
                  pl.BlockSpec((COMBINE_TOKENS, D_MODEL), lambda i: (i, 0)),
                  pl.BlockSpec((COMBINE_TOKENS, LANES), lambda i: (i, 0)),
                  pl.BlockSpec((1, D_MODEL), lambda i: (0, 0)),
                  pl.BlockSpec(memory_space=pl.ANY)],
        out_specs=pl.BlockSpec((COMBINE_TOKENS, D_MODEL), lambda i: (i, 0)),
        out_shape=jax.ShapeDtypeStruct((t, D_MODEL), F32),
        scratch_shapes=[pltpu.VMEM((2, TOP_K, COMBINE_TOKENS, D_MODEL), F32), pltpu.SemaphoreType.DMA((2,))],
        compiler_params=_cparams(("arbitrary",)),
    )(dest_t, dest_t, h, gate_pad, norm_w.reshape(1, D_MODEL), y_rows)


def _moe(h, norm_ffn, w_router, b_router, w_gate, b_gate, w_up, b_up, w_down, b_down, norm_final):
    t = h.shape[0]
    tk = t * TOP_K
    n_blocks = (tk + N_EXPERTS * (MOE_BLOCK - 1) + MOE_BLOCK - 1) // MOE_BLOCK
    n_rows = n_blocks * MOE_BLOCK
    idx_pad, gate_pad, pos_pad, cnt_pad = _router(h, norm_ffn, w_router, b_router)
    dest_t, bounds = _dest_rows(idx_pad, pos_pad, cnt_pad)
    x_rows = _scatter_rows(h, norm_ffn, dest_t, bounds, n_rows)
    y_rows = _experts(x_rows, bounds, w_gate, b_gate, w_up, b_up, w_down, b_down)
    return _combine(h, y_rows, dest_t, gate_pad, norm_final)


def kernel(x, mem, norm_mix, w_in, conv_w, conv_b, dt_bias, a_log, d_skip, ssd_norm, pool_w, pool_scale,
           w_out, norm_xattn, norm_mem, w_q, w_kv, w_o, norm_ffn, w_router, b_router, w_gate, b_gate,
           w_up, b_up, w_down, b_down, norm_final):
    batch, seq, d = x.shape
    t = batch * seq
    h = x.reshape(t, d)
    depth = norm_mix.shape[0]
    assert depth == 1, "the final norm is fused into the MoE combine of the single layer"
    for l in range(depth):
        dt_lo = D_SSD + D_CONV
        w_in_t = jnp.swapaxes(w_in[l], 0, 1)
        w_dt_t = jnp.pad(w_in_t[dt_lo:dt_lo + SSD_HEADS], ((0, LANES - SSD_HEADS), (0, 0)))
        proj = _mm(h, w_in_t, norm_w=norm_mix[l], n_cols=dt_lo, w_transposed=True)
        proj_u = _mm(h, w_in_t[dt_lo + SSD_HEADS:], norm_w=norm_mix[l], w_transposed=True)
        dtp = _mm(h, w_dt_t, norm_w=norm_mix[l], w_transposed=True)
        mix = _mixer(proj, proj_u, dtp, conv_w[l], conv_b[l], dt_bias[l], a_log[l], d_skip[l], ssd_norm[l],
                     pool_w[l], pool_scale[l], batch, seq)
        h = _mm(mix, w_out[l], res=h)

        q = _mm(h, w_q[l], norm_w=norm_xattn[l], out_dtype=BF16)
        kv = _mm(mem.reshape(batch * N_MEM, d), w_kv[l], norm_w=norm_mem[l], out_dtype=BF16)
        o = _attn(q, kv, batch, seq)
        h = _mm(o, w_o[l], res=h)

        out = _moe(h, norm_ffn[l], w_router[l], b_router[l], w_gate[l], b_gate[l], w_up[l], b_up[l],
                   w_down[l], b_down[l], norm_final)
    return out.reshape(batch, seq, d)
```

```python
import functools

import jax
import jax.numpy as jnp
from jax import lax
from jax.experimental import pallas as pl
from jax.experimental.pallas import tpu as pltpu

F32 = jnp.float32
BF16 = jnp.bfloat16

D_MODEL = 2048
D_SSD = 1024
SSD_HEAD_DIM = 64
SSD_HEADS = 16
SSD_GROUPS = 4
HEADS_PER_GROUP = SSD_HEADS // SSD_GROUPS
D_STATE = 128
CONV_WIDTH = 4
CHUNK = 128
D_CONV = D_SSD + 2 * SSD_GROUPS * D_STATE
D_POOL = 1024
POOL_WINDOWS = (2, 4, 8, 16)
POOL_GROUP_DIM = 256
N_MEM = 256
XATTN_HEADS = 4
XATTN_HEAD_DIM = 512
N_EXPERTS = 32
TOP_K = 4
D_FF = 2048
SWIGLU_LIMIT = 7.0
SWIGLU_ALPHA = 1.702
EPS = 1e-5

LANES = 128
SUBLANES = 8
VMEM_LIMIT = 56 * 1024 * 1024
NEG_BIG = -1e30

MOE_BLOCK = 128
MOE_ITEM_SIZES = (4 * MOE_BLOCK, 2 * MOE_BLOCK, MOE_BLOCK)
MOE_ITEM = MOE_ITEM_SIZES[0]
MOE_UP_TN = 512
MOE_DOWN_TN = 1024
MOE_IN_BUFS = 3
MOE_OUT_BUFS = 2
W_CHUNKS = 4
W_RING = 4
ROUTER_TOKENS = 512
SCATTER_TOKENS = 512
COMBINE_TOKENS = 256
ROW_DMA_PRIORITY = 1


def _cparams(sem):
    return pltpu.CompilerParams(dimension_semantics=sem, vmem_limit_bytes=VMEM_LIMIT)


def _rms(x, w):
    ms = jnp.mean(x * x, axis=-1, keepdims=True)
    return x * lax.rsqrt(ms + EPS) * w


def _mm_kernel(*refs, has_norm, has_res, w_transposed):
    it = iter(refs)
    a_ref = next(it)
    w_ref = next(it)
    nw_ref = next(it) if has_norm else None
    r_ref = next(it) if has_res else None
    o_ref = next(it)
    wb_ref = next(it)

    @pl.when(pl.program_id(1) == 0)
    def _():
        w = w_ref[...]
        wb_ref[...] = (w.T if w_transposed else w).astype(BF16)

    a = a_ref[...]
    if has_norm:
        a = _rms(a.astype(F32), nw_ref[...])
    acc = jnp.dot(a.astype(BF16), wb_ref[...], preferred_element_type=F32)
    if has_res:
        acc = acc + r_ref[...]
    o_ref[...] = acc.astype(o_ref.dtype)


def _mm(a, w, norm_w=None, res=None, out_dtype=F32, tm=1024, tn=1024, n_cols=None, w_transposed=False):
    m, k = a.shape
    n = w.shape[0 if w_transposed else 1] if n_cols is None else n_cols
    tn = min(tn, n)
    tm = min(tm, m)
    assert m % tm == 0 and n % tn == 0
    w_spec = (pl.BlockSpec((tn, k), lambda j, i: (j, 0)) if w_transposed
              else pl.BlockSpec((k, tn), lambda j, i: (0, j)))
    in_specs = [pl.BlockSpec((tm, k), lambda j, i: (i, 0)), w_spec]
    args = [a, w]
    if norm_w is not None:
        in_specs.append(pl.BlockSpec((1, k), lambda j, i: (0, 0)))
        args.append(norm_w.reshape(1, k))
    if res is not None:
        in_specs.append(pl.BlockSpec((tm, tn), lambda j, i: (i, j)))
        args.append(res)
    return pl.pallas_call(
        functools.partial(_mm_kernel, has_norm=norm_w is not None, has_res=res is not None,
                          w_transposed=w_transposed),
        grid=(n // tn, m // tm),
        in_specs=in_specs,
        out_specs=pl.BlockSpec((tm, tn), lambda j, i: (i, j)),
        out_shape=jax.ShapeDtypeStruct((m, n), out_dtype),
        scratch_shapes=[pltpu.VMEM((k, tn), BF16)],
        compiler_params=_cparams(("arbitrary", "arbitrary")),
    )(*args)


def _sigmoid(x):
    return 1.0 / (1.0 + jnp.exp(-x))


def _shift_rows(cur, prev, j, rows):
    return jnp.where(rows < j, pltpu.roll(prev, j, 0), pltpu.roll(cur, j, 0))


def _conv_silu(cur, prev, w_ref, b_ref, col0, rows):
    width = cur.shape[1]
    acc = cur * w_ref[CONV_WIDTH - 1:CONV_WIDTH, col0:col0 + width] + b_ref[:, col0:col0 + width]
    for j in range(1, CONV_WIDTH):
        k = CONV_WIDTH - 1 - j
        acc = acc + _shift_rows(cur, prev, j, rows) * w_ref[k:k + 1, col0:col0 + width]
    return acc * _sigmoid(acc)


def _mixer_kernel(z_ref, x_ref, bc_ref, u_ref, dt_ref, convw_ref, convb_ref, dtb_ref, alog_ref,
                  dskip_ref, ssdn_ref, poolw_ref, pscale_ref, out_ref,
                  px_ref, pbc_ref, pu_ref, state_ref):
    c = pl.program_id(1)

    @pl.when(c == 0)
    def _():
        px_ref[...] = jnp.zeros_like(px_ref)
        pbc_ref[...] = jnp.zeros_like(pbc_ref)
        pu_ref[...] = jnp.zeros_like(pu_ref)
        state_ref[...] = jnp.zeros_like(state_ref)

    rows = lax.broadcasted_iota(jnp.int32, (CHUNK, D_SSD), 0)
    x_raw = x_ref[...]
    bc_raw = bc_ref[...]
    xs = _conv_silu(x_raw, px_ref[...], convw_ref, convb_ref, 0, rows)
    bcs = _conv_silu(bc_raw, pbc_ref[...], convw_ref, convb_ref, D_SSD, rows)
    px_ref[...] = x_raw
    pbc_ref[...] = bc_raw

    dt_in = dt_ref[...] + dtb_ref[...]
    dt = jnp.maximum(dt_in, 0.0) + jnp.log(1.0 + jnp.exp(-jnp.abs(dt_in)))
    a = -jnp.exp(alog_ref[...])
    dta = dt * a
    ri = lax.broadcasted_iota(jnp.int32, (CHUNK, CHUNK), 0)
    ci = lax.broadcasted_iota(jnp.int32, (CHUNK, CHUNK), 1)
    causal = ri >= ci
    tril = jnp.where(causal, 1.0, 0.0).astype(F32)
    acs = jnp.dot(tril, dta, preferred_element_type=F32, precision=lax.Precision.HIGHEST)
    acs_t = acs.T
    dt_t = dt.T
    lo = ci < SSD_HEAD_DIM

    y_pairs = []
    for g in range(SSD_GROUPS):
        bg = bcs[:, g * D_STATE:(g + 1) * D_STATE].astype(BF16)
        cg = bcs[:, SSD_GROUPS * D_STATE + g * D_STATE:SSD_GROUPS * D_STATE + (g + 1) * D_STATE].astype(BF16)
        cb = lax.dot_general(cg, bg, (((1,), (1,)), ((), ())), preferred_element_type=F32)
        gw = HEADS_PER_GROUP * SSD_HEAD_DIM
        prev_g = state_ref[g * gw:(g + 1) * gw, :]
        yoff = lax.dot_general(cg, prev_g.astype(BF16), (((1,), (1,)), ((), ())),
                               preferred_element_type=F32)
        xw_parts = []
        for pr in range(HEADS_PER_GROUP // 2):
            pair = g * (HEADS_PER_GROUP // 2) + pr
            xs_pair = xs[:, pair * LANES:(pair + 1) * LANES]
            y_pair = None
            ecols = []
            wcols = []
            for q in range(2):
                h = pair * 2 + q
                col = acs[:, h:h + 1]
                seg = col - acs_t[h:h + 1, :]
                decay = jnp.exp(jnp.where(causal, seg, NEG_BIG))
                mat = (cb * decay * dt_t[h:h + 1, :]).astype(BF16)
                keep = lo if q == 0 else jnp.logical_not(lo)
                xh = jnp.where(keep, xs_pair, 0.0).astype(BF16)
                yd = jnp.dot(mat, xh, preferred_element_type=F32)
                y_pair = yd if y_pair is None else y_pair + yd
                ecols.append(jnp.exp(col))
                wcols.append(dt[:, h:h + 1] * jnp.exp(acs[CHUNK - 1:CHUNK, h:h + 1] - col))
            e_pair = jnp.where(lo, ecols[0], ecols[1])
            w_pair = jnp.where(lo, wcols[0], wcols[1])
            y_pairs.append(y_pair + yoff[:, pr * LANES:(pr + 1) * LANES] * e_pair)
            xw_parts.append(xs_pair * w_pair)
        xw = jnp.concatenate(xw_parts, axis=1).astype(BF16)
        s_new = lax.dot_general(xw, bg, (((0,), (0,)), ((), ())), preferred_element_type=F32)
        for r in range(HEADS_PER_GROUP):
            h = g * HEADS_PER_GROUP + r
            dec = jnp.exp(acs[CHUNK - 1:CHUNK, h:h + 1])
            lo_r = h * SSD_HEAD_DIM
            state_ref[lo_r:lo_r + SSD_HEAD_DIM, :] = (
                state_ref[lo_r:lo_r + SSD_HEAD_DIM, :] * dec
                + s_new[r * SSD_HEAD_DIM:(r + 1) * SSD_HEAD_DIM, :])

    y = jnp.concatenate(y_pairs, axis=1)
    y = y + dskip_ref[...] * xs
    z = z_ref[...]
    y = y * (z * _sigmoid(z))
    gdim = D_SSD // SSD_GROUPS
    for g in range(SSD_GROUPS):
        yg = _rms(y[:, g * gdim:(g + 1) * gdim], ssdn_ref[:, g * gdim:(g + 1) * gdim])
        out_ref[:, g * gdim:(g + 1) * gdim] = yg.astype(out_ref.dtype)

    u_cur = u_ref[...]
    u_prev = pu_ref[...]
    pos = (c * CHUNK + lax.broadcasted_iota(jnp.int32, (CHUNK, POOL_GROUP_DIM), 0) + 1).astype(F32)
    for g, win in enumerate(POOL_WINDOWS):
        sl = slice(g * POOL_GROUP_DIM, (g + 1) * POOL_GROUP_DIM)
        ug = u_cur[:, sl]
        ext = jnp.concatenate([u_prev[:, sl], ug], axis=0)
        step = 1
        while step < win:
            ext = ext + pltpu.roll(ext, step, 0)
            step *= 2
        pooled = ext[CHUNK:, :] / jnp.minimum(pos, float(win)) - ug
        mixed = jnp.dot(pooled.astype(BF16), poolw_ref[g].astype(BF16), preferred_element_type=F32)
        out_ref[:, D_SSD + g * POOL_GROUP_DIM:D_SSD + (g + 1) * POOL_GROUP_DIM] = (
            mixed * pscale_ref[:, sl]).astype(out_ref.dtype)
    pu_ref[...] = u_cur


def _mixer(proj, proj_u, dtp, conv_w, conv_b, dt_bias, a_log, d_skip, ssd_norm, pool_w, pool_scale, batch, seq):
    nchunk = seq // CHUNK
    row_map = lambda col: (lambda b, c: (b * nchunk + c, col))
    const2 = lambda b, c: (0, 0)
    pad = LANES - SSD_HEADS
    dtb = jnp.pad(dt_bias, (0, pad)).reshape(1, LANES)
    alog = jnp.pad(a_log, (0, pad)).reshape(1, LANES)
    dskip = jnp.repeat(d_skip, SSD_HEAD_DIM).reshape(1, D_SSD)
    return pl.pallas_call(
        _mixer_kernel,
        grid=(batch, nchunk),
        in_specs=[
            pl.BlockSpec((CHUNK, D_SSD), row_map(0)),
            pl.BlockSpec((CHUNK, D_SSD), row_map(1)),
            pl.BlockSpec((CHUNK, D_SSD), row_map(2)),
            pl.BlockSpec((CHUNK, D_POOL), row_map(0)),
            pl.BlockSpec((CHUNK, LANES), row_map(0)),
            pl.BlockSpec((CONV_WIDTH, D_CONV), const2),
            pl.BlockSpec((1, D_CONV), const2),
            pl.BlockSpec((1, LANES), const2),
            pl.BlockSpec((1, LANES), const2),
            pl.BlockSpec((1, D_SSD), const2),
            pl.BlockSpec((1, D_SSD), const2),
            pl.BlockSpec((len(POOL_WINDOWS), POOL_GROUP_DIM, POOL_GROUP_DIM), lambda b, c: (0, 0, 0)),
            pl.BlockSpec((1, D_POOL), const2),
        ],
        out_specs=pl.BlockSpec((CHUNK, D_MODEL), lambda b, c: (b * nchunk + c, 0)),
        out_shape=jax.ShapeDtypeStruct((batch * seq, D_MODEL), BF16),
        scratch_shapes=[pltpu.VMEM((CHUNK, D_SSD), F32), pltpu.VMEM((CHUNK, D_SSD), F32),
                        pltpu.VMEM((CHUNK, D_POOL), F32), pltpu.VMEM((SSD_HEADS * SSD_HEAD_DIM, D_STATE), F32)],
        compiler_params=_cparams(("arbitrary", "arbitrary")),
    )(proj, proj, proj, proj_u, dtp, conv_w, conv_b.reshape(1, D_CONV), dtb, alog, dskip,
      ssd_norm.reshape(1, D_SSD), pool_w, pool_scale.reshape(1, D_POOL))


def _attn_kernel(q_ref, k_ref, v_ref, o_ref):
    scale = XATTN_HEAD_DIM ** -0.5
    for h in range(XATTN_HEADS):
        sl = slice(h * XATTN_HEAD_DIM, (h + 1) * XATTN_HEAD_DIM)
        s = lax.dot_general(q_ref[:, sl], k_ref[:, sl], (((1,), (1,)), ((), ())),
                            preferred_element_type=F32) * scale
        s = s - jnp.max(s, axis=-1, keepdims=True)
        p = jnp.exp(s)
        p = p / jnp.sum(p, axis=-1, keepdims=True)
        o_ref[:, sl] = jnp.dot(p.astype(BF16), v_ref[:, sl], preferred_element_type=F32).astype(o_ref.dtype)


def _attn(q, kv, batch, seq, tq=512):
    nq = seq // tq
    return pl.pallas_call(
        _attn_kernel,
        grid=(batch, nq),
        in_specs=[pl.BlockSpec((tq, D_MODEL), lambda b, i: (b * nq + i, 0)),
                  pl.BlockSpec((N_MEM, D_MODEL), lambda b, i: (b, 0)),
                  pl.BlockSpec((N_MEM, D_MODEL), lambda b, i: (b, 1))],
        out_specs=pl.BlockSpec((tq, D_MODEL), lambda b, i: (b * nq + i, 0)),
        out_shape=jax.ShapeDtypeStruct((batch * seq, D_MODEL), BF16),
        compiler_params=_cparams(("arbitrary", "arbitrary")),
    )(q, kv, kv)


def _router_kernel(h_ref, nw_ref, wr_ref, br_ref, idx_ref, gate_ref, pos_ref, cnt_ref, run_ref):
    @pl.when(pl.program_id(0) == 0)
    def _():
        run_ref[...] = jnp.zeros_like(run_ref)

    hn = _rms(h_ref[...], nw_ref[...])
    w = wr_ref[...]
    h_hi = hn.astype(BF16)
    h_lo = (hn - h_hi.astype(F32)).astype(BF16)
    w_hi = w.astype(BF16)
    w_lo = (w - w_hi.astype(F32)).astype(BF16)
    logits = (jnp.dot(h_hi, w_hi, preferred_element_type=F32)
              + (jnp.dot(h_hi, w_lo, preferred_element_type=F32)
                 + jnp.dot(h_lo, w_hi, preferred_element_type=F32))) + br_ref[...]
    tm = logits.shape[0]
    lane = lax.broadcasted_iota(jnp.int32, logits.shape, 1)
    vals = []
    hots = []
    idx_out = jnp.zeros(logits.shape, jnp.int32)
    for k in range(TOP_K):
        m = jnp.max(logits, axis=-1, keepdims=True)
        idx = jnp.min(jnp.where(logits == m, lane, LANES), axis=-1, keepdims=True)
        hit = lane == idx
        idx_out = jnp.where(lane == k, idx, idx_out)
        logits = jnp.where(hit, -jnp.inf, logits)
        vals.append(m)
        hots.append(jnp.where(hit, 1.0, 0.0))
    exps = [jnp.exp(v - vals[0]) for v in vals]
    den = exps[0] + exps[1] + exps[2] + exps[3]
    gate_out = jnp.zeros(logits.shape, F32)
    for k in range(TOP_K):
        gate_out = jnp.where(lane == k, exps[k] / den, gate_out)
    idx_ref[...] = idx_out
    gate_ref[...] = gate_out

    hot = hots[0] + hots[1] + hots[2] + hots[3]
    ri = lax.broadcasted_iota(jnp.int32, (tm, tm), 0)
    ci = lax.broadcasted_iota(jnp.int32, (tm, tm), 1)
    before = jnp.where(ri > ci, 1.0, 0.0).astype(BF16)
    ahead = jnp.dot(before, hot.astype(BF16), preferred_element_type=F32) + run_ref[...]
    pos_out = jnp.zeros(logits.shape, F32)
    for k in range(TOP_K):
        pos_out = jnp.where(lane == k, jnp.sum(ahead * hots[k], axis=-1, keepdims=True), pos_out)
    pos_ref[...] = pos_out
    run_ref[...] = run_ref[...] + jnp.sum(hot, axis=0, keepdims=True)
    cnt_ref[...] = run_ref[...]


def _router(h, norm_w, w_router, b_router):
    t = h.shape[0]
    tm = ROUTER_TOKENS
    pad = LANES - N_EXPERTS
    wr = jnp.pad(w_router, ((0, 0), (0, pad)))
    br = jnp.pad(b_router, (0, pad), constant_values=NEG_BIG).reshape(1, LANES)
    tok_spec = pl.BlockSpec((tm, LANES), lambda i: (i, 0))
    return pl.pallas_call(
        _router_kernel,
        grid=(t // tm,),
        in_specs=[pl.BlockSpec((tm, D_MODEL), lambda i: (i, 0)),
                  pl.BlockSpec((1, D_MODEL), lambda i: (0, 0)),
                  pl.BlockSpec((D_MODEL, LANES), lambda i: (0, 0)),
                  pl.BlockSpec((1, LANES), lambda i: (0, 0))],
        out_specs=[tok_spec, tok_spec, tok_spec, pl.BlockSpec((1, LANES), lambda i: (0, 0))],
        out_shape=[jax.ShapeDtypeStruct((t, LANES), jnp.int32),
                   jax.ShapeDtypeStruct((t, LANES), F32),
                   jax.ShapeDtypeStruct((t, LANES), F32),
                   jax.ShapeDtypeStruct((1, LANES), F32)],
        scratch_shapes=[pltpu.VMEM((1, LANES), F32)],
        compiler_params=_cparams(("arbitrary",)),
    )(h, norm_w.reshape(1, D_MODEL), wr, br)


def _dest_kernel(idx_ref, pos_ref, cnt_ref, dest_ref, bounds_ref):
    cnt = jnp.broadcast_to(cnt_ref[...], (SUBLANES, LANES))
    padded = jnp.floor((cnt + (MOE_BLOCK - 1)) * (1.0 / MOE_BLOCK)) * MOE_BLOCK
    ri = lax.broadcasted_iota(jnp.int32, (LANES, LANES), 0)
    ci = lax.broadcasted_iota(jnp.int32, (LANES, LANES), 1)
    upto = jnp.where(ri <= ci, 1.0, 0.0).astype(F32)
    seg_end = jnp.dot(padded, upto, preferred_element_type=F32, precision=lax.Precision.HIGHEST)
    seg_start = seg_end - padded
    blocks = padded * (1.0 / MOE_BLOCK)
    full = jnp.floor(blocks * (1.0 * MOE_BLOCK / MOE_ITEM))
    rem = blocks - full * (MOE_ITEM // MOE_BLOCK)
    mid = jnp.floor(rem * 0.5)
    items = full + mid + (rem - 2.0 * mid)
    items_end = jnp.dot(items, upto, preferred_element_type=F32, precision=lax.Precision.HIGHEST)
    row = lax.broadcasted_iota(jnp.int32, (SUBLANES, LANES), 0)
    table = jnp.zeros((SUBLANES, LANES), F32)
    for r, v in enumerate((seg_start, seg_end, items_end, items, full, mid)):
        table = jnp.where(row == r, v, table)
    bounds_ref[...] = table.astype(jnp.int32)

    idx = idx_ref[...]
    lane = lax.broadcasted_iota(jnp.int32, idx.shape, 1)
    out = pos_ref[...]
    for k in range(TOP_K):
        hit = lane == idx[:, k:k + 1]
        base = jnp.sum(jnp.where(hit, seg_start[0:1, :], 0.0), axis=-1, keepdims=True)
        out = jnp.where(lane == k, out + base, out)
    dest_ref[...] = out.T[0:SUBLANES, :].astype(jnp.int32)


def _dest_rows(idx_pad, pos_pad, cnt_pad):
    t = idx_pad.shape[0]
    tm = ROUTER_TOKENS
    tok_spec = pl.BlockSpec((tm, LANES), lambda i: (i, 0))
    return pl.pallas_call(
        _dest_kernel,
        grid=(t // tm,),
        in_specs=[tok_spec, tok_spec, pl.BlockSpec((1, LANES), lambda i: (0, 0))],
        out_specs=[pl.BlockSpec((SUBLANES, tm), lambda i: (0, i)),
                   pl.BlockSpec((SUBLANES, LANES), lambda i: (0, 0))],
        out_shape=[jax.ShapeDtypeStruct((SUBLANES, t), jnp.int32),
                   jax.ShapeDtypeStruct((SUBLANES, LANES), jnp.int32)],
        compiler_params=_cparams(("arbitrary",)),
    )(idx_pad, pos_pad, cnt_pad)


def _pack_halves(x):
    c = x.shape[1] // 2
    lo = pltpu.bitcast(x[:, :c].astype(BF16).astype(F32), jnp.uint32)
    hi = pltpu.bitcast(x[:, c:].astype(BF16).astype(F32), jnp.uint32)
    return (lo >> 16) | hi


def _unpack_halves(w):
    lo = pltpu.bitcast(w << 16, F32)
    hi = pltpu.bitcast(w & jnp.uint32(0xFFFF0000), F32)
    return jnp.concatenate([lo, hi], axis=1).astype(BF16)


def _scatter_kernel(dest_ref, bounds_ref, h_ref, nw_ref, x_ref, zero_ref, packed_ref, zsem, sem):
    n_blocks = x_ref.shape[0] // MOE_BLOCK
    i = pl.program_id(0)
    slot = i % 2
    packed_ref[slot] = _pack_halves(_rms(h_ref[...], nw_ref[...]))

    @pl.when(i == 0)
    def _():
        zero_ref[...] = jnp.zeros_like(zero_ref)
        n_used = bounds_ref[1, N_EXPERTS - 1] // MOE_BLOCK

        def zero_copy(b):
            return pltpu.make_async_copy(zero_ref, x_ref.at[pl.ds(pl.multiple_of(b * MOE_BLOCK, MOE_BLOCK),
                                                                   MOE_BLOCK)], zsem)

        def last_block(e):
            return bounds_ref[1, e] // MOE_BLOCK - 1, bounds_ref[1, e] > bounds_ref[0, e]

        def start_e(e, carry):
            b, nonempty = last_block(e)

            @pl.when(nonempty)
            def _():
                zero_copy(b).start()
            return carry

        def wait_e(e, carry):
            b, nonempty = last_block(e)

            @pl.when(nonempty)
            def _():
                zero_copy(b).wait()
            return carry

        lax.fori_loop(0, N_EXPERTS, start_e, 0)
        lax.fori_loop(n_used, n_blocks, lambda b, c: (zero_copy(b).start(), c)[1], 0)
        lax.fori_loop(0, N_EXPERTS, wait_e, 0)
        lax.fori_loop(n_used, n_blocks, lambda b, c: (zero_copy(b).wait(), c)[1], 0)

    def issue(r, carry):
        for k in range(TOP_K):
            d = dest_ref[k, r]
            pltpu.make_async_copy(packed_ref.at[slot, pl.ds(r, 1)], x_ref.at[pl.ds(d, 1)],
                                  sem.at[slot]).start(priority=k % 2)
        return carry

    lax.fori_loop(0, SCATTER_TOKENS, issue, 0)

    def wait_rows(s):
        n_sent = SCATTER_TOKENS * TOP_K
        pltpu.make_async_copy(x_ref.at[pl.ds(0, n_sent)], x_ref.at[pl.ds(0, n_sent)], sem.at[s]).wait()

    @pl.when(i > 0)
    def _():
        wait_rows(1 - slot)

    @pl.when(i == pl.num_programs(0) - 1)
    def _():
        wait_rows(slot)


def _scatter_rows(h, norm_w, dest_t, bounds, n_rows):
    t = h.shape[0]
    steps = t // SCATTER_TOKENS
    half = D_MODEL // 2
    return pl.pallas_call(
        _scatter_kernel,
        grid=(steps,),
        in_specs=[pl.BlockSpec((SUBLANES, SCATTER_TOKENS), lambda i: (0, i), memory_space=pltpu.SMEM),
                  pl.BlockSpec(memory_space=pltpu.SMEM),
                  pl.BlockSpec((SCATTER_TOKENS, D_MODEL), lambda i: (i, 0)),
                  pl.BlockSpec((1, D_MODEL), lambda i: (0, 0))],
        out_specs=pl.BlockSpec(memory_space=pl.ANY),
        out_shape=jax.ShapeDtypeStruct((n_rows, half), jnp.uint32),
        scratch_shapes=[pltpu.VMEM((MOE_BLOCK, half), jnp.uint32), pltpu.VMEM((2, SCATTER_TOKENS, half), jnp.uint32),
                        pltpu.SemaphoreType.DMA(()), pltpu.SemaphoreType.DMA((2,))],
        compiler_params=_cparams(("arbitrary",)),
    )(dest_t, bounds, h, norm_w.reshape(1, D_MODEL))


def _item_tables(bounds, n_j, n_items_max):
    items_e = bounds[3]
    first_e = (bounds[2] - items_e) * n_j
    end_e = bounds[2] * n_j
    n_items = end_e[N_EXPERTS - 1]
    g = jnp.minimum(jnp.arange(n_items_max, dtype=jnp.int32), n_items - 1)
    e = jnp.sum((g[:, None] >= end_e[None, :N_EXPERTS]).astype(jnp.int32), axis=1)
    own = e[:, None] == jnp.arange(LANES, dtype=jnp.int32)[None, :]
    pick = lambda v: jnp.sum(jnp.where(own, v[None, :], 0), axis=1)
    first = pick(first_e)
    nb = jnp.maximum(pick(items_e), 1)
    q = g - first
    j = q // nb
    i = q % nb
    full = pick(bounds[4])
    mid = pick(bounds[5])
    code = jnp.where(i < full, 0, jnp.where((i == full) & (mid > 0), 1, 2))
    per_full = MOE_ITEM // MOE_BLOCK
    offset = jnp.where(code == 0, i * per_full, full * per_full + jnp.where(code == 2, mid * (per_full // 2), 0))
    row = pick(bounds[0] // MOE_BLOCK) + offset
    meta = e * 32 + code * 8 + j * 2 + (i == 0).astype(jnp.int32)
    group_end = first + (j + 1) * nb
    counts = jnp.stack([n_items, bounds[1, N_EXPERTS - 1] // MOE_BLOCK])
    return meta.astype(jnp.int32), row.astype(jnp.int32), group_end.astype(jnp.int32), counts.astype(jnp.int32)


def _meta_expert(m):
    return m >> 5


def _meta_tile(m):
    return (m >> 1) & 3


def _meta_size_code(m):
    return (m >> 3) & 3


def _item_loop(meta_ref, row_ref, gend_ref, n_ref, src_ref, dst_ref, w_refs, ring, ring_sem, wb_ref,
               in_buf, out_buf, in_sem, out_sem, compute):
    n = n_ref[0]
    n_in = in_buf.shape[0]
    n_out = out_buf.shape[0]
    tn = out_buf.shape[2]
    n_j = dst_ref.shape[1] // tn
    n_mats = len(w_refs)
    cpm = W_CHUNKS // n_mats
    kc = w_refs[0].shape[1] // cpm

    def by_size(m, fn):
        for code, size in enumerate(MOE_ITEM_SIZES):
            @pl.when(_meta_size_code(m) == code)
            def _(size=size):
                fn(size)

    def rows(g, size):
        return pl.ds(pl.multiple_of(row_ref[g] * MOE_BLOCK, MOE_BLOCK), size)

    def cols(m):
        return pl.ds(pl.multiple_of(_meta_tile(m) * tn, tn), tn)

    def in_copy(g, size):
        slot = g % n_in
        return pltpu.make_async_copy(src_ref.at[rows(g, size)], in_buf.at[slot, pl.ds(0, size)], in_sem.at[slot])

    def out_copy(g, size):
        slot = g % n_out
        return pltpu.make_async_copy(out_buf.at[slot, pl.ds(0, size)], dst_ref.at[rows(g, size), cols(meta_ref[g])],
                                     out_sem.at[slot])

    def start_in(g):
        by_size(meta_ref[g], lambda size: in_copy(g, size).start(priority=ROW_DMA_PRIORITY))

    def wait_in(g):
        by_size(meta_ref[g], lambda size: in_copy(g, size).wait())

    def start_out(g):
        by_size(meta_ref[g], lambda size: out_copy(g, size).start(priority=ROW_DMA_PRIORITY))

    def wait_out(g):
        by_size(meta_ref[g], lambda size: out_copy(g, size).wait())

    def krows(c):
        r0 = (c % cpm) * kc
        return pl.ds(r0 if isinstance(c, int) else pl.multiple_of(r0, kc), kc)

    def chunk_copy(m, c, k):
        slot = c % W_RING
        return pltpu.make_async_copy(w_refs[k].at[_meta_expert(m), krows(c), cols(m)], ring.at[slot],
                                     ring_sem.at[slot])

    def for_matrix(c, fn):
        if isinstance(c, int):
            return fn(c // cpm)
        for k in range(n_mats):
            @pl.when(c // cpm == k)
            def _(k=k):
                fn(k)

    def start_chunk(m, c):
        for_matrix(c, lambda k: chunk_copy(m, c, k).start())

    def finish_chunk(m, c, wslot):
        def fn(k):
            chunk_copy(m, c, k).wait()
            wb_ref[wslot, k, krows(c), :] = ring[c % W_RING].astype(BF16)
        for_matrix(c, fn)

        @pl.when(c + W_RING < W_CHUNKS)
        def _():
            start_chunk(m, c + W_RING)

    for c0 in range(W_RING):
        start_chunk(meta_ref[0], c0)
    for g0 in range(n_in - 1):
        @pl.when(g0 < n)
        def _():
            start_in(g0)

    def body(g, carry):
        parity, done, nm = carry
        m = meta_ref[g]
        first = (m & 1) == 1

        @pl.when(first)
        def _():
            lax.fori_loop(done, W_CHUNKS, lambda c, z: (finish_chunk(m, c, 1 - parity), z)[1], 0)
            nxt = gend_ref[g]

            @pl.when(nxt < n)
            def _():
                for c0 in range(W_RING):
                    start_chunk(meta_ref[nxt], c0)

        nxt = gend_ref[g]
        parity = jnp.where(first, 1 - parity, parity)
        done = jnp.where(first, 0, done)
        nm = jnp.where(first, jnp.where(nxt < n, meta_ref[jnp.minimum(nxt, n - 1)], -1), nm)

        wait_in(g)

        @pl.when(g + n_in - 1 < n)
        def _():
            start_in(g + n_in - 1)

        @pl.when(g >= n_out)
        def _():
            wait_out(g - n_out)

        def run(size):
            x = in_buf[g % n_in, pl.ds(0, size)]
            out_buf[g % n_out, pl.ds(0, size)] = compute(x, wb_ref.at[parity], m).astype(out_buf.dtype)

        by_size(m, run)
        start_out(g)

        active = (nm >= 0) & (done < W_CHUNKS)

        @pl.when(active)
        def _():
            finish_chunk(nm, done, 1 - parity)

        return parity, jnp.where(active, done + 1, done), nm

    lax.fori_loop(0, n, body, (jnp.int32(1), jnp.int32(0), meta_ref[0]))
    lax.fori_loop(jnp.maximum(n - n_out, 0), n, lambda g, c: (wait_out(g), c)[1], 0)

    n_blocks = dst_ref.shape[0] // MOE_BLOCK
    n_used = n_ref[1]
    out_buf[0] = jnp.zeros(out_buf.shape[1:], out_buf.dtype)
    for jt in range(n_j):
        def tail_copy(b, jt=jt):
            r = pl.ds(pl.multiple_of(b * MOE_BLOCK, MOE_BLOCK), MOE_BLOCK)
            return pltpu.make_async_copy(out_buf.at[0, pl.ds(0, MOE_BLOCK)], dst_ref.at[r, jt * tn:(jt + 1) * tn],
                                         out_sem.at[0])

        lax.fori_loop(n_used, n_blocks, lambda b, c: (tail_copy(b).start(), c)[1], 0)
        lax.fori_loop(n_used, n_blocks, lambda b, c: (tail_copy(b).wait(), c)[1], 0)


def _expert_up_kernel(meta_ref, row_ref, gend_ref, n_ref, x_ref, bg_ref, bu_ref, wg_ref, wu_ref, act_ref,
                      ring, wb_ref, x_buf, a_buf, ring_sem, in_sem, out_sem):
    n_j = D_FF // MOE_UP_TN

    def compute(x, w, m):
        tile = _meta_expert(m) * n_j + _meta_tile(m)
        xb = _unpack_halves(x)
        g = jnp.dot(xb, w[0], preferred_element_type=F32) + bg_ref[pl.ds(tile, 1), :]
        u = jnp.dot(xb, w[1], preferred_element_type=F32) + bu_ref[pl.ds(tile, 1), :]
        g = jnp.minimum(g, SWIGLU_LIMIT)
        u = jnp.clip(u, -SWIGLU_LIMIT, SWIGLU_LIMIT)
        return g * _sigmoid(SWIGLU_ALPHA * g) * (u + 1.0)

    _item_loop(meta_ref, row_ref, gend_ref, n_ref, x_ref, act_ref, (wg_ref, wu_ref), ring, ring_sem, wb_ref,
               x_buf, a_buf, in_sem, out_sem, compute)


def _expert_down_kernel(meta_ref, row_ref, gend_ref, n_ref, a_ref, bd_ref, wd_ref, y_ref,
                        ring, wb_ref, a_buf, y_buf, ring_sem, in_sem, out_sem):
    n_j = D_MODEL // MOE_DOWN_TN

    def compute(a, w, m):
        tile = _meta_expert(m) * n_j + _meta_tile(m)
        return jnp.dot(a, w[0], preferred_element_type=F32) + bd_ref[pl.ds(tile, 1), :]

    _item_loop(meta_ref, row_ref, gend_ref, n_ref, a_ref, y_ref, (wd_ref,), ring, ring_sem, wb_ref,
               a_buf, y_buf, in_sem, out_sem, compute)


def _experts(x_rows, bounds, w_gate, b_gate, w_up, b_up, w_down, b_down):
    n_rows = x_rows.shape[0]
    n_items_max = n_rows // MOE_ITEM + (len(MOE_ITEM_SIZES) - 1) * N_EXPERTS
    any_spec = pl.BlockSpec(memory_space=pl.ANY)
    sems = [pltpu.SemaphoreType.DMA((W_RING,)), pltpu.SemaphoreType.DMA((MOE_IN_BUFS,)),
            pltpu.SemaphoreType.DMA((MOE_OUT_BUFS,))]

    def full(shape):
        return pl.BlockSpec(shape, lambda i, *_: (0,) * len(shape))

    n_j = D_FF // MOE_UP_TN
    act = pl.pallas_call(
        _expert_up_kernel,
        grid_spec=pltpu.PrefetchScalarGridSpec(
            num_scalar_prefetch=4, grid=(1,),
            in_specs=[any_spec, full((N_EXPERTS * n_j, MOE_UP_TN)),
                      full((N_EXPERTS * n_j, MOE_UP_TN)), any_spec, any_spec],
            out_specs=any_spec,
            scratch_shapes=[pltpu.VMEM((W_RING, 2 * D_MODEL // W_CHUNKS, MOE_UP_TN), F32),
                            pltpu.VMEM((2, 2, D_MODEL, MOE_UP_TN), BF16),
                            pltpu.VMEM((MOE_IN_BUFS, MOE_ITEM, D_MODEL // 2), jnp.uint32),
                            pltpu.VMEM((MOE_OUT_BUFS, MOE_ITEM, MOE_UP_TN), BF16)] + sems),
        out_shape=jax.ShapeDtypeStruct((n_rows, D_FF), BF16),
        compiler_params=_cparams(("arbitrary",)),
    )(*_item_tables(bounds, n_j, n_items_max * n_j), x_rows,
      b_gate.reshape(N_EXPERTS * n_j, MOE_UP_TN), b_up.reshape(N_EXPERTS * n_j, MOE_UP_TN), w_gate, w_up)

    n_j = D_MODEL // MOE_DOWN_TN
    return pl.pallas_call(
        _expert_down_kernel,
        grid_spec=pltpu.PrefetchScalarGridSpec(
            num_scalar_prefetch=4, grid=(1,),
            in_specs=[any_spec, full((N_EXPERTS * n_j, MOE_DOWN_TN)), any_spec],
            out_specs=any_spec,
            scratch_shapes=[pltpu.VMEM((W_RING, D_FF // W_CHUNKS, MOE_DOWN_TN), F32),
                            pltpu.VMEM((2, 1, D_FF, MOE_DOWN_TN), BF16),
                            pltpu.VMEM((MOE_IN_BUFS, MOE_ITEM, D_FF), BF16),
                            pltpu.VMEM((MOE_OUT_BUFS, MOE_ITEM, MOE_DOWN_TN), F32)] + sems),
        out_shape=jax.ShapeDtypeStruct((n_rows, D_MODEL), F32),
        compiler_params=_cparams(("arbitrary",)),
    )(*_item_tables(bounds, n_j, n_items_max * n_j), act,
      b_down.reshape(N_EXPERTS * n_j, MOE_DOWN_TN), w_down)


def _combine_kernel(dest_ref, next_ref, h_ref, gate_ref, nw_ref, y_ref, o_ref, buf_ref, sem):
    i = pl.program_id(0)
    steps = pl.num_programs(0)

    def issue(d_ref, slot):
        def body(r, carry):
            for k in range(TOP_K):
                pltpu.make_async_copy(y_ref.at[pl.ds(d_ref[k, r], 1)], buf_ref.at[slot, k, pl.ds(r, 1)],
                                      sem.at[slot]).start(priority=k % 2)
            return carry
        lax.fori_loop(0, COMBINE_TOKENS, body, 0)

    @pl.when(i == 0)
    def _():
        issue(dest_ref, 0)

    @pl.when(i + 1 < steps)
    def _():
        issue(next_ref, (i + 1) % 2)

    slot = i % 2
    pltpu.make_async_copy(buf_ref.at[slot], buf_ref.at[slot], sem.at[slot]).wait()
    acc = h_ref[...]
    gate = gate_ref[...]
    for k in range(TOP_K):
        acc = acc + gate[:, k:k + 1] * buf_ref[slot, k]
    o_ref[...] = _rms(acc, nw_ref[...])


def _combine(h, y_rows, dest_t, gate_pad, norm_w):
    t = h.shape[0]
    steps = t // COMBINE_TOKENS
    return pl.pallas_call(
        _combine_kernel,
        grid=(steps,),
        in_specs=[pl.BlockSpec((SUBLANES, COMBINE_TOKENS), lambda i: (0, i), memory_space=pltpu.SMEM),
                  pl.BlockSpec((SUBLANES, COMBINE_TOKENS), lambda i: (0, jnp.minimum(i + 1, steps - 1)),
                               memory_space=pltpu.SMEM),
                  pl.BlockSpec((COMBINE_TOKENS, D_MODEL), lambda i: (i, 0)),
                  pl.BlockSpec((COMBINE_TOKENS, LANES), lambda i: (i, 0)),
                  pl.BlockSpec((1, D_MODEL), lambda i: (0, 0)),
                  pl.BlockSpec(memory_space=pl.ANY)],
        out_specs=pl.BlockSpec((COMBINE_TOKENS, D_MODEL), lambda i: (i, 0)),
        out_shape=jax.ShapeDtypeStruct((t, D_MODEL), F32),
        scratch_shapes=[pltpu.VMEM((2, TOP_K, COMBINE_TOKENS, D_MODEL), F32), pltpu.SemaphoreType.DMA((2,))],
        compiler_params=_cparams(("arbitrary",)),
    )(dest_t, dest_t, h, gate_pad, norm_w.reshape(1, D_MODEL), y_rows)


def _moe(h, norm_ffn, w_router, b_router, w_gate, b_gate, w_up, b_up, w_down, b_down, norm_final):
    t = h.shape[0]
    tk = t * TOP_K
    n_blocks = (tk + N_EXPERTS * (MOE_BLOCK - 1) + MOE_BLOCK - 1) // MOE_BLOCK
    n_rows = n_blocks * MOE_BLOCK
    idx_pad, gate_pad, pos_pad, cnt_pad = _router(h, norm_ffn, w_router, b_router)
    dest_t, bounds = _dest_rows(idx_pad, pos_pad, cnt_pad)
    x_rows = _scatter_rows(h, norm_ffn, dest_t, bounds, n_rows)
    y_rows = _experts(x_rows, bounds, w_gate, b_gate, w_up, b_up, w_down, b_down)
    return _combine(h, y_rows, dest_t, gate_pad, norm_final)


def kernel(x, mem, norm_mix, w_in, conv_w, conv_b, dt_bias, a_log, d_skip, ssd_norm, pool_w, pool_scale,
           w_out, norm_xattn, norm_mem, w_q, w_kv, w_o, norm_ffn, w_router, b_router, w_gate, b_gate,
           w_up, b_up, w_down, b_down, norm_final):
    batch, seq, d = x.shape
    t = batch * seq
    h = x.reshape(t, d)
    depth = norm_mix.shape[0]
    assert depth == 1, "the final norm is fused into the MoE combine of the single layer"
    for l in range(depth):
        dt_lo = D_SSD + D_CONV
        w_in_t = jnp.swapaxes(w_in[l], 0, 1)
        w_dt_t = jnp.pad(w_in_t[dt_lo:dt_lo + SSD_HEADS], ((0, LANES - SSD_HEADS), (0, 0)))
        proj = _mm(h, w_in_t, norm_w=norm_mix[l], n_cols=dt_lo, w_transposed=True)
        proj_u = _mm(h, w_in_t[dt_lo + SSD_HEADS:], norm_w=norm_mix[l], w_transposed=True)
        dtp = _mm(h, w_dt_t, norm_w=norm_mix[l], w_transposed=True)
        mix = _mixer(proj, proj_u, dtp, conv_w[l], conv_b[l], dt_bias[l], a_log[l], d_skip[l], ssd_norm[l],
                     pool_w[l], pool_scale[l], batch, seq)
        h = _mm(mix, w_out[l], res=h)

        q = _mm(h, w_q[l], norm_w=norm_xattn[l], out_dtype=BF16)
        kv = _mm(mem.reshape(batch * N_MEM, d), w_kv[l], norm_w=norm_mem[l], out_dtype=BF16)
        o = _attn(q, kv, batch, seq)
        h = _mm(o, w_o[l], res=h)

        out = _moe(h, norm_ffn[l], w_router[l], b_router[l], w_gate[l], b_gate[l], w_up[l], b_up[l],
                   w_down[l], b_down[l], norm_final)
    return out.reshape(batch, seq, d)
```

```python
import functools

import jax
import jax.numpy as jnp
from jax import lax
from jax.experimental import pallas as pl
from jax.experimental.pallas import tpu as pltpu

F32 = jnp.float32
BF16 = jnp.bfloat16

D_MODEL = 2048
D_SSD = 1024
SSD_HEAD_DIM = 64
SSD_HEADS = 16
SSD_GROUPS = 4
HEADS_PER_GROUP = SSD_HEADS // SSD_GROUPS
D_STATE = 128
CONV_WIDTH = 4
CHUNK = 128
D_CONV = D_SSD + 2 * SSD_GROUPS * D_STATE
D_POOL = 1024
POOL_WINDOWS = (2, 4, 8, 16)
POOL_GROUP_DIM = 256
N_MEM = 256
XATTN_HEADS = 4
XATTN_HEAD_DIM = 512
N_EXPERTS = 32
TOP_K = 4
D_FF = 2048
SWIGLU_LIMIT = 7.0
SWIGLU_ALPHA = 1.702
EPS = 1e-5

LANES = 128
SUBLANES = 8
VMEM_LIMIT = 56 * 1024 * 1024
NEG_BIG = -1e30

MOE_BLOCK = 128
MOE_ITEM_SIZES = (4 * MOE_BLOCK, 3 * MOE_BLOCK, 2 * MOE_BLOCK, MOE_BLOCK)
MOE_ITEM = MOE_ITEM_SIZES[0]
MOE_UP_TN = 512
MOE_DOWN_TN = 1024
MOE_IN_BUFS = 3
MOE_OUT_BUFS = 2
W_CHUNKS = 4
W_RING = 4
ROUTER_TOKENS = 512
SCATTER_TOKENS = 512
COMBINE_TOKENS = 256
ROW_DMA_PRIORITY = 1


def _cparams(sem):
    return pltpu.CompilerParams(dimension_semantics=sem, vmem_limit_bytes=VMEM_LIMIT)


def _rms(x, w):
    ms = jnp.mean(x * x, axis=-1, keepdims=True)
    return x * lax.rsqrt(ms + EPS) * w


def _mm_kernel(*refs, has_norm, has_res, w_transposed):
    it = iter(refs)
    a_ref = next(it)
    w_ref = next(it)
    nw_ref = next(it) if has_norm else None
    r_ref = next(it) if has_res else None
    o_ref = next(it)
    wb_ref = next(it)

    @pl.when(pl.program_id(1) == 0)
    def _():
        w = w_ref[...]
        wb_ref[...] = (w.T if w_transposed else w).astype(BF16)

    a = a_ref[...]
    if has_norm:
        a = _rms(a.astype(F32), nw_ref[...])
    acc = jnp.dot(a.astype(BF16), wb_ref[...], preferred_element_type=F32)
    if has_res:
        acc = acc + r_ref[...]
    o_ref[...] = acc.astype(o_ref.dtype)


def _mm(a, w, norm_w=None, res=None, out_dtype=F32, tm=1024, tn=1024, n_cols=None, w_transposed=False):
    m, k = a.shape
    n = w.shape[0 if w_transposed else 1] if n_cols is None else n_cols
    tn = min(tn, n)
    tm = min(tm, m)
    assert m % tm == 0 and n % tn == 0
    w_spec = (pl.BlockSpec((tn, k), lambda j, i: (j, 0)) if w_transposed
              else pl.BlockSpec((k, tn), lambda j, i: (0, j)))
    in_specs = [pl.BlockSpec((tm, k), lambda j, i: (i, 0)), w_spec]
    args = [a, w]
    if norm_w is not None:
        in_specs.append(pl.BlockSpec((1, k), lambda j, i: (0, 0)))
        args.append(norm_w.reshape(1, k))
    if res is not None:
        in_specs.append(pl.BlockSpec((tm, tn), lambda j, i: (i, j)))
        args.append(res)
    return pl.pallas_call(
        functools.partial(_mm_kernel, has_norm=norm_w is not None, has_res=res is not None,
                          w_transposed=w_transposed),
        grid=(n // tn, m // tm),
        in_specs=in_specs,
        out_specs=pl.BlockSpec((tm, tn), lambda j, i: (i, j)),
        out_shape=jax.ShapeDtypeStruct((m, n), out_dtype),
        scratch_shapes=[pltpu.VMEM((k, tn), BF16)],
        compiler_params=_cparams(("arbitrary", "arbitrary")),
    )(*args)


def _sigmoid(x):
    return 1.0 / (1.0 + jnp.exp(-x))


def _shift_rows(cur, prev, j, rows):
    return jnp.where(rows < j, pltpu.roll(prev, j, 0), pltpu.roll(cur, j, 0))


def _conv_silu(cur, prev, w_ref, b_ref, col0, rows):
    width = cur.shape[1]
    acc = cur * w_ref[CONV_WIDTH - 1:CONV_WIDTH, col0:col0 + width] + b_ref[:, col0:col0 + width]
    for j in range(1, CONV_WIDTH):
        k = CONV_WIDTH - 1 - j
        acc = acc + _shift_rows(cur, prev, j, rows) * w_ref[k:k + 1, col0:col0 + width]
    return acc * _sigmoid(acc)


def _mixer_kernel(z_ref, x_ref, bc_ref, u_ref, dt_ref, convw_ref, convb_ref, dtb_ref, alog_ref,
                  dskip_ref, ssdn_ref, poolw_ref, pscale_ref, out_ref,
                  px_ref, pbc_ref, pu_ref, state_ref):
    c = pl.program_id(1)

    @pl.when(c == 0)
    def _():
        px_ref[...] = jnp.zeros_like(px_ref)
        pbc_ref[...] = jnp.zeros_like(pbc_ref)
        pu_ref[...] = jnp.zeros_like(pu_ref)
        state_ref[...] = jnp.zeros_like(state_ref)

    rows = lax.broadcasted_iota(jnp.int32, (CHUNK, D_SSD), 0)
    x_raw = x_ref[...]
    bc_raw = bc_ref[...]
    xs = _conv_silu(x_raw, px_ref[...], convw_ref, convb_ref, 0, rows)
    bcs = _conv_silu(bc_raw, pbc_ref[...], convw_ref, convb_ref, D_SSD, rows)
    px_ref[...] = x_raw
    pbc_ref[...] = bc_raw

    dt_in = dt_ref[...] + dtb_ref[...]
    dt = jnp.maximum(dt_in, 0.0) + jnp.log(1.0 + jnp.exp(-jnp.abs(dt_in)))
    a = -jnp.exp(alog_ref[...])
    dta = dt * a
    ri = lax.broadcasted_iota(jnp.int32, (CHUNK, CHUNK), 0)
    ci = lax.broadcasted_iota(jnp.int32, (CHUNK, CHUNK), 1)
    causal = ri >= ci
    tril = jnp.where(causal, 1.0, 0.0).astype(F32)
    acs = jnp.dot(tril, dta, preferred_element_type=F32, precision=lax.Precision.HIGHEST)
    acs_t = acs.T
    dt_t = dt.T
    lo = ci < SSD_HEAD_DIM

    y_pairs = []
    for g in range(SSD_GROUPS):
        bg = bcs[:, g * D_STATE:(g + 1) * D_STATE].astype(BF16)
        cg = bcs[:, SSD_GROUPS * D_STATE + g * D_STATE:SSD_GROUPS * D_STATE + (g + 1) * D_STATE].astype(BF16)
        cb = lax.dot_general(cg, bg, (((1,), (1,)), ((), ())), preferred_element_type=F32)
        gw = HEADS_PER_GROUP * SSD_HEAD_DIM
        prev_g = state_ref[g * gw:(g + 1) * gw, :]
        yoff = lax.dot_general(cg, prev_g.astype(BF16), (((1,), (1,)), ((), ())),
                               preferred_element_type=F32)
        xw_parts = []
        for pr in range(HEADS_PER_GROUP // 2):
            pair = g * (HEADS_PER_GROUP // 2) + pr
            xs_pair = xs[:, pair * LANES:(pair + 1) * LANES]
            y_pair = None
            ecols = []
            wcols = []
            for q in range(2):
                h = pair * 2 + q
                col = acs[:, h:h + 1]
                seg = col - acs_t[h:h + 1, :]
                decay = jnp.exp(jnp.where(causal, seg, NEG_BIG))
                mat = (cb * decay * dt_t[h:h + 1, :]).astype(BF16)
                keep = lo if q == 0 else jnp.logical_not(lo)
                xh = jnp.where(keep, xs_pair, 0.0).astype(BF16)
                yd = jnp.dot(mat, xh, preferred_element_type=F32)
                y_pair = yd if y_pair is None else y_pair + yd
                ecols.append(jnp.exp(col))
                wcols.append(dt[:, h:h + 1] * jnp.exp(acs[CHUNK - 1:CHUNK, h:h + 1] - col))
            e_pair = jnp.where(lo, ecols[0], ecols[1])
            w_pair = jnp.where(lo, wcols[0], wcols[1])
            y_pairs.append(y_pair + yoff[:, pr * LANES:(pr + 1) * LANES] * e_pair)
            xw_parts.append(xs_pair * w_pair)
        xw = jnp.concatenate(xw_parts, axis=1).astype(BF16)
        s_new = lax.dot_general(xw, bg, (((0,), (0,)), ((), ())), preferred_element_type=F32)
        for r in range(HEADS_PER_GROUP):
            h = g * HEADS_PER_GROUP + r
            dec = jnp.exp(acs[CHUNK - 1:CHUNK, h:h + 1])
            lo_r = h * SSD_HEAD_DIM
            state_ref[lo_r:lo_r + SSD_HEAD_DIM, :] = (
                state_ref[lo_r:lo_r + SSD_HEAD_DIM, :] * dec
                + s_new[r * SSD_HEAD_DIM:(r + 1) * SSD_HEAD_DIM, :])

    y = jnp.concatenate(y_pairs, axis=1)
    y = y + dskip_ref[...] * xs
    z = z_ref[...]
    y = y * (z * _sigmoid(z))
    gdim = D_SSD // SSD_GROUPS
    for g in range(SSD_GROUPS):
        yg = _rms(y[:, g * gdim:(g + 1) * gdim], ssdn_ref[:, g * gdim:(g + 1) * gdim])
        out_ref[:, g * gdim:(g + 1) * gdim] = yg.astype(out_ref.dtype)

    u_cur = u_ref[...]
    u_prev = pu_ref[...]
    pos = (c * CHUNK + lax.broadcasted_iota(jnp.int32, (CHUNK, POOL_GROUP_DIM), 0) + 1).astype(F32)
    for g, win in enumerate(POOL_WINDOWS):
        sl = slice(g * POOL_GROUP_DIM, (g + 1) * POOL_GROUP_DIM)
        ug = u_cur[:, sl]
        ext = jnp.concatenate([u_prev[:, sl], ug], axis=0)
        step = 1
        while step < win:
            ext = ext + pltpu.roll(ext, step, 0)
            step *= 2
        pooled = ext[CHUNK:, :] / jnp.minimum(pos, float(win)) - ug
        mixed = jnp.dot(pooled.astype(BF16), poolw_ref[g].astype(BF16), preferred_element_type=F32)
        out_ref[:, D_SSD + g * POOL_GROUP_DIM:D_SSD + (g + 1) * POOL_GROUP_DIM] = (
            mixed * pscale_ref[:, sl]).astype(out_ref.dtype)
    pu_ref[...] = u_cur


def _mixer(proj, proj_u, dtp, conv_w, conv_b, dt_bias, a_log, d_skip, ssd_norm, pool_w, pool_scale, batch, seq):
    nchunk = seq // CHUNK
    row_map = lambda col: (lambda b, c: (b * nchunk + c, col))
    const2 = lambda b, c: (0, 0)
    pad = LANES - SSD_HEADS
    dtb = jnp.pad(dt_bias, (0, pad)).reshape(1, LANES)
    alog = jnp.pad(a_log, (0, pad)).reshape(1, LANES)
    dskip = jnp.repeat(d_skip, SSD_HEAD_DIM).reshape(1, D_SSD)
    return pl.pallas_call(
        _mixer_kernel,
        grid=(batch, nchunk),
        in_specs=[
            pl.BlockSpec((CHUNK, D_SSD), row_map(0)),
            pl.BlockSpec((CHUNK, D_SSD), row_map(1)),
            pl.BlockSpec((CHUNK, D_SSD), row_map(2)),
            pl.BlockSpec((CHUNK, D_POOL), row_map(0)),
            pl.BlockSpec((CHUNK, LANES), row_map(0)),
            pl.BlockSpec((CONV_WIDTH, D_CONV), const2),
            pl.BlockSpec((1, D_CONV), const2),
            pl.BlockSpec((1, LANES), const2),
            pl.BlockSpec((1, LANES), const2),
            pl.BlockSpec((1, D_SSD), const2),
            pl.BlockSpec((1, D_SSD), const2),
            pl.BlockSpec((len(POOL_WINDOWS), POOL_GROUP_DIM, POOL_GROUP_DIM), lambda b, c: (0, 0, 0)),
            pl.BlockSpec((1, D_POOL), const2),
        ],
        out_specs=pl.BlockSpec((CHUNK, D_MODEL), lambda b, c: (b * nchunk + c, 0)),
        out_shape=jax.ShapeDtypeStruct((batch * seq, D_MODEL), BF16),
        scratch_shapes=[pltpu.VMEM((CHUNK, D_SSD), F32), pltpu.VMEM((CHUNK, D_SSD), F32),
                        pltpu.VMEM((CHUNK, D_POOL), F32), pltpu.VMEM((SSD_HEADS * SSD_HEAD_DIM, D_STATE), F32)],
        compiler_params=_cparams(("arbitrary", "arbitrary")),
    )(proj, proj, proj, proj_u, dtp, conv_w, conv_b.reshape(1, D_CONV), dtb, alog, dskip,
      ssd_norm.reshape(1, D_SSD), pool_w, pool_scale.reshape(1, D_POOL))


def _attn_kernel(q_ref, k_ref, v_ref, o_ref):
    scale = XATTN_HEAD_DIM ** -0.5
    for h in range(XATTN_HEADS):
        sl = slice(h * XATTN_HEAD_DIM, (h + 1) * XATTN_HEAD_DIM)
        s = lax.dot_general(q_ref[:, sl], k_ref[:, sl], (((1,), (1,)), ((), ())),
                            preferred_element_type=F32) * scale
        s = s - jnp.max(s, axis=-1, keepdims=True)
        p = jnp.exp(s)
        p = p / jnp.sum(p, axis=-1, keepdims=True)
        o_ref[:, sl] = jnp.dot(p.astype(BF16), v_ref[:, sl], preferred_element_type=F32).astype(o_ref.dtype)


def _attn(q, kv, batch, seq, tq=512):
    nq = seq // tq
    return pl.pallas_call(
        _attn_kernel,
        grid=(batch, nq),
        in_specs=[pl.BlockSpec((tq, D_MODEL), lambda b, i: (b * nq + i, 0)),
                  pl.BlockSpec((N_MEM, D_MODEL), lambda b, i: (b, 0)),
                  pl.BlockSpec((N_MEM, D_MODEL), lambda b, i: (b, 1))],
        out_specs=pl.BlockSpec((tq, D_MODEL), lambda b, i: (b * nq + i, 0)),
        out_shape=jax.ShapeDtypeStruct((batch * seq, D_MODEL), BF16),
        compiler_params=_cparams(("arbitrary", "arbitrary")),
    )(q, kv, kv)


def _router_kernel(h_ref, nw_ref, wr_ref, br_ref, idx_ref, gate_ref, pos_ref, cnt_ref, run_ref):
    @pl.when(pl.program_id(0) == 0)
    def _():
        run_ref[...] = jnp.zeros_like(run_ref)

    hn = _rms(h_ref[...], nw_ref[...])
    w = wr_ref[...]
    h_hi = hn.astype(BF16)
    h_lo = (hn - h_hi.astype(F32)).astype(BF16)
    w_hi = w.astype(BF16)
    w_lo = (w - w_hi.astype(F32)).astype(BF16)
    logits = (jnp.dot(h_hi, w_hi, preferred_element_type=F32)
              + (jnp.dot(h_hi, w_lo, preferred_element_type=F32)
                 + jnp.dot(h_lo, w_hi, preferred_element_type=F32))) + br_ref[...]
    tm = logits.shape[0]
    lane = lax.broadcasted_iota(jnp.int32, logits.shape, 1)
    vals = []
    hots = []
    idx_out = jnp.zeros(logits.shape, jnp.int32)
    for k in range(TOP_K):
        m = jnp.max(logits, axis=-1, keepdims=True)
        idx = jnp.min(jnp.where(logits == m, lane, LANES), axis=-1, keepdims=True)
        hit = lane == idx
        idx_out = jnp.where(lane == k, idx, idx_out)
        logits = jnp.where(hit, -jnp.inf, logits)
        vals.append(m)
        hots.append(jnp.where(hit, 1.0, 0.0))
    exps = [jnp.exp(v - vals[0]) for v in vals]
    den = exps[0] + exps[1] + exps[2] + exps[3]
    gate_out = jnp.zeros(logits.shape, F32)
    for k in range(TOP_K):
        gate_out = jnp.where(lane == k, exps[k] / den, gate_out)
    idx_ref[...] = idx_out
    gate_ref[...] = gate_out

    hot = hots[0] + hots[1] + hots[2] + hots[3]
    ri = lax.broadcasted_iota(jnp.int32, (tm, tm), 0)
    ci = lax.broadcasted_iota(jnp.int32, (tm, tm), 1)
    before = jnp.where(ri > ci, 1.0, 0.0).astype(BF16)
    ahead = jnp.dot(before, hot.astype(BF16), preferred_element_type=F32) + run_ref[...]
    pos_out = jnp.zeros(logits.shape, F32)
    for k in range(TOP_K):
        pos_out = jnp.where(lane == k, jnp.sum(ahead * hots[k], axis=-1, keepdims=True), pos_out)
    pos_ref[...] = pos_out
    run_ref[...] = run_ref[...] + jnp.sum(hot, axis=0, keepdims=True)
    cnt_ref[...] = run_ref[...]


def _router(h, norm_w, w_router, b_router):
    t = h.shape[0]
    tm = ROUTER_TOKENS
    pad = LANES - N_EXPERTS
    wr = jnp.pad(w_router, ((0, 0), (0, pad)))
    br = jnp.pad(b_router, (0, pad), constant_values=NEG_BIG).reshape(1, LANES)
    tok_spec = pl.BlockSpec((tm, LANES), lambda i: (i, 0))
    return pl.pallas_call(
        _router_kernel,
        grid=(t // tm,),
        in_specs=[pl.BlockSpec((tm, D_MODEL), lambda i: (i, 0)),
                  pl.BlockSpec((1, D_MODEL), lambda i: (0, 0)),
                  pl.BlockSpec((D_MODEL, LANES), lambda i: (0, 0)),
                  pl.BlockSpec((1, LANES), lambda i: (0, 0))],
        out_specs=[tok_spec, tok_spec, tok_spec, pl.BlockSpec((1, LANES), lambda i: (0, 0))],
        out_shape=[jax.ShapeDtypeStruct((t, LANES), jnp.int32),
                   jax.ShapeDtypeStruct((t, LANES), F32),
                   jax.ShapeDtypeStruct((t, LANES), F32),
                   jax.ShapeDtypeStruct((1, LANES), F32)],
        scratch_shapes=[pltpu.VMEM((1, LANES), F32)],
        compiler_params=_cparams(("arbitrary",)),
    )(h, norm_w.reshape(1, D_MODEL), wr, br)


def _dest_kernel(idx_ref, pos_ref, cnt_ref, dest_ref, bounds_ref):
    cnt = jnp.broadcast_to(cnt_ref[...], (SUBLANES, LANES))
    padded = jnp.floor((cnt + (MOE_BLOCK - 1)) * (1.0 / MOE_BLOCK)) * MOE_BLOCK
    ri = lax.broadcasted_iota(jnp.int32, (LANES, LANES), 0)
    ci = lax.broadcasted_iota(jnp.int32, (LANES, LANES), 1)
    upto = jnp.where(ri <= ci, 1.0, 0.0).astype(F32)
    seg_end = jnp.dot(padded, upto, preferred_element_type=F32, precision=lax.Precision.HIGHEST)
    seg_start = seg_end - padded
    blocks = padded * (1.0 / MOE_BLOCK)
    full = jnp.floor(blocks * (1.0 * MOE_BLOCK / MOE_ITEM))
    rem = blocks - full * (MOE_ITEM // MOE_BLOCK)
    items = full + jnp.minimum(rem, 1.0)
    items_end = jnp.dot(items, upto, preferred_element_type=F32, precision=lax.Precision.HIGHEST)
    row = lax.broadcasted_iota(jnp.int32, (SUBLANES, LANES), 0)
    table = jnp.zeros((SUBLANES, LANES), F32)
    for r, v in enumerate((seg_start, seg_end, items_end, items, full, rem)):
        table = jnp.where(row == r, v, table)
    bounds_ref[...] = table.astype(jnp.int32)

    idx = idx_ref[...]
    lane = lax.broadcasted_iota(jnp.int32, idx.shape, 1)
    out = pos_ref[...]
    for k in range(TOP_K):
        hit = lane == idx[:, k:k + 1]
        base = jnp.sum(jnp.where(hit, seg_start[0:1, :], 0.0), axis=-1, keepdims=True)
        out = jnp.where(lane == k, out + base, out)
    dest_ref[...] = out.T[0:SUBLANES, :].astype(jnp.int32)


def _dest_rows(idx_pad, pos_pad, cnt_pad):
    t = idx_pad.shape[0]
    tm = ROUTER_TOKENS
    tok_spec = pl.BlockSpec((tm, LANES), lambda i: (i, 0))
    return pl.pallas_call(
        _dest_kernel,
        grid=(t // tm,),
        in_specs=[tok_spec, tok_spec, pl.BlockSpec((1, LANES), lambda i: (0, 0))],
        out_specs=[pl.BlockSpec((SUBLANES, tm), lambda i: (0, i)),
                   pl.BlockSpec((SUBLANES, LANES), lambda i: (0, 0))],
        out_shape=[jax.ShapeDtypeStruct((SUBLANES, t), jnp.int32),
                   jax.ShapeDtypeStruct((SUBLANES, LANES), jnp.int32)],
        compiler_params=_cparams(("arbitrary",)),
    )(idx_pad, pos_pad, cnt_pad)


def _pack_halves(x):
    c = x.shape[1] // 2
    lo = pltpu.bitcast(x[:, :c].astype(BF16).astype(F32), jnp.uint32)
    hi = pltpu.bitcast(x[:, c:].astype(BF16).astype(F32), jnp.uint32)
    return (lo >> 16) | hi


def _unpack_halves(w):
    lo = pltpu.bitcast(w << 16, F32)
    hi = pltpu.bitcast(w & jnp.uint32(0xFFFF0000), F32)
    return jnp.concatenate([lo, hi], axis=1).astype(BF16)


def _scatter_kernel(dest_ref, bounds_ref, h_ref, nw_ref, x_ref, zero_ref, packed_ref, zsem, sem):
    n_blocks = x_ref.shape[0] // MOE_BLOCK
    i = pl.program_id(0)
    slot = i % 2
    packed_ref[slot] = _pack_halves(_rms(h_ref[...], nw_ref[...]))

    @pl.when(i == 0)
    def _():
        zero_ref[...] = jnp.zeros_like(zero_ref)
        n_used = bounds_ref[1, N_EXPERTS - 1] // MOE_BLOCK

        def zero_copy(b):
            return pltpu.make_async_copy(zero_ref, x_ref.at[pl.ds(pl.multiple_of(b * MOE_BLOCK, MOE_BLOCK),
                                                                   MOE_BLOCK)], zsem)

        def last_block(e):
            return bounds_ref[1, e] // MOE_BLOCK - 1, bounds_ref[1, e] > bounds_ref[0, e]

        def start_e(e, carry):
            b, nonempty = last_block(e)

            @pl.when(nonempty)
            def _():
                zero_copy(b).start()
            return carry

        def wait_e(e, carry):
            b, nonempty = last_block(e)

            @pl.when(nonempty)
            def _():
                zero_copy(b).wait()
            return carry

        lax.fori_loop(0, N_EXPERTS, start_e, 0)
        lax.fori_loop(n_used, n_blocks, lambda b, c: (zero_copy(b).start(), c)[1], 0)
        lax.fori_loop(0, N_EXPERTS, wait_e, 0)
        lax.fori_loop(n_used, n_blocks, lambda b, c: (zero_copy(b).wait(), c)[1], 0)

    def issue(r, carry):
        for k in range(TOP_K):
            d = dest_ref[k, r]
            pltpu.make_async_copy(packed_ref.at[slot, pl.ds(r, 1)], x_ref.at[pl.ds(d, 1)],
                                  sem.at[slot]).start(priority=k % 2)
        return carry

    lax.fori_loop(0, SCATTER_TOKENS, issue, 0)

    def wait_rows(s):
        n_sent = SCATTER_TOKENS * TOP_K
        pltpu.make_async_copy(x_ref.at[pl.ds(0, n_sent)], x_ref.at[pl.ds(0, n_sent)], sem.at[s]).wait()

    @pl.when(i > 0)
    def _():
        wait_rows(1 - slot)

    @pl.when(i == pl.num_programs(0) - 1)
    def _():
        wait_rows(slot)


def _scatter_rows(h, norm_w, dest_t, bounds, n_rows):
    t = h.shape[0]
    steps = t // SCATTER_TOKENS
    half = D_MODEL // 2
    return pl.pallas_call(
        _scatter_kernel,
        grid=(steps,),
        in_specs=[pl.BlockSpec((SUBLANES, SCATTER_TOKENS), lambda i: (0, i), memory_space=pltpu.SMEM),
                  pl.BlockSpec(memory_space=pltpu.SMEM),
                  pl.BlockSpec((SCATTER_TOKENS, D_MODEL), lambda i: (i, 0)),
                  pl.BlockSpec((1, D_MODEL), lambda i: (0, 0))],
        out_specs=pl.BlockSpec(memory_space=pl.ANY),
        out_shape=jax.ShapeDtypeStruct((n_rows, half), jnp.uint32),
        scratch_shapes=[pltpu.VMEM((MOE_BLOCK, half), jnp.uint32), pltpu.VMEM((2, SCATTER_TOKENS, half), jnp.uint32),
                        pltpu.SemaphoreType.DMA(()), pltpu.SemaphoreType.DMA((2,))],
        compiler_params=_cparams(("arbitrary",)),
    )(dest_t, bounds, h, norm_w.reshape(1, D_MODEL))


def _item_tables(bounds, n_j, n_items_max):
    items_e = bounds[3]
    first_e = (bounds[2] - items_e) * n_j
    end_e = bounds[2] * n_j
    n_items = end_e[N_EXPERTS - 1]
    g = jnp.minimum(jnp.arange(n_items_max, dtype=jnp.int32), n_items - 1)
    e = jnp.sum((g[:, None] >= end_e[None, :N_EXPERTS]).astype(jnp.int32), axis=1)
    own = e[:, None] == jnp.arange(LANES, dtype=jnp.int32)[None, :]
    pick = lambda v: jnp.sum(jnp.where(own, v[None, :], 0), axis=1)
    first = pick(first_e)
    nb = jnp.maximum(pick(items_e), 1)
    q = g - first
    j = q // nb
    i = q % nb
    per_full = MOE_ITEM // MOE_BLOCK
    code = jnp.where(i < pick(bounds[4]), 0, per_full - pick(bounds[5]))
    row = pick(bounds[0] // MOE_BLOCK) + i * per_full
    meta = e * 32 + code * 8 + j * 2 + (i == 0).astype(jnp.int32)
    group_end = first + (j + 1) * nb
    counts = jnp.stack([n_items, bounds[1, N_EXPERTS - 1] // MOE_BLOCK])
    return meta.astype(jnp.int32), row.astype(jnp.int32), group_end.astype(jnp.int32), counts.astype(jnp.int32)


def _meta_expert(m):
    return m >> 5


def _meta_tile(m):
    return (m >> 1) & 3


def _meta_size_code(m):
    return (m >> 3) & 3


def _item_loop(meta_ref, row_ref, gend_ref, n_ref, src_ref, dst_ref, w_refs, ring, ring_sem, wb_ref,
               in_buf, out_buf, in_sem, out_sem, compute):
    n = n_ref[0]
    n_in = in_buf.shape[0]
    n_out = out_buf.shape[0]
    tn = out_buf.shape[2]
    n_j = dst_ref.shape[1] // tn
    n_mats = len(w_refs)
    cpm = W_CHUNKS // n_mats
    kc = w_refs[0].shape[1] // cpm

    def by_size(m, fn):
        for code, size in enumerate(MOE_ITEM_SIZES):
            @pl.when(_meta_size_code(m) == code)
            def _(size=size):
                fn(size)

    def rows(g, size):
        return pl.ds(pl.multiple_of(row_ref[g] * MOE_BLOCK, MOE_BLOCK), size)

    def cols(m):
        return pl.ds(pl.multiple_of(_meta_tile(m) * tn, tn), tn)

    def in_copy(g, size):
        slot = g % n_in
        return pltpu.make_async_copy(src_ref.at[rows(g, size)], in_buf.at[slot, pl.ds(0, size)], in_sem.at[slot])

    def out_copy(g, size):
        slot = g % n_out
        return pltpu.make_async_copy(out_buf.at[slot, pl.ds(0, size)], dst_ref.at[rows(g, size), cols(meta_ref[g])],
                                     out_sem.at[slot])

    def start_in(g):
        by_size(meta_ref[g], lambda size: in_copy(g, size).start(priority=ROW_DMA_PRIORITY))

    def wait_in(g):
        by_size(meta_ref[g], lambda size: in_copy(g, size).wait())

    def start_out(g):
        by_size(meta_ref[g], lambda size: out_copy(g, size).start(priority=ROW_DMA_PRIORITY))

    def wait_out(g):
        by_size(meta_ref[g], lambda size: out_copy(g, size).wait())

    def krows(c):
        r0 = (c % cpm) * kc
        return pl.ds(r0 if isinstance(c, int) else pl.multiple_of(r0, kc), kc)

    def chunk_copy(m, c, k):
        slot = c % W_RING
        return pltpu.make_async_copy(w_refs[k].at[_meta_expert(m), krows(c), cols(m)], ring.at[slot],
                                     ring_sem.at[slot])

    def for_matrix(c, fn):
        if isinstance(c, int):
            return fn(c // cpm)
        for k in range(n_mats):
            @pl.when(c // cpm == k)
            def _(k=k):
                fn(k)

    def start_chunk(m, c):
        for_matrix(c, lambda k: chunk_copy(m, c, k).start())

    def finish_chunk(m, c, wslot):
        def fn(k):
            chunk_copy(m, c, k).wait()
            wb_ref[wslot, k, krows(c), :] = ring[c % W_RING].astype(BF16)
        for_matrix(c, fn)

        @pl.when(c + W_RING < W_CHUNKS)
        def _():
            start_chunk(m, c + W_RING)

    for c0 in range(W_RING):
        start_chunk(meta_ref[0], c0)
    for g0 in range(n_in - 1):
        @pl.when(g0 < n)
        def _():
            start_in(g0)

    def body(g, carry):
        parity, done, nm = carry
        m = meta_ref[g]
        first = (m & 1) == 1

        @pl.when(first)
        def _():
            lax.fori_loop(done, W_CHUNKS, lambda c, z: (finish_chunk(m, c, 1 - parity), z)[1], 0)
            nxt = gend_ref[g]

            @pl.when(nxt < n)
            def _():
                for c0 in range(W_RING):
                    start_chunk(meta_ref[nxt], c0)

        nxt = gend_ref[g]
        parity = jnp.where(first, 1 - parity, parity)
        done = jnp.where(first, 0, done)
        nm = jnp.where(first, jnp.where(nxt < n, meta_ref[jnp.minimum(nxt, n - 1)], -1), nm)

        wait_in(g)

        @pl.when(g + n_in - 1 < n)
        def _():
            start_in(g + n_in - 1)

        @pl.when(g >= n_out)
        def _():
            wait_out(g - n_out)

        def run(size):
            x = in_buf[g % n_in, pl.ds(0, size)]
            out_buf[g % n_out, pl.ds(0, size)] = compute(x, wb_ref.at[parity], m).astype(out_buf.dtype)

        by_size(m, run)
        start_out(g)

        active = (nm >= 0) & (done < W_CHUNKS)

        @pl.when(active)
        def _():
            finish_chunk(nm, done, 1 - parity)

        return parity, jnp.where(active, done + 1, done), nm

    lax.fori_loop(0, n, body, (jnp.int32(1), jnp.int32(0), meta_ref[0]))
    lax.fori_loop(jnp.maximum(n - n_out, 0), n, lambda g, c: (wait_out(g), c)[1], 0)

    n_blocks = dst_ref.shape[0] // MOE_BLOCK
    n_used = n_ref[1]
    out_buf[0] = jnp.zeros(out_buf.shape[1:], out_buf.dtype)
    for jt in range(n_j):
        def tail_copy(b, jt=jt):
            r = pl.ds(pl.multiple_of(b * MOE_BLOCK, MOE_BLOCK), MOE_BLOCK)
            return pltpu.make_async_copy(out_buf.at[0, pl.ds(0, MOE_BLOCK)], dst_ref.at[r, jt * tn:(jt + 1) * tn],
                                         out_sem.at[0])

        lax.fori_loop(n_used, n_blocks, lambda b, c: (tail_copy(b).start(), c)[1], 0)
        lax.fori_loop(n_used, n_blocks, lambda b, c: (tail_copy(b).wait(), c)[1], 0)


def _expert_up_kernel(meta_ref, row_ref, gend_ref, n_ref, x_ref, bg_ref, bu_ref, wg_ref, wu_ref, act_ref,
                      ring, wb_ref, x_buf, a_buf, ring_sem, in_sem, out_sem):
    n_j = D_FF // MOE_UP_TN

    def compute(x, w, m):
        tile = _meta_expert(m) * n_j + _meta_tile(m)
        xb = _unpack_halves(x)
        g = jnp.dot(xb, w[0], preferred_element_type=F32) + bg_ref[pl.ds(tile, 1), :]
        u = jnp.dot(xb, w[1], preferred_element_type=F32) + bu_ref[pl.ds(tile, 1), :]
        g = jnp.minimum(g, SWIGLU_LIMIT)
        u = jnp.clip(u, -SWIGLU_LIMIT, SWIGLU_LIMIT)
        return g * _sigmoid(SWIGLU_ALPHA * g) * (u + 1.0)

    _item_loop(meta_ref, row_ref, gend_ref, n_ref, x_ref, act_ref, (wg_ref, wu_ref), ring, ring_sem, wb_ref,
               x_buf, a_buf, in_sem, out_sem, compute)


def _expert_down_kernel(meta_ref, row_ref, gend_ref, n_ref, a_ref, bd_ref, wd_ref, y_ref,
                        ring, wb_ref, a_buf, y_buf, ring_sem, in_sem, out_sem):
    n_j = D_MODEL // MOE_DOWN_TN

    def compute(a, w, m):
        tile = _meta_expert(m) * n_j + _meta_tile(m)
        return jnp.dot(a, w[0], preferred_element_type=F32) + bd_ref[pl.ds(tile, 1), :]

    _item_loop(meta_ref, row_ref, gend_ref, n_ref, a_ref, y_ref, (wd_ref,), ring, ring_sem, wb_ref,
               a_buf, y_buf, in_sem, out_sem, compute)


def _experts(x_rows, bounds, w_gate, b_gate, w_up, b_up, w_down, b_down):
    n_rows = x_rows.shape[0]
    n_items_max = n_rows // MOE_ITEM + (len(MOE_ITEM_SIZES) - 1) * N_EXPERTS
    any_spec = pl.BlockSpec(memory_space=pl.ANY)
    sems = [pltpu.SemaphoreType.DMA((W_RING,)), pltpu.SemaphoreType.DMA((MOE_IN_BUFS,)),
            pltpu.SemaphoreType.DMA((MOE_OUT_BUFS,))]

    def full(shape):
        return pl.BlockSpec(shape, lambda i, *_: (0,) * len(shape))

    n_j = D_FF // MOE_UP_TN
    act = pl.pallas_call(
        _expert_up_kernel,
        grid_spec=pltpu.PrefetchScalarGridSpec(
            num_scalar_prefetch=4, grid=(1,),
            in_specs=[any_spec, full((N_EXPERTS * n_j, MOE_UP_TN)),
                      full((N_EXPERTS * n_j, MOE_UP_TN)), any_spec, any_spec],
            out_specs=any_spec,
            scratch_shapes=[pltpu.VMEM((W_RING, 2 * D_MODEL // W_CHUNKS, MOE_UP_TN), F32),
                            pltpu.VMEM((2, 2, D_MODEL, MOE_UP_TN), BF16),
                            pltpu.VMEM((MOE_IN_BUFS, MOE_ITEM, D_MODEL // 2), jnp.uint32),
                            pltpu.VMEM((MOE_OUT_BUFS, MOE_ITEM, MOE_UP_TN), BF16)] + sems),
        out_shape=jax.ShapeDtypeStruct((n_rows, D_FF), BF16),
        compiler_params=_cparams(("arbitrary",)),
    )(*_item_tables(bounds, n_j, n_items_max * n_j), x_rows,
      b_gate.reshape(N_EXPERTS * n_j, MOE_UP_TN), b_up.reshape(N_EXPERTS * n_j, MOE_UP_TN), w_gate, w_up)

    n_j = D_MODEL // MOE_DOWN_TN
    return pl.pallas_call(
        _expert_down_kernel,
        grid_spec=pltpu.PrefetchScalarGridSpec(
            num_scalar_prefetch=4, grid=(1,),
            in_specs=[any_spec, full((N_EXPERTS * n_j, MOE_DOWN_TN)), any_spec],
            out_specs=any_spec,
            scratch_shapes=[pltpu.VMEM((W_RING, D_FF // W_CHUNKS, MOE_DOWN_TN), F32),
                            pltpu.VMEM((2, 1, D_FF, MOE_DOWN_TN), BF16),
                            pltpu.VMEM((MOE_IN_BUFS, MOE_ITEM, D_FF), BF16),
                            pltpu.VMEM((MOE_OUT_BUFS, MOE_ITEM, MOE_DOWN_TN), F32)] + sems),
        out_shape=jax.ShapeDtypeStruct((n_rows, D_MODEL), F32),
        compiler_params=_cparams(("arbitrary",)),
    )(*_item_tables(bounds, n_j, n_items_max * n_j), act,
      b_down.reshape(N_EXPERTS * n_j, MOE_DOWN_TN), w_down)


def _combine_kernel(dest_ref, next_ref, h_ref, gate_ref, nw_ref, y_ref, o_ref, buf_ref, sem):
    i = pl.program_id(0)
    steps = pl.num_programs(0)

    def issue(d_ref, slot):
        def body(r, carry):
            for k in range(TOP_K):
                pltpu.make_async_copy(y_ref.at[pl.ds(d_ref[k, r], 1)], buf_ref.at[slot, k, pl.ds(r, 1)],
                                      sem.at[slot]).start(priority=k % 2)
            return carry
        lax.fori_loop(0, COMBINE_TOKENS, body, 0)

    @pl.when(i == 0)
    def _():
        issue(dest_ref, 0)

    @pl.when(i + 1 < steps)
    def _():
        issue(next_ref, (i + 1) % 2)

    slot = i % 2
    pltpu.make_async_copy(buf_ref.at[slot], buf_ref.at[slot], sem.at[slot]).wait()
    acc = h_ref[...]
    gate = gate_ref[...]
    for k in range(TOP_K):
        acc = acc + gate[:, k:k + 1] * buf_ref[slot, k]
    o_ref[...] = _rms(acc, nw_ref[...])


def _combine(h, y_rows, dest_t, gate_pad, norm_w):
    t = h.shape[0]
    steps = t // COMBINE_TOKENS
    return pl.pallas_call(
        _combine_kernel,
        grid=(steps,),
        in_specs=[pl.BlockSpec((SUBLANES, COMBINE_TOKENS), lambda i: (0, i), memory_space=pltpu.SMEM),
                  pl.BlockSpec((SUBLANES, COMBINE_TOKENS), lambda i: (0, jnp.minimum(i + 1, steps - 1)),
                               memory_space=pltpu.SMEM),
                  pl.BlockSpec((COMBINE_TOKENS, D_MODEL), lambda i: (i, 0)),
                  pl.BlockSpec((COMBINE_TOKENS, LANES), lambda i: (i, 0)),
                  pl.BlockSpec((1, D_MODEL), lambda i: (0, 0)),
                  pl.BlockSpec(memory_space=pl.ANY)],
        out_specs=pl.BlockSpec((COMBINE_TOKENS, D_MODEL), lambda i: (i, 0)),
        out_shape=jax.ShapeDtypeStruct((t, D_MODEL), F32),
        scratch_shapes=[pltpu.VMEM((2, TOP_K, COMBINE_TOKENS, D_MODEL), F32), pltpu.SemaphoreType.DMA((2,))],
        compiler_params=_cparams(("arbitrary",)),
    )(dest_t, dest_t, h, gate_pad, norm_w.reshape(1, D_MODEL), y_rows)


def _moe(h, norm_ffn, w_router, b_router, w_gate, b_gate, w_up, b_up, w_down, b_down, norm_final):
    t = h.shape[0]
    tk = t * TOP_K
    n_blocks = (tk + N_EXPERTS * (MOE_BLOCK - 1) + MOE_BLOCK - 1) // MOE_BLOCK
    n_rows = n_blocks * MOE_BLOCK
    idx_pad, gate_pad, pos_pad, cnt_pad = _router(h, norm_ffn, w_router, b_router)
    dest_t, bounds = _dest_rows(idx_pad, pos_pad, cnt_pad)
    x_rows = _scatter_rows(h, norm_ffn, dest_t, bounds, n_rows)
    y_rows = _experts(x_rows, bounds, w_gate, b_gate, w_up, b_up, w_down, b_down)
    return _combine(h, y_rows, dest_t, gate_pad, norm_final)


def kernel(x, mem, norm_mix, w_in, conv_w, conv_b, dt_bias, a_log, d_skip, ssd_norm, pool_w, pool_scale,
           w_out, norm_xattn, norm_mem, w_q, w_kv, w_o, norm_ffn, w_router, b_router, w_gate, b_gate,
           w_up, b_up, w_down, b_down, norm_final):
    batch, seq, d = x.shape
    t = batch * seq
    h = x.reshape(t, d)
    depth = norm_mix.shape[0]
    assert depth == 1, "the final norm is fused into the MoE combine of the single layer"
    for l in range(depth):
        dt_lo = D_SSD + D_CONV
        w_in_t = jnp.swapaxes(w_in[l], 0, 1)
        w_dt_t = jnp.pad(w_in_t[dt_lo:dt_lo + SSD_HEADS], ((0, LANES - SSD_HEADS), (0, 0)))
        proj = _mm(h, w_in_t, norm_w=norm_mix[l], n_cols=dt_lo, w_transposed=True)
        proj_u = _mm(h, w_in_t[dt_lo + SSD_HEADS:], norm_w=norm_mix[l], w_transposed=True)
        dtp = _mm(h, w_dt_t, norm_w=norm_mix[l], w_transposed=True)
        mix = _mixer(proj, proj_u, dtp, conv_w[l], conv_b[l], dt_bias[l], a_log[l], d_skip[l], ssd_norm[l],
                     pool_w[l], pool_scale[l], batch, seq)
        h = _mm(mix, w_out[l], res=h)

        q = _mm(h, w_q[l], norm_w=norm_xattn[l], out_dtype=BF16)
        kv = _mm(mem.reshape(batch * N_MEM, d), w_kv[l], norm_w=norm_mem[l], out_dtype=BF16)
        o = _attn(q, kv, batch, seq)
        h = _mm(o, w_o[l], res=h)

        out = _moe(h, norm_ffn[l], w_router[l], b_router[l], w_gate[l], b_gate[l], w_up[l], b_up[l],
                   w_down[l], b_down[l], norm_final)
    return out.reshape(batch, seq, d)
```

```python
import functools

import jax
import jax.numpy as jnp
from jax import lax
from jax.experimental import pallas as pl
from jax.experimental.pallas import tpu as pltpu

F32 = jnp.float32
BF16 = jnp.bfloat16

D_MODEL = 2048
D_SSD = 1024
SSD_HEAD_DIM = 64
SSD_HEADS = 16
SSD_GROUPS = 4
HEADS_PER_GROUP = SSD_HEADS // SSD_GROUPS
D_STATE = 128
CONV_WIDTH = 4
CHUNK = 128
D_CONV = D_SSD + 2 * SSD_GROUPS * D_STATE
D_POOL = 1024
POOL_WINDOWS = (2, 4, 8, 16)
POOL_GROUP_DIM = 256
N_MEM = 256
XATTN_HEADS = 4
XATTN_HEAD_DIM = 512
N_EXPERTS = 32
TOP_K = 4
D_FF = 2048
SWIGLU_LIMIT = 7.0
SWIGLU_ALPHA = 1.702
EPS = 1e-5

LANES = 128
SUBLANES = 8
VMEM_LIMIT = 56 * 1024 * 1024
NEG_BIG = -1e30

MOE_BLOCK = 128
MOE_ITEM_SIZES = (4 * MOE_BLOCK, 2 * MOE_BLOCK, MOE_BLOCK)
MOE_ITEM = MOE_ITEM_SIZES[0]
MOE_UP_TN = 512
MOE_DOWN_TN = 1024
MOE_IN_BUFS = 3
MOE_OUT_BUFS = 2
W_CHUNKS = 4
W_RING = 4
MM_RING = 3
ROUTER_TOKENS = 512
SCATTER_TOKENS = 512
COMBINE_TOKENS = 256
ROW_DMA_PRIORITY = 1


def _cparams(sem):
    return pltpu.CompilerParams(dimension_semantics=sem, vmem_limit_bytes=VMEM_LIMIT)


def _rms(x, w):
    ms = jnp.mean(x * x, axis=-1, keepdims=True)
    return x * lax.rsqrt(ms + EPS) * w


def _mm_kernel(*refs, has_norm, has_res, w_transposed, a_ring):
    it = iter(refs)
    a_ref = next(it)
    w_ref = next(it)
    nw_ref = next(it) if has_norm else None
    r_ref = next(it) if has_res else None
    o_ref = next(it)
    wb_ref = next(it)

    @pl.when(pl.program_id(1) == 0)
    def _():
        w = w_ref[...]
        wb_ref[...] = (w.T if w_transposed else w).astype(BF16)

    if a_ring:
        a_buf, a_sem = next(it), next(it)
        tm = a_buf.shape[1]
        n_i = pl.num_programs(1)
        step = pl.program_id(0) * n_i + pl.program_id(1)
        total = pl.num_programs(0) * n_i

        def a_copy(s):
            slot = lax.rem(s, MM_RING)
            rows = pl.multiple_of(lax.rem(s, n_i) * tm, tm)
            return pltpu.make_async_copy(a_ref.at[pl.ds(rows, tm)], a_buf.at[slot], a_sem.at[slot])

        @pl.when(step == 0)
        def _():
            for s in range(MM_RING - 1):
                a_copy(jnp.int32(s)).start()

        @pl.when(step + MM_RING - 1 < total)
        def _():
            a_copy(step + MM_RING - 1).start()

        a_copy(step).wait()
        a = a_buf[lax.rem(step, MM_RING)]
    else:
        a = a_ref[...]
    if has_norm:
        a = _rms(a.astype(F32), nw_ref[...])
    acc = jnp.dot(a.astype(BF16), wb_ref[...], preferred_element_type=F32)
    if has_res:
        acc = acc + r_ref[...]
    o_ref[...] = acc.astype(o_ref.dtype)


def _mm(a, w, norm_w=None, res=None, out_dtype=F32, tm=1024, tn=1024, n_cols=None, w_transposed=False):
    m, k = a.shape
    n = w.shape[0 if w_transposed else 1] if n_cols is None else n_cols
    tn = min(tn, n)
    tm = min(tm, m)
    assert m % tm == 0 and n % tn == 0
    w_spec = (pl.BlockSpec((tn, k), lambda j, i: (j, 0)) if w_transposed
              else pl.BlockSpec((k, tn), lambda j, i: (0, j)))
    a_ring = a.dtype == BF16 and (m // tm) * (n // tn) >= MM_RING
    a_spec = pl.BlockSpec(memory_space=pl.ANY) if a_ring else pl.BlockSpec((tm, k), lambda j, i: (i, 0))
    in_specs = [a_spec, w_spec]
    args = [a, w]
    if norm_w is not None:
        in_specs.append(pl.BlockSpec((1, k), lambda j, i: (0, 0)))
        args.append(norm_w.reshape(1, k))
    if res is not None:
        in_specs.append(pl.BlockSpec((tm, tn), lambda j, i: (i, j)))
        args.append(res)
    return pl.pallas_call(
        functools.partial(_mm_kernel, has_norm=norm_w is not None, has_res=res is not None,
                          w_transposed=w_transposed, a_ring=a_ring),
        grid=(n // tn, m // tm),
        in_specs=in_specs,
        out_specs=pl.BlockSpec((tm, tn), lambda j, i: (i, j)),
        out_shape=jax.ShapeDtypeStruct((m, n), out_dtype),
        scratch_shapes=[pltpu.VMEM((k, tn), BF16)] + (
            [pltpu.VMEM((MM_RING, tm, k), a.dtype), pltpu.SemaphoreType.DMA((MM_RING,))] if a_ring else []),
        compiler_params=_cparams(("arbitrary", "arbitrary")),
    )(*args)


def _sigmoid(x):
    return 1.0 / (1.0 + jnp.exp(-x))


def _shift_rows(cur, prev, j, rows):
    return jnp.where(rows < j, pltpu.roll(prev, j, 0), pltpu.roll(cur, j, 0))


def _conv_silu(cur, prev, w_ref, b_ref, col0, rows):
    width = cur.shape[1]
    acc = cur * w_ref[CONV_WIDTH - 1:CONV_WIDTH, col0:col0 + width] + b_ref[:, col0:col0 + width]
    for j in range(1, CONV_WIDTH):
        k = CONV_WIDTH - 1 - j
        acc = acc + _shift_rows(cur, prev, j, rows) * w_ref[k:k + 1, col0:col0 + width]
    return acc * _sigmoid(acc)


def _mixer_kernel(z_ref, x_ref, bc_ref, u_ref, dt_ref, convw_ref, convb_ref, dtb_ref, alog_ref,
                  dskip_ref, ssdn_ref, poolw_ref, pscale_ref, out_ref,
                  px_ref, pbc_ref, pu_ref, state_ref):
    c = pl.program_id(1)

    @pl.when(c == 0)
    def _():
        px_ref[...] = jnp.zeros_like(px_ref)
        pbc_ref[...] = jnp.zeros_like(pbc_ref)
        pu_ref[...] = jnp.zeros_like(pu_ref)
        state_ref[...] = jnp.zeros_like(state_ref)

    rows = lax.broadcasted_iota(jnp.int32, (CHUNK, D_SSD), 0)
    x_raw = x_ref[...]
    bc_raw = bc_ref[...]
    xs = _conv_silu(x_raw, px_ref[...], convw_ref, convb_ref, 0, rows)
    bcs = _conv_silu(bc_raw, pbc_ref[...], convw_ref, convb_ref, D_SSD, rows)
    px_ref[...] = x_raw
    pbc_ref[...] = bc_raw

    dt_in = dt_ref[...] + dtb_ref[...]
    dt = jnp.maximum(dt_in, 0.0) + jnp.log(1.0 + jnp.exp(-jnp.abs(dt_in)))
    a = -jnp.exp(alog_ref[...])
    dta = dt * a
    ri = lax.broadcasted_iota(jnp.int32, (CHUNK, CHUNK), 0)
    ci = lax.broadcasted_iota(jnp.int32, (CHUNK, CHUNK), 1)
    causal = ri >= ci
    tril = jnp.where(causal, 1.0, 0.0).astype(F32)
    acs = jnp.dot(tril, dta, preferred_element_type=F32, precision=lax.Precision.HIGHEST)
    acs_t = acs.T
    dt_t = dt.T
    lo = ci < SSD_HEAD_DIM

    y_pairs = []
    for g in range(SSD_GROUPS):
        bg = bcs[:, g * D_STATE:(g + 1) * D_STATE].astype(BF16)
        cg = bcs[:, SSD_GROUPS * D_STATE + g * D_STATE:SSD_GROUPS * D_STATE + (g + 1) * D_STATE].astype(BF16)
        cb = lax.dot_general(cg, bg, (((1,), (1,)), ((), ())), preferred_element_type=F32)
        gw = HEADS_PER_GROUP * SSD_HEAD_DIM
        prev_g = state_ref[g * gw:(g + 1) * gw, :]
        yoff = lax.dot_general(cg, prev_g.astype(BF16), (((1,), (1,)), ((), ())),
                               preferred_element_type=F32)
        xw_parts = []
        for pr in range(HEADS_PER_GROUP // 2):
            pair = g * (HEADS_PER_GROUP // 2) + pr
            xs_pair = xs[:, pair * LANES:(pair + 1) * LANES]
            y_pair = None
            ecols = []
            wcols = []
            for q in range(2):
                h = pair * 2 + q
                col = acs[:, h:h + 1]
                seg = col - acs_t[h:h + 1, :]
                decay = jnp.exp(jnp.where(causal, seg, NEG_BIG))
                mat = (cb * decay * dt_t[h:h + 1, :]).astype(BF16)
                keep = lo if q == 0 else jnp.logical_not(lo)
                xh = jnp.where(keep, xs_pair, 0.0).astype(BF16)
                yd = jnp.dot(mat, xh, preferred_element_type=F32)
                y_pair = yd if y_pair is None else y_pair + yd
                ecols.append(jnp.exp(col))
                wcols.append(dt[:, h:h + 1] * jnp.exp(acs[CHUNK - 1:CHUNK, h:h + 1] - col))
            e_pair = jnp.where(lo, ecols[0], ecols[1])
            w_pair = jnp.where(lo, wcols[0], wcols[1])
            y_pairs.append(y_pair + yoff[:, pr * LANES:(pr + 1) * LANES] * e_pair)
            xw_parts.append(xs_pair * w_pair)
        xw = jnp.concatenate(xw_parts, axis=1).astype(BF16)
        s_new = lax.dot_general(xw, bg, (((0,), (0,)), ((), ())), preferred_element_type=F32)
        for r in range(HEADS_PER_GROUP):
            h = g * HEADS_PER_GROUP + r
            dec = jnp.exp(acs[CHUNK - 1:CHUNK, h:h + 1])
            lo_r = h * SSD_HEAD_DIM
            state_ref[lo_r:lo_r + SSD_HEAD_DIM, :] = (
                state_ref[lo_r:lo_r + SSD_HEAD_DIM, :] * dec
                + s_new[r * SSD_HEAD_DIM:(r + 1) * SSD_HEAD_DIM, :])

    y = jnp.concatenate(y_pairs, axis=1)
    y = y + dskip_ref[...] * xs
    z = z_ref[...]
    y = y * (z * _sigmoid(z))
    gdim = D_SSD // SSD_GROUPS
    for g in range(SSD_GROUPS):
        yg = _rms(y[:, g * gdim:(g + 1) * gdim], ssdn_ref[:, g * gdim:(g + 1) * gdim])
        out_ref[:, g * gdim:(g + 1) * gdim] = yg.astype(out_ref.dtype)

    u_cur = u_ref[...]
    u_prev = pu_ref[...]
    pos = (c * CHUNK + lax.broadcasted_iota(jnp.int32, (CHUNK, POOL_GROUP_DIM), 0) + 1).astype(F32)
    for g, win in enumerate(POOL_WINDOWS):
        sl = slice(g * POOL_GROUP_DIM, (g + 1) * POOL_GROUP_DIM)
        ug = u_cur[:, sl]
        ext = jnp.concatenate([u_prev[:, sl], ug], axis=0)
        step = 1
        while step < win:
            ext = ext + pltpu.roll(ext, step, 0)
            step *= 2
        pooled = ext[CHUNK:, :] / jnp.minimum(pos, float(win)) - ug
        mixed = jnp.dot(pooled.astype(BF16), poolw_ref[g].astype(BF16), preferred_element_type=F32)
        out_ref[:, D_SSD + g * POOL_GROUP_DIM:D_SSD + (g + 1) * POOL_GROUP_DIM] = (
            mixed * pscale_ref[:, sl]).astype(out_ref.dtype)
    pu_ref[...] = u_cur


def _mixer(proj, proj_u, dtp, conv_w, conv_b, dt_bias, a_log, d_skip, ssd_norm, pool_w, pool_scale, batch, seq):
    nchunk = seq // CHUNK
    row_map = lambda col: (lambda b, c: (b * nchunk + c, col))
    const2 = lambda b, c: (0, 0)
    pad = LANES - SSD_HEADS
    dtb = jnp.pad(dt_bias, (0, pad)).reshape(1, LANES)
    alog = jnp.pad(a_log, (0, pad)).reshape(1, LANES)
    dskip = jnp.repeat(d_skip, SSD_HEAD_DIM).reshape(1, D_SSD)
    return pl.pallas_call(
        _mixer_kernel,
        grid=(batch, nchunk),
        in_specs=[
            pl.BlockSpec((CHUNK, D_SSD), row_map(0)),
            pl.BlockSpec((CHUNK, D_SSD), row_map(1)),
            pl.BlockSpec((CHUNK, D_SSD), row_map(2)),
            pl.BlockSpec((CHUNK, D_POOL), row_map(0)),
            pl.BlockSpec((CHUNK, LANES), row_map(0)),
            pl.BlockSpec((CONV_WIDTH, D_CONV), const2),
            pl.BlockSpec((1, D_CONV), const2),
            pl.BlockSpec((1, LANES), const2),
            pl.BlockSpec((1, LANES), const2),
            pl.BlockSpec((1, D_SSD), const2),
            pl.BlockSpec((1, D_SSD), const2),
            pl.BlockSpec((len(POOL_WINDOWS), POOL_GROUP_DIM, POOL_GROUP_DIM), lambda b, c: (0, 0, 0)),
            pl.BlockSpec((1, D_POOL), const2),
        ],
        out_specs=pl.BlockSpec((CHUNK, D_MODEL), lambda b, c: (b * nchunk + c, 0)),
        out_shape=jax.ShapeDtypeStruct((batch * seq, D_MODEL), BF16),
        scratch_shapes=[pltpu.VMEM((CHUNK, D_SSD), F32), pltpu.VMEM((CHUNK, D_SSD), F32),
                        pltpu.VMEM((CHUNK, D_POOL), F32), pltpu.VMEM((SSD_HEADS * SSD_HEAD_DIM, D_STATE), F32)],
        compiler_params=_cparams(("arbitrary", "arbitrary")),
    )(proj, proj, proj, proj_u, dtp, conv_w, conv_b.reshape(1, D_CONV), dtb, alog, dskip,
      ssd_norm.reshape(1, D_SSD), pool_w, pool_scale.reshape(1, D_POOL))


def _attn_kernel(q_ref, k_ref, v_ref, o_ref):
    scale = XATTN_HEAD_DIM ** -0.5
    for h in range(XATTN_HEADS):
        sl = slice(h * XATTN_HEAD_DIM, (h + 1) * XATTN_HEAD_DIM)
        s = lax.dot_general(q_ref[:, sl], k_ref[:, sl], (((1,), (1,)), ((), ())),
                            preferred_element_type=F32) * scale
        s = s - jnp.max(s, axis=-1, keepdims=True)
        p = jnp.exp(s)
        p = p / jnp.sum(p, axis=-1, keepdims=True)
        o_ref[:, sl] = jnp.dot(p.astype(BF16), v_ref[:, sl], preferred_element_type=F32).astype(o_ref.dtype)


def _attn(q, kv, batch, seq, tq=512):
    nq = seq // tq
    return pl.pallas_call(
        _attn_kernel,
        grid=(batch, nq),
        in_specs=[pl.BlockSpec((tq, D_MODEL), lambda b, i: (b * nq + i, 0)),
                  pl.BlockSpec((N_MEM, D_MODEL), lambda b, i: (b, 0)),
                  pl.BlockSpec((N_MEM, D_MODEL), lambda b, i: (b, 1))],
        out_specs=pl.BlockSpec((tq, D_MODEL), lambda b, i: (b * nq + i, 0)),
        out_shape=jax.ShapeDtypeStruct((batch * seq, D_MODEL), BF16),
        compiler_params=_cparams(("arbitrary", "arbitrary")),
    )(q, kv, kv)


def _router_kernel(h_ref, nw_ref, wr_ref, br_ref, idx_ref, gate_ref, pos_ref, cnt_ref, run_ref):
    @pl.when(pl.program_id(0) == 0)
    def _():
        run_ref[...] = jnp.zeros_like(run_ref)

    hn = _rms(h_ref[...], nw_ref[...])
    w = wr_ref[...]
    h_hi = hn.astype(BF16)
    h_lo = (hn - h_hi.astype(F32)).astype(BF16)
    w_hi = w.astype(BF16)
    w_lo = (w - w_hi.astype(F32)).astype(BF16)
    logits = (jnp.dot(h_hi, w_hi, preferred_element_type=F32)
              + (jnp.dot(h_hi, w_lo, preferred_element_type=F32)
                 + jnp.dot(h_lo, w_hi, preferred_element_type=F32))) + br_ref[...]
    tm = logits.shape[0]
    lane = lax.broadcasted_iota(jnp.int32, logits.shape, 1)
    vals = []
    hots = []
    idx_out = jnp.zeros(logits.shape, jnp.int32)
    for k in range(TOP_K):
        m = jnp.max(logits, axis=-1, keepdims=True)
        idx = jnp.min(jnp.where(logits == m, lane, LANES), axis=-1, keepdims=True)
        hit = lane == idx
        idx_out = jnp.where(lane == k, idx, idx_out)
        logits = jnp.where(hit, -jnp.inf, logits)
        vals.append(m)
        hots.append(jnp.where(hit, 1.0, 0.0))
    exps = [jnp.exp(v - vals[0]) for v in vals]
    den = exps[0] + exps[1] + exps[2] + exps[3]
    gate_out = jnp.zeros(logits.shape, F32)
    for k in range(TOP_K):
        gate_out = jnp.where(lane == k, exps[k] / den, gate_out)
    idx_ref[...] = idx_out
    gate_ref[...] = gate_out

    hot = hots[0] + hots[1] + hots[2] + hots[3]
    ri = lax.broadcasted_iota(jnp.int32, (tm, tm), 0)
    ci = lax.broadcasted_iota(jnp.int32, (tm, tm), 1)
    before = jnp.where(ri > ci, 1.0, 0.0).astype(BF16)
    ahead = jnp.dot(before, hot.astype(BF16), preferred_element_type=F32) + run_ref[...]
    pos_out = jnp.zeros(logits.shape, F32)
    for k in range(TOP_K):
        pos_out = jnp.where(lane == k, jnp.sum(ahead * hots[k], axis=-1, keepdims=True), pos_out)
    pos_ref[...] = pos_out
    run_ref[...] = run_ref[...] + jnp.sum(hot, axis=0, keepdims=True)
    cnt_ref[...] = run_ref[...]


def _router(h, norm_w, w_router, b_router):
    t = h.shape[0]
    tm = ROUTER_TOKENS
    pad = LANES - N_EXPERTS
    wr = jnp.pad(w_router, ((0, 0), (0, pad)))
    br = jnp.pad(b_router, (0, pad), constant_values=NEG_BIG).reshape(1, LANES)
    tok_spec = pl.BlockSpec((tm, LANES), lambda i: (i, 0))
    return pl.pallas_call(
        _router_kernel,
        grid=(t // tm,),
        in_specs=[pl.BlockSpec((tm, D_MODEL), lambda i: (i, 0)),
                  pl.BlockSpec((1, D_MODEL), lambda i: (0, 0)),
                  pl.BlockSpec((D_MODEL, LANES), lambda i: (0, 0)),
                  pl.BlockSpec((1, LANES), lambda i: (0, 0))],
        out_specs=[tok_spec, tok_spec, tok_spec, pl.BlockSpec((1, LANES), lambda i: (0, 0))],
        out_shape=[jax.ShapeDtypeStruct((t, LANES), jnp.int32),
                   jax.ShapeDtypeStruct((t, LANES), F32),
                   jax.ShapeDtypeStruct((t, LANES), F32),
                   jax.ShapeDtypeStruct((1, LANES), F32)],
        scratch_shapes=[pltpu.VMEM((1, LANES), F32)],
        compiler_params=_cparams(("arbitrary",)),
    )(h, norm_w.reshape(1, D_MODEL), wr, br)


def _dest_kernel(idx_ref, pos_ref, cnt_ref, dest_ref, bounds_ref):
    cnt = jnp.broadcast_to(cnt_ref[...], (SUBLANES, LANES))
    padded = jnp.floor((cnt + (MOE_BLOCK - 1)) * (1.0 / MOE_BLOCK)) * MOE_BLOCK
    ri = lax.broadcasted_iota(jnp.int32, (LANES, LANES), 0)
    ci = lax.broadcasted_iota(jnp.int32, (LANES, LANES), 1)
    upto = jnp.where(ri <= ci, 1.0, 0.0).astype(F32)
    seg_end = jnp.dot(padded, upto, preferred_element_type=F32, precision=lax.Precision.HIGHEST)
    seg_start = seg_end - padded
    blocks = padded * (1.0 / MOE_BLOCK)
    full = jnp.floor(blocks * (1.0 * MOE_BLOCK / MOE_ITEM))
    rem = blocks - full * (MOE_ITEM // MOE_BLOCK)
    mid = jnp.floor(rem * 0.5)
    items = full + mid + (rem - 2.0 * mid)
    items_end = jnp.dot(items, upto, preferred_element_type=F32, precision=lax.Precision.HIGHEST)
    row = lax.broadcasted_iota(jnp.int32, (SUBLANES, LANES), 0)
    table = jnp.zeros((SUBLANES, LANES), F32)
    for r, v in enumerate((seg_start, seg_end, items_end, items, full, mid)):
        table = jnp.where(row == r, v, table)
    bounds_ref[...] = table.astype(jnp.int32)

    idx = idx_ref[...]
    lane = lax.broadcasted_iota(jnp.int32, idx.shape, 1)
    out = pos_ref[...]
    for k in range(TOP_K):
        hit = lane == idx[:, k:k + 1]
        base = jnp.sum(jnp.where(hit, seg_start[0:1, :], 0.0), axis=-1, keepdims=True)
        out = jnp.where(lane == k, out + base, out)
    dest_ref[...] = out.T[0:SUBLANES, :].astype(jnp.int32)


def _dest_rows(idx_pad, pos_pad, cnt_pad):
    t = idx_pad.shape[0]
    tm = ROUTER_TOKENS
    tok_spec = pl.BlockSpec((tm, LANES), lambda i: (i, 0))
    return pl.pallas_call(
        _dest_kernel,
        grid=(t // tm,),
        in_specs=[tok_spec, tok_spec, pl.BlockSpec((1, LANES), lambda i: (0, 0))],
        out_specs=[pl.BlockSpec((SUBLANES, tm), lambda i: (0, i)),
                   pl.BlockSpec((SUBLANES, LANES), lambda i: (0, 0))],
        out_shape=[jax.ShapeDtypeStruct((SUBLANES, t), jnp.int32),
                   jax.ShapeDtypeStruct((SUBLANES, LANES), jnp.int32)],
        compiler_params=_cparams(("arbitrary",)),
    )(idx_pad, pos_pad, cnt_pad)


def _pack_halves(x):
    c = x.shape[1] // 2
    lo = pltpu.bitcast(x[:, :c].astype(BF16).astype(F32), jnp.uint32)
    hi = pltpu.bitcast(x[:, c:].astype(BF16).astype(F32), jnp.uint32)
    return (lo >> 16) | hi


def _unpack_halves(w):
    lo = pltpu.bitcast(w << 16, F32)
    hi = pltpu.bitcast(w & jnp.uint32(0xFFFF0000), F32)
    return jnp.concatenate([lo, hi], axis=1).astype(BF16)


def _scatter_kernel(dest_ref, bounds_ref, h_ref, nw_ref, x_ref, zero_ref, packed_ref, zsem, sem):
    n_blocks = x_ref.shape[0] // MOE_BLOCK
    i = pl.program_id(0)
    slot = i % 2
    packed_ref[slot] = _pack_halves(_rms(h_ref[...], nw_ref[...]))

    @pl.when(i == 0)
    def _():
        zero_ref[...] = jnp.zeros_like(zero_ref)
        n_used = bounds_ref[1, N_EXPERTS - 1] // MOE_BLOCK

        def zero_copy(b):
            return pltpu.make_async_copy(zero_ref, x_ref.at[pl.ds(pl.multiple_of(b * MOE_BLOCK, MOE_BLOCK),
                                                                   MOE_BLOCK)], zsem)

        def last_block(e):
            return bounds_ref[1, e] // MOE_BLOCK - 1, bounds_ref[1, e] > bounds_ref[0, e]

        def start_e(e, carry):
            b, nonempty = last_block(e)

            @pl.when(nonempty)
            def _():
                zero_copy(b).start()
            return carry

        def wait_e(e, carry):
            b, nonempty = last_block(e)

            @pl.when(nonempty)
            def _():
                zero_copy(b).wait()
            return carry

        lax.fori_loop(0, N_EXPERTS, start_e, 0)
        lax.fori_loop(n_used, n_blocks, lambda b, c: (zero_copy(b).start(), c)[1], 0)
        lax.fori_loop(0, N_EXPERTS, wait_e, 0)
        lax.fori_loop(n_used, n_blocks, lambda b, c: (zero_copy(b).wait(), c)[1], 0)

    def issue(r, carry):
        for k in range(TOP_K):
            d = dest_ref[k, r]
            pltpu.make_async_copy(packed_ref.at[slot, pl.ds(r, 1)], x_ref.at[pl.ds(d, 1)],
                                  sem.at[slot]).start(priority=k % 2)
        return carry

    lax.fori_loop(0, SCATTER_TOKENS, issue, 0)

    def wait_rows(s):
        n_sent = SCATTER_TOKENS * TOP_K
        pltpu.make_async_copy(x_ref.at[pl.ds(0, n_sent)], x_ref.at[pl.ds(0, n_sent)], sem.at[s]).wait()

    @pl.when(i > 0)
    def _():
        wait_rows(1 - slot)

    @pl.when(i == pl.num_programs(0) - 1)
    def _():
        wait_rows(slot)


def _scatter_rows(h, norm_w, dest_t, bounds, n_rows):
    t = h.shape[0]
    steps = t // SCATTER_TOKENS
    half = D_MODEL // 2
    return pl.pallas_call(
        _scatter_kernel,
        grid=(steps,),
        in_specs=[pl.BlockSpec((SUBLANES, SCATTER_TOKENS), lambda i: (0, i), memory_space=pltpu.SMEM),
                  pl.BlockSpec(memory_space=pltpu.SMEM),
                  pl.BlockSpec((SCATTER_TOKENS, D_MODEL), lambda i: (i, 0)),
                  pl.BlockSpec((1, D_MODEL), lambda i: (0, 0))],
        out_specs=pl.BlockSpec(memory_space=pl.ANY),
        out_shape=jax.ShapeDtypeStruct((n_rows, half), jnp.uint32),
        scratch_shapes=[pltpu.VMEM((MOE_BLOCK, half), jnp.uint32), pltpu.VMEM((2, SCATTER_TOKENS, half), jnp.uint32),
                        pltpu.SemaphoreType.DMA(()), pltpu.SemaphoreType.DMA((2,))],
        compiler_params=_cparams(("arbitrary",)),
    )(dest_t, bounds, h, norm_w.reshape(1, D_MODEL))


def _item_tables(bounds, n_j, n_items_max):
    items_e = bounds[3]
    first_e = (bounds[2] - items_e) * n_j
    end_e = bounds[2] * n_j
    n_items = end_e[N_EXPERTS - 1]
    g = jnp.minimum(jnp.arange(n_items_max, dtype=jnp.int32), n_items - 1)
    e = jnp.sum((g[:, None] >= end_e[None, :N_EXPERTS]).astype(jnp.int32), axis=1)
    own = e[:, None] == jnp.arange(LANES, dtype=jnp.int32)[None, :]
    pick = lambda v: jnp.sum(jnp.where(own, v[None, :], 0), axis=1)
    first = pick(first_e)
    nb = jnp.maximum(pick(items_e), 1)
    q = g - first
    j = q // nb
    i = q % nb
    full = pick(bounds[4])
    mid = pick(bounds[5])
    code = jnp.where(i < full, 0, jnp.where((i == full) & (mid > 0), 1, 2))
    per_full = MOE_ITEM // MOE_BLOCK
    offset = jnp.where(code == 0, i * per_full, full * per_full + jnp.where(code == 2, mid * (per_full // 2), 0))
    row = pick(bounds[0] // MOE_BLOCK) + offset
    meta = e * 32 + code * 8 + j * 2 + (i == 0).astype(jnp.int32)
    group_end = first + (j + 1) * nb
    counts = jnp.stack([n_items, bounds[1, N_EXPERTS - 1] // MOE_BLOCK])
    return meta.astype(jnp.int32), row.astype(jnp.int32), group_end.astype(jnp.int32), counts.astype(jnp.int32)


def _meta_expert(m):
    return m >> 5


def _meta_tile(m):
    return (m >> 1) & 3


def _meta_size_code(m):
    return (m >> 3) & 3


def _item_loop(meta_ref, row_ref, gend_ref, n_ref, src_ref, dst_ref, w_refs, ring, ring_sem, wb_ref,
               in_buf, out_buf, in_sem, out_sem, compute):
    n = n_ref[0]
    n_in = in_buf.shape[0]
    n_out = out_buf.shape[0]
    tn = out_buf.shape[2]
    n_j = dst_ref.shape[1] // tn
    n_mats = len(w_refs)
    cpm = W_CHUNKS // n_mats
    kc = w_refs[0].shape[1] // cpm

    def by_size(m, fn):
        for code, size in enumerate(MOE_ITEM_SIZES):
            @pl.when(_meta_size_code(m) == code)
            def _(size=size):
                fn(size)

    def rows(g, size):
        return pl.ds(pl.multiple_of(row_ref[g] * MOE_BLOCK, MOE_BLOCK), size)

    def cols(m):
        return pl.ds(pl.multiple_of(_meta_tile(m) * tn, tn), tn)

    def in_copy(g, size):
        slot = g % n_in
        return pltpu.make_async_copy(src_ref.at[rows(g, size)], in_buf.at[slot, pl.ds(0, size)], in_sem.at[slot])

    def out_copy(g, size):
        slot = g % n_out
        return pltpu.make_async_copy(out_buf.at[slot, pl.ds(0, size)], dst_ref.at[rows(g, size), cols(meta_ref[g])],
                                     out_sem.at[slot])

    def start_in(g):
        by_size(meta_ref[g], lambda size: in_copy(g, size).start(priority=ROW_DMA_PRIORITY))

    def wait_in(g):
        by_size(meta_ref[g], lambda size: in_copy(g, size).wait())

    def start_out(g):
        by_size(meta_ref[g], lambda size: out_copy(g, size).start(priority=ROW_DMA_PRIORITY))

    def wait_out(g):
        by_size(meta_ref[g], lambda size: out_copy(g, size).wait())

    def krows(c):
        r0 = (c % cpm) * kc
        return pl.ds(r0 if isinstance(c, int) else pl.multiple_of(r0, kc), kc)

    def chunk_copy(m, c, k):
        slot = c % W_RING
        return pltpu.make_async_copy(w_refs[k].at[_meta_expert(m), krows(c), cols(m)], ring.at[slot],
                                     ring_sem.at[slot])

    def for_matrix(c, fn):
        if isinstance(c, int):
            return fn(c // cpm)
        for k in range(n_mats):
            @pl.when(c // cpm == k)
            def _(k=k):
                fn(k)

    def start_chunk(m, c):
        for_matrix(c, lambda k: chunk_copy(m, c, k).start())

    def finish_chunk(m, c, wslot):
        def fn(k):
            chunk_copy(m, c, k).wait()
            wb_ref[wslot, k, krows(c), :] = ring[c % W_RING].astype(BF16)
        for_matrix(c, fn)

        @pl.when(c + W_RING < W_CHUNKS)
        def _():
            start_chunk(m, c + W_RING)

    for c0 in range(W_RING):
        start_chunk(meta_ref[0], c0)
    for g0 in range(n_in - 1):
        @pl.when(g0 < n)
        def _():
            start_in(g0)

    def body(g, carry):
        parity, done, nm = carry
        m = meta_ref[g]
        first = (m & 1) == 1

        @pl.when(first)
        def _():
            lax.fori_loop(done, W_CHUNKS, lambda c, z: (finish_chunk(m, c, 1 - parity), z)[1], 0)
            nxt = gend_ref[g]

            @pl.when(nxt < n)
            def _():
                for c0 in range(W_RING):
                    start_chunk(meta_ref[nxt], c0)

        nxt = gend_ref[g]
        parity = jnp.where(first, 1 - parity, parity)
        done = jnp.where(first, 0, done)
        nm = jnp.where(first, jnp.where(nxt < n, meta_ref[jnp.minimum(nxt, n - 1)], -1), nm)

        wait_in(g)

        @pl.when(g + n_in - 1 < n)
        def _():
            start_in(g + n_in - 1)

        @pl.when(g >= n_out)
        def _():
            wait_out(g - n_out)

        def run(size):
            x = in_buf[g % n_in, pl.ds(0, size)]
            out_buf[g % n_out, pl.ds(0, size)] = compute(x, wb_ref.at[parity], m).astype(out_buf.dtype)

        by_size(m, run)
        start_out(g)

        active = (nm >= 0) & (done < W_CHUNKS)

        @pl.when(active)
        def _():
            finish_chunk(nm, done, 1 - parity)

        return parity, jnp.where(active, done + 1, done), nm

    lax.fori_loop(0, n, body, (jnp.int32(1), jnp.int32(0), meta_ref[0]))
    lax.fori_loop(jnp.maximum(n - n_out, 0), n, lambda g, c: (wait_out(g), c)[1], 0)

    n_blocks = dst_ref.shape[0] // MOE_BLOCK
    n_used = n_ref[1]
    out_buf[0] = jnp.zeros(out_buf.shape[1:], out_buf.dtype)
    for jt in range(n_j):
        def tail_copy(b, jt=jt):
            r = pl.ds(pl.multiple_of(b * MOE_BLOCK, MOE_BLOCK), MOE_BLOCK)
            return pltpu.make_async_copy(out_buf.at[0, pl.ds(0, MOE_BLOCK)], dst_ref.at[r, jt * tn:(jt + 1) * tn],
                                         out_sem.at[0])

        lax.fori_loop(n_used, n_blocks, lambda b, c: (tail_copy(b).start(), c)[1], 0)
        lax.fori_loop(n_used, n_blocks, lambda b, c: (tail_copy(b).wait(), c)[1], 0)


def _expert_up_kernel(meta_ref, row_ref, gend_ref, n_ref, x_ref, bg_ref, bu_ref, wg_ref, wu_ref, act_ref,
                      ring, wb_ref, x_buf, a_buf, ring_sem, in_sem, out_sem):
    n_j = D_FF // MOE_UP_TN

    def compute(x, w, m):
        tile = _meta_expert(m) * n_j + _meta_tile(m)
        xb = _unpack_halves(x)
        g = jnp.dot(xb, w[0], preferred_element_type=F32) + bg_ref[pl.ds(tile, 1), :]
        u = jnp.dot(xb, w[1], preferred_element_type=F32) + bu_ref[pl.ds(tile, 1), :]
        g = jnp.minimum(g, SWIGLU_LIMIT)
        u = jnp.clip(u, -SWIGLU_LIMIT, SWIGLU_LIMIT)
        return g * _sigmoid(SWIGLU_ALPHA * g) * (u + 1.0)

    _item_loop(meta_ref, row_ref, gend_ref, n_ref, x_ref, act_ref, (wg_ref, wu_ref), ring, ring_sem, wb_ref,
               x_buf, a_buf, in_sem, out_sem, compute)


def _expert_down_kernel(meta_ref, row_ref, gend_ref, n_ref, a_ref, bd_ref, wd_ref, y_ref,
                        ring, wb_ref, a_buf, y_buf, ring_sem, in_sem, out_sem):
    n_j = D_MODEL // MOE_DOWN_TN

    def compute(a, w, m):
        tile = _meta_expert(m) * n_j + _meta_tile(m)
        return jnp.dot(a, w[0], preferred_element_type=F32) + bd_ref[pl.ds(tile, 1), :]

    _item_loop(meta_ref, row_ref, gend_ref, n_ref, a_ref, y_ref, (wd_ref,), ring, ring_sem, wb_ref,
               a_buf, y_buf, in_sem, out_sem, compute)


def _experts(x_rows, bounds, w_gate, b_gate, w_up, b_up, w_down, b_down):
    n_rows = x_rows.shape[0]
    n_items_max = n_rows // MOE_ITEM + (len(MOE_ITEM_SIZES) - 1) * N_EXPERTS
    any_spec = pl.BlockSpec(memory_space=pl.ANY)
    sems = [pltpu.SemaphoreType.DMA((W_RING,)), pltpu.SemaphoreType.DMA((MOE_IN_BUFS,)),
            pltpu.SemaphoreType.DMA((MOE_OUT_BUFS,))]

    def full(shape):
        return pl.BlockSpec(shape, lambda i, *_: (0,) * len(shape))

    n_j = D_FF // MOE_UP_TN
    act = pl.pallas_call(
        _expert_up_kernel,
        grid_spec=pltpu.PrefetchScalarGridSpec(
            num_scalar_prefetch=4, grid=(1,),
            in_specs=[any_spec, full((N_EXPERTS * n_j, MOE_UP_TN)),
                      full((N_EXPERTS * n_j, MOE_UP_TN)), any_spec, any_spec],
            out_specs=any_spec,
            scratch_shapes=[pltpu.VMEM((W_RING, 2 * D_MODEL // W_CHUNKS, MOE_UP_TN), F32),
                            pltpu.VMEM((2, 2, D_MODEL, MOE_UP_TN), BF16),
                            pltpu.VMEM((MOE_IN_BUFS, MOE_ITEM, D_MODEL // 2), jnp.uint32),
                            pltpu.VMEM((MOE_OUT_BUFS, MOE_ITEM, MOE_UP_TN), BF16)] + sems),
        out_shape=jax.ShapeDtypeStruct((n_rows, D_FF), BF16),
        compiler_params=_cparams(("arbitrary",)),
    )(*_item_tables(bounds, n_j, n_items_max * n_j), x_rows,
      b_gate.reshape(N_EXPERTS * n_j, MOE_UP_TN), b_up.reshape(N_EXPERTS * n_j, MOE_UP_TN), w_gate, w_up)

    n_j = D_MODEL // MOE_DOWN_TN
    return pl.pallas_call(
        _expert_down_kernel,
        grid_spec=pltpu.PrefetchScalarGridSpec(
            num_scalar_prefetch=4, grid=(1,),
            in_specs=[any_spec, full((N_EXPERTS * n_j, MOE_DOWN_TN)), any_spec],
            out_specs=any_spec,
            scratch_shapes=[pltpu.VMEM((W_RING, D_FF // W_CHUNKS, MOE_DOWN_TN), F32),
                            pltpu.VMEM((2, 1, D_FF, MOE_DOWN_TN), BF16),
                            pltpu.VMEM((MOE_IN_BUFS, MOE_ITEM, D_FF), BF16),
                            pltpu.VMEM((MOE_OUT_BUFS, MOE_ITEM, MOE_DOWN_TN), F32)] + sems),
        out_shape=jax.ShapeDtypeStruct((n_rows, D_MODEL), F32),
        compiler_params=_cparams(("arbitrary",)),
    )(*_item_tables(bounds, n_j, n_items_max * n_j), act,
      b_down.reshape(N_EXPERTS * n_j, MOE_DOWN_TN), w_down)


def _combine_kernel(dest_ref, next_ref, h_ref, gate_ref, nw_ref, y_ref, o_ref, buf_ref, sem):
    i = pl.program_id(0)
    steps = pl.num_programs(0)

    def issue(d_ref, slot):
        def body(r, carry):
            for k in range(TOP_K):
                pltpu.make_async_copy(y_ref.at[pl.ds(d_ref[k, r], 1)], buf_ref.at[slot, k, pl.ds(r, 1)],
                                      sem.at[slot]).start(priority=k % 2)
            return carry
        lax.fori_loop(0, COMBINE_TOKENS, body, 0)

    @pl.when(i == 0)
    def _():
        issue(dest_ref, 0)

    @pl.when(i + 1 < steps)
    def _():
        issue(next_ref, (i + 1) % 2)

    slot = i % 2
    pltpu.make_async_copy(buf_ref.at[slot], buf_ref.at[slot], sem.at[slot]).wait()
    acc = h_ref[...]
    gate = gate_ref[...]
    for k in range(TOP_K):
        acc = acc + gate[:, k:k + 1] * buf_ref[slot, k]
    o_ref[...] = _rms(acc, nw_ref[...])


def _combine(h, y_rows, dest_t, gate_pad, norm_w):
    t = h.shape[0]
    steps = t // COMBINE_TOKENS
    return pl.pallas_call(
        _combine_kernel,
        grid=(steps,),
        in_specs=[pl.BlockSpec((SUBLANES, COMBINE_TOKENS), lambda i: (0, i), memory_space=pltpu.SMEM),
                  pl.BlockSpec((SUBLANES, COMBINE_TOKENS), lambda i: (0, jnp.minimum(i + 1, steps - 1)),
                               memory_space=pltpu.SMEM),
                  pl.BlockSpec((COMBINE_TOKENS, D_MODEL), lambda i: (i, 0)),
                  pl.BlockSpec((COMBINE_TOKENS, LANES), lambda i: (i, 0)),
                  pl.BlockSpec((1, D_MODEL), lambda i: (0, 0)),
                  pl.BlockSpec(memory_space=pl.ANY)],
        out_specs=pl.BlockSpec((COMBINE_TOKENS, D_MODEL), lambda i: (i, 0)),
        out_shape=jax.ShapeDtypeStruct((t, D_MODEL), F32),
        scratch_shapes=[pltpu.VMEM((2, TOP_K, COMBINE_TOKENS, D_MODEL), F32), pltpu.SemaphoreType.DMA((2,))],
        compiler_params=_cparams(("arbitrary",)),
    )(dest_t, dest_t, h, gate_pad, norm_w.reshape(1, D_MODEL), y_rows)


def _moe(h, norm_ffn, w_router, b_router, w_gate, b_gate, w_up, b_up, w_down, b_down, norm_final):
    t = h.shape[0]
    tk = t * TOP_K
    n_blocks = (tk + N_EXPERTS * (MOE_BLOCK - 1) + MOE_BLOCK - 1) // MOE_BLOCK
    n_rows = n_blocks * MOE_BLOCK
    idx_pad, gate_pad, pos_pad, cnt_pad = _router(h, norm_ffn, w_router, b_router)
    dest_t, bounds = _dest_rows(idx_pad, pos_pad, cnt_pad)
    x_rows = _scatter_rows(h, norm_ffn, dest_t, bounds, n_rows)
    y_rows = _experts(x_rows, bounds, w_gate, b_gate, w_up, b_up, w_down, b_down)
    return _combine(h, y_rows, dest_t, gate_pad, norm_final)


def kernel(x, mem, norm_mix, w_in, conv_w, conv_b, dt_bias, a_log, d_skip, ssd_norm, pool_w, pool_scale,
           w_out, norm_xattn, norm_mem, w_q, w_kv, w_o, norm_ffn, w_router, b_router, w_gate, b_gate,
           w_up, b_up, w_down, b_down, norm_final):
    batch, seq, d = x.shape
    t = batch * seq
    h = x.reshape(t, d)
    depth = norm_mix.shape[0]
    assert depth == 1, "the final norm is fused into the MoE combine of the single layer"
    for l in range(depth):
        dt_lo = D_SSD + D_CONV
        w_in_t = jnp.swapaxes(w_in[l], 0, 1)
        w_dt_t = jnp.pad(w_in_t[dt_lo:dt_lo + SSD_HEADS], ((0, LANES - SSD_HEADS), (0, 0)))
        proj = _mm(h, w_in_t, norm_w=norm_mix[l], n_cols=dt_lo, w_transposed=True)
        proj_u = _mm(h, w_in_t[dt_lo + SSD_HEADS:], norm_w=norm_mix[l], w_transposed=True)
        dtp = _mm(h, w_dt_t, norm_w=norm_mix[l], w_transposed=True)
        mix = _mixer(proj, proj_u, dtp, conv_w[l], conv_b[l], dt_bias[l], a_log[l], d_skip[l], ssd_norm[l],
                     pool_w[l], pool_scale[l], batch, seq)
        h = _mm(mix, w_out[l], res=h)

        q = _mm(h, w_q[l], norm_w=norm_xattn[l], out_dtype=BF16)
        kv = _mm(mem.reshape(batch * N_MEM, d), w_kv[l], norm_w=norm_mem[l], out_dtype=BF16)
        o = _attn(q, kv, batch, seq)
        h = _mm(o, w_o[l], res=h)

        out = _moe(h, norm_ffn[l], w_router[l], b_router[l], w_gate[l], b_gate[l], w_up[l], b_up[l],
                   w_down[l], b_down[l], norm_final)
    return out.reshape(batch, seq, d)
```
